```python
import math
import jax
import jax.numpy as jnp
from jax import lax
import numpy as np

D_MODEL = 1024
BATCH = 32
SEQ = 256
DEPTH = 4
DEC_BATCH = 2
DEC_SEQ = 1024
PAST_LEN = 256

GRID_W = 64
N_EVEN = (DEPTH + 1) // 2
N_ODD = DEPTH // 2
EPS = 1e-6
CONV_K = 5
NH_A = 4
DK_A = 128
DV_A = 128
CHUNK_A = 64
A_QK = NH_A * DK_A
A_V = NH_A * DV_A
NH_B = 8
HP_B = 64
DSTATE = 128
NG_B = 2
R_B = NH_B // NG_B
CHUNK_B = 64
B_INNER = NH_B * HP_B
B_BC = NG_B * DSTATE
B_XBC = B_INNER + 2 * B_BC
EVEN_SIZES = (2 * A_QK, A_V, A_V, 2 * NH_A, 2 * NH_A, B_INNER, B_XBC, 2 * NH_B)
EVEN_IN = 2 * A_QK + 2 * A_V + 4 * NH_A + B_INNER + B_XBC + 2 * NH_B
EVEN_OUT = A_V + B_INNER
NH_C = 8
NKV_C = 2
G_C = NH_C // NKV_C
HD_C = 64
WINDOW = 128
QBLK = 128
NH_D = 8
Q_RANK = 256
KV_RANK = 128
NOPE_D = 64
ROPE_D = 32
V_D = 64
MLA_SCALE = (NOPE_D + ROPE_D) ** -0.5
ODD_SIZES = (NH_C * HD_C, NKV_C * HD_C, NKV_C * HD_C, Q_RANK, KV_RANK, ROPE_D)
ODD_IN = NH_C * HD_C + 2 * NKV_C * HD_C + Q_RANK + KV_RANK + ROPE_D
ODD_OUT = NH_C * HD_C + NH_D * V_D
D_FF = 4 * D_MODEL
ROPE_BASE = 10000.0

kernel_name = "hybrid_mlstm_ssd_swa_mla_diffusion_step"

F32 = jnp.float32


def _split(x, sizes):
    out, start = [], 0
    for s in sizes:
        out.append(x[..., start:start + s])
        start += s
    return out


def _flip(a):
    return jnp.flip(a, axis=1)


def _chunks(a, L):
    Bsz, T = a.shape[:2]
    return jnp.moveaxis(a.reshape((Bsz, T // L, L) + a.shape[2:]), 1, 0)


def _unchunk(a):
    nc, Bsz, L = a.shape[:3]
    return jnp.moveaxis(a, 0, 1).reshape((Bsz, nc * L) + a.shape[3:])


def rmsnorm(x, w):
    xf = x.astype(F32)
    y = xf * lax.rsqrt(jnp.mean(xf * xf, axis=-1, keepdims=True) + EPS)
    return (y * w.astype(F32)).astype(x.dtype)


def group_rms(x, n_groups, w):
    Bsz, T, C = x.shape
    xg = x.astype(F32).reshape(Bsz, T, n_groups, C // n_groups)
    xg = xg * lax.rsqrt(jnp.mean(xg * xg, axis=-1, keepdims=True) + EPS)
    return xg.reshape(Bsz, T, C) * w.astype(F32)


def dwconv(x, w, b):
    C = x.shape[-1]
    y = lax.conv_general_dilated(x, w[:, None, :].astype(x.dtype), window_strides=(1,),
                                 padding=[(CONV_K // 2, CONV_K // 2)],
                                 dimension_numbers=('NWC', 'WIO', 'NWC'),
                                 feature_group_count=C)
    return y + b.astype(x.dtype)


def axial_rope(rows, rot_dim):
    quarter = rot_dim // 4
    inv = ROPE_BASE ** (-jnp.arange(quarter, dtype=F32) / quarter)
    r = jnp.repeat(jnp.arange(rows, dtype=F32), GRID_W)
    col = jnp.tile(jnp.arange(GRID_W, dtype=F32), rows)
    ang = jnp.concatenate([r[:, None] * inv, col[:, None] * inv], axis=-1)
    return jnp.cos(ang), jnp.sin(ang)


def apply_rope(x, cos, sin):
    half = x.shape[-1] // 2
    x1, x2 = x[..., :half], x[..., half:]
    c = cos[:, None, :].astype(x.dtype)
    s = sin[:, None, :].astype(x.dtype)
    return jnp.concatenate([x1 * c - x2 * s, x1 * s + x2 * c], axis=-1)


def modulation(cond, w_ada, b_ada):
    m = jax.nn.silu(cond) @ w_ada + b_ada
    return jnp.split(m[:, None, :], 6, axis=-1)


def mlstm_scan(q, k, v, log_i, log_f, C0, n0, m0):
    mask = jnp.tril(jnp.ones((CHUNK_A, CHUNK_A), bool))[None, :, :, None]

    def step(carry, inp):
        C, n, m = carry
        qc, kc, vc, ic, fc = inp
        b = jnp.cumsum(fc, axis=1)
        dmat = jnp.where(mask, b[:, :, None, :] - b[:, None, :, :] + ic[:, None, :, :], -jnp.inf)
        inter = b + m[:, None, :]
        m_t = jnp.maximum(inter, jnp.max(dmat, axis=2))
        s = jnp.einsum('bthd,bshd->btsh', qc, kc) * jnp.exp(dmat - m_t[:, :, None, :])
        w_inter = jnp.exp(inter - m_t)
        num = jnp.einsum('btsh,bshv->bthv', s, vc) + w_inter[..., None] * jnp.einsum('bthd,bhdv->bthv', qc, C)
        den = jnp.sum(s, axis=2) + w_inter * jnp.einsum('bthd,bhd->bth', qc, n)
        h = num / jnp.maximum(jnp.abs(den), jnp.exp(-m_t))[..., None]
        b_last = b[:, -1]
        g = b_last[:, None, :] - b + ic
        m_new = jnp.maximum(b_last + m, jnp.max(g, axis=1))
        wg = jnp.exp(g - m_new[:, None, :])
        keep = jnp.exp(b_last + m - m_new)
        C = keep[..., None, None] * C + jnp.einsum('bsh,bshd,bshv->bhdv', wg, kc, vc)
        n = keep[..., None] * n + jnp.einsum('bsh,bshd->bhd', wg, kc)
        return (C, n, m_new), h

    xs = (_chunks(q, CHUNK_A), _chunks(k, CHUNK_A), _chunks(v, CHUNK_A),
          _chunks(log_i, CHUNK_A), _chunks(log_f, CHUNK_A))
    (C, n, m), hs = lax.scan(step, (C0, n0, m0), xs)
    return _unchunk(hs), C, n, m


def ssd_scan(x, dt, A, Bm, Cm, S0):
    mask = jnp.tril(jnp.ones((CHUNK_B, CHUNK_B), bool))[None, :, :, None, None]

    def step(S, inp):
        xc, dtc, Bc, Cc = inp
        acum = jnp.cumsum(dtc * A, axis=1)
        decay = jnp.exp(jnp.where(mask, acum[:, :, None] - acum[:, None, :], -jnp.inf))
        w = jnp.einsum('btgn,bsgn->btsg', Cc, Bc)[..., None] * decay * dtc[:, None]
        y = (jnp.einsum('btsgr,bsgrp->btgrp', w, xc)
             + jnp.einsum('btgn,bgrpn->btgrp', Cc, S) * jnp.exp(acum)[..., None])
        a_last = acum[:, -1]
        wS = jnp.exp(a_last[:, None] - acum) * dtc
        S = jnp.exp(a_last)[..., None, None] * S + jnp.einsum('bsgr,bsgrp,bsgn->bgrpn', wS, xc, Bc)
        return S, y

    xs = (_chunks(x, CHUNK_B), _chunks(dt, CHUNK_B), _chunks(Bm, CHUNK_B), _chunks(Cm, CHUNK_B))
    S, ys = lax.scan(step, S0, xs)
    return _unchunk(ys), S


def even_mixer(h, w_in, conv_a_w, conv_a_b, conv_b_w, conv_b_b, gate_b, a_norm_w,
               dt_bias, a_log, d_skip, b_norm_w, w_out, C0, n0, m0, S0):
    Bsz, T, _ = h.shape
    qk, v, o, ig, fg, z, xbc, dt = _split(h @ w_in, EVEN_SIZES)
    qk = jax.nn.silu(dwconv(qk, conv_a_w, conv_a_b)).astype(F32)
    q = qk[..., :A_QK].reshape(Bsz, T, NH_A, DK_A)
    k = qk[..., A_QK:].reshape(Bsz, T, NH_A, DK_A) * (DK_A ** -0.5)
    v = v.astype(F32).reshape(Bsz, T, NH_A, DV_A)
    gates = (jnp.concatenate([ig, fg], axis=-1) + gate_b).astype(F32)
    log_i = gates[..., :2 * NH_A]
    log_f = jax.nn.log_sigmoid(gates[..., 2 * NH_A:])
    C0, n0, m0 = C0.astype(F32), n0.astype(F32), m0.astype(F32)
    h_f, Cf, nf, mf = mlstm_scan(q, k, v, log_i[..., :NH_A], log_f[..., :NH_A], C0[:, 0], n0[:, 0], m0[:, 0])
    h_b, Cb, nb_, mb = mlstm_scan(_flip(q), _flip(k), _flip(v), _flip(log_i[..., NH_A:]),
                                  _flip(log_f[..., NH_A:]), C0[:, 1], n0[:, 1], m0[:, 1])
    ha = jax.nn.sigmoid(o.astype(F32)) * (h_f + _flip(h_b)).reshape(Bsz, T, A_V)
    ha = group_rms(ha, NH_A, a_norm_w).astype(h.dtype)
    xbc = jax.nn.silu(dwconv(xbc, conv_b_w, conv_b_b)).astype(F32)
    xs = xbc[..., :B_INNER].reshape(Bsz, T, NG_B, R_B, HP_B)
    Bm = xbc[..., B_INNER:B_INNER + B_BC].reshape(Bsz, T, NG_B, DSTATE)
    Cm = xbc[..., B_INNER + B_BC:].reshape(Bsz, T, NG_B, DSTATE)
    dt = jax.nn.softplus(dt.astype(F32).reshape(Bsz, T, 2, NH_B) + dt_bias.astype(F32))
    dt = dt.reshape(Bsz, T, 2, NG_B, R_B)
    A = -jnp.exp(a_log.astype(F32)).reshape(2, NG_B, R_B)
    S0 = S0.astype(F32).reshape(Bsz, 2, NG_B, R_B, HP_B, DSTATE)
    y_f, Sf = ssd_scan(xs, dt[:, :, 0], A[0], Bm, Cm, S0[:, 0])
    y_b, Sb = ssd_scan(_flip(xs), _flip(dt[:, :, 1]), A[1], _flip(Bm), _flip(Cm), S0[:, 1])
    y = y_f + _flip(y_b) + d_skip.astype(F32).reshape(NG_B, R_B)[..., None] * xs
    y = y.reshape(Bsz, T, B_INNER) * jax.nn.silu(z.astype(F32))
    yb = group_rms(y, NG_B, b_norm_w).astype(h.dtype)
    out = jnp.concatenate([ha, yb], axis=-1) @ w_out
    C_new = jnp.stack([Cf, Cb], axis=1)
    n_new = jnp.stack([nf, nb_], axis=1)
    m_new = jnp.stack([mf, mb], axis=1)
    S_new = jnp.stack([Sf, Sb], axis=1).reshape(Bsz, 2, NH_B, HP_B, DSTATE)
    return out, (C_new, n_new, m_new, S_new)


def softmax_sink(s, sink):
    if sink is None:
        return jax.nn.softmax(s, axis=-1)
    sk = jnp.broadcast_to(sink.astype(F32)[None, :, :, None, None], s.shape[:-1] + (1,))
    return jax.nn.softmax(jnp.concatenate([s, sk], axis=-1), axis=-1)[..., :-1]


def blocked_attention(q, k, v, sink, scale):
    Bsz, Tq = q.shape[:2]
    nb = Tq // QBLK
    qb = jnp.moveaxis(q.reshape((Bsz, nb, QBLK) + q.shape[2:]), 1, 0)

    def one(qi):
        s = jnp.einsum('bqhgd,bkhd->bhgqk', qi, k).astype(F32) * scale
        p = softmax_sink(s, sink)
        return jnp.einsum('bhgqk,bkhd->bqhgd', p.astype(v.dtype), v)

    o = lax.map(one, qb)
    return jnp.moveaxis(o, 0, 1).reshape((Bsz, Tq) + o.shape[3:])


def banded_attention(q, k, v, kc, vc, sink, scale):
    Bsz, T = q.shape[:2]
    nb = T // QBLK
    pad = ((0, 0), (QBLK, QBLK), (0, 0), (0, 0))
    kp = jnp.pad(k, pad).reshape((Bsz, nb + 2, QBLK) + k.shape[2:])
    vp = jnp.pad(v, pad).reshape((Bsz, nb + 2, QBLK) + v.shape[2:])
    kw = jnp.concatenate([kp[:, :-2], kp[:, 1:-1], kp[:, 2:]], axis=2)
    vw = jnp.concatenate([vp[:, :-2], vp[:, 1:-1], vp[:, 2:]], axis=2)
    qpos = jnp.arange(T).reshape(nb, QBLK)
    kpos = (jnp.arange(nb)[:, None] - 1) * QBLK + jnp.arange(3 * QBLK)[None, :]
    valid = ((kpos[:, None, :] >= 0) & (kpos[:, None, :] < T)
             & (jnp.abs(qpos[:, :, None] - kpos[:, None, :]) <= WINDOW))
    qb = q.reshape((Bsz, nb, QBLK) + q.shape[2:])
    n_loc = 3 * QBLK

    def one(args):
        qi, kwi, vwi, vi = args
        s_loc = jnp.einsum('bqhgd,bkhd->bhgqk', qi, kwi).astype(F32) * scale
        s_loc = jnp.where(vi, s_loc, -jnp.inf)
        s_ctx = jnp.einsum('bqhgd,bkhd->bhgqk', qi, kc).astype(F32) * scale
        p = softmax_sink(jnp.concatenate([s_loc, s_ctx], axis=-1), sink).astype(v.dtype)
        return (jnp.einsum('bhgqk,bkhd->bqhgd', p[..., :n_loc], vwi)
                + jnp.einsum('bhgqk,bkhd->bqhgd', p[..., n_loc:], vc))

    o = lax.map(one, (jnp.moveaxis(qb, 1, 0), jnp.moveaxis(kw, 1, 0), jnp.moveaxis(vw, 1, 0), valid))
    return jnp.moveaxis(o, 0, 1).reshape((Bsz, T) + o.shape[3:])


def odd_project(h, w_in, q_a_norm, kv_a_norm, w_q_b):
    Bsz, T, _ = h.shape
    qc, kc, vc, qa, kva, kpe = _split(h @ w_in, ODD_SIZES)
    qc = qc.reshape(Bsz, T, NH_C, HD_C)
    kc = kc.reshape(Bsz, T, NKV_C, HD_C)
    vc = vc.reshape(Bsz, T, NKV_C, HD_C)
    qd = (rmsnorm(qa, q_a_norm) @ w_q_b).reshape(Bsz, T, NH_D, NOPE_D + ROPE_D)
    ckv = rmsnorm(kva, kv_a_norm)
    return qc, kc, vc, qd, ckv, kpe


def mla_keys_values(ckv, kpe, w_kv_b):
    Bsz, S, _ = ckv.shape
    kv = (ckv @ w_kv_b).reshape(Bsz, S, NH_D, NOPE_D + V_D)
    k = jnp.concatenate([kv[..., :NOPE_D],
                         jnp.broadcast_to(kpe[:, :, None, :], (Bsz, S, NH_D, ROPE_D))], axis=-1)
    return k, kv[..., NOPE_D:]


def odd_mixer_ctx(h, w_in, sink, q_a_norm, kv_a_norm, w_q_b, w_kv_b, w_out):
    Bsz, T, _ = h.shape
    qc, kc, vc, qd, ckv, kpe = odd_project(h, w_in, q_a_norm, kv_a_norm, w_q_b)
    oc = blocked_attention(qc.reshape(Bsz, T, NKV_C, G_C, HD_C), kc, vc, sink.reshape(NKV_C, G_C), HD_C ** -0.5)
    k, v = mla_keys_values(ckv, kpe, w_kv_b)
    od = blocked_attention(qd[:, :, :, None], k, v, None, MLA_SCALE)
    y = jnp.concatenate([oc.reshape(Bsz, T, -1), od.reshape(Bsz, T, -1)], axis=-1) @ w_out
    return y, (kc, vc, ckv, kpe)


def odd_mixer_lat(h, k_ctx, v_ctx, ckv_ctx, kpe_ctx, cos_c, sin_c, cos_d, sin_d,
                  w_in, sink, q_a_norm, kv_a_norm, w_q_b, w_kv_b, w_out):
    Bsz, T, _ = h.shape
    qc, kc, vc, qd, ckv, kpe = odd_project(h, w_in, q_a_norm, kv_a_norm, w_q_b)
    qc = apply_rope(qc, cos_c, sin_c)
    kc = apply_rope(kc, cos_c, sin_c)
    oc = banded_attention(qc.reshape(Bsz, T, NKV_C, G_C, HD_C), kc, vc, k_ctx, v_ctx,
                          sink.reshape(NKV_C, G_C), HD_C ** -0.5)
    qd = jnp.concatenate([qd[..., :NOPE_D], apply_rope(qd[..., NOPE_D:], cos_d, sin_d)], axis=-1)
    kpe = apply_rope(kpe[:, :, None, :], cos_d, sin_d)[:, :, 0]
    k, v = mla_keys_values(jnp.concatenate([ckv_ctx, ckv], axis=1),
                           jnp.concatenate([kpe_ctx, kpe], axis=1), w_kv_b)
    od = blocked_attention(qd[:, :, :, None], k, v, None, MLA_SCALE)
    return jnp.concatenate([oc.reshape(Bsz, T, -1), od.reshape(Bsz, T, -1)], axis=-1) @ w_out


def sq_relu_mlp(h, w_up, w_down):
    return jnp.square(jax.nn.relu(h @ w_up)) @ w_down


def setup_inputs(seed: int = 0) -> dict:
    key = jax.random.key(seed)
    ks = iter(jax.random.split(key, 64))
    D = D_MODEL

    def nrm(shape, scale):
        return scale * jax.random.normal(next(ks), shape, F32)

    x_prompt = nrm((BATCH, SEQ, D), 1.0)
    x_sample = nrm((DEC_BATCH, DEC_SEQ, D), 1.0)
    c = nrm((DEC_BATCH, D), 1.0)
    state_mlstm_C = nrm((DEC_BATCH, N_EVEN, 2, NH_A, DK_A, DV_A), 0.1)
    state_mlstm_n = nrm((DEC_BATCH, N_EVEN, 2, NH_A, DK_A), 0.1)
    state_mlstm_m = nrm((DEC_BATCH, N_EVEN, 2, NH_A), 0.5)
    state_ssd = nrm((DEC_BATCH, N_EVEN, 2, NH_B, HP_B, DSTATE), 0.1)
    cache_gqa_k = nrm((DEC_BATCH, N_ODD, PAST_LEN, NKV_C, HD_C), 1.0)
    cache_gqa_v = nrm((DEC_BATCH, N_ODD, PAST_LEN, NKV_C, HD_C), 1.0)
    cache_mla_ckv = nrm((DEC_BATCH, N_ODD, PAST_LEN, KV_RANK), 1.0)
    cache_mla_kpe = nrm((DEC_BATCH, N_ODD, PAST_LEN, ROPE_D), 1.0)
    c_ctx = nrm((D,), 1.0)
    w_ada = nrm((DEPTH, D, 6 * D), 0.5 * D ** -0.5)
    b_ada = nrm((DEPTH, 6 * D), 0.02)
    norm_g = 1.0 + nrm((DEPTH, 4, D), 0.05)
    w_up = nrm((DEPTH, D, D_FF), D ** -0.5)
    w_down = nrm((DEPTH, D_FF, D), D_FF ** -0.5)
    w_in_even = nrm((N_EVEN, D, EVEN_IN), D ** -0.5)
    conv_a_w = nrm((N_EVEN, CONV_K, 2 * A_QK), CONV_K ** -0.5)
    conv_a_b = nrm((N_EVEN, 2 * A_QK), 0.02)
    conv_b_w = nrm((N_EVEN, CONV_K, B_XBC), CONV_K ** -0.5)
    conv_b_b = nrm((N_EVEN, B_XBC), 0.02)
    gate_b = jnp.concatenate([nrm((N_EVEN, 2 * NH_A), 0.1),
                              3.0 + nrm((N_EVEN, 2 * NH_A), 0.5)], axis=-1)
    a_norm_w = 1.0 + nrm((N_EVEN, A_V), 0.05)
    dt0 = jnp.exp(jax.random.uniform(next(ks), (N_EVEN, 2, NH_B), F32, math.log(1e-3), math.log(1e-1)))
    dt_bias = dt0 + jnp.log(-jnp.expm1(-dt0))
    a_log = jnp.log(jax.random.uniform(next(ks), (N_EVEN, 2, NH_B), F32, 1.0, 16.0))
    d_skip = 1.0 + nrm((N_EVEN, NH_B), 0.1)
    b_norm_w = 1.0 + nrm((N_EVEN, B_INNER), 0.05)
    w_out_even = nrm((N_EVEN, EVEN_OUT, D), EVEN_OUT ** -0.5)
    w_in_odd = nrm((N_ODD, D, ODD_IN), D ** -0.5)
    sink = nrm((N_ODD, NH_C), 0.5)
    q_a_norm = 1.0 + nrm((N_ODD, Q_RANK), 0.05)
    kv_a_norm = 1.0 + nrm((N_ODD, KV_RANK), 0.05)
    w_q_b = nrm((N_ODD, Q_RANK, NH_D * (NOPE_D + ROPE_D)), Q_RANK ** -0.5)
    w_kv_b = nrm((N_ODD, KV_RANK, NH_D * (NOPE_D + V_D)), KV_RANK ** -0.5)
    w_out_odd = nrm((N_ODD, ODD_OUT, D), ODD_OUT ** -0.5)
    return {"x_prompt": x_prompt, "x_sample": x_sample, "c": c,
            "state_mlstm_C": state_mlstm_C, "state_mlstm_n": state_mlstm_n,
            "state_mlstm_m": state_mlstm_m, "state_ssd": state_ssd,
            "cache_gqa_k": cache_gqa_k, "cache_gqa_v": cache_gqa_v,
            "cache_mla_ckv": cache_mla_ckv, "cache_mla_kpe": cache_mla_kpe,
            "c_ctx": c_ctx, "w_ada": w_ada, "b_ada": b_ada, "norm_g": norm_g,
            "w_up": w_up, "w_down": w_down, "w_in_even": w_in_even,
            "conv_a_w": conv_a_w, "conv_a_b": conv_a_b, "conv_b_w": conv_b_w, "conv_b_b": conv_b_b,
            "gate_b": gate_b, "a_norm_w": a_norm_w, "dt_bias": dt_bias, "a_log": a_log,
            "d_skip": d_skip, "b_norm_w": b_norm_w, "w_out_even": w_out_even,
            "w_in_odd": w_in_odd, "sink": sink, "q_a_norm": q_a_norm, "kv_a_norm": kv_a_norm,
            "w_q_b": w_q_b, "w_kv_b": w_kv_b, "w_out_odd": w_out_odd}


def reference(x_prompt, x_sample, c, state_mlstm_C, state_mlstm_n, state_mlstm_m, state_ssd,
              cache_gqa_k, cache_gqa_v, cache_mla_ckv, cache_mla_kpe, c_ctx, w_ada, b_ada, norm_g,
              w_up, w_down, w_in_even, conv_a_w, conv_a_b, conv_b_w, conv_b_b, gate_b, a_norm_w,
              dt_bias, a_log, d_skip, b_norm_w, w_out_even, w_in_odd, sink, q_a_norm, kv_a_norm,
              w_q_b, w_kv_b, w_out_odd):
    xp, xs = x_prompt, x_sample
    nbp = xp.shape[0]
    rows = xs.shape[1] // GRID_W
    cos_c, sin_c = axial_rope(rows, HD_C)
    cos_d, sin_d = axial_rope(rows, ROPE_D)
    zC = jnp.zeros((nbp, 2, NH_A, DK_A, DV_A), F32)
    zn = jnp.zeros((nbp, 2, NH_A, DK_A), F32)
    zm = jnp.zeros((nbp, 2, NH_A), F32)
    zS = jnp.zeros((nbp, 2, NH_B, HP_B, DSTATE), F32)
    l_C, l_n, l_m, l_S, l_k, l_v, l_ckv, l_kpe = [], [], [], [], [], [], [], []
    for l in range(DEPTH):
        j = l // 2
        sh1p, sc1p, g1p, sh2p, sc2p, g2p = modulation(c_ctx[None, :], w_ada[l], b_ada[l])
        sh1s, sc1s, g1s, sh2s, sc2s, g2s = modulation(c, w_ada[l], b_ada[l])
        hp = rmsnorm(xp, norm_g[l, 0]) * (1.0 + sc1p) + sh1p
        hs = rmsnorm(xs, norm_g[l, 0]) * (1.0 + sc1s) + sh1s
        if l % 2 == 0:
            ew = (w_in_even[j], conv_a_w[j], conv_a_b[j], conv_b_w[j], conv_b_b[j], gate_b[j],
                  a_norm_w[j], dt_bias[j], a_log[j], d_skip[j], b_norm_w[j], w_out_even[j])
            yp, (Cn, nn_, mn, Sn) = even_mixer(hp, *ew, zC, zn, zm, zS)
            ys, _ = even_mixer(hs, *ew, state_mlstm_C[:, j], state_mlstm_n[:, j],
                               state_mlstm_m[:, j], state_ssd[:, j])
            l_C.append(Cn)
            l_n.append(nn_)
            l_m.append(mn)
            l_S.append(Sn)
        else:
            ow = (w_in_odd[j], sink[j], q_a_norm[j], kv_a_norm[j], w_q_b[j], w_kv_b[j], w_out_odd[j])
            yp, (kn, vn, ckvn, kpen) = odd_mixer_ctx(hp, *ow)
            ys = odd_mixer_lat(hs, cache_gqa_k[:, j], cache_gqa_v[:, j], cache_mla_ckv[:, j],
                               cache_mla_kpe[:, j], cos_c, sin_c, cos_d, sin_d, *ow)
            l_k.append(kn)
            l_v.append(vn)
            l_ckv.append(ckvn)
            l_kpe.append(kpen)
        xp = xp + g1p * rmsnorm(yp, norm_g[l, 1])
        xs = xs + g1s * rmsnorm(ys, norm_g[l, 1])
        hp = rmsnorm(xp, norm_g[l, 2]) * (1.0 + sc2p) + sh2p
        hs = rmsnorm(xs, norm_g[l, 2]) * (1.0 + sc2s) + sh2s
        xp = xp + g2p * rmsnorm(sq_relu_mlp(hp, w_up[l], w_down[l]), norm_g[l, 3])
        xs = xs + g2s * rmsnorm(sq_relu_mlp(hs, w_up[l], w_down[l]), norm_g[l, 3])
    new_mlstm_C = jnp.stack(l_C, axis=1)
    new_mlstm_n = jnp.stack(l_n, axis=1)
    new_mlstm_m = jnp.stack(l_m, axis=1)
    new_ssd = jnp.stack(l_S, axis=1)
    new_gqa_k = jnp.stack(l_k, axis=1)
    new_gqa_v = jnp.stack(l_v, axis=1)
    new_mla_ckv = jnp.stack(l_ckv, axis=1)
    new_mla_kpe = jnp.stack(l_kpe, axis=1)
    return (xp, xs, new_mlstm_C, new_mlstm_n, new_mlstm_m, new_ssd,
            new_gqa_k, new_gqa_v, new_mla_ckv, new_mla_kpe)
```

```python
import functools

import jax
import jax.numpy as jnp
from jax import lax
from jax.experimental import pallas as pl
from jax.experimental.pallas import tpu as pltpu

F32 = jnp.float32
BF16 = jnp.bfloat16

D_MODEL = 1024
BATCH = 32
SEQ = 256
DEPTH = 4
DEC_BATCH = 2
DEC_SEQ = 1024
PAST_LEN = 256
GRID_W = 64
EPS = 1e-6
CONV_K = 5
NH_A = 4
DK_A = 128
DV_A = 128
A_QK = NH_A * DK_A
A_V = NH_A * DV_A
NH_B = 8
HP_B = 64
DSTATE = 128
NG_B = 2
R_B = NH_B // NG_B
B_INNER = NH_B * HP_B
B_BC = NG_B * DSTATE
B_XBC = B_INNER + 2 * B_BC
NH_C = 8
NKV_C = 2
G_C = NH_C // NKV_C
HD_C = 64
WINDOW = 128
NH_D = 8
Q_RANK = 256
KV_RANK = 128
NOPE_D = 64
ROPE_D = 32
V_D = 64
MLA_SCALE = (NOPE_D + ROPE_D) ** -0.5
D_FF = 4 * D_MODEL
ROPE_BASE = 10000.0

LANES = 128
VMEM_LIMIT_BYTES = 56 * 1024 * 1024
ROW_TILE = 512
FF_TILE = 1024
Q_TILE = 256
ADA_TILE = 1536
MOD_ROWS = 8

EVEN_WIDTHS = (2 * A_QK, A_V, A_V, B_INNER, B_XBC, 3 * LANES)
ODD_WIDTHS = (NH_C * HD_C, NKV_C * HD_C, NKV_C * HD_C, Q_RANK, KV_RANK, LANES)

_NT = (((1,), (1,)), ((), ()))


def _params(*sem):
    return pltpu.CompilerParams(dimension_semantics=sem, vmem_limit_bytes=VMEM_LIMIT_BYTES)


def _rms(x, g):
    return x * lax.rsqrt(jnp.mean(x * x, axis=-1, keepdims=True) + EPS) * g


def _silu(x):
    return x * jax.nn.sigmoid(x)


def _softplus(x):
    return jnp.maximum(x, 0.0) + jnp.log1p(jnp.exp(-jnp.abs(x)))


def _dot(a, b):
    return jnp.dot(a, b, preferred_element_type=F32)


def _dot_nt(a, b):
    return lax.dot_general(a, b, _NT, preferred_element_type=F32)


def _ada_kernel(c_ref, w_ref, b_ref, o_ref):
    s = _silu(c_ref[...]).astype(BF16)
    o_ref[...] = _dot(s, w_ref[...].astype(BF16)) + b_ref[...]


def _modulations(cond, w_ada, b_ada):
    out = pl.pallas_call(
        _ada_kernel,
        grid=(DEPTH, 6 * D_MODEL // ADA_TILE),
        in_specs=[pl.BlockSpec((MOD_ROWS, D_MODEL), lambda l, n: (0, 0)),
                  pl.BlockSpec((None, D_MODEL, ADA_TILE), lambda l, n: (l, 0, n)),
                  pl.BlockSpec((None, 1, ADA_TILE), lambda l, n: (l, 0, n))],
        out_specs=pl.BlockSpec((None, MOD_ROWS, ADA_TILE), lambda l, n: (l, 0, n)),
        out_shape=jax.ShapeDtypeStruct((DEPTH, MOD_ROWS, 6 * D_MODEL), F32),
        compiler_params=_params("arbitrary", "arbitrary"),
        name="ada",
    )(cond, w_ada, b_ada.reshape(DEPTH, 1, 6 * D_MODEL))
    return out.reshape(DEPTH, MOD_ROWS, 6, D_MODEL)


def _mod_spec(layer, latent):
    if latent:
        per_seq = DEC_SEQ // ROW_TILE
        return pl.BlockSpec((None, None, 6, D_MODEL), lambda i, *_: (layer, 1 + i // per_seq, 0, 0))
    return pl.BlockSpec((None, None, 6, D_MODEL), lambda i, *_: (layer, 0, 0, 0))


def _proj_kernel(x_ref, mod_ref, g_ref, w_ref, *o_refs, widths):
    h = _rms(x_ref[...], g_ref[...]) * (1.0 + mod_ref[1:2, :]) + mod_ref[0:1, :]
    hb = h.astype(BF16)
    off = 0
    for o_ref, wd in zip(o_refs, widths):
        o_ref[...] = _dot(hb, w_ref[:, off:off + wd])
        off += wd


def _project(x, mods, layer, latent, gain, w, widths):
    n = x.shape[0]
    return pl.pallas_call(
        functools.partial(_proj_kernel, widths=widths),
        grid=(n // ROW_TILE,),
        in_specs=[pl.BlockSpec((ROW_TILE, D_MODEL), lambda i: (i, 0)),
                  _mod_spec(layer, latent),
                  pl.BlockSpec((1, D_MODEL), lambda i: (0, 0)),
                  pl.BlockSpec(w.shape, lambda i: (0, 0))],
        out_specs=[pl.BlockSpec((ROW_TILE, wd), lambda i: (i, 0)) for wd in widths],
        out_shape=[jax.ShapeDtypeStruct((n, wd), F32) for wd in widths],
        compiler_params=_params("arbitrary"),
        name="proj",
    )(x, mods, gain, w)


def _channel_kernel(a1_ref, a2_ref, x_ref, mod_ref, g_ref, wo_ref, wu_ref, wd_ref, o_ref, x1_s, h_s, acc_s):
    j = pl.program_id(1)
    half = a1_ref.shape[1]

    @pl.when(j == 0)
    def _():
        y = _dot(a1_ref[...].astype(BF16), wo_ref[0:half, :]) + _dot(a2_ref[...].astype(BF16), wo_ref[half:, :])
        x1 = x_ref[...] + mod_ref[2:3, :] * _rms(y, g_ref[0:1, :])
        x1_s[...] = x1
        h = _rms(x1, g_ref[1:2, :]) * (1.0 + mod_ref[4:5, :]) + mod_ref[3:4, :]
        h_s[...] = h.astype(BF16)
        acc_s[...] = jnp.zeros_like(acc_s)

    u = jnp.square(jnp.maximum(_dot(h_s[...], wu_ref[...]), 0.0))
    acc_s[...] += _dot(u.astype(BF16), wd_ref[...])

    @pl.when(j == pl.num_programs(1) - 1)
    def _():
        o_ref[...] = x1_s[...] + mod_ref[5:6, :] * _rms(acc_s[...], g_ref[2:3, :])


def _channel(a1, a2, x, mods, layer, latent, gains, w_out, w_up, w_down):
    n = x.shape[0]
    half = a1.shape[1]
    return pl.pallas_call(
        _channel_kernel,
        grid=(n // ROW_TILE, D_FF // FF_TILE),
        in_specs=[pl.BlockSpec((ROW_TILE, half), lambda i, j: (i, 0)),
                  pl.BlockSpec((ROW_TILE, half), lambda i, j: (i, 0)),
                  pl.BlockSpec((ROW_TILE, D_MODEL), lambda i, j: (i, 0)),
                  _mod_spec(layer, latent),
                  pl.BlockSpec((3, D_MODEL), lambda i, j: (0, 0)),
                  pl.BlockSpec((2 * half, D_MODEL), lambda i, j: (0, 0)),
                  pl.BlockSpec((D_MODEL, FF_TILE), lambda i, j: (0, j)),
                  pl.BlockSpec((FF_TILE, D_MODEL), lambda i, j: (j, 0))],
        out_specs=pl.BlockSpec((ROW_TILE, D_MODEL), lambda i, j: (i, 0)),
        out_shape=jax.ShapeDtypeStruct((n, D_MODEL), F32),
        scratch_shapes=[pltpu.VMEM((ROW_TILE, D_MODEL), F32),
                        pltpu.VMEM((ROW_TILE, D_MODEL), BF16),
                        pltpu.VMEM((ROW_TILE, D_MODEL), F32)],
        compiler_params=_params("arbitrary", "arbitrary"),
        name="channel",
    )(a1, a2, x, mods, gains, w_out, w_up, w_down)


def _row_iota(shape):
    return lax.broadcasted_iota(jnp.int32, shape, 0)


def _lane_iota(shape):
    return lax.broadcasted_iota(jnp.int32, shape, 1)


def _cumsum_rows(x):
    t = x.shape[0]
    row = _row_iota(x.shape)
    k = 1
    while k < t:
        x = x + jnp.where(row >= k, pltpu.roll(x, k, 0), 0.0)
        k *= 2
    return x


def _cummax_rows(x, reverse):
    t = x.shape[0]
    row = _row_iota(x.shape)
    k = 1
    while k < t:
        if reverse:
            shifted = jnp.where(row < t - k, pltpu.roll(x, t - k, 0), -jnp.inf)
        else:
            shifted = jnp.where(row >= k, pltpu.roll(x, k, 0), -jnp.inf)
        x = jnp.maximum(x, shifted)
        k *= 2
    return x


def _dwconv_silu(x, w, b):
    t = x.shape[0]
    row = _row_iota(x.shape)
    acc = x * w[CONV_K // 2:CONV_K // 2 + 1, :] + b
    for j in range(CONV_K):
        d = j - CONV_K // 2
        if d == 0:
            continue
        shifted = pltpu.roll(x, (-d) % t, 0)
        valid = (row >= -d) if d < 0 else (row < t - d)
        acc = acc + jnp.where(valid, shifted, 0.0) * w[j:j + 1, :]
    return _silu(acc)


def _causal_exponent(expo, r0, k0, reverse):
    ti = r0 + _row_iota(expo.shape)
    si = k0 + _lane_iota(expo.shape)
    keep = (si >= ti) if reverse else (si <= ti)
    return jnp.where(keep, expo, -jnp.inf)


def _mlstm_kernel(*refs, seq, has_state, emit_state):
    assert not (has_state and emit_state)
    it = iter(refs)
    qk_ref, v_ref, o_ref, g_ref, cw_ref, cb_ref, gb_ref, anw_ref = (next(it) for _ in range(8))
    if has_state:
        c0_ref, m0_ref = next(it), next(it)
    ha_ref = next(it)
    if emit_state:
        cn_ref, nn_ref, mn_ref = next(it), next(it), next(it)

    lane = _lane_iota((seq, LANES))
    fwd = lane < NH_A
    log_i = g_ref[:, 0:LANES] + gb_ref[0:1, :]
    f_pre = g_ref[:, LANES:2 * LANES] + gb_ref[1:2, :]
    log_f = jnp.minimum(f_pre, 0.0) - jnp.log1p(jnp.exp(-jnp.abs(f_pre)))
    pre = _cumsum_rows(log_f)
    total = pre[seq - 1:seq, :]
    b = jnp.where(fwd, pre, total - pre + log_f)
    a = log_i - b
    m_run = jnp.where(fwd, _cummax_rows(a, False), _cummax_rows(a, True))
    if has_state:
        m0 = m0_ref[...]
        m_run = jnp.maximum(m_run, m0)
        w_inter = jnp.exp(m0 - m_run)
    else:
        m_run = jnp.maximum(m_run, 0.0)
    inv_floor = jnp.exp(-(b + m_run))
    a_t = a.T
    fwd_row = fwd[0:1, :]
    b_last = jnp.where(fwd_row, b[seq - 1:seq, :], b[0:1, :])
    m_last = jnp.where(fwd_row, m_run[seq - 1:seq, :], m_run[0:1, :])
    if emit_state:
        mn_ref[...] = b_last + m_last

    ones_col = (lane == 0).astype(F32)
    for h in range(NH_A):
        cq = slice(h * DK_A, (h + 1) * DK_A)
        ck = slice(A_QK + h * DK_A, A_QK + (h + 1) * DK_A)
        cv = slice(h * DV_A, (h + 1) * DV_A)
        q = _dwconv_silu(qk_ref[:, cq], cw_ref[:, cq], cb_ref[:, cq])
        k = _dwconv_silu(qk_ref[:, ck], cw_ref[:, ck], cb_ref[:, ck]) * (DK_A ** -0.5)
        qb = q.astype(BF16)
        kb = k.astype(BF16)
        vh = v_ref[:, cv]
        vaug = jnp.concatenate([vh, ones_col], axis=1).astype(BF16)
        for r0 in range(0, seq, Q_TILE):
            rows = slice(r0, r0 + Q_TILE)
            s = _dot_nt(qb[rows], kb)
            hsum = None
            for d in range(2):
                c = d * NH_A + h
                k0, k1 = (0, r0 + Q_TILE) if d == 0 else (r0, seq)
                expo = _causal_exponent(a_t[c:c + 1, k0:k1] - m_run[rows, c:c + 1], r0, k0, d == 1)
                p = (s[:, k0:k1] * jnp.exp(expo)).astype(BF16)
                acc = _dot(p, vaug[k0:k1])
                if has_state:
                    acc = acc + w_inter[rows, c:c + 1] * _dot(qb[rows], c0_ref[d, h].astype(BF16))
                num = acc[:, 0:DV_A]
                den = acc[:, DV_A:DV_A + 1]
                hd = num / jnp.maximum(jnp.abs(den), inv_floor[rows, c:c + 1])
                hsum = hd if hsum is None else hsum + hd
            og = jax.nn.sigmoid(o_ref[rows, cv]) * hsum
            ha_ref[rows, cv] = _rms(og, anw_ref[:, cv])
        if emit_state:
            for d in range(2):
                c = d * NH_A + h
                kw = k * jnp.exp(a[:, c:c + 1] - m_last[:, c:c + 1])
                cn_ref[d, h] = _dot(kw.T.astype(BF16), vh.astype(BF16))
                nn_ref[d, h:h + 1, :] = jnp.sum(kw, axis=0, keepdims=True)


def _mlstm(qk, v, o, g, conv_w, conv_b, gate_bias, norm_w, seq, state=None):
    n = qk.shape[0]
    nseq = n // seq
    has_state = state is not None
    emit_state = not has_state
    row = lambda s: (s, 0)
    fixed = lambda s: (0, 0)
    in_specs = [pl.BlockSpec((seq, 2 * A_QK), row), pl.BlockSpec((seq, A_V), row), pl.BlockSpec((seq, A_V), row),
                pl.BlockSpec((seq, 3 * LANES), row),
                pl.BlockSpec((CONV_K, 2 * A_QK), fixed), pl.BlockSpec((1, 2 * A_QK), fixed),
                pl.BlockSpec((2, LANES), fixed), pl.BlockSpec((1, A_V), fixed)]
    args = [qk, v, o, g, conv_w, conv_b, gate_bias, norm_w]
    out_specs = [pl.BlockSpec((seq, A_V), row)]
    out_shape = [jax.ShapeDtypeStruct((n, A_V), F32)]
    if has_state:
        c0_aug, m0 = state
        in_specs += [pl.BlockSpec((None, 2, NH_A, DK_A, 2 * LANES), lambda s: (s, 0, 0, 0, 0)),
                     pl.BlockSpec((None, 1, LANES), lambda s: (s, 0, 0))]
        args += [c0_aug, m0]
    if emit_state:
        out_specs += [pl.BlockSpec((None, 2, NH_A, DK_A, DV_A), lambda s: (s, 0, 0, 0, 0)),
                      pl.BlockSpec((None, 2, NH_A, DK_A), lambda s: (s, 0, 0, 0)),
                      pl.BlockSpec((None, 1, LANES), lambda s: (s, 0, 0))]
        out_shape += [jax.ShapeDtypeStruct((nseq, 2, NH_A, DK_A, DV_A), F32),
                      jax.ShapeDtypeStruct((nseq, 2, NH_A, DK_A), F32),
                      jax.ShapeDtypeStruct((nseq, 1, LANES), F32)]
    return pl.pallas_call(
        functools.partial(_mlstm_kernel, seq=seq, has_state=has_state, emit_state=emit_state),
        grid=(nseq,), in_specs=in_specs, out_specs=out_specs, out_shape=out_shape,
        compiler_params=_params("arbitrary"), name="mlstm",
    )(*args)


def _ssd_kernel(*refs, seq, has_state, emit_state):
    assert not (has_state and emit_state)
    it = iter(refs)
    xbc_ref, z_ref, g_ref, cw_ref, cb_ref, dtb_ref, alog_ref, dsk_ref, bnw_ref = (next(it) for _ in range(9))
    if has_state:
        s0_ref = next(it)
    yb_ref = next(it)
    if emit_state:
        sn_ref = next(it)

    lane = _lane_iota((seq, LANES))
    fwd = lane < NH_B
    dt = _softplus(g_ref[:, 2 * LANES:3 * LANES] + dtb_ref[...])
    da = dt * (-jnp.exp(alog_ref[...]))
    pre = _cumsum_rows(da)
    total = pre[seq - 1:seq, :]
    acum = jnp.where(fwd, pre, total - pre + da)
    acum_t = acum.T
    if has_state:
        carry_in = jnp.exp(acum)
    if emit_state:
        a_last = jnp.where(fwd[0:1, :], acum[seq - 1:seq, :], acum[0:1, :])
        w_state = jnp.exp(a_last - acum) * dt

    gw = R_B * HP_B
    for g in range(NG_B):
        cx = slice(g * gw, (g + 1) * gw)
        cb_ = slice(B_INNER + g * DSTATE, B_INNER + (g + 1) * DSTATE)
        cc = slice(B_INNER + B_BC + g * DSTATE, B_INNER + B_BC + (g + 1) * DSTATE)
        xg = _dwconv_silu(xbc_ref[:, cx], cw_ref[:, cx], cb_ref[:, cx])
        bg = _dwconv_silu(xbc_ref[:, cb_], cw_ref[:, cb_], cb_ref[:, cb_]).astype(BF16)
        cg = _dwconv_silu(xbc_ref[:, cc], cw_ref[:, cc], cb_ref[:, cc]).astype(BF16)
        u = [[(xg[:, r * HP_B:(r + 1) * HP_B] * dt[:, d * NH_B + g * R_B + r:d * NH_B + g * R_B + r + 1]).astype(BF16)
              for r in range(R_B)] for d in range(2)]
        for r0 in range(0, seq, Q_TILE):
            rows = slice(r0, r0 + Q_TILE)
            cb_scores = _dot_nt(cg[rows], bg)
            ys = []
            for r in range(R_B):
                h = g * R_B + r
                yh = None
                for d in range(2):
                    c = d * NH_B + h
                    k0, k1 = (0, r0 + Q_TILE) if d == 0 else (r0, seq)
                    expo = _causal_exponent(acum[rows, c:c + 1] - acum_t[c:c + 1, k0:k1], r0, k0, d == 1)
                    w = (cb_scores[:, k0:k1] * jnp.exp(expo)).astype(BF16)
                    yd = _dot(w, u[d][r][k0:k1])
                    if has_state:
                        yd = yd + carry_in[rows, c:c + 1] * _dot_nt(cg[rows], s0_ref[d, h].astype(BF16))
                    yh = yd if yh is None else yh + yd
                ys.append(yh)
            y = jnp.concatenate(ys, axis=1) + dsk_ref[:, cx] * xg[rows]
            y = y * _silu(z_ref[rows, cx])
            yb_ref[rows, cx] = _rms(y, bnw_ref[:, cx])
        if emit_state:
            for d in range(2):
                xw = jnp.concatenate(
                    [xg[:, r * HP_B:(r + 1) * HP_B] * w_state[:, d * NH_B + g * R_B + r:d * NH_B + g * R_B + r + 1]
                     for r in range(R_B)], axis=1)
                sn = _dot(xw.T.astype(BF16), bg)
                for r in range(R_B):
                    sn_ref[d, g * R_B + r] = sn[r * HP_B:(r + 1) * HP_B, :]


def _ssd(xbc, z, g, conv_w, conv_b, dt_bias, a_log, d_skip, norm_w, seq, state=None):
    n = xbc.shape[0]
    nseq = n // seq
    has_state = state is not None
    emit_state = not has_state
    row = lambda s: (s, 0)
    fixed = lambda s: (0, 0)
    in_specs = [pl.BlockSpec((seq, B_XBC), row), pl.BlockSpec((seq, B_INNER), row), pl.BlockSpec((seq, 3 * LANES), row),
                pl.BlockSpec((CONV_K, B_XBC), fixed), pl.BlockSpec((1, B_XBC), fixed),
                pl.BlockSpec((1, LANES), fixed), pl.BlockSpec((1, LANES), fixed),
                pl.BlockSpec((1, B_INNER), fixed), pl.BlockSpec((1, B_INNER), fixed)]
    args = [xbc, z, g, conv_w, conv_b, dt_bias, a_log, d_skip, norm_w]
    out_specs = [pl.BlockSpec((seq, B_INNER), row)]
    out_shape = [jax.ShapeDtypeStruct((n, B_INNER), F32)]
    state_spec = pl.BlockSpec((None, 2, NH_B, HP_B, DSTATE), lambda s: (s, 0, 0, 0, 0))
    if has_state:
        in_specs.append(state_spec)
        args.append(state)
    if emit_state:
        out_specs.append(state_spec)
        out_shape.append(jax.ShapeDtypeStruct((nseq, 2, NH_B, HP_B, DSTATE), F32))
    return pl.pallas_call(
        functools.partial(_ssd_kernel, seq=seq, has_state=has_state, emit_state=emit_state),
        grid=(nseq,), in_specs=in_specs, out_specs=out_specs, out_shape=out_shape,
        compiler_params=_params("arbitrary"), name="ssd",
    )(*args)


def _softmax_pv(s, v, sink=None):
    m = jnp.max(s, axis=1, keepdims=True)
    if sink is not None:
        m = jnp.maximum(m, sink)
    p = jnp.exp(s - m)
    l = jnp.sum(p, axis=1, keepdims=True)
    if sink is not None:
        l = l + jnp.exp(sink - m)
    return _dot(p.astype(BF16), v) / l


def _sink_column(sink_ref, kh, rows):
    return jnp.concatenate([jnp.broadcast_to(sink_ref[0:1, kh * G_C + g:kh * G_C + g + 1], (rows, 1))
                            for g in range(G_C)], axis=0)


def _stack_heads(x, kh, rows):
    return jnp.concatenate([x[rows, (kh * G_C + g) * HD_C:(kh * G_C + g + 1) * HD_C] for g in range(G_C)], axis=0)


def _mla_queries_keys(qa_ref, kva_ref, qan_ref, kvn_ref, wqb_ref):
    qd = _dot(_rms(qa_ref[...], qan_ref[...]).astype(BF16), wqb_ref[...])
    ckv = _rms(kva_ref[...], kvn_ref[...])
    return qd, ckv


def _attn_ctx_kernel(qc_ref, kc_ref, vc_ref, qa_ref, kva_ref, kpe_ref, sink_ref, qan_ref, kvn_ref, wqb_ref, wkvb_ref,
                     oc_ref, od_ref, ckv_ref, *, seq):
    full = slice(0, seq)
    for kh in range(NKV_C):
        ch = slice(kh * HD_C, (kh + 1) * HD_C)
        q4 = _stack_heads(qc_ref, kh, full).astype(BF16)
        s = _dot_nt(q4, kc_ref[:, ch].astype(BF16)) * (HD_C ** -0.5)
        o4 = _softmax_pv(s, vc_ref[:, ch].astype(BF16), _sink_column(sink_ref, kh, seq))
        for g in range(G_C):
            n = kh * G_C + g
            oc_ref[:, n * HD_C:(n + 1) * HD_C] = o4[g * seq:(g + 1) * seq, :]

    qd, ckv = _mla_queries_keys(qa_ref, kva_ref, qan_ref, kvn_ref, wqb_ref)
    ckv_ref[...] = ckv
    kv = _dot(ckv.astype(BF16), wkvb_ref[...]).astype(BF16)
    qdb = qd.astype(BF16)
    kpe = kpe_ref[:, 0:ROPE_D].astype(BF16)
    nope_w = NH_D * NOPE_D
    for n in range(NH_D):
        s = (_dot_nt(qdb[:, n * NOPE_D:(n + 1) * NOPE_D], kv[:, n * NOPE_D:(n + 1) * NOPE_D])
             + _dot_nt(qdb[:, nope_w + n * ROPE_D:nope_w + (n + 1) * ROPE_D], kpe)) * MLA_SCALE
        od_ref[:, n * V_D:(n + 1) * V_D] = _softmax_pv(s, kv[:, nope_w + n * V_D:nope_w + (n + 1) * V_D])


def _attn_ctx(qc, kc, vc, qa, kva, kpe, sink, qan, kvn, wqb, wkvb, seq):
    n = qc.shape[0]
    row = lambda s: (s, 0)
    fixed = lambda s: (0, 0)
    widths = ODD_WIDTHS
    in_specs = [pl.BlockSpec((seq, wd), row) for wd in widths]
    in_specs += [pl.BlockSpec(a.shape, fixed) for a in (sink, qan, kvn, wqb, wkvb)]
    half = NH_C * HD_C
    return pl.pallas_call(
        functools.partial(_attn_ctx_kernel, seq=seq),
        grid=(n // seq,), in_specs=in_specs,
        out_specs=[pl.BlockSpec((seq, half), row), pl.BlockSpec((seq, half), row), pl.BlockSpec((seq, KV_RANK), row)],
        out_shape=[jax.ShapeDtypeStruct((n, half), F32), jax.ShapeDtypeStruct((n, half), F32),
                   jax.ShapeDtypeStruct((n, KV_RANK), F32)],
        compiler_params=_params("arbitrary"), name="attn_ctx",
    )(qc, kc, vc, qa, kva, kpe, sink, qan, kvn, wqb, wkvb)


def _rope(x, cos, sin, half):
    parts = []
    lane = _lane_iota((x.shape[0], LANES))
    first = (lane & (2 * half - 1)) < half
    for i in range(x.shape[1] // LANES):
        xi = x[:, i * LANES:(i + 1) * LANES]
        partner = jnp.where(first, -pltpu.roll(xi, LANES - half, 1), pltpu.roll(xi, half, 1))
        parts.append(xi * cos + partner * sin)
    return parts[0] if len(parts) == 1 else jnp.concatenate(parts, axis=1)


def _attn_lat_kernel(qc_ref, qa_ref, ropeq_ref, kc_ref, vc_ref, kva_ref, kpe_ref, kctx_ref, vctx_ref, ckvctx_ref,
                     kpectx_ref, rope_ref, sink_ref, qan_ref, kvn_ref, wqb_ref, wkvb_ref, oc_ref, od_ref,
                     kwin_s, vwin_s, kv_s, kpe_s, *, seq, past):
    qi = pl.program_id(1)
    nope_w = NH_D * NOPE_D

    @pl.when(qi == 0)
    def _():
        zeros = jnp.zeros((WINDOW, NKV_C * HD_C), BF16)
        for ref, lat, ctx in ((kwin_s, _rope(kc_ref[...], rope_ref[0], rope_ref[1], HD_C // 2), kctx_ref[...]),
                              (vwin_s, vc_ref[...], vctx_ref[...])):
            ref[0:WINDOW, :] = zeros
            ref[WINDOW:WINDOW + seq, :] = lat.astype(BF16)
            ref[WINDOW + seq:2 * WINDOW + seq, :] = zeros
            ref[2 * WINDOW + seq:, :] = ctx.astype(BF16)
        ckv = _rms(kva_ref[...], kvn_ref[...])
        kv_s[0:past, :] = _dot(ckvctx_ref[...].astype(BF16), wkvb_ref[...]).astype(BF16)
        kv_s[past:, :] = _dot(ckv.astype(BF16), wkvb_ref[...]).astype(BF16)
        kpe_s[0:past, :] = kpectx_ref[...].astype(BF16)
        kpe_s[past:, :] = _rope(kpe_ref[...], rope_ref[2], rope_ref[3], ROPE_D // 2).astype(BF16)

    r0 = pl.multiple_of(qi * Q_TILE, Q_TILE)
    nloc = Q_TILE + 2 * WINDOW
    qr = _rope(qc_ref[...], ropeq_ref[0], ropeq_ref[1], HD_C // 2)
    full = slice(0, Q_TILE)
    for kh in range(NKV_C):
        ch = slice(kh * HD_C, (kh + 1) * HD_C)
        q4 = _stack_heads(qr, kh, full).astype(BF16)
        keys = jnp.concatenate([kwin_s[pl.ds(r0, nloc), ch], kwin_s[2 * WINDOW + seq:, ch]], axis=0)
        vals = jnp.concatenate([vwin_s[pl.ds(r0, nloc), ch], vwin_s[2 * WINDOW + seq:, ch]], axis=0)
        s = _dot_nt(q4, keys) * (HD_C ** -0.5)
        ti = r0 + (_row_iota(s.shape) & (Q_TILE - 1))
        col = _lane_iota(s.shape)
        pos = r0 - WINDOW + col
        valid = (col >= nloc) | ((jnp.abs(ti - pos) <= WINDOW) & (pos >= 0) & (pos < seq))
        s = jnp.where(valid, s, -jnp.inf)
        o4 = _softmax_pv(s, vals, _sink_column(sink_ref, kh, Q_TILE))
        for g in range(G_C):
            n = kh * G_C + g
            oc_ref[:, n * HD_C:(n + 1) * HD_C] = o4[g * Q_TILE:(g + 1) * Q_TILE, :]

    qd = _dot(_rms(qa_ref[...], qan_ref[...]).astype(BF16), wqb_ref[...])
    q_nope = qd[:, 0:nope_w].astype(BF16)
    q_pe = _rope(qd[:, nope_w:], ropeq_ref[2], ropeq_ref[3], ROPE_D // 2).astype(BF16)
    for n in range(NH_D):
        s = (_dot_nt(q_nope[:, n * NOPE_D:(n + 1) * NOPE_D], kv_s[:, n * NOPE_D:(n + 1) * NOPE_D])
             + _dot_nt(q_pe[:, n * ROPE_D:(n + 1) * ROPE_D], kpe_s[:, 0:ROPE_D])) * MLA_SCALE
        od_ref[:, n * V_D:(n + 1) * V_D] = _softmax_pv(s, kv_s[:, nope_w + n * V_D:nope_w + (n + 1) * V_D])


def _attn_lat(qc, kc, vc, qa, kva, kpe, kctx, vctx, ckvctx, kpectx, rope, sink, qan, kvn, wqb, wkvb, seq):
    n = qc.shape[0]
    past = kctx.shape[1]
    nq = seq // Q_TILE
    qrow = lambda b, q: (b * nq + q, 0)
    krow = lambda b, q: (b, 0)
    fixed = lambda b, q: (0, 0)
    kvw = NKV_C * HD_C
    in_specs = [pl.BlockSpec((Q_TILE, NH_C * HD_C), qrow), pl.BlockSpec((Q_TILE, Q_RANK), qrow),
                pl.BlockSpec((4, Q_TILE, LANES), lambda b, q: (0, q, 0)),
                pl.BlockSpec((seq, kvw), krow), pl.BlockSpec((seq, kvw), krow),
                pl.BlockSpec((seq, KV_RANK), krow), pl.BlockSpec((seq, LANES), krow)]
    in_specs += [pl.BlockSpec((None, past, a.shape[2]), lambda b, q: (b, 0, 0)) for a in (kctx, vctx, ckvctx, kpectx)]
    in_specs += [pl.BlockSpec(rope.shape, lambda b, q: (0, 0, 0))]
    in_specs += [pl.BlockSpec(a.shape, fixed) for a in (sink, qan, kvn, wqb, wkvb)]
    half = NH_C * HD_C
    return pl.pallas_call(
        functools.partial(_attn_lat_kernel, seq=seq, past=past),
        grid=(n // seq, nq), in_specs=in_specs,
        out_specs=[pl.BlockSpec((Q_TILE, half), qrow), pl.BlockSpec((Q_TILE, half), qrow)],
        out_shape=[jax.ShapeDtypeStruct((n, half), F32), jax.ShapeDtypeStruct((n, half), F32)],
        scratch_shapes=[pltpu.VMEM((2 * WINDOW + seq + past, kvw), BF16), pltpu.VMEM((2 * WINDOW + seq + past, kvw), BF16),
                        pltpu.VMEM((past + seq, NH_D * (NOPE_D + V_D)), BF16), pltpu.VMEM((past + seq, LANES), BF16)],
        compiler_params=_params("arbitrary", "arbitrary"), name="attn_lat",
    )(qc, qa, rope, kc, vc, kva, kpe, kctx, vctx, ckvctx, kpectx, rope, sink, qan, kvn, wqb, wkvb)


def _pad_lanes(x, width=LANES):
    return jnp.pad(x, [(0, 0)] * (x.ndim - 1) + [(0, width - x.shape[-1])])


def _even_in_weight(w):
    o0 = 2 * A_QK + 2 * A_V
    ig, fg = w[:, o0:o0 + 2 * NH_A], w[:, o0 + 2 * NH_A:o0 + 4 * NH_A]
    z0 = o0 + 4 * NH_A
    x0 = z0 + B_INNER
    d0 = x0 + B_XBC
    return jnp.concatenate([w[:, :o0], w[:, z0:x0], w[:, x0:d0], _pad_lanes(ig), _pad_lanes(fg),
                            _pad_lanes(w[:, d0:])], axis=1).astype(BF16)


def _odd_in_weight(w):
    return _pad_lanes(w, sum(ODD_WIDTHS)).astype(BF16)


def _split_heads_weight(w, heads, first, second):
    w3 = w.reshape(w.shape[0], heads, first + second)
    return jnp.concatenate([w3[:, :, :first].reshape(w.shape[0], heads * first),
                            w3[:, :, first:].reshape(w.shape[0], heads * second)], axis=1).astype(BF16)


def _rope_tables(rows):
    def table(rot_dim):
        quarter = rot_dim // 4
        inv = ROPE_BASE ** (-jnp.arange(quarter, dtype=F32) / quarter)
        r = jnp.repeat(jnp.arange(rows, dtype=F32), GRID_W)
        col = jnp.tile(jnp.arange(GRID_W, dtype=F32), rows)
        ang = jnp.concatenate([r[:, None] * inv, col[:, None] * inv], axis=-1)
        reps = LANES // (rot_dim // 2)
        return jnp.tile(jnp.cos(ang), (1, reps)), jnp.tile(jnp.sin(ang), (1, reps))
    cos_c, sin_c = table(HD_C)
    cos_d, sin_d = table(ROPE_D)
    return jnp.stack([cos_c, sin_c, cos_d, sin_d])


def kernel(x_prompt, x_sample, c, state_mlstm_C, state_mlstm_n, state_mlstm_m, state_ssd, cache_gqa_k, cache_gqa_v,
           cache_mla_ckv, cache_mla_kpe, c_ctx, w_ada, b_ada, norm_g, w_up, w_down, w_in_even, conv_a_w, conv_a_b,
           conv_b_w, conv_b_b, gate_b, a_norm_w, dt_bias, a_log, d_skip, b_norm_w, w_out_even, w_in_odd, sink,
           q_a_norm, kv_a_norm, w_q_b, w_kv_b, w_out_odd):
    xp = x_prompt.reshape(BATCH * SEQ, D_MODEL)
    xs = x_sample.reshape(DEC_BATCH * DEC_SEQ, D_MODEL)
    cond = jnp.concatenate([c_ctx[None, :], c, jnp.zeros((MOD_ROWS - 1 - DEC_BATCH, D_MODEL), F32)], axis=0)
    mods = _modulations(cond, w_ada, b_ada)
    rope = _rope_tables(DEC_SEQ // GRID_W)

    new_c, new_n, new_m, new_s, new_k, new_v, new_ckv, new_kpe = [], [], [], [], [], [], [], []
    for l in range(DEPTH):
        j = l // 2
        gain_in = norm_g[l, 0].reshape(1, D_MODEL)
        if l % 2 == 0:
            w_in = _even_in_weight(w_in_even[j])
            gate_bias = jnp.stack([_pad_lanes(gate_b[j, :2 * NH_A]), _pad_lanes(gate_b[j, 2 * NH_A:])])
            a_args = (conv_a_w[j], conv_a_b[j].reshape(1, -1), gate_bias, a_norm_w[j].reshape(1, -1))
            b_args = (conv_b_w[j], conv_b_b[j].reshape(1, -1), _pad_lanes(dt_bias[j].reshape(1, -1)),
                      _pad_lanes(a_log[j].reshape(1, -1)), jnp.repeat(d_skip[j], HP_B).reshape(1, -1),
                      b_norm_w[j].reshape(1, -1))
            qk, v, o, z, xbc, g = _project(xp, mods, l, False, gain_in, w_in, EVEN_WIDTHS)
            a1p, cn, nn, mn = _mlstm(qk, v, o, g, *a_args, seq=SEQ)
            a2p, sn = _ssd(xbc, z, g, *b_args, seq=SEQ)
            new_c.append(cn)
            new_n.append(nn)
            new_m.append(mn[:, 0, :2 * NH_A].reshape(BATCH, 2, NH_A))
            new_s.append(sn)
            qk, v, o, z, xbc, g = _project(xs, mods, l, True, gain_in, w_in, EVEN_WIDTHS)
            c0_aug = jnp.concatenate([state_mlstm_C[:, j], _pad_lanes(state_mlstm_n[:, j][..., None])], axis=-1)
            m0 = _pad_lanes(state_mlstm_m[:, j].reshape(DEC_BATCH, 1, 2 * NH_A))
            a1s, = _mlstm(qk, v, o, g, *a_args, seq=DEC_SEQ, state=(c0_aug, m0))
            a2s, = _ssd(xbc, z, g, *b_args, seq=DEC_SEQ, state=state_ssd[:, j])
            w_out = w_out_even[j].astype(BF16)
        else:
            w_in = _odd_in_weight(w_in_odd[j])
            o_args = (_pad_lanes(sink[j].reshape(1, -1)), q_a_norm[j].reshape(1, -1), kv_a_norm[j].reshape(1, -1),
                      _split_heads_weight(w_q_b[j], NH_D, NOPE_D, ROPE_D),
                      _split_heads_weight(w_kv_b[j], NH_D, NOPE_D, V_D))
            qc, kc, vc, qa, kva, kpe = _project(xp, mods, l, False, gain_in, w_in, ODD_WIDTHS)
            a1p, a2p, ckv = _attn_ctx(qc, kc, vc, qa, kva, kpe, *o_args, seq=SEQ)
            new_k.append(kc.reshape(BATCH, SEQ, NKV_C, HD_C))
            new_v.append(vc.reshape(BATCH, SEQ, NKV_C, HD_C))
            new_ckv.append(ckv.reshape(BATCH, SEQ, KV_RANK))
            new_kpe.append(kpe[:, :ROPE_D].reshape(BATCH, SEQ, ROPE_D))
            qc, kc, vc, qa, kva, kpe = _project(xs, mods, l, True, gain_in, w_in, ODD_WIDTHS)
            a1s, a2s = _attn_lat(qc, kc, vc, qa, kva, kpe,
                                 cache_gqa_k[:, j].reshape(DEC_BATCH, PAST_LEN, NKV_C * HD_C),
                                 cache_gqa_v[:, j].reshape(DEC_BATCH, PAST_LEN, NKV_C * HD_C),
                                 cache_mla_ckv[:, j], _pad_lanes(cache_mla_kpe[:, j]), rope, *o_args, seq=DEC_SEQ)
            w_out = w_out_odd[j].astype(BF16)
        gains = norm_g[l, 1:4]
        wu, wd = w_up[l].astype(BF16), w_down[l].astype(BF16)
        xp = _channel(a1p, a2p, xp, mods, l, False, gains, w_out, wu, wd)
        xs = _channel(a1s, a2s, xs, mods, l, True, gains, w_out, wu, wd)

    return (xp.reshape(BATCH, SEQ, D_MODEL), xs.reshape(DEC_BATCH, DEC_SEQ, D_MODEL),
            jnp.stack(new_c, axis=1), jnp.stack(new_n, axis=1), jnp.stack(new_m, axis=1), jnp.stack(new_s, axis=1),
            jnp.stack(new_k, axis=1), jnp.stack(new_v, axis=1), jnp.stack(new_ckv, axis=1),
            jnp.stack(new_kpe, axis=1))
```

```python
import functools

import jax
import jax.numpy as jnp
from jax import lax
from jax.experimental import pallas as pl
from jax.experimental.pallas import tpu as pltpu

F32 = jnp.float32
BF16 = jnp.bfloat16

D_MODEL = 1024
BATCH = 32
SEQ = 256
DEPTH = 4
DEC_BATCH = 2
DEC_SEQ = 1024
PAST_LEN = 256
GRID_W = 64
N_EVEN = (DEPTH + 1) // 2
N_ODD = DEPTH // 2
EPS = 1e-6
CONV_K = 5
NH_A = 4
DK_A = 128
DV_A = 128
A_QK = NH_A * DK_A
A_V = NH_A * DV_A
NH_B = 8
HP_B = 64
DSTATE = 128
NG_B = 2
R_B = NH_B // NG_B
B_INNER = NH_B * HP_B
B_BC = NG_B * DSTATE
B_XBC = B_INNER + 2 * B_BC
NH_C = 8
NKV_C = 2
G_C = NH_C // NKV_C
HD_C = 64
WINDOW = 128
NH_D = 8
Q_RANK = 256
KV_RANK = 128
NOPE_D = 64
ROPE_D = 32
V_D = 64
MLA_SCALE = (NOPE_D + ROPE_D) ** -0.5
D_FF = 4 * D_MODEL
ROPE_BASE = 10000.0

LANES = 128
VMEM_LIMIT_BYTES = 56 * 1024 * 1024
ROW_TILE = 512
FF_TILE = 1024
SUB_ROWS = 256
STAGE_ROWS = 512
Q_TILE = 256
ADA_TILE = 1536
MOD_ROWS = 8

EVEN_WIDTHS = (2 * A_QK, A_V, A_V, B_INNER, B_XBC, 3 * LANES)
ODD_WIDTHS = (NH_C * HD_C, NKV_C * HD_C, NKV_C * HD_C, Q_RANK, KV_RANK, LANES)

_NT = (((1,), (1,)), ((), ()))


def _params(*sem):
    return pltpu.CompilerParams(dimension_semantics=sem, vmem_limit_bytes=VMEM_LIMIT_BYTES)


def _rms(x, g):
    return x * lax.rsqrt(jnp.mean(x * x, axis=-1, keepdims=True) + EPS) * g


def _silu(x):
    return x * jax.nn.sigmoid(x)


def _softplus(x):
    return jnp.maximum(x, 0.0) + jnp.log1p(jnp.exp(-jnp.abs(x)))


def _dot(a, b):
    return jnp.dot(a, b, preferred_element_type=F32)


def _dot_nt(a, b):
    return lax.dot_general(a, b, _NT, preferred_element_type=F32)


def _ada_kernel(c_ref, w_ref, b_ref, o_ref):
    s = _silu(c_ref[...]).astype(BF16)
    o_ref[...] = _dot(s, w_ref[...].astype(BF16)) + b_ref[...]


def _modulations(cond, w_ada, b_ada):
    out = pl.pallas_call(
        _ada_kernel,
        grid=(DEPTH, 6 * D_MODEL // ADA_TILE),
        in_specs=[pl.BlockSpec((MOD_ROWS, D_MODEL), lambda l, n: (0, 0)),
                  pl.BlockSpec((None, D_MODEL, ADA_TILE), lambda l, n: (l, 0, n)),
                  pl.BlockSpec((None, 1, ADA_TILE), lambda l, n: (l, 0, n))],
        out_specs=pl.BlockSpec((None, MOD_ROWS, ADA_TILE), lambda l, n: (l, 0, n)),
        out_shape=jax.ShapeDtypeStruct((DEPTH, MOD_ROWS, 6 * D_MODEL), F32),
        compiler_params=_params("arbitrary", "arbitrary"),
        name="ada",
    )(cond, w_ada, b_ada.reshape(DEPTH, 1, 6 * D_MODEL))
    return out.reshape(DEPTH, MOD_ROWS, 6, D_MODEL)


def _mod_spec(layer, latent):
    if latent:
        per_seq = DEC_SEQ // ROW_TILE
        return pl.BlockSpec((None, None, 6, D_MODEL), lambda i, *_: (layer, 1 + i // per_seq, 0, 0))
    return pl.BlockSpec((None, None, 6, D_MODEL), lambda i, *_: (layer, 0, 0, 0))


def _proj_kernel(x_ref, mod_ref, g_ref, w_ref, *o_refs, widths):
    h = _rms(x_ref[...], g_ref[...]) * (1.0 + mod_ref[1:2, :]) + mod_ref[0:1, :]
    hb = h.astype(BF16)
    off = 0
    for o_ref, wd in zip(o_refs, widths):
        o_ref[...] = _dot(hb, w_ref[:, off:off + wd])
        off += wd


def _project(x, mods, layer, latent, gain, w, widths):
    n = x.shape[0]
    return pl.pallas_call(
        functools.partial(_proj_kernel, widths=widths),
        grid=(n // ROW_TILE,),
        in_specs=[pl.BlockSpec((ROW_TILE, D_MODEL), lambda i: (i, 0)),
                  _mod_spec(layer, latent),
                  pl.BlockSpec((1, D_MODEL), lambda i: (0, 0)),
                  pl.BlockSpec(w.shape, lambda i: (0, 0))],
        out_specs=[pl.BlockSpec((ROW_TILE, wd), lambda i: (i, 0)) for wd in widths],
        out_shape=[jax.ShapeDtypeStruct((n, wd), F32) for wd in widths],
        compiler_params=_params("arbitrary"),
        name="proj",
    )(x, mods, gain, w)


def _weight_chunks(layer, sub_layer, wo_hbm, wu_hbm, wd_hbm, wo_s, wu_s, wd_s):
    chunks = []
    for r in range(0, D_MODEL, STAGE_ROWS):
        chunks.append((wo_hbm.at[sub_layer, pl.ds(r, STAGE_ROWS), :], wo_s.at[pl.ds(r, STAGE_ROWS), :]))
    for r in range(0, D_MODEL, STAGE_ROWS):
        for c in range(0, D_FF, D_MODEL):
            chunks.append((wu_hbm.at[layer, pl.ds(r, STAGE_ROWS), pl.ds(c, D_MODEL)],
                           wu_s.at[pl.ds(r, STAGE_ROWS), pl.ds(c, D_MODEL)]))
    for r in range(0, D_FF, STAGE_ROWS):
        chunks.append((wd_hbm.at[layer, pl.ds(r, STAGE_ROWS), :], wd_s.at[pl.ds(r, STAGE_ROWS), :]))
    return chunks


def _channel_kernel(a1_ref, a2_ref, x_ref, mod_ref, g_ref, wo_hbm, wu_hbm, wd_hbm, o_ref,
                    wo_s, wu_s, wd_s, stage, sem, *, layer, sub_layer):
    half = a1_ref.shape[1]

    @pl.when(pl.program_id(0) == 0)
    def _():
        chunks = _weight_chunks(layer, sub_layer, wo_hbm, wu_hbm, wd_hbm, wo_s, wu_s, wd_s)
        copies = [pltpu.make_async_copy(src, stage.at[k % 2], sem.at[k % 2]) for k, (src, _) in enumerate(chunks)]
        copies[0].start()
        for k, (_, dst) in enumerate(chunks):
            if k + 1 < len(chunks):
                copies[k + 1].start()
            copies[k].wait()
            dst[...] = stage[k % 2].astype(BF16)

    for r0 in range(0, ROW_TILE, SUB_ROWS):
        rows = slice(r0, r0 + SUB_ROWS)
        y = (_dot(a1_ref[rows, :].astype(BF16), wo_s[0:half, :])
             + _dot(a2_ref[rows, :].astype(BF16), wo_s[half:, :]))
        x1 = x_ref[rows, :] + mod_ref[2:3, :] * _rms(y, g_ref[0:1, :])
        h = (_rms(x1, g_ref[1:2, :]) * (1.0 + mod_ref[4:5, :]) + mod_ref[3:4, :]).astype(BF16)
        acc = None
        for c in range(0, D_FF, FF_TILE):
            u = jnp.square(jnp.maximum(_dot(h, wu_s[:, c:c + FF_TILE]), 0.0)).astype(BF16)
            part = _dot(u, wd_s[c:c + FF_TILE, :])
            acc = part if acc is None else acc + part
        o_ref[rows, :] = x1 + mod_ref[5:6, :] * _rms(acc, g_ref[2:3, :])


def _channel(a1, a2, x, mods, layer, latent, gains, w_out, w_up, w_down):
    n = x.shape[0]
    half = a1.shape[1]
    row = lambda i: (i, 0)
    hbm = pl.BlockSpec(memory_space=pl.ANY)
    return pl.pallas_call(
        functools.partial(_channel_kernel, layer=layer, sub_layer=layer // 2),
        grid=(n // ROW_TILE,),
        in_specs=[pl.BlockSpec((ROW_TILE, half), row), pl.BlockSpec((ROW_TILE, half), row),
                  pl.BlockSpec((ROW_TILE, D_MODEL), row), _mod_spec(layer, latent),
                  pl.BlockSpec((3, D_MODEL), lambda i: (0, 0)), hbm, hbm, hbm],
        out_specs=pl.BlockSpec((ROW_TILE, D_MODEL), row),
        out_shape=jax.ShapeDtypeStruct((n, D_MODEL), F32),
        scratch_shapes=[pltpu.VMEM((D_MODEL, D_MODEL), BF16), pltpu.VMEM((D_MODEL, D_FF), BF16),
                        pltpu.VMEM((D_FF, D_MODEL), BF16), pltpu.VMEM((2, STAGE_ROWS, D_MODEL), F32),
                        pltpu.SemaphoreType.DMA((2,))],
        compiler_params=_params("arbitrary"),
        name="channel",
    )(a1, a2, x, mods, gains, w_out, w_up, w_down)


def _row_iota(shape):
    return lax.broadcasted_iota(jnp.int32, shape, 0)


def _lane_iota(shape):
    return lax.broadcasted_iota(jnp.int32, shape, 1)


def _cumsum_rows(x):
    t = x.shape[0]
    row = _row_iota(x.shape)
    k = 1
    while k < t:
        x = x + jnp.where(row >= k, pltpu.roll(x, k, 0), 0.0)
        k *= 2
    return x


def _cummax_rows(x, reverse):
    t = x.shape[0]
    row = _row_iota(x.shape)
    k = 1
    while k < t:
        if reverse:
            shifted = jnp.where(row < t - k, pltpu.roll(x, t - k, 0), -jnp.inf)
        else:
            shifted = jnp.where(row >= k, pltpu.roll(x, k, 0), -jnp.inf)
        x = jnp.maximum(x, shifted)
        k *= 2
    return x


def _dwconv_silu(x, w, b):
    t = x.shape[0]
    row = _row_iota(x.shape)
    acc = x * w[CONV_K // 2:CONV_K // 2 + 1, :] + b
    for j in range(CONV_K):
        d = j - CONV_K // 2
        if d == 0:
            continue
        shifted = pltpu.roll(x, (-d) % t, 0)
        valid = (row >= -d) if d < 0 else (row < t - d)
        acc = acc + jnp.where(valid, shifted, 0.0) * w[j:j + 1, :]
    return _silu(acc)


def _causal_exponent(expo, r0, k0, reverse):
    ti = r0 + _row_iota(expo.shape)
    si = k0 + _lane_iota(expo.shape)
    keep = (si >= ti) if reverse else (si <= ti)
    return jnp.where(keep, expo, -jnp.inf)


def _mlstm_kernel(*refs, seq, has_state, emit_state, n_carried, slot):
    assert not (has_state and emit_state)
    it = iter(refs)
    qk_ref, v_ref, o_ref, g_ref, cw_ref, cb_ref, gb_ref, anw_ref = (next(it) for _ in range(8))
    if has_state:
        c0_ref, m0_ref = next(it), next(it)
    for _ in range(n_carried):
        next(it)
    ha_ref = next(it)
    if emit_state:
        cn_ref, nn_ref, mn_ref = next(it), next(it), next(it)
        if n_carried == 0:
            for other in range(N_EVEN):
                if other != slot:
                    cn_ref[other] = jnp.zeros(cn_ref.shape[1:], F32)
                    nn_ref[other] = jnp.zeros(nn_ref.shape[1:], F32)
            cn_ref, nn_ref = cn_ref.at[slot], nn_ref.at[slot]

    lane = _lane_iota((seq, LANES))
    fwd = lane < NH_A
    log_i = g_ref[:, 0:LANES] + gb_ref[0:1, :]
    f_pre = g_ref[:, LANES:2 * LANES] + gb_ref[1:2, :]
    log_f = jnp.minimum(f_pre, 0.0) - jnp.log1p(jnp.exp(-jnp.abs(f_pre)))
    pre = _cumsum_rows(log_f)
    total = pre[seq - 1:seq, :]
    b = jnp.where(fwd, pre, total - pre + log_f)
    a = log_i - b
    m_run = jnp.where(fwd, _cummax_rows(a, False), _cummax_rows(a, True))
    if has_state:
        m0 = m0_ref[...]
        m_run = jnp.maximum(m_run, m0)
        w_inter = jnp.exp(m0 - m_run)
    else:
        m_run = jnp.maximum(m_run, 0.0)
    inv_floor = jnp.exp(-(b + m_run))
    a_t = a.T
    fwd_row = fwd[0:1, :]
    b_last = jnp.where(fwd_row, b[seq - 1:seq, :], b[0:1, :])
    m_last = jnp.where(fwd_row, m_run[seq - 1:seq, :], m_run[0:1, :])
    if emit_state:
        mn_ref[...] = b_last + m_last

    ones_col = (lane == 0).astype(F32)
    for h in range(NH_A):
        cq = slice(h * DK_A, (h + 1) * DK_A)
        ck = slice(A_QK + h * DK_A, A_QK + (h + 1) * DK_A)
        cv = slice(h * DV_A, (h + 1) * DV_A)
        q = _dwconv_silu(qk_ref[:, cq], cw_ref[:, cq], cb_ref[:, cq])
        k = _dwconv_silu(qk_ref[:, ck], cw_ref[:, ck], cb_ref[:, ck]) * (DK_A ** -0.5)
        qb = q.astype(BF16)
        kb = k.astype(BF16)
        vh = v_ref[:, cv]
        vaug = jnp.concatenate([vh, ones_col], axis=1).astype(BF16)
        for r0 in range(0, seq, Q_TILE):
            rows = slice(r0, r0 + Q_TILE)
            s = _dot_nt(qb[rows], kb)
            hsum = None
            for d in range(2):
                c = d * NH_A + h
                k0, k1 = (0, r0 + Q_TILE) if d == 0 else (r0, seq)
                expo = _causal_exponent(a_t[c:c + 1, k0:k1] - m_run[rows, c:c + 1], r0, k0, d == 1)
                p = (s[:, k0:k1] * jnp.exp(expo)).astype(BF16)
                acc = _dot(p, vaug[k0:k1])
                if has_state:
                    acc = acc + w_inter[rows, c:c + 1] * _dot(qb[rows], c0_ref[d, h].astype(BF16))
                num = acc[:, 0:DV_A]
                den = acc[:, DV_A:DV_A + 1]
                hd = num / jnp.maximum(jnp.abs(den), inv_floor[rows, c:c + 1])
                hsum = hd if hsum is None else hsum + hd
            og = jax.nn.sigmoid(o_ref[rows, cv]) * hsum
            ha_ref[rows, cv] = _rms(og, anw_ref[:, cv])
        if emit_state:
            for d in range(2):
                c = d * NH_A + h
                kw = k * jnp.exp(a[:, c:c + 1] - m_last[:, c:c + 1])
                cn_ref[d, h] = _dot(kw.T.astype(BF16), vh.astype(BF16))
                nn_ref[d, h:h + 1, :] = jnp.sum(kw, axis=0, keepdims=True)


def _mlstm(qk, v, o, g, conv_w, conv_b, gate_bias, norm_w, seq, state=None, sub_layer=0, carried=None):
    n = qk.shape[0]
    nseq = n // seq
    has_state = state is not None
    emit_state = not has_state
    aliases = {}
    row = lambda s: (s, 0)
    fixed = lambda s: (0, 0)
    in_specs = [pl.BlockSpec((seq, 2 * A_QK), row), pl.BlockSpec((seq, A_V), row), pl.BlockSpec((seq, A_V), row),
                pl.BlockSpec((seq, 3 * LANES), row),
                pl.BlockSpec((CONV_K, 2 * A_QK), fixed), pl.BlockSpec((1, 2 * A_QK), fixed),
                pl.BlockSpec((2, LANES), fixed), pl.BlockSpec((1, A_V), fixed)]
    args = [qk, v, o, g, conv_w, conv_b, gate_bias, norm_w]
    out_specs = [pl.BlockSpec((seq, A_V), row)]
    out_shape = [jax.ShapeDtypeStruct((n, A_V), F32)]
    if has_state:
        c0_aug, m0 = state
        in_specs += [pl.BlockSpec((None, 2, NH_A, DK_A, 2 * LANES), lambda s: (s, 0, 0, 0, 0)),
                     pl.BlockSpec((None, 1, LANES), lambda s: (s, 0, 0))]
        args += [c0_aug, m0]
    if emit_state:
        if carried is None:
            out_specs += [pl.BlockSpec((None, N_EVEN, 2, NH_A, DK_A, DV_A), lambda s: (s, 0, 0, 0, 0, 0)),
                          pl.BlockSpec((None, N_EVEN, 2, NH_A, DK_A), lambda s: (s, 0, 0, 0, 0))]
        else:
            out_specs += [pl.BlockSpec((None, None, 2, NH_A, DK_A, DV_A), lambda s: (s, sub_layer, 0, 0, 0, 0)),
                          pl.BlockSpec((None, None, 2, NH_A, DK_A), lambda s: (s, sub_layer, 0, 0, 0))]
        out_specs += [pl.BlockSpec((None, 1, LANES), lambda s: (s, 0, 0))]
        out_shape += [jax.ShapeDtypeStruct((nseq, N_EVEN, 2, NH_A, DK_A, DV_A), F32),
                      jax.ShapeDtypeStruct((nseq, N_EVEN, 2, NH_A, DK_A), F32),
                      jax.ShapeDtypeStruct((nseq, 1, LANES), F32)]
        if carried is not None:
            aliases = {len(args): 1, len(args) + 1: 2}
            in_specs += [pl.BlockSpec(memory_space=pl.ANY)] * 2
            args += list(carried)
    return pl.pallas_call(
        functools.partial(_mlstm_kernel, seq=seq, has_state=has_state, emit_state=emit_state,
                          n_carried=len(aliases), slot=sub_layer),
        grid=(nseq,), in_specs=in_specs, out_specs=out_specs, out_shape=out_shape,
        input_output_aliases=aliases, compiler_params=_params("arbitrary"), name="mlstm",
    )(*args)


def _ssd_kernel(*refs, seq, has_state, emit_state, n_carried, slot):
    assert not (has_state and emit_state)
    it = iter(refs)
    xbc_ref, z_ref, g_ref, cw_ref, cb_ref, dtb_ref, alog_ref, dsk_ref, bnw_ref = (next(it) for _ in range(9))
    if has_state:
        s0_ref = next(it)
    for _ in range(n_carried):
        next(it)
    yb_ref = next(it)
    if emit_state:
        sn_ref = next(it)
        if n_carried == 0:
            for other in range(N_EVEN):
                if other != slot:
                    sn_ref[other] = jnp.zeros(sn_ref.shape[1:], F32)
            sn_ref = sn_ref.at[slot]

    lane = _lane_iota((seq, LANES))
    fwd = lane < NH_B
    dt = _softplus(g_ref[:, 2 * LANES:3 * LANES] + dtb_ref[...])
    da = dt * (-jnp.exp(alog_ref[...]))
    pre = _cumsum_rows(da)
    total = pre[seq - 1:seq, :]
    acum = jnp.where(fwd, pre, total - pre + da)
    acum_t = acum.T
    if has_state:
        carry_in = jnp.exp(acum)
    if emit_state:
        a_last = jnp.where(fwd[0:1, :], acum[seq - 1:seq, :], acum[0:1, :])
        w_state = jnp.exp(a_last - acum) * dt

    gw = R_B * HP_B
    for g in range(NG_B):
        cx = slice(g * gw, (g + 1) * gw)
        cb_ = slice(B_INNER + g * DSTATE, B_INNER + (g + 1) * DSTATE)
        cc = slice(B_INNER + B_BC + g * DSTATE, B_INNER + B_BC + (g + 1) * DSTATE)
        xg = _dwconv_silu(xbc_ref[:, cx], cw_ref[:, cx], cb_ref[:, cx])
        bg = _dwconv_silu(xbc_ref[:, cb_], cw_ref[:, cb_], cb_ref[:, cb_]).astype(BF16)
        cg = _dwconv_silu(xbc_ref[:, cc], cw_ref[:, cc], cb_ref[:, cc]).astype(BF16)
        u = [[(xg[:, r * HP_B:(r + 1) * HP_B] * dt[:, d * NH_B + g * R_B + r:d * NH_B + g * R_B + r + 1]).astype(BF16)
              for r in range(R_B)] for d in range(2)]
        for r0 in range(0, seq, Q_TILE):
            rows = slice(r0, r0 + Q_TILE)
            cb_scores = _dot_nt(cg[rows], bg)
            ys = []
            for r in range(R_B):
                h = g * R_B + r
                yh = None
                for d in range(2):
                    c = d * NH_B + h
                    k0, k1 = (0, r0 + Q_TILE) if d == 0 else (r0, seq)
                    expo = _causal_exponent(acum[rows, c:c + 1] - acum_t[c:c + 1, k0:k1], r0, k0, d == 1)
                    w = (cb_scores[:, k0:k1] * jnp.exp(expo)).astype(BF16)
                    yd = _dot(w, u[d][r][k0:k1])
                    if has_state:
                        yd = yd + carry_in[rows, c:c + 1] * _dot_nt(cg[rows], s0_ref[d, h].astype(BF16))
                    yh = yd if yh is None else yh + yd
                ys.append(yh)
            y = jnp.concatenate(ys, axis=1) + dsk_ref[:, cx] * xg[rows]
            y = y * _silu(z_ref[rows, cx])
            yb_ref[rows, cx] = _rms(y, bnw_ref[:, cx])
        if emit_state:
            for d in range(2):
                xw = jnp.concatenate(
                    [xg[:, r * HP_B:(r + 1) * HP_B] * w_state[:, d * NH_B + g * R_B + r:d * NH_B + g * R_B + r + 1]
                     for r in range(R_B)], axis=1)
                sn = _dot(xw.T.astype(BF16), bg)
                for r in range(R_B):
                    sn_ref[d, g * R_B + r] = sn[r * HP_B:(r + 1) * HP_B, :]


def _ssd(xbc, z, g, conv_w, conv_b, dt_bias, a_log, d_skip, norm_w, seq, state=None, sub_layer=0, carried=None):
    n = xbc.shape[0]
    nseq = n // seq
    has_state = state is not None
    emit_state = not has_state
    aliases = {}
    row = lambda s: (s, 0)
    fixed = lambda s: (0, 0)
    in_specs = [pl.BlockSpec((seq, B_XBC), row), pl.BlockSpec((seq, B_INNER), row), pl.BlockSpec((seq, 3 * LANES), row),
                pl.BlockSpec((CONV_K, B_XBC), fixed), pl.BlockSpec((1, B_XBC), fixed),
                pl.BlockSpec((1, LANES), fixed), pl.BlockSpec((1, LANES), fixed),
                pl.BlockSpec((1, B_INNER), fixed), pl.BlockSpec((1, B_INNER), fixed)]
    args = [xbc, z, g, conv_w, conv_b, dt_bias, a_log, d_skip, norm_w]
    out_specs = [pl.BlockSpec((seq, B_INNER), row)]
    out_shape = [jax.ShapeDtypeStruct((n, B_INNER), F32)]
    state_spec = pl.BlockSpec((None, None, 2, NH_B, HP_B, DSTATE), lambda s: (s, sub_layer, 0, 0, 0, 0))
    if has_state:
        in_specs.append(state_spec)
        args.append(state)
    if emit_state:
        out_specs.append(state_spec if carried is not None else
                         pl.BlockSpec((None, N_EVEN, 2, NH_B, HP_B, DSTATE), lambda s: (s, 0, 0, 0, 0, 0)))
        out_shape.append(jax.ShapeDtypeStruct((nseq, N_EVEN, 2, NH_B, HP_B, DSTATE), F32))
        if carried is not None:
            aliases = {len(args): 1}
            in_specs.append(pl.BlockSpec(memory_space=pl.ANY))
            args.append(carried)
    return pl.pallas_call(
        functools.partial(_ssd_kernel, seq=seq, has_state=has_state, emit_state=emit_state, n_carried=len(aliases),
                          slot=sub_layer),
        grid=(nseq,), in_specs=in_specs, out_specs=out_specs, out_shape=out_shape,
        input_output_aliases=aliases, compiler_params=_params("arbitrary"), name="ssd",
    )(*args)


def _softmax_pv(s, v, sink=None):
    m = jnp.max(s, axis=1, keepdims=True)
    if sink is not None:
        m = jnp.maximum(m, sink)
    p = jnp.exp(s - m)
    l = jnp.sum(p, axis=1, keepdims=True)
    if sink is not None:
        l = l + jnp.exp(sink - m)
    return _dot(p.astype(BF16), v) / l


def _sink_column(sink_ref, kh, rows):
    return jnp.concatenate([jnp.broadcast_to(sink_ref[0:1, kh * G_C + g:kh * G_C + g + 1], (rows, 1))
                            for g in range(G_C)], axis=0)


def _stack_heads(x, kh, rows):
    return jnp.concatenate([x[rows, (kh * G_C + g) * HD_C:(kh * G_C + g + 1) * HD_C] for g in range(G_C)], axis=0)


def _mla_queries_keys(qa_ref, kva_ref, qan_ref, kvn_ref, wqb_ref):
    qd = _dot(_rms(qa_ref[...], qan_ref[...]).astype(BF16), wqb_ref[...])
    ckv = _rms(kva_ref[...], kvn_ref[...])
    return qd, ckv


def _attn_ctx_kernel(qc_ref, kc_ref, vc_ref, qa_ref, kva_ref, kpe_ref, sink_ref, qan_ref, kvn_ref, wqb_ref, wkvb_ref,
                     oc_ref, od_ref, ckv_ref, *, seq):
    full = slice(0, seq)
    for kh in range(NKV_C):
        ch = slice(kh * HD_C, (kh + 1) * HD_C)
        q4 = _stack_heads(qc_ref, kh, full).astype(BF16)
        s = _dot_nt(q4, kc_ref[:, ch].astype(BF16)) * (HD_C ** -0.5)
        o4 = _softmax_pv(s, vc_ref[:, ch].astype(BF16), _sink_column(sink_ref, kh, seq))
        for g in range(G_C):
            n = kh * G_C + g
            oc_ref[:, n * HD_C:(n + 1) * HD_C] = o4[g * seq:(g + 1) * seq, :]

    qd, ckv = _mla_queries_keys(qa_ref, kva_ref, qan_ref, kvn_ref, wqb_ref)
    ckv_ref[...] = ckv
    kv = _dot(ckv.astype(BF16), wkvb_ref[...]).astype(BF16)
    qdb = qd.astype(BF16)
    kpe = kpe_ref[:, 0:ROPE_D].astype(BF16)
    nope_w = NH_D * NOPE_D
    for n in range(NH_D):
        s = (_dot_nt(qdb[:, n * NOPE_D:(n + 1) * NOPE_D], kv[:, n * NOPE_D:(n + 1) * NOPE_D])
             + _dot_nt(qdb[:, nope_w + n * ROPE_D:nope_w + (n + 1) * ROPE_D], kpe)) * MLA_SCALE
        od_ref[:, n * V_D:(n + 1) * V_D] = _softmax_pv(s, kv[:, nope_w + n * V_D:nope_w + (n + 1) * V_D])


def _attn_ctx(qc, kc, vc, qa, kva, kpe, sink, qan, kvn, wqb, wkvb, seq):
    n = qc.shape[0]
    row = lambda s: (s, 0)
    fixed = lambda s: (0, 0)
    widths = ODD_WIDTHS
    in_specs = [pl.BlockSpec((seq, wd), row) for wd in widths]
    in_specs += [pl.BlockSpec(a.shape, fixed) for a in (sink, qan, kvn, wqb, wkvb)]
    half = NH_C * HD_C
    return pl.pallas_call(
        functools.partial(_attn_ctx_kernel, seq=seq),
        grid=(n // seq,), in_specs=in_specs,
        out_specs=[pl.BlockSpec((seq, half), row), pl.BlockSpec((seq, half), row), pl.BlockSpec((seq, KV_RANK), row)],
        out_shape=[jax.ShapeDtypeStruct((n, half), F32), jax.ShapeDtypeStruct((n, half), F32),
                   jax.ShapeDtypeStruct((n, KV_RANK), F32)],
        compiler_params=_params("arbitrary"), name="attn_ctx",
    )(qc, kc, vc, qa, kva, kpe, sink, qan, kvn, wqb, wkvb)


def _rope(x, cos, sin, half):
    parts = []
    lane = _lane_iota((x.shape[0], LANES))
    first = (lane & (2 * half - 1)) < half
    for i in range(x.shape[1] // LANES):
        xi = x[:, i * LANES:(i + 1) * LANES]
        partner = jnp.where(first, -pltpu.roll(xi, LANES - half, 1), pltpu.roll(xi, half, 1))
        parts.append(xi * cos + partner * sin)
    return parts[0] if len(parts) == 1 else jnp.concatenate(parts, axis=1)


def _attn_lat_kernel(qc_ref, qa_ref, ropeq_ref, kc_ref, vc_ref, kva_ref, kpe_ref, kctx_ref, vctx_ref, ckvctx_ref,
                     kpectx_ref, rope_ref, sink_ref, qan_ref, kvn_ref, wqb_ref, wkvb_ref, oc_ref, od_ref,
                     kwin_s, vwin_s, kv_s, kpe_s, *, seq, past):
    qi = pl.program_id(1)
    nope_w = NH_D * NOPE_D

    @pl.when(qi == 0)
    def _():
        zeros = jnp.zeros((WINDOW, NKV_C * HD_C), BF16)
        for ref, lat, ctx in ((kwin_s, _rope(kc_ref[...], rope_ref[0], rope_ref[1], HD_C // 2), kctx_ref[...]),
                              (vwin_s, vc_ref[...], vctx_ref[...])):
            ref[0:WINDOW, :] = zeros
            ref[WINDOW:WINDOW + seq, :] = lat.astype(BF16)
            ref[WINDOW + seq:2 * WINDOW + seq, :] = zeros
            ref[2 * WINDOW + seq:, :] = ctx.astype(BF16)
        ckv = _rms(kva_ref[...], kvn_ref[...])
        kv_s[0:past, :] = _dot(ckvctx_ref[...].astype(BF16), wkvb_ref[...]).astype(BF16)
        kv_s[past:, :] = _dot(ckv.astype(BF16), wkvb_ref[...]).astype(BF16)
        kpe_s[0:past, :] = kpectx_ref[...].astype(BF16)
        kpe_s[past:, :] = _rope(kpe_ref[...], rope_ref[2], rope_ref[3], ROPE_D // 2).astype(BF16)

    r0 = pl.multiple_of(qi * Q_TILE, Q_TILE)
    nloc = Q_TILE + 2 * WINDOW
    qr = _rope(qc_ref[...], ropeq_ref[0], ropeq_ref[1], HD_C // 2)
    full = slice(0, Q_TILE)
    for kh in range(NKV_C):
        ch = slice(kh * HD_C, (kh + 1) * HD_C)
        q4 = _stack_heads(qr, kh, full).astype(BF16)
        keys = jnp.concatenate([kwin_s[pl.ds(r0, nloc), ch], kwin_s[2 * WINDOW + seq:, ch]], axis=0)
        vals = jnp.concatenate([vwin_s[pl.ds(r0, nloc), ch], vwin_s[2 * WINDOW + seq:, ch]], axis=0)
        s = _dot_nt(q4, keys) * (HD_C ** -0.5)
        ti = r0 + (_row_iota(s.shape) & (Q_TILE - 1))
        col = _lane_iota(s.shape)
        pos = r0 - WINDOW + col
        valid = (col >= nloc) | ((jnp.abs(ti - pos) <= WINDOW) & (pos >= 0) & (pos < seq))
        s = jnp.where(valid, s, -jnp.inf)
        o4 = _softmax_pv(s, vals, _sink_column(sink_ref, kh, Q_TILE))
        for g in range(G_C):
            n = kh * G_C + g
            oc_ref[:, n * HD_C:(n + 1) * HD_C] = o4[g * Q_TILE:(g + 1) * Q_TILE, :]

    qd = _dot(_rms(qa_ref[...], qan_ref[...]).astype(BF16), wqb_ref[...])
    q_nope = qd[:, 0:nope_w].astype(BF16)
    q_pe = _rope(qd[:, nope_w:], ropeq_ref[2], ropeq_ref[3], ROPE_D // 2).astype(BF16)
    for n in range(NH_D):
        s = (_dot_nt(q_nope[:, n * NOPE_D:(n + 1) * NOPE_D], kv_s[:, n * NOPE_D:(n + 1) * NOPE_D])
             + _dot_nt(q_pe[:, n * ROPE_D:(n + 1) * ROPE_D], kpe_s[:, 0:ROPE_D])) * MLA_SCALE
        od_ref[:, n * V_D:(n + 1) * V_D] = _softmax_pv(s, kv_s[:, nope_w + n * V_D:nope_w + (n + 1) * V_D])


def _attn_lat(qc, kc, vc, qa, kva, kpe, kctx, vctx, ckvctx, kpectx, rope, sink, qan, kvn, wqb, wkvb, seq):
    n = qc.shape[0]
    past = kctx.shape[1]
    nq = seq // Q_TILE
    qrow = lambda b, q: (b * nq + q, 0)
    krow = lambda b, q: (b, 0)
    fixed = lambda b, q: (0, 0)
    kvw = NKV_C * HD_C
    in_specs = [pl.BlockSpec((Q_TILE, NH_C * HD_C), qrow), pl.BlockSpec((Q_TILE, Q_RANK), qrow),
                pl.BlockSpec((4, Q_TILE, LANES), lambda b, q: (0, q, 0)),
                pl.BlockSpec((seq, kvw), krow), pl.BlockSpec((seq, kvw), krow),
                pl.BlockSpec((seq, KV_RANK), krow), pl.BlockSpec((seq, LANES), krow)]
    in_specs += [pl.BlockSpec((None, past, a.shape[2]), lambda b, q: (b, 0, 0)) for a in (kctx, vctx, ckvctx, kpectx)]
    in_specs += [pl.BlockSpec(rope.shape, lambda b, q: (0, 0, 0))]
    in_specs += [pl.BlockSpec(a.shape, fixed) for a in (sink, qan, kvn, wqb, wkvb)]
    half = NH_C * HD_C
    return pl.pallas_call(
        functools.partial(_attn_lat_kernel, seq=seq, past=past),
        grid=(n // seq, nq), in_specs=in_specs,
        out_specs=[pl.BlockSpec((Q_TILE, half), qrow), pl.BlockSpec((Q_TILE, half), qrow)],
        out_shape=[jax.ShapeDtypeStruct((n, half), F32), jax.ShapeDtypeStruct((n, half), F32)],
        scratch_shapes=[pltpu.VMEM((2 * WINDOW + seq + past, kvw), BF16), pltpu.VMEM((2 * WINDOW + seq + past, kvw), BF16),
                        pltpu.VMEM((past + seq, NH_D * (NOPE_D + V_D)), BF16), pltpu.VMEM((past + seq, LANES), BF16)],
        compiler_params=_params("arbitrary", "arbitrary"), name="attn_lat",
    )(qc, qa, rope, kc, vc, kva, kpe, kctx, vctx, ckvctx, kpectx, rope, sink, qan, kvn, wqb, wkvb)


def _pad_lanes(x, width=LANES):
    return jnp.pad(x, [(0, 0)] * (x.ndim - 1) + [(0, width - x.shape[-1])])


def _even_in_weight(w):
    o0 = 2 * A_QK + 2 * A_V
    ig, fg = w[:, o0:o0 + 2 * NH_A], w[:, o0 + 2 * NH_A:o0 + 4 * NH_A]
    z0 = o0 + 4 * NH_A
    x0 = z0 + B_INNER
    d0 = x0 + B_XBC
    return jnp.concatenate([w[:, :o0], w[:, z0:x0], w[:, x0:d0], _pad_lanes(ig), _pad_lanes(fg),
                            _pad_lanes(w[:, d0:])], axis=1).astype(BF16)


def _odd_in_weight(w):
    return _pad_lanes(w, sum(ODD_WIDTHS)).astype(BF16)


def _split_heads_weight(w, heads, first, second):
    w3 = w.reshape(w.shape[0], heads, first + second)
    return jnp.concatenate([w3[:, :, :first].reshape(w.shape[0], heads * first),
                            w3[:, :, first:].reshape(w.shape[0], heads * second)], axis=1).astype(BF16)


def _rope_tables(rows):
    def table(rot_dim):
        quarter = rot_dim // 4
        inv = ROPE_BASE ** (-jnp.arange(quarter, dtype=F32) / quarter)
        r = jnp.repeat(jnp.arange(rows, dtype=F32), GRID_W)
        col = jnp.tile(jnp.arange(GRID_W, dtype=F32), rows)
        ang = jnp.concatenate([r[:, None] * inv, col[:, None] * inv], axis=-1)
        reps = LANES // (rot_dim // 2)
        return jnp.tile(jnp.cos(ang), (1, reps)), jnp.tile(jnp.sin(ang), (1, reps))
    cos_c, sin_c = table(HD_C)
    cos_d, sin_d = table(ROPE_D)
    return jnp.stack([cos_c, sin_c, cos_d, sin_d])


def kernel(x_prompt, x_sample, c, state_mlstm_C, state_mlstm_n, state_mlstm_m, state_ssd, cache_gqa_k, cache_gqa_v,
           cache_mla_ckv, cache_mla_kpe, c_ctx, w_ada, b_ada, norm_g, w_up, w_down, w_in_even, conv_a_w, conv_a_b,
           conv_b_w, conv_b_b, gate_b, a_norm_w, dt_bias, a_log, d_skip, b_norm_w, w_out_even, w_in_odd, sink,
           q_a_norm, kv_a_norm, w_q_b, w_kv_b, w_out_odd):
    xp = x_prompt.reshape(BATCH * SEQ, D_MODEL)
    xs = x_sample.reshape(DEC_BATCH * DEC_SEQ, D_MODEL)
    cond = jnp.concatenate([c_ctx[None, :], c, jnp.zeros((MOD_ROWS - 1 - DEC_BATCH, D_MODEL), F32)], axis=0)
    mods = _modulations(cond, w_ada, b_ada)
    rope = _rope_tables(DEC_SEQ // GRID_W)

    new_m, new_k, new_v, new_ckv, new_kpe = [], [], [], [], []
    mem_state, ssd_state = None, None
    for l in range(DEPTH):
        j = l // 2
        gain_in = norm_g[l, 0].reshape(1, D_MODEL)
        if l % 2 == 0:
            w_in = _even_in_weight(w_in_even[j])
            gate_bias = jnp.stack([_pad_lanes(gate_b[j, :2 * NH_A]), _pad_lanes(gate_b[j, 2 * NH_A:])])
            a_args = (conv_a_w[j], conv_a_b[j].reshape(1, -1), gate_bias, a_norm_w[j].reshape(1, -1))
            b_args = (conv_b_w[j], conv_b_b[j].reshape(1, -1), _pad_lanes(dt_bias[j].reshape(1, -1)),
                      _pad_lanes(a_log[j].reshape(1, -1)), jnp.repeat(d_skip[j], HP_B).reshape(1, -1),
                      b_norm_w[j].reshape(1, -1))
            qk, v, o, z, xbc, g = _project(xp, mods, l, False, gain_in, w_in, EVEN_WIDTHS)
            a1p, cn, nn, mn = _mlstm(qk, v, o, g, *a_args, seq=SEQ, sub_layer=j, carried=mem_state)
            a2p, ssd_state = _ssd(xbc, z, g, *b_args, seq=SEQ, sub_layer=j, carried=ssd_state)
            mem_state = (cn, nn)
            new_m.append(mn[:, 0, :2 * NH_A].reshape(BATCH, 2, NH_A))
            qk, v, o, z, xbc, g = _project(xs, mods, l, True, gain_in, w_in, EVEN_WIDTHS)
            c0_aug = jnp.concatenate([state_mlstm_C[:, j], _pad_lanes(state_mlstm_n[:, j][..., None])], axis=-1)
            m0 = _pad_lanes(state_mlstm_m[:, j].reshape(DEC_BATCH, 1, 2 * NH_A))
            a1s, = _mlstm(qk, v, o, g, *a_args, seq=DEC_SEQ, state=(c0_aug, m0))
            a2s, = _ssd(xbc, z, g, *b_args, seq=DEC_SEQ, state=state_ssd, sub_layer=j)
            w_out = w_out_even
        else:
            w_in = _odd_in_weight(w_in_odd[j])
            o_args = (_pad_lanes(sink[j].reshape(1, -1)), q_a_norm[j].reshape(1, -1), kv_a_norm[j].reshape(1, -1),
                      _split_heads_weight(w_q_b[j], NH_D, NOPE_D, ROPE_D),
                      _split_heads_weight(w_kv_b[j], NH_D, NOPE_D, V_D))
            qc, kc, vc, qa, kva, kpe = _project(xp, mods, l, False, gain_in, w_in, ODD_WIDTHS)
            a1p, a2p, ckv = _attn_ctx(qc, kc, vc, qa, kva, kpe, *o_args, seq=SEQ)
            new_k.append(kc.reshape(BATCH, SEQ, NKV_C, HD_C))
            new_v.append(vc.reshape(BATCH, SEQ, NKV_C, HD_C))
            new_ckv.append(ckv.reshape(BATCH, SEQ, KV_RANK))
            new_kpe.append(kpe[:, :ROPE_D].reshape(BATCH, SEQ, ROPE_D))
            qc, kc, vc, qa, kva, kpe = _project(xs, mods, l, True, gain_in, w_in, ODD_WIDTHS)
            a1s, a2s = _attn_lat(qc, kc, vc, qa, kva, kpe,
                                 cache_gqa_k[:, j].reshape(DEC_BATCH, PAST_LEN, NKV_C * HD_C),
                                 cache_gqa_v[:, j].reshape(DEC_BATCH, PAST_LEN, NKV_C * HD_C),
                                 cache_mla_ckv[:, j], _pad_lanes(cache_mla_kpe[:, j]), rope, *o_args, seq=DEC_SEQ)
            w_out = w_out_odd
        gains = norm_g[l, 1:4]
        xp = _channel(a1p, a2p, xp, mods, l, False, gains, w_out, w_up, w_down)
        xs = _channel(a1s, a2s, xs, mods, l, True, gains, w_out, w_up, w_down)

    return (xp.reshape(BATCH, SEQ, D_MODEL), xs.reshape(DEC_BATCH, DEC_SEQ, D_MODEL),
            mem_state[0], mem_state[1], jnp.stack(new_m, axis=1), ssd_state,
            jnp.stack(new_k, axis=1), jnp.stack(new_v, axis=1), jnp.stack(new_ckv, axis=1),
            jnp.stack(new_kpe, axis=1))
```

```python
import functools

import jax
import jax.numpy as jnp
from jax import lax
from jax.experimental import pallas as pl
from jax.experimental.pallas import tpu as pltpu

F32 = jnp.float32
BF16 = jnp.bfloat16

D_MODEL = 1024
BATCH = 32
SEQ = 256
DEPTH = 4
DEC_BATCH = 2
DEC_SEQ = 1024
PAST_LEN = 256
GRID_W = 64
N_EVEN = (DEPTH + 1) // 2
N_ODD = DEPTH // 2
EPS = 1e-6
CONV_K = 5
NH_A = 4
DK_A = 128
DV_A = 128
A_QK = NH_A * DK_A
A_V = NH_A * DV_A
NH_B = 8
HP_B = 64
DSTATE = 128
NG_B = 2
R_B = NH_B // NG_B
B_INNER = NH_B * HP_B
B_BC = NG_B * DSTATE
B_XBC = B_INNER + 2 * B_BC
NH_C = 8
NKV_C = 2
G_C = NH_C // NKV_C
HD_C = 64
WINDOW = 128
NH_D = 8
Q_RANK = 256
KV_RANK = 128
NOPE_D = 64
ROPE_D = 32
V_D = 64
MLA_SCALE = (NOPE_D + ROPE_D) ** -0.5
D_FF = 4 * D_MODEL
ROPE_BASE = 10000.0

LANES = 128
VMEM_LIMIT_BYTES = 56 * 1024 * 1024
ROW_TILE = 512
FF_TILE = 1024
SUB_ROWS = 256
STAGE_ROWS = 512
Q_TILE = 256
ADA_TILE = 1536
MOD_ROWS = 8

EVEN_WIDTHS = (2 * A_QK, A_V, A_V, B_INNER, B_XBC, 3 * LANES)
ODD_WIDTHS = (NH_C * HD_C, NKV_C * HD_C, NKV_C * HD_C, Q_RANK, KV_RANK, LANES)

_NT = (((1,), (1,)), ((), ()))


def _params(*sem):
    return pltpu.CompilerParams(dimension_semantics=sem, vmem_limit_bytes=VMEM_LIMIT_BYTES)


def _rms(x, g):
    return x * lax.rsqrt(jnp.mean(x * x, axis=-1, keepdims=True) + EPS) * g


def _silu(x):
    return x * jax.nn.sigmoid(x)


def _softplus(x):
    return jnp.maximum(x, 0.0) + jnp.log1p(jnp.exp(-jnp.abs(x)))


def _dot(a, b):
    return jnp.dot(a, b, preferred_element_type=F32)


def _dot_nt(a, b):
    return lax.dot_general(a, b, _NT, preferred_element_type=F32)


def _ada_kernel(c_ref, w_ref, b_ref, o_ref):
    s = _silu(c_ref[...]).astype(BF16)
    o_ref[...] = _dot(s, w_ref[...].astype(BF16)) + b_ref[...]


def _modulations(cond, w_ada, b_ada):
    out = pl.pallas_call(
        _ada_kernel,
        grid=(DEPTH, 6 * D_MODEL // ADA_TILE),
        in_specs=[pl.BlockSpec((MOD_ROWS, D_MODEL), lambda l, n: (0, 0)),
                  pl.BlockSpec((None, D_MODEL, ADA_TILE), lambda l, n: (l, 0, n)),
                  pl.BlockSpec((None, 1, ADA_TILE), lambda l, n: (l, 0, n))],
        out_specs=pl.BlockSpec((None, MOD_ROWS, ADA_TILE), lambda l, n: (l, 0, n)),
        out_shape=jax.ShapeDtypeStruct((DEPTH, MOD_ROWS, 6 * D_MODEL), F32),
        compiler_params=_params("arbitrary", "arbitrary"),
        name="ada",
    )(cond, w_ada, b_ada.reshape(DEPTH, 1, 6 * D_MODEL))
    return out.reshape(DEPTH, MOD_ROWS, 6, D_MODEL)


def _mod_spec(layer, latent):
    if latent:
        per_seq = DEC_SEQ // ROW_TILE
        return pl.BlockSpec((None, None, 6, D_MODEL), lambda i, *_: (layer, 1 + i // per_seq, 0, 0))
    return pl.BlockSpec((None, None, 6, D_MODEL), lambda i, *_: (layer, 0, 0, 0))


def _proj_kernel(x_ref, mod_ref, g_ref, w_ref, *o_refs, widths):
    h = _rms(x_ref[...], g_ref[...]) * (1.0 + mod_ref[1:2, :]) + mod_ref[0:1, :]
    hb = h.astype(BF16)
    off = 0
    for o_ref, wd in zip(o_refs, widths):
        o_ref[...] = _dot(hb, w_ref[:, off:off + wd])
        off += wd


def _project(x, mods, layer, latent, gain, w, widths):
    n = x.shape[0]
    return pl.pallas_call(
        functools.partial(_proj_kernel, widths=widths),
        grid=(n // ROW_TILE,),
        in_specs=[pl.BlockSpec((ROW_TILE, D_MODEL), lambda i: (i, 0)),
                  _mod_spec(layer, latent),
                  pl.BlockSpec((1, D_MODEL), lambda i: (0, 0)),
                  pl.BlockSpec(w.shape, lambda i: (0, 0))],
        out_specs=[pl.BlockSpec((ROW_TILE, wd), lambda i: (i, 0)) for wd in widths],
        out_shape=[jax.ShapeDtypeStruct((n, wd), F32) for wd in widths],
        compiler_params=_params("arbitrary"),
        name="proj",
    )(x, mods, gain, w)


def _weight_chunks(layer, sub_layer, wo_hbm, wu_hbm, wd_hbm, wo_s, wu_s, wd_s):
    chunks = []
    for r in range(0, D_MODEL, STAGE_ROWS):
        chunks.append((wo_hbm.at[sub_layer, pl.ds(r, STAGE_ROWS), :], wo_s.at[pl.ds(r, STAGE_ROWS), :]))
    for r in range(0, D_MODEL, STAGE_ROWS):
        for c in range(0, D_FF, D_MODEL):
            chunks.append((wu_hbm.at[layer, pl.ds(r, STAGE_ROWS), pl.ds(c, D_MODEL)],
                           wu_s.at[pl.ds(r, STAGE_ROWS), pl.ds(c, D_MODEL)]))
    for r in range(0, D_FF, STAGE_ROWS):
        chunks.append((wd_hbm.at[layer, pl.ds(r, STAGE_ROWS), :], wd_s.at[pl.ds(r, STAGE_ROWS), :]))
    return chunks


def _channel_kernel(a1_ref, a2_ref, x_ref, mod_ref, g_ref, wo_hbm, wu_hbm, wd_hbm, o_ref,
                    wo_s, wu_s, wd_s, stage, sem, *, layer, sub_layer):
    half = a1_ref.shape[1]

    @pl.when(pl.program_id(0) == 0)
    def _():
        chunks = _weight_chunks(layer, sub_layer, wo_hbm, wu_hbm, wd_hbm, wo_s, wu_s, wd_s)
        copies = [pltpu.make_async_copy(src, stage.at[k % 2], sem.at[k % 2]) for k, (src, _) in enumerate(chunks)]
        copies[0].start()
        for k, (_, dst) in enumerate(chunks):
            if k + 1 < len(chunks):
                copies[k + 1].start()
            copies[k].wait()
            dst[...] = stage[k % 2].astype(BF16)

    for r0 in range(0, ROW_TILE, SUB_ROWS):
        rows = slice(r0, r0 + SUB_ROWS)
        y = (_dot(a1_ref[rows, :].astype(BF16), wo_s[0:half, :])
             + _dot(a2_ref[rows, :].astype(BF16), wo_s[half:, :]))
        x1 = x_ref[rows, :] + mod_ref[2:3, :] * _rms(y, g_ref[0:1, :])
        h = (_rms(x1, g_ref[1:2, :]) * (1.0 + mod_ref[4:5, :]) + mod_ref[3:4, :]).astype(BF16)
        acc = None
        for c in range(0, D_FF, FF_TILE):
            u = jnp.square(jnp.maximum(_dot(h, wu_s[:, c:c + FF_TILE]), 0.0)).astype(BF16)
            part = _dot(u, wd_s[c:c + FF_TILE, :])
            acc = part if acc is None else acc + part
        o_ref[rows, :] = x1 + mod_ref[5:6, :] * _rms(acc, g_ref[2:3, :])


def _channel(a1, a2, x, mods, layer, latent, gains, w_out, w_up, w_down):
    n = x.shape[0]
    half = a1.shape[1]
    row = lambda i: (i, 0)
    hbm = pl.BlockSpec(memory_space=pl.ANY)
    return pl.pallas_call(
        functools.partial(_channel_kernel, layer=layer, sub_layer=layer // 2),
        grid=(n // ROW_TILE,),
        in_specs=[pl.BlockSpec((ROW_TILE, half), row), pl.BlockSpec((ROW_TILE, half), row),
                  pl.BlockSpec((ROW_TILE, D_MODEL), row), _mod_spec(layer, latent),
                  pl.BlockSpec((3, D_MODEL), lambda i: (0, 0)), hbm, hbm, hbm],
        out_specs=pl.BlockSpec((ROW_TILE, D_MODEL), row),
        out_shape=jax.ShapeDtypeStruct((n, D_MODEL), F32),
        scratch_shapes=[pltpu.VMEM((D_MODEL, D_MODEL), BF16), pltpu.VMEM((D_MODEL, D_FF), BF16),
                        pltpu.VMEM((D_FF, D_MODEL), BF16), pltpu.VMEM((2, STAGE_ROWS, D_MODEL), F32),
                        pltpu.SemaphoreType.DMA((2,))],
        compiler_params=_params("arbitrary"),
        name="channel",
    )(a1, a2, x, mods, gains, w_out, w_up, w_down)


def _row_iota(shape):
    return lax.broadcasted_iota(jnp.int32, shape, 0)


def _lane_iota(shape):
    return lax.broadcasted_iota(jnp.int32, shape, 1)


def _cumsum_rows(x):
    t = x.shape[0]
    row = _row_iota(x.shape)
    k = 1
    while k < t:
        x = x + jnp.where(row >= k, pltpu.roll(x, k, 0), 0.0)
        k *= 2
    return x


def _cummax_rows(x, reverse):
    t = x.shape[0]
    row = _row_iota(x.shape)
    k = 1
    while k < t:
        if reverse:
            shifted = jnp.where(row < t - k, pltpu.roll(x, t - k, 0), -jnp.inf)
        else:
            shifted = jnp.where(row >= k, pltpu.roll(x, k, 0), -jnp.inf)
        x = jnp.maximum(x, shifted)
        k *= 2
    return x


def _dwconv_silu(x, w, b):
    t = x.shape[0]
    row = _row_iota(x.shape)
    acc = x * w[CONV_K // 2:CONV_K // 2 + 1, :] + b
    for j in range(CONV_K):
        d = j - CONV_K // 2
        if d == 0:
            continue
        shifted = pltpu.roll(x, (-d) % t, 0)
        valid = (row >= -d) if d < 0 else (row < t - d)
        acc = acc + jnp.where(valid, shifted, 0.0) * w[j:j + 1, :]
    return _silu(acc)


def _causal_exponent(expo, r0, k0, reverse):
    ti = r0 + _row_iota(expo.shape)
    si = k0 + _lane_iota(expo.shape)
    keep = (si >= ti) if reverse else (si <= ti)
    return jnp.where(keep, expo, -jnp.inf)


def _mlstm_kernel(*refs, seq, has_state, emit_state, n_carried, slot):
    assert not (has_state and emit_state)
    it = iter(refs)
    qk_ref, v_ref, o_ref, g_ref, cw_ref, cb_ref, gb_ref, anw_ref = (next(it) for _ in range(8))
    if has_state:
        c0_ref, m0_ref = next(it), next(it)
    for _ in range(n_carried):
        next(it)
    ha_ref = next(it)
    if emit_state:
        cn_ref, nn_ref, mn_ref = next(it), next(it), next(it)
        if n_carried == 0:
            for other in range(N_EVEN):
                if other != slot:
                    cn_ref[other] = jnp.zeros(cn_ref.shape[1:], F32)
                    nn_ref[other] = jnp.zeros(nn_ref.shape[1:], F32)
            cn_ref, nn_ref = cn_ref.at[slot], nn_ref.at[slot]

    lane = _lane_iota((seq, LANES))
    fwd = lane < NH_A
    log_i = g_ref[:, 0:LANES] + gb_ref[0:1, :]
    f_pre = g_ref[:, LANES:2 * LANES] + gb_ref[1:2, :]
    log_f = jnp.minimum(f_pre, 0.0) - jnp.log1p(jnp.exp(-jnp.abs(f_pre)))
    pre = _cumsum_rows(log_f)
    total = pre[seq - 1:seq, :]
    b = jnp.where(fwd, pre, total - pre + log_f)
    a = log_i - b
    m_run = jnp.where(fwd, _cummax_rows(a, False), _cummax_rows(a, True))
    if has_state:
        m0 = m0_ref[...]
        m_run = jnp.maximum(m_run, m0)
        w_inter = jnp.exp(m0 - m_run)
    else:
        m_run = jnp.maximum(m_run, 0.0)
    inv_floor = jnp.exp(-(b + m_run))
    a_t = a.T
    fwd_row = fwd[0:1, :]
    b_last = jnp.where(fwd_row, b[seq - 1:seq, :], b[0:1, :])
    m_last = jnp.where(fwd_row, m_run[seq - 1:seq, :], m_run[0:1, :])
    if emit_state:
        mn_ref[...] = b_last + m_last

    ones_col = jnp.ones((seq, LANES), F32)
    for h in range(NH_A):
        cq = slice(h * DK_A, (h + 1) * DK_A)
        ck = slice(A_QK + h * DK_A, A_QK + (h + 1) * DK_A)
        cv = slice(h * DV_A, (h + 1) * DV_A)
        q = _dwconv_silu(qk_ref[:, cq], cw_ref[:, cq], cb_ref[:, cq])
        k = _dwconv_silu(qk_ref[:, ck], cw_ref[:, ck], cb_ref[:, ck]) * (DK_A ** -0.5)
        qb = q.astype(BF16)
        kb = k.astype(BF16)
        vh = v_ref[:, cv]
        vaug = jnp.concatenate([vh, ones_col], axis=1).astype(BF16)
        for r0 in range(0, seq, Q_TILE):
            rows = slice(r0, r0 + Q_TILE)
            s = _dot_nt(qb[rows], kb)
            hsum = None
            for d in range(2):
                c = d * NH_A + h
                k0, k1 = (0, r0 + Q_TILE) if d == 0 else (r0, seq)
                expo = _causal_exponent(a_t[c:c + 1, k0:k1] - m_run[rows, c:c + 1], r0, k0, d == 1)
                p = (s[:, k0:k1] * jnp.exp(expo)).astype(BF16)
                acc = _dot(p, vaug[k0:k1])
                if has_state:
                    acc = acc + w_inter[rows, c:c + 1] * _dot(qb[rows], c0_ref[d, h].astype(BF16))
                num = acc[:, 0:DV_A]
                den = acc[:, DV_A:]
                hd = num / jnp.maximum(jnp.abs(den), inv_floor[rows, c:c + 1])
                hsum = hd if hsum is None else hsum + hd
            og = jax.nn.sigmoid(o_ref[rows, cv]) * hsum
            ha_ref[rows, cv] = _rms(og, anw_ref[:, cv])
        if emit_state:
            for d in range(2):
                c = d * NH_A + h
                kw = k * jnp.exp(a[:, c:c + 1] - m_last[:, c:c + 1])
                cn_ref[d, h] = _dot(kw.T.astype(BF16), vh.astype(BF16))
                nn_ref[d, h:h + 1, :] = jnp.sum(kw, axis=0, keepdims=True)


def _mlstm(qk, v, o, g, conv_w, conv_b, gate_bias, norm_w, seq, state=None, sub_layer=0, carried=None):
    n = qk.shape[0]
    nseq = n // seq
    has_state = state is not None
    emit_state = not has_state
    aliases = {}
    row = lambda s: (s, 0)
    fixed = lambda s: (0, 0)
    in_specs = [pl.BlockSpec((seq, 2 * A_QK), row), pl.BlockSpec((seq, A_V), row), pl.BlockSpec((seq, A_V), row),
                pl.BlockSpec((seq, 3 * LANES), row),
                pl.BlockSpec((CONV_K, 2 * A_QK), fixed), pl.BlockSpec((1, 2 * A_QK), fixed),
                pl.BlockSpec((2, LANES), fixed), pl.BlockSpec((1, A_V), fixed)]
    args = [qk, v, o, g, conv_w, conv_b, gate_bias, norm_w]
    out_specs = [pl.BlockSpec((seq, A_V), row)]
    out_shape = [jax.ShapeDtypeStruct((n, A_V), F32)]
    if has_state:
        c0_aug, m0 = state
        in_specs += [pl.BlockSpec((None, 2, NH_A, DK_A, 2 * LANES), lambda s: (s, 0, 0, 0, 0)),
                     pl.BlockSpec((None, 1, LANES), lambda s: (s, 0, 0))]
        args += [c0_aug, m0]
    if emit_state:
        if carried is None:
            out_specs += [pl.BlockSpec((None, N_EVEN, 2, NH_A, DK_A, DV_A), lambda s: (s, 0, 0, 0, 0, 0)),
                          pl.BlockSpec((None, N_EVEN, 2, NH_A, DK_A), lambda s: (s, 0, 0, 0, 0))]
        else:
            out_specs += [pl.BlockSpec((None, None, 2, NH_A, DK_A, DV_A), lambda s: (s, sub_layer, 0, 0, 0, 0)),
                          pl.BlockSpec((None, None, 2, NH_A, DK_A), lambda s: (s, sub_layer, 0, 0, 0))]
        out_specs += [pl.BlockSpec((None, 1, LANES), lambda s: (s, 0, 0))]
        out_shape += [jax.ShapeDtypeStruct((nseq, N_EVEN, 2, NH_A, DK_A, DV_A), F32),
                      jax.ShapeDtypeStruct((nseq, N_EVEN, 2, NH_A, DK_A), F32),
                      jax.ShapeDtypeStruct((nseq, 1, LANES), F32)]
        if carried is not None:
            aliases = {len(args): 1, len(args) + 1: 2}
            in_specs += [pl.BlockSpec(memory_space=pl.ANY)] * 2
            args += list(carried)
    return pl.pallas_call(
        functools.partial(_mlstm_kernel, seq=seq, has_state=has_state, emit_state=emit_state,
                          n_carried=len(aliases), slot=sub_layer),
        grid=(nseq,), in_specs=in_specs, out_specs=out_specs, out_shape=out_shape,
        input_output_aliases=aliases, compiler_params=_params("arbitrary"), name="mlstm",
    )(*args)


def _ssd_kernel(*refs, seq, has_state, emit_state, n_carried, slot):
    assert not (has_state and emit_state)
    it = iter(refs)
    xbc_ref, z_ref, g_ref, cw_ref, cb_ref, dtb_ref, alog_ref, dsk_ref, bnw_ref = (next(it) for _ in range(9))
    if has_state:
        s0_ref = next(it)
    for _ in range(n_carried):
        next(it)
    yb_ref = next(it)
    if emit_state:
        sn_ref = next(it)
        if n_carried == 0:
            for other in range(N_EVEN):
                if other != slot:
                    sn_ref[other] = jnp.zeros(sn_ref.shape[1:], F32)
            sn_ref = sn_ref.at[slot]

    lane = _lane_iota((seq, LANES))
    fwd = lane < NH_B
    dt = _softplus(g_ref[:, 2 * LANES:3 * LANES] + dtb_ref[...])
    da = dt * (-jnp.exp(alog_ref[...]))
    pre = _cumsum_rows(da)
    total = pre[seq - 1:seq, :]
    acum = jnp.where(fwd, pre, total - pre + da)
    acum_t = acum.T
    if has_state:
        carry_in = jnp.exp(acum)
    if emit_state:
        a_last = jnp.where(fwd[0:1, :], acum[seq - 1:seq, :], acum[0:1, :])
        w_state = jnp.exp(a_last - acum) * dt

    gw = R_B * HP_B
    for g in range(NG_B):
        cx = slice(g * gw, (g + 1) * gw)
        cb_ = slice(B_INNER + g * DSTATE, B_INNER + (g + 1) * DSTATE)
        cc = slice(B_INNER + B_BC + g * DSTATE, B_INNER + B_BC + (g + 1) * DSTATE)
        xg = _dwconv_silu(xbc_ref[:, cx], cw_ref[:, cx], cb_ref[:, cx])
        bg = _dwconv_silu(xbc_ref[:, cb_], cw_ref[:, cb_], cb_ref[:, cb_]).astype(BF16)
        cg = _dwconv_silu(xbc_ref[:, cc], cw_ref[:, cc], cb_ref[:, cc]).astype(BF16)
        u = [[(xg[:, r * HP_B:(r + 1) * HP_B] * dt[:, d * NH_B + g * R_B + r:d * NH_B + g * R_B + r + 1]).astype(BF16)
              for r in range(R_B)] for d in range(2)]
        for r0 in range(0, seq, Q_TILE):
            rows = slice(r0, r0 + Q_TILE)
            cb_scores = _dot_nt(cg[rows], bg)
            ys = []
            for r in range(R_B):
                h = g * R_B + r
                yh = None
                for d in range(2):
                    c = d * NH_B + h
                    k0, k1 = (0, r0 + Q_TILE) if d == 0 else (r0, seq)
                    expo = _causal_exponent(acum[rows, c:c + 1] - acum_t[c:c + 1, k0:k1], r0, k0, d == 1)
                    w = (cb_scores[:, k0:k1] * jnp.exp(expo)).astype(BF16)
                    yd = _dot(w, u[d][r][k0:k1])
                    if has_state:
                        yd = yd + carry_in[rows, c:c + 1] * _dot_nt(cg[rows], s0_ref[d, h].astype(BF16))
                    yh = yd if yh is None else yh + yd
                ys.append(yh)
            y = jnp.concatenate(ys, axis=1) + dsk_ref[:, cx] * xg[rows]
            y = y * _silu(z_ref[rows, cx])
            yb_ref[rows, cx] = _rms(y, bnw_ref[:, cx])
        if emit_state:
            for d in range(2):
                xw = jnp.concatenate(
                    [xg[:, r * HP_B:(r + 1) * HP_B] * w_state[:, d * NH_B + g * R_B + r:d * NH_B + g * R_B + r + 1]
                     for r in range(R_B)], axis=1)
                sn = _dot(xw.T.astype(BF16), bg)
                for r in range(R_B):
                    sn_ref[d, g * R_B + r] = sn[r * HP_B:(r + 1) * HP_B, :]


def _ssd(xbc, z, g, conv_w, conv_b, dt_bias, a_log, d_skip, norm_w, seq, state=None, sub_layer=0, carried=None):
    n = xbc.shape[0]
    nseq = n // seq
    has_state = state is not None
    emit_state = not has_state
    aliases = {}
    row = lambda s: (s, 0)
    fixed = lambda s: (0, 0)
    in_specs = [pl.BlockSpec((seq, B_XBC), row), pl.BlockSpec((seq, B_INNER), row), pl.BlockSpec((seq, 3 * LANES), row),
                pl.BlockSpec((CONV_K, B_XBC), fixed), pl.BlockSpec((1, B_XBC), fixed),
                pl.BlockSpec((1, LANES), fixed), pl.BlockSpec((1, LANES), fixed),
                pl.BlockSpec((1, B_INNER), fixed), pl.BlockSpec((1, B_INNER), fixed)]
    args = [xbc, z, g, conv_w, conv_b, dt_bias, a_log, d_skip, norm_w]
    out_specs = [pl.BlockSpec((seq, B_INNER), row)]
    out_shape = [jax.ShapeDtypeStruct((n, B_INNER), F32)]
    state_spec = pl.BlockSpec((None, None, 2, NH_B, HP_B, DSTATE), lambda s: (s, sub_layer, 0, 0, 0, 0))
    if has_state:
        in_specs.append(state_spec)
        args.append(state)
    if emit_state:
        out_specs.append(state_spec if carried is not None else
                         pl.BlockSpec((None, N_EVEN, 2, NH_B, HP_B, DSTATE), lambda s: (s, 0, 0, 0, 0, 0)))
        out_shape.append(jax.ShapeDtypeStruct((nseq, N_EVEN, 2, NH_B, HP_B, DSTATE), F32))
        if carried is not None:
            aliases = {len(args): 1}
            in_specs.append(pl.BlockSpec(memory_space=pl.ANY))
            args.append(carried)
    return pl.pallas_call(
        functools.partial(_ssd_kernel, seq=seq, has_state=has_state, emit_state=emit_state, n_carried=len(aliases),
                          slot=sub_layer),
        grid=(nseq,), in_specs=in_specs, out_specs=out_specs, out_shape=out_shape,
        input_output_aliases=aliases, compiler_params=_params("arbitrary"), name="ssd",
    )(*args)


def _pair_split(x):
    first = _lane_iota(x.shape) < HD_C
    zero = jnp.zeros_like(x)
    return jnp.concatenate([jnp.where(first, x, zero), jnp.where(first, zero, x)], axis=0)


def _shared_split(x, x_swapped, kh):
    first = _lane_iota(x.shape) < HD_C
    zero = jnp.zeros_like(x)
    top, bottom = (x, x_swapped) if kh == 0 else (x_swapped, x)
    return jnp.concatenate([jnp.where(first, top, zero), jnp.where(first, zero, bottom)], axis=0)


def _pair_attention(q, kbd, vbd, sinks=None, valid=None):
    s = _dot_nt(q, kbd)
    n_keys = kbd.shape[0] // 2
    probs, maxes = [], []
    for i in range(2):
        si = s[:, i * n_keys:(i + 1) * n_keys]
        if valid is not None:
            si = jnp.where(valid, si, -jnp.inf)
        m = jnp.max(si, axis=1, keepdims=True)
        if sinks is not None:
            m = jnp.maximum(m, sinks[i])
        probs.append(jnp.exp(si - m))
        maxes.append(m)
    o = _dot(jnp.concatenate(probs, axis=1).astype(BF16), vbd)
    den = o[:, LANES:]
    if sinks is not None:
        first = _lane_iota(den.shape) < HD_C
        den = den + jnp.where(first, jnp.exp(sinks[0] - maxes[0]), jnp.exp(sinks[1] - maxes[1]))
    return o[:, :LANES] / den


def _pair_sinks(sink_ref, n):
    return sink_ref[0:1, n:n + 1], sink_ref[0:1, n + 1:n + 2]


def _mla_queries(qa_ref, qan_ref, wqb_ref):
    return _dot(_rms(qa_ref[...], qan_ref[...]).astype(BF16), wqb_ref[...]) * MLA_SCALE


def _attn_ctx_kernel(qc_ref, kc_ref, vc_ref, qa_ref, kva_ref, kpe_ref, sink_ref, qan_ref, kvn_ref, wqb_ref, wkvb_ref,
                     oc_ref, od_ref, ckv_ref, *, seq):
    ones_bd = _pair_split(jnp.ones((seq, LANES), F32))
    kc, vc = kc_ref[...], vc_ref[...]
    kc_sw, vc_sw = pltpu.roll(kc, HD_C, 1), pltpu.roll(vc, HD_C, 1)
    for kh in range(NKV_C):
        kbd = _shared_split(kc, kc_sw, kh).astype(BF16)
        vbd = jnp.concatenate([_shared_split(vc, vc_sw, kh), ones_bd], axis=1).astype(BF16)
        for n in range(kh * G_C, (kh + 1) * G_C, 2):
            cols = slice(n * HD_C, (n + 2) * HD_C)
            q = (qc_ref[:, cols] * (HD_C ** -0.5)).astype(BF16)
            oc_ref[:, cols] = _pair_attention(q, kbd, vbd, sinks=_pair_sinks(sink_ref, n))

    qd = _mla_queries(qa_ref, qan_ref, wqb_ref)
    ckv = _rms(kva_ref[...], kvn_ref[...])
    ckv_ref[...] = ckv
    kv = _dot(ckv.astype(BF16), wkvb_ref[...])
    kpe = kpe_ref[...]
    kpe_bd = jnp.concatenate([kpe, pltpu.roll(kpe, ROPE_D, 1)], axis=0)
    nope_w = NH_D * NOPE_D
    for i in range(NH_D // 2):
        cols = slice(i * LANES, (i + 1) * LANES)
        vcols = slice(nope_w + i * LANES, nope_w + (i + 1) * LANES)
        q = jnp.concatenate([qd[:, cols], qd[:, vcols]], axis=1).astype(BF16)
        kbd = jnp.concatenate([_pair_split(kv[:, cols]), kpe_bd], axis=1).astype(BF16)
        vbd = jnp.concatenate([_pair_split(kv[:, vcols]), ones_bd], axis=1).astype(BF16)
        od_ref[:, cols] = _pair_attention(q, kbd, vbd)


def _attn_ctx(qc, kc, vc, qa, kva, kpe, sink, qan, kvn, wqb, wkvb, seq):
    n = qc.shape[0]
    row = lambda s: (s, 0)
    fixed = lambda s: (0, 0)
    widths = ODD_WIDTHS
    in_specs = [pl.BlockSpec((seq, wd), row) for wd in widths]
    in_specs += [pl.BlockSpec(a.shape, fixed) for a in (sink, qan, kvn, wqb, wkvb)]
    half = NH_C * HD_C
    return pl.pallas_call(
        functools.partial(_attn_ctx_kernel, seq=seq),
        grid=(n // seq,), in_specs=in_specs,
        out_specs=[pl.BlockSpec((seq, half), row), pl.BlockSpec((seq, half), row), pl.BlockSpec((seq, KV_RANK), row)],
        out_shape=[jax.ShapeDtypeStruct((n, half), F32), jax.ShapeDtypeStruct((n, half), F32),
                   jax.ShapeDtypeStruct((n, KV_RANK), F32)],
        compiler_params=_params("arbitrary"), name="attn_ctx",
    )(qc, kc, vc, qa, kva, kpe, sink, qan, kvn, wqb, wkvb)


def _rope(x, cos, sin, half):
    parts = []
    lane = _lane_iota((x.shape[0], LANES))
    first = (lane & (2 * half - 1)) < half
    for i in range(x.shape[1] // LANES):
        xi = x[:, i * LANES:(i + 1) * LANES]
        partner = jnp.where(first, -pltpu.roll(xi, LANES - half, 1), pltpu.roll(xi, half, 1))
        parts.append(xi * cos + partner * sin)
    return parts[0] if len(parts) == 1 else jnp.concatenate(parts, axis=1)


def _attn_lat_kernel(qc_ref, qa_ref, ropeq_ref, kc_ref, vc_ref, kva_ref, kpe_ref, kctx_ref, vctx_ref, ckvctx_ref,
                     kpectx_ref, rope_ref, sink_ref, qan_ref, kvn_ref, wqb_ref, wkvb_ref, oc_ref, od_ref,
                     kwin_s, vwin_s, kext_s, vext_s, *, seq, past):
    qi = pl.program_id(1)
    nope_w = NH_D * NOPE_D
    n_all = past + seq
    ctx0 = 2 * WINDOW + seq

    @pl.when(qi == 0)
    def _():
        zeros = jnp.zeros((WINDOW, LANES), BF16)
        for ref, lat, ctx in ((kwin_s, _rope(kc_ref[...], rope_ref[0], rope_ref[1], HD_C // 2), kctx_ref[...]),
                              (vwin_s, vc_ref[...], vctx_ref[...])):
            lat_sw, ctx_sw = pltpu.roll(lat, HD_C, 1), pltpu.roll(ctx, HD_C, 1)
            for kh in range(NKV_C):
                lat_bd = _shared_split(lat, lat_sw, kh).astype(BF16)
                ctx_bd = _shared_split(ctx, ctx_sw, kh).astype(BF16)
                for i in range(2):
                    ref[kh, i, 0:WINDOW, :] = zeros
                    ref[kh, i, WINDOW:WINDOW + seq, :] = lat_bd[i * seq:(i + 1) * seq]
                    ref[kh, i, WINDOW + seq:ctx0, :] = zeros
                    ref[kh, i, ctx0:, :] = ctx_bd[i * past:(i + 1) * past]
        ckv = _rms(kva_ref[...], kvn_ref[...])
        kv = jnp.concatenate([_dot(ckvctx_ref[...].astype(BF16), wkvb_ref[...]),
                              _dot(ckv.astype(BF16), wkvb_ref[...])], axis=0)
        kpe = jnp.concatenate([kpectx_ref[...], _rope(kpe_ref[...], rope_ref[2], rope_ref[3], ROPE_D // 2)], axis=0)
        kpe_bd = jnp.concatenate([kpe, pltpu.roll(kpe, ROPE_D, 1)], axis=0).astype(BF16)
        ones_bd = _pair_split(jnp.ones((n_all, LANES), F32)).astype(BF16)
        for i in range(NH_D // 2):
            kext_s[i, :, 0:LANES] = _pair_split(kv[:, i * LANES:(i + 1) * LANES]).astype(BF16)
            kext_s[i, :, LANES:] = kpe_bd
            vext_s[i, :, 0:LANES] = _pair_split(kv[:, nope_w + i * LANES:nope_w + (i + 1) * LANES]).astype(BF16)
            vext_s[i, :, LANES:] = ones_bd

    r0 = pl.multiple_of(qi * Q_TILE, Q_TILE)
    nloc = Q_TILE + 2 * WINDOW
    n_keys = nloc + past
    qr = _rope(qc_ref[...], ropeq_ref[0], ropeq_ref[1], HD_C // 2) * (HD_C ** -0.5)
    ti = r0 + _row_iota((Q_TILE, n_keys))
    col = _lane_iota((Q_TILE, n_keys))
    pos = r0 - WINDOW + col
    valid = (col >= nloc) | ((jnp.abs(ti - pos) <= WINDOW) & (pos >= 0) & (pos < seq))
    ones_bd = _pair_split(jnp.ones((n_keys, LANES), F32)).astype(BF16)
    for kh in range(NKV_C):
        kbd = jnp.concatenate([kwin_s[kh, 0, pl.ds(r0, nloc), :], kwin_s[kh, 0, ctx0:, :],
                               kwin_s[kh, 1, pl.ds(r0, nloc), :], kwin_s[kh, 1, ctx0:, :]], axis=0)
        vals = jnp.concatenate([vwin_s[kh, 0, pl.ds(r0, nloc), :], vwin_s[kh, 0, ctx0:, :],
                                vwin_s[kh, 1, pl.ds(r0, nloc), :], vwin_s[kh, 1, ctx0:, :]], axis=0)
        vbd = jnp.concatenate([vals, ones_bd], axis=1)
        for n in range(kh * G_C, (kh + 1) * G_C, 2):
            cols = slice(n * HD_C, (n + 2) * HD_C)
            oc_ref[:, cols] = _pair_attention(qr[:, cols].astype(BF16), kbd, vbd, sinks=_pair_sinks(sink_ref, n),
                                              valid=valid)

    qd = _mla_queries(qa_ref, qan_ref, wqb_ref)
    q_pe = _rope(qd[:, nope_w:], ropeq_ref[2], ropeq_ref[3], ROPE_D // 2)
    for i in range(NH_D // 2):
        cols = slice(i * LANES, (i + 1) * LANES)
        q = jnp.concatenate([qd[:, cols], q_pe[:, cols]], axis=1).astype(BF16)
        od_ref[:, cols] = _pair_attention(q, kext_s[i], vext_s[i])


def _attn_lat(qc, kc, vc, qa, kva, kpe, kctx, vctx, ckvctx, kpectx, rope, sink, qan, kvn, wqb, wkvb, seq):
    n = qc.shape[0]
    past = kctx.shape[1]
    nq = seq // Q_TILE
    qrow = lambda b, q: (b * nq + q, 0)
    krow = lambda b, q: (b, 0)
    fixed = lambda b, q: (0, 0)
    kvw = NKV_C * HD_C
    in_specs = [pl.BlockSpec((Q_TILE, NH_C * HD_C), qrow), pl.BlockSpec((Q_TILE, Q_RANK), qrow),
                pl.BlockSpec((4, Q_TILE, LANES), lambda b, q: (0, q, 0)),
                pl.BlockSpec((seq, kvw), krow), pl.BlockSpec((seq, kvw), krow),
                pl.BlockSpec((seq, KV_RANK), krow), pl.BlockSpec((seq, LANES), krow)]
    in_specs += [pl.BlockSpec((None, past, a.shape[2]), lambda b, q: (b, 0, 0)) for a in (kctx, vctx, ckvctx, kpectx)]
    in_specs += [pl.BlockSpec(rope.shape, lambda b, q: (0, 0, 0))]
    in_specs += [pl.BlockSpec(a.shape, fixed) for a in (sink, qan, kvn, wqb, wkvb)]
    half = NH_C * HD_C
    win_rows = 2 * WINDOW + seq + past
    return pl.pallas_call(
        functools.partial(_attn_lat_kernel, seq=seq, past=past),
        grid=(n // seq, nq), in_specs=in_specs,
        out_specs=[pl.BlockSpec((Q_TILE, half), qrow), pl.BlockSpec((Q_TILE, half), qrow)],
        out_shape=[jax.ShapeDtypeStruct((n, half), F32), jax.ShapeDtypeStruct((n, half), F32)],
        scratch_shapes=[pltpu.VMEM((NKV_C, 2, win_rows, LANES), BF16), pltpu.VMEM((NKV_C, 2, win_rows, LANES), BF16),
                        pltpu.VMEM((NH_D // 2, 2 * (past + seq), 2 * LANES), BF16),
                        pltpu.VMEM((NH_D // 2, 2 * (past + seq), 2 * LANES), BF16)],
        compiler_params=_params("arbitrary", "arbitrary"), name="attn_lat",
    )(qc, qa, rope, kc, vc, kva, kpe, kctx, vctx, ckvctx, kpectx, rope, sink, qan, kvn, wqb, wkvb)


def _pad_lanes(x, width=LANES):
    return jnp.pad(x, [(0, 0)] * (x.ndim - 1) + [(0, width - x.shape[-1])])


def _even_in_weight(w):
    o0 = 2 * A_QK + 2 * A_V
    ig, fg = w[:, o0:o0 + 2 * NH_A], w[:, o0 + 2 * NH_A:o0 + 4 * NH_A]
    z0 = o0 + 4 * NH_A
    x0 = z0 + B_INNER
    d0 = x0 + B_XBC
    return jnp.concatenate([w[:, :o0], w[:, z0:x0], w[:, x0:d0], _pad_lanes(ig), _pad_lanes(fg),
                            _pad_lanes(w[:, d0:])], axis=1).astype(BF16)


def _odd_in_weight(w):
    return _pad_lanes(w, sum(ODD_WIDTHS)).astype(BF16)


def _split_heads_weight(w, heads, first, second):
    w3 = w.reshape(w.shape[0], heads, first + second)
    return jnp.concatenate([w3[:, :, :first].reshape(w.shape[0], heads * first),
                            w3[:, :, first:].reshape(w.shape[0], heads * second)], axis=1).astype(BF16)


def _mla_query_weight(w):
    w3 = w.reshape(w.shape[0], NH_D // 2, 2, NOPE_D + ROPE_D)
    nope = w3[..., :NOPE_D].reshape(w.shape[0], NH_D * NOPE_D)
    pe = _pad_lanes(w3[..., NOPE_D:].reshape(w.shape[0], NH_D // 2, 2 * ROPE_D))
    return jnp.concatenate([nope, pe.reshape(w.shape[0], NH_D // 2 * LANES)], axis=1).astype(BF16)


def _rope_tables(rows):
    def table(rot_dim):
        quarter = rot_dim // 4
        inv = ROPE_BASE ** (-jnp.arange(quarter, dtype=F32) / quarter)
        r = jnp.repeat(jnp.arange(rows, dtype=F32), GRID_W)
        col = jnp.tile(jnp.arange(GRID_W, dtype=F32), rows)
        ang = jnp.concatenate([r[:, None] * inv, col[:, None] * inv], axis=-1)
        reps = LANES // (rot_dim // 2)
        return jnp.tile(jnp.cos(ang), (1, reps)), jnp.tile(jnp.sin(ang), (1, reps))
    cos_c, sin_c = table(HD_C)
    cos_d, sin_d = table(ROPE_D)
    return jnp.stack([cos_c, sin_c, cos_d, sin_d])


def kernel(x_prompt, x_sample, c, state_mlstm_C, state_mlstm_n, state_mlstm_m, state_ssd, cache_gqa_k, cache_gqa_v,
           cache_mla_ckv, cache_mla_kpe, c_ctx, w_ada, b_ada, norm_g, w_up, w_down, w_in_even, conv_a_w, conv_a_b,
           conv_b_w, conv_b_b, gate_b, a_norm_w, dt_bias, a_log, d_skip, b_norm_w, w_out_even, w_in_odd, sink,
           q_a_norm, kv_a_norm, w_q_b, w_kv_b, w_out_odd):
    xp = x_prompt.reshape(BATCH * SEQ, D_MODEL)
    xs = x_sample.reshape(DEC_BATCH * DEC_SEQ, D_MODEL)
    cond = jnp.concatenate([c_ctx[None, :], c, jnp.zeros((MOD_ROWS - 1 - DEC_BATCH, D_MODEL), F32)], axis=0)
    mods = _modulations(cond, w_ada, b_ada)
    rope = _rope_tables(DEC_SEQ // GRID_W)

    new_m, new_k, new_v, new_ckv, new_kpe = [], [], [], [], []
    mem_state, ssd_state = None, None
    for l in range(DEPTH):
        j = l // 2
        gain_in = norm_g[l, 0].reshape(1, D_MODEL)
        if l % 2 == 0:
            w_in = _even_in_weight(w_in_even[j])
            gate_bias = jnp.stack([_pad_lanes(gate_b[j, :2 * NH_A]), _pad_lanes(gate_b[j, 2 * NH_A:])])
            a_args = (conv_a_w[j], conv_a_b[j].reshape(1, -1), gate_bias, a_norm_w[j].reshape(1, -1))
            b_args = (conv_b_w[j], conv_b_b[j].reshape(1, -1), _pad_lanes(dt_bias[j].reshape(1, -1)),
                      _pad_lanes(a_log[j].reshape(1, -1)), jnp.repeat(d_skip[j], HP_B).reshape(1, -1),
                      b_norm_w[j].reshape(1, -1))
            qk, v, o, z, xbc, g = _project(xp, mods, l, False, gain_in, w_in, EVEN_WIDTHS)
            a1p, cn, nn, mn = _mlstm(qk, v, o, g, *a_args, seq=SEQ, sub_layer=j, carried=mem_state)
            a2p, ssd_state = _ssd(xbc, z, g, *b_args, seq=SEQ, sub_layer=j, carried=ssd_state)
            mem_state = (cn, nn)
            new_m.append(mn[:, 0, :2 * NH_A].reshape(BATCH, 2, NH_A))
            qk, v, o, z, xbc, g = _project(xs, mods, l, True, gain_in, w_in, EVEN_WIDTHS)
            n0 = state_mlstm_n[:, j][..., None]
            c0_aug = jnp.concatenate([state_mlstm_C[:, j], jnp.broadcast_to(n0, n0.shape[:-1] + (LANES,))], axis=-1)
            m0 = _pad_lanes(state_mlstm_m[:, j].reshape(DEC_BATCH, 1, 2 * NH_A))
            a1s, = _mlstm(qk, v, o, g, *a_args, seq=DEC_SEQ, state=(c0_aug, m0))
            a2s, = _ssd(xbc, z, g, *b_args, seq=DEC_SEQ, state=state_ssd, sub_layer=j)
            w_out = w_out_even
        else:
            w_in = _odd_in_weight(w_in_odd[j])
            o_args = (_pad_lanes(sink[j].reshape(1, -1)), q_a_norm[j].reshape(1, -1), kv_a_norm[j].reshape(1, -1),
                      _mla_query_weight(w_q_b[j]),
                      _split_heads_weight(w_kv_b[j], NH_D, NOPE_D, V_D))
            qc, kc, vc, qa, kva, kpe = _project(xp, mods, l, False, gain_in, w_in, ODD_WIDTHS)
            a1p, a2p, ckv = _attn_ctx(qc, kc, vc, qa, kva, kpe, *o_args, seq=SEQ)
            new_k.append(kc.reshape(BATCH, SEQ, NKV_C, HD_C))
            new_v.append(vc.reshape(BATCH, SEQ, NKV_C, HD_C))
            new_ckv.append(ckv.reshape(BATCH, SEQ, KV_RANK))
            new_kpe.append(kpe[:, :ROPE_D].reshape(BATCH, SEQ, ROPE_D))
            qc, kc, vc, qa, kva, kpe = _project(xs, mods, l, True, gain_in, w_in, ODD_WIDTHS)
            a1s, a2s = _attn_lat(qc, kc, vc, qa, kva, kpe,
                                 cache_gqa_k[:, j].reshape(DEC_BATCH, PAST_LEN, NKV_C * HD_C),
                                 cache_gqa_v[:, j].reshape(DEC_BATCH, PAST_LEN, NKV_C * HD_C),
                                 cache_mla_ckv[:, j], _pad_lanes(cache_mla_kpe[:, j]), rope, *o_args, seq=DEC_SEQ)
            w_out = w_out_odd
        gains = norm_g[l, 1:4]
        xp = _channel(a1p, a2p, xp, mods, l, False, gains, w_out, w_up, w_down)
        xs = _channel(a1s, a2s, xs, mods, l, True, gains, w_out, w_up, w_down)

    return (xp.reshape(BATCH, SEQ, D_MODEL), xs.reshape(DEC_BATCH, DEC_SEQ, D_MODEL),
            mem_state[0], mem_state[1], jnp.stack(new_m, axis=1), ssd_state,
            jnp.stack(new_k, axis=1), jnp.stack(new_v, axis=1), jnp.stack(new_ckv, axis=1),
            jnp.stack(new_kpe, axis=1))
```

```python
import functools

import jax
import jax.numpy as jnp
from jax import lax
from jax.experimental import pallas as pl
from jax.experimental.pallas import tpu as pltpu

F32 = jnp.float32
BF16 = jnp.bfloat16

D_MODEL = 1024
BATCH = 32
SEQ = 256
DEPTH = 4
DEC_BATCH = 2
DEC_SEQ = 1024
PAST_LEN = 256
GRID_W = 64
N_EVEN = (DEPTH + 1) // 2
N_ODD = DEPTH // 2
EPS = 1e-6
CONV_K = 5
NH_A = 4
DK_A = 128
DV_A = 128
A_QK = NH_A * DK_A
A_V = NH_A * DV_A
NH_B = 8
HP_B = 64
DSTATE = 128
NG_B = 2
R_B = NH_B // NG_B
B_INNER = NH_B * HP_B
B_BC = NG_B * DSTATE
B_XBC = B_INNER + 2 * B_BC
NH_C = 8
NKV_C = 2
G_C = NH_C // NKV_C
HD_C = 64
WINDOW = 128
NH_D = 8
Q_RANK = 256
KV_RANK = 128
NOPE_D = 64
ROPE_D = 32
V_D = 64
MLA_SCALE = (NOPE_D + ROPE_D) ** -0.5
D_FF = 4 * D_MODEL
ROPE_BASE = 10000.0

LANES = 128
VMEM_LIMIT_BYTES = 56 * 1024 * 1024
ROW_TILE = 512
FF_TILE = 1024
SUB_ROWS = 512
STAGE_ROWS = 512
Q_TILE = 256
ADA_TILE = 1536
MOD_ROWS = 8
GATE_ROWS = 4 * NH_A + 2 * NH_B

EVEN_WIDTHS = (2 * A_QK, A_V, A_V, B_INNER, B_XBC)
ODD_WIDTHS = (NH_C * HD_C, NKV_C * HD_C, NKV_C * HD_C, Q_RANK, KV_RANK, LANES)

_NT = (((1,), (1,)), ((), ()))


def _params(*sem):
    return pltpu.CompilerParams(dimension_semantics=sem, vmem_limit_bytes=VMEM_LIMIT_BYTES)


def _rms(x, g):
    return x * lax.rsqrt(jnp.mean(x * x, axis=-1, keepdims=True) + EPS) * g


def _silu(x):
    return x * jax.nn.sigmoid(x)


def _softplus(x):
    return jnp.maximum(x, 0.0) + jnp.log1p(jnp.exp(-jnp.abs(x)))


def _dot(a, b):
    return jnp.dot(a, b, preferred_element_type=F32)


def _dot_nt(a, b):
    return lax.dot_general(a, b, _NT, preferred_element_type=F32)


def _layer_spec(arr, layer):
    tail = arr.shape[1:]
    zeros = (0,) * len(tail)
    return pl.BlockSpec((None,) + tail, lambda *_: (layer,) + zeros)


def _ada_kernel(c_ref, w_ref, b_ref, o_ref):
    s = _silu(c_ref[...]).astype(BF16)
    o_ref[...] = _dot(s, w_ref[...].astype(BF16)) + b_ref[...]


def _modulations(cond, w_ada, b_ada):
    out = pl.pallas_call(
        _ada_kernel,
        grid=(DEPTH, 6 * D_MODEL // ADA_TILE),
        in_specs=[pl.BlockSpec((MOD_ROWS, D_MODEL), lambda l, n: (0, 0)),
                  pl.BlockSpec((None, D_MODEL, ADA_TILE), lambda l, n: (l, 0, n)),
                  pl.BlockSpec((None, 1, ADA_TILE), lambda l, n: (l, 0, n))],
        out_specs=pl.BlockSpec((None, MOD_ROWS, ADA_TILE), lambda l, n: (l, 0, n)),
        out_shape=jax.ShapeDtypeStruct((DEPTH, MOD_ROWS, 6 * D_MODEL), F32),
        compiler_params=_params("arbitrary", "arbitrary"),
        name="ada",
    )(cond, w_ada, b_ada.reshape(DEPTH, 1, 6 * D_MODEL))
    return out.reshape(DEPTH, MOD_ROWS, 6, D_MODEL)


def _mod_spec(layer, latent):
    if latent:
        per_seq = DEC_SEQ // ROW_TILE
        return pl.BlockSpec((None, None, 6, D_MODEL), lambda i, *_: (layer, 1 + i // per_seq, 0, 0))
    return pl.BlockSpec((None, None, 6, D_MODEL), lambda i, *_: (layer, 0, 0, 0))


def _proj_kernel(x_ref, mod_ref, g_ref, w_ref, *rest, widths, has_gates):
    h = _rms(x_ref[...], g_ref[0:1, :]) * (1.0 + mod_ref[1:2, :]) + mod_ref[0:1, :]
    hb = h.astype(BF16)
    o_refs = rest[1:] if has_gates else rest
    off = 0
    for o_ref, wd in zip(o_refs, widths):
        o_ref[...] = _dot(hb, w_ref[:, off:off + wd])
        off += wd
    if has_gates:
        o_refs[-1][...] = _dot_nt(rest[0][...], hb)


def _project(x, mods, layer, latent, gains, w_all, widths, w_gates_t=None):
    n = x.shape[0]
    in_specs = [pl.BlockSpec((ROW_TILE, D_MODEL), lambda i: (i, 0)), _mod_spec(layer, latent),
                _layer_spec(gains, layer), _layer_spec(w_all, layer // 2)]
    args = [x, mods, gains, w_all]
    out_specs = [pl.BlockSpec((ROW_TILE, wd), lambda i: (i, 0)) for wd in widths]
    out_shape = [jax.ShapeDtypeStruct((n, wd), F32) for wd in widths]
    if w_gates_t is not None:
        gate_rows = w_gates_t.shape[1]
        in_specs.append(_layer_spec(w_gates_t, layer // 2))
        args.append(w_gates_t)
        out_specs.append(pl.BlockSpec((gate_rows, ROW_TILE), lambda i: (0, i)))
        out_shape.append(jax.ShapeDtypeStruct((gate_rows, n), F32))
    return pl.pallas_call(
        functools.partial(_proj_kernel, widths=widths, has_gates=w_gates_t is not None),
        grid=(n // ROW_TILE,), in_specs=in_specs, out_specs=out_specs, out_shape=out_shape,
        compiler_params=_params("arbitrary"),
        name="proj",
    )(*args)


def _weight_chunks(layer, sub_layer, wo_hbm, wu_hbm, wd_hbm, wo_s, wu_s, wd_s):
    chunks = []
    for r in range(0, D_MODEL, STAGE_ROWS):
        chunks.append((wo_hbm.at[sub_layer, pl.ds(r, STAGE_ROWS), :], wo_s.at[pl.ds(r, STAGE_ROWS), :]))
    for r in range(0, D_MODEL, STAGE_ROWS):
        for c in range(0, D_FF, D_MODEL):
            chunks.append((wu_hbm.at[layer, pl.ds(r, STAGE_ROWS), pl.ds(c, D_MODEL)],
                           wu_s.at[pl.ds(r, STAGE_ROWS), pl.ds(c, D_MODEL)]))
    for r in range(0, D_FF, STAGE_ROWS):
        chunks.append((wd_hbm.at[layer, pl.ds(r, STAGE_ROWS), :], wd_s.at[pl.ds(r, STAGE_ROWS), :]))
    return chunks


def _channel_kernel(a1_ref, a2_ref, x_ref, mod_ref, g_ref, wo_hbm, wu_hbm, wd_hbm, o_ref,
                    wo_s, wu_s, wd_s, stage, sem, *, layer, sub_layer):
    half = a1_ref.shape[1]

    @pl.when(pl.program_id(0) == 0)
    def _():
        chunks = _weight_chunks(layer, sub_layer, wo_hbm, wu_hbm, wd_hbm, wo_s, wu_s, wd_s)
        copies = [pltpu.make_async_copy(src, stage.at[k % 2], sem.at[k % 2]) for k, (src, _) in enumerate(chunks)]
        copies[0].start()
        for k, (_, dst) in enumerate(chunks):
            if k + 1 < len(chunks):
                copies[k + 1].start()
            copies[k].wait()
            dst[...] = stage[k % 2].astype(BF16)

    for r0 in range(0, ROW_TILE, SUB_ROWS):
        rows = slice(r0, r0 + SUB_ROWS)
        y = (_dot(a1_ref[rows, :].astype(BF16), wo_s[0:half, :])
             + _dot(a2_ref[rows, :].astype(BF16), wo_s[half:, :]))
        x1 = x_ref[rows, :] + mod_ref[2:3, :] * _rms(y, g_ref[1:2, :])
        h = (_rms(x1, g_ref[2:3, :]) * (1.0 + mod_ref[4:5, :]) + mod_ref[3:4, :]).astype(BF16)
        acc = None
        for c in range(0, D_FF, FF_TILE):
            u = jnp.square(jnp.maximum(_dot(h, wu_s[:, c:c + FF_TILE]), 0.0)).astype(BF16)
            part = _dot(u, wd_s[c:c + FF_TILE, :])
            acc = part if acc is None else acc + part
        o_ref[rows, :] = x1 + mod_ref[5:6, :] * _rms(acc, g_ref[3:4, :])


def _channel(a1, a2, x, mods, layer, latent, gains, w_out, w_up, w_down):
    n = x.shape[0]
    half = a1.shape[1]
    row = lambda i: (i, 0)
    hbm = pl.BlockSpec(memory_space=pl.ANY)
    return pl.pallas_call(
        functools.partial(_channel_kernel, layer=layer, sub_layer=layer // 2),
        grid=(n // ROW_TILE,),
        in_specs=[pl.BlockSpec((ROW_TILE, half), row), pl.BlockSpec((ROW_TILE, half), row),
                  pl.BlockSpec((ROW_TILE, D_MODEL), row), _mod_spec(layer, latent),
                  _layer_spec(gains, layer), hbm, hbm, hbm],
        out_specs=pl.BlockSpec((ROW_TILE, D_MODEL), row),
        out_shape=jax.ShapeDtypeStruct((n, D_MODEL), F32),
        scratch_shapes=[pltpu.VMEM((D_MODEL, D_MODEL), BF16), pltpu.VMEM((D_MODEL, D_FF), BF16),
                        pltpu.VMEM((D_FF, D_MODEL), BF16), pltpu.VMEM((2, STAGE_ROWS, D_MODEL), F32),
                        pltpu.SemaphoreType.DMA((2,))],
        compiler_params=_params("arbitrary"),
        name="channel",
    )(a1, a2, x, mods, gains, w_out, w_up, w_down)


def _row_iota(shape):
    return lax.broadcasted_iota(jnp.int32, shape, 0)


def _lane_iota(shape):
    return lax.broadcasted_iota(jnp.int32, shape, 1)


def _pair_lanes(shape):
    return _lane_iota(shape) < LANES // 2


def _cumsum_lanes(x, n_fwd):
    t = x.shape[1]
    si, ti = _row_iota((t, t)), _lane_iota((t, t))
    upper = jnp.where(si <= ti, 1.0, 0.0).astype(BF16)
    lower = jnp.where(si >= ti, 1.0, 0.0).astype(BF16)
    hi = x.astype(BF16)
    rest = x - hi.astype(F32)
    mid = rest.astype(BF16)
    lo = (rest - mid.astype(F32)).astype(BF16)
    pre = _dot(hi, upper) + _dot(mid, upper) + _dot(lo, upper)
    suf = _dot(hi, lower) + _dot(mid, lower) + _dot(lo, lower)
    return jnp.where(_row_iota(x.shape) < n_fwd, pre, suf)


def _cummax_lanes(x, n_fwd):
    t = x.shape[1]
    lane = _lane_iota(x.shape)
    pre, suf = x, x
    k = 1
    while k < t:
        pre = jnp.maximum(pre, jnp.where(lane >= k, pltpu.roll(pre, k, 1), -jnp.inf))
        suf = jnp.maximum(suf, jnp.where(lane < t - k, pltpu.roll(suf, t - k, 1), -jnp.inf))
        k *= 2
    return jnp.where(_row_iota(x.shape) < n_fwd, pre, suf)


def _columns(row_arrays):
    t = row_arrays[0].shape[1]
    used = sum(a.shape[0] for a in row_arrays)
    return jnp.concatenate(list(row_arrays) + [jnp.zeros((LANES - used, t), F32)], axis=0).T


def _dwconv_silu(x, w, b):
    t = x.shape[0]
    row = _row_iota(x.shape)
    acc = x * w[CONV_K // 2:CONV_K // 2 + 1, :] + b
    for j in range(CONV_K):
        d = j - CONV_K // 2
        if d == 0:
            continue
        shifted = pltpu.roll(x, (-d) % t, 0)
        valid = (row >= -d) if d < 0 else (row < t - d)
        acc = acc + jnp.where(valid, shifted, 0.0) * w[j:j + 1, :]
    return _silu(acc)


def _causal_exponent(expo, r0, k0, reverse):
    ti = r0 + _row_iota(expo.shape)
    si = k0 + _lane_iota(expo.shape)
    keep = (si >= ti) if reverse else (si <= ti)
    return jnp.where(keep, expo, -jnp.inf)


def _pair_split(x):
    first = _pair_lanes(x.shape)
    zero = jnp.zeros_like(x)
    return jnp.concatenate([jnp.where(first, x, zero), jnp.where(first, zero, x)], axis=0)


def _key_range(d, r0, seq):
    return (0, r0 + Q_TILE) if d == 0 else (r0, seq)


def _mlstm_kernel(*refs, seq, has_state, emit_state, n_carried, slot):
    assert not (has_state and emit_state)
    it = iter(refs)
    qk_ref, v_ref, o_ref, g_ref, cw_ref, cb_ref, gb_ref, anw_ref = (next(it) for _ in range(8))
    if has_state:
        c0_ref, m0_ref = next(it), next(it)
    for _ in range(n_carried):
        next(it)
    ha_ref = next(it)
    if emit_state:
        cn_ref, nn_ref, mn_ref = next(it), next(it), next(it)
        if n_carried == 0:
            for other in range(N_EVEN):
                if other != slot:
                    cn_ref[other] = jnp.zeros(cn_ref.shape[1:], F32)
                    nn_ref[other] = jnp.zeros(nn_ref.shape[1:], F32)
                    mn_ref[other] = jnp.zeros(mn_ref.shape[1:], F32)
            cn_ref, nn_ref, mn_ref = cn_ref.at[slot], nn_ref.at[slot], mn_ref.at[slot]

    n_ch = 2 * NH_A
    log_i = g_ref[0:n_ch, :] + gb_ref[0:n_ch, 0:1]
    f_pre = g_ref[n_ch:2 * n_ch, :] + gb_ref[n_ch:2 * n_ch, 0:1]
    log_f = jnp.minimum(f_pre, 0.0) - jnp.log1p(jnp.exp(-jnp.abs(f_pre)))
    b = _cumsum_lanes(log_f, NH_A)
    a = log_i - b
    m_run = _cummax_lanes(a, NH_A)
    if has_state:
        m0 = m0_ref[:, 0:1]
        m_run = jnp.maximum(m_run, m0)
    else:
        m_run = jnp.maximum(m_run, 0.0)
    by_time = [m_run, jnp.exp(-(b + m_run))]
    if has_state:
        by_time.append(jnp.exp(m0 - m_run))
    if emit_state:
        fwd = _row_iota((n_ch, 1)) < NH_A
        b_last = jnp.where(fwd, b[:, seq - 1:seq], b[:, 0:1])
        m_last = jnp.where(fwd, m_run[:, seq - 1:seq], m_run[:, 0:1])
        mn_ref[...] = jnp.broadcast_to(b_last + m_last, (n_ch, LANES))
        by_time.append(jnp.exp(a - m_last))
    cols = _columns(by_time)

    ones = jnp.ones((seq, LANES), F32)
    for h in range(NH_A):
        cq = slice(h * DK_A, (h + 1) * DK_A)
        ck = slice(A_QK + h * DK_A, A_QK + (h + 1) * DK_A)
        cv = slice(h * DV_A, (h + 1) * DV_A)
        q = _dwconv_silu(qk_ref[:, cq], cw_ref[:, cq], cb_ref[:, cq])
        k = _dwconv_silu(qk_ref[:, ck], cw_ref[:, ck], cb_ref[:, ck]) * (DK_A ** -0.5)
        qb = q.astype(BF16)
        kb = k.astype(BF16)
        vh = v_ref[:, cv]
        vaug = jnp.concatenate([vh, ones], axis=1).astype(BF16)
        for r0 in range(0, seq, Q_TILE):
            rows = slice(r0, r0 + Q_TILE)
            s = _dot_nt(qb[rows], kb)
            hsum = None
            for d in range(2):
                c = d * NH_A + h
                k0, k1 = _key_range(d, r0, seq)
                expo = _causal_exponent(a[c:c + 1, k0:k1] - cols[rows, c:c + 1], r0, k0, d == 1)
                p = (s[:, k0:k1] * jnp.exp(expo)).astype(BF16)
                acc = _dot(p, vaug[k0:k1])
                if has_state:
                    acc = acc + cols[rows, 2 * n_ch + c:2 * n_ch + c + 1] * _dot(qb[rows], c0_ref[d, h].astype(BF16))
                hd = acc[:, 0:DV_A] / jnp.maximum(jnp.abs(acc[:, DV_A:]), cols[rows, n_ch + c:n_ch + c + 1])
                hsum = hd if hsum is None else hsum + hd
            og = jax.nn.sigmoid(o_ref[rows, cv]) * hsum
            ha_ref[rows, cv] = _rms(og, anw_ref[:, cv])
        if emit_state:
            for d in range(2):
                c = d * NH_A + h
                kw = k * cols[:, 2 * n_ch + c:2 * n_ch + c + 1]
                cn_ref[d, h] = _dot(kw.T.astype(BF16), vh.astype(BF16))
                nn_ref[d, h:h + 1, :] = jnp.sum(kw, axis=0, keepdims=True)


def _mlstm(qk, v, o, gates, params, seq, sub_layer, state=None, carried=None):
    n = qk.shape[0]
    nseq = n // seq
    has_state = state is not None
    emit_state = not has_state
    aliases = {}
    row = lambda s: (s, 0)
    in_specs = [pl.BlockSpec((seq, 2 * A_QK), row), pl.BlockSpec((seq, A_V), row), pl.BlockSpec((seq, A_V), row),
                pl.BlockSpec((GATE_ROWS, seq), lambda s: (0, s))]
    in_specs += [_layer_spec(p, sub_layer) for p in params]
    args = [qk, v, o, gates, *params]
    out_specs = [pl.BlockSpec((seq, A_V), row)]
    out_shape = [jax.ShapeDtypeStruct((n, A_V), F32)]
    if has_state:
        c0_aug, m0 = state
        in_specs += [pl.BlockSpec((None, None) + c0_aug.shape[2:], lambda s: (s, sub_layer, 0, 0, 0, 0)),
                     pl.BlockSpec((None, None) + m0.shape[2:], lambda s: (s, sub_layer, 0, 0))]
        args += [c0_aug, m0]
    if emit_state:
        shapes = [(2, NH_A, DK_A, DV_A), (2, NH_A, DK_A), (2 * NH_A, LANES)]
        for shp in shapes:
            zeros = (0,) * len(shp)
            if carried is None:
                out_specs.append(pl.BlockSpec((None, N_EVEN) + shp, lambda s, z=zeros: (s, 0) + z))
            else:
                out_specs.append(pl.BlockSpec((None, None) + shp, lambda s, z=zeros: (s, sub_layer) + z))
            out_shape.append(jax.ShapeDtypeStruct((nseq, N_EVEN) + shp, F32))
        if carried is not None:
            aliases = {len(args) + i: 1 + i for i in range(len(shapes))}
            in_specs += [pl.BlockSpec(memory_space=pl.ANY)] * len(shapes)
            args += list(carried)
    return pl.pallas_call(
        functools.partial(_mlstm_kernel, seq=seq, has_state=has_state, emit_state=emit_state,
                          n_carried=len(aliases), slot=sub_layer),
        grid=(nseq,), in_specs=in_specs, out_specs=out_specs, out_shape=out_shape,
        input_output_aliases=aliases, compiler_params=_params("arbitrary"), name="mlstm",
    )(*args)


def _ssd_kernel(*refs, seq, has_state, emit_state, n_carried, slot):
    assert not (has_state and emit_state)
    it = iter(refs)
    xbc_ref, z_ref, g_ref, cw_ref, cb_ref, dtb_ref, alog_ref, dsk_ref, bnw_ref = (next(it) for _ in range(9))
    if has_state:
        s0_ref = next(it)
    for _ in range(n_carried):
        next(it)
    yb_ref = next(it)
    if emit_state:
        sn_ref = next(it)
        if n_carried == 0:
            for other in range(N_EVEN):
                if other != slot:
                    sn_ref[other] = jnp.zeros(sn_ref.shape[1:], F32)
            sn_ref = sn_ref.at[slot]

    n_ch = 2 * NH_B
    dt = _softplus(g_ref[4 * NH_A:4 * NH_A + n_ch, :] + dtb_ref[:, 0:1])
    acum = _cumsum_lanes(dt * (-jnp.exp(alog_ref[:, 0:1])), NH_B)
    key_shift = acum - jnp.log(dt)
    by_time = [acum]
    if has_state:
        by_time.append(jnp.exp(acum))
    if emit_state:
        fwd = _row_iota((n_ch, 1)) < NH_B
        a_last = jnp.where(fwd, acum[:, seq - 1:seq], acum[:, 0:1])
        by_time.append(jnp.exp(a_last - acum) * dt)
    cols = _columns(by_time)

    gw = R_B * HP_B
    first = _pair_lanes((seq, LANES))
    for g in range(NG_B):
        cx = slice(g * gw, (g + 1) * gw)
        cb_ = slice(B_INNER + g * DSTATE, B_INNER + (g + 1) * DSTATE)
        cc = slice(B_INNER + B_BC + g * DSTATE, B_INNER + B_BC + (g + 1) * DSTATE)
        xg = _dwconv_silu(xbc_ref[:, cx], cw_ref[:, cx], cb_ref[:, cx])
        bg = _dwconv_silu(xbc_ref[:, cb_], cw_ref[:, cb_], cb_ref[:, cb_]).astype(BF16)
        cg = _dwconv_silu(xbc_ref[:, cc], cw_ref[:, cc], cb_ref[:, cc]).astype(BF16)
        xbd = [_pair_split(xg[:, p * LANES:(p + 1) * LANES]).astype(BF16) for p in range(R_B // 2)]
        for r0 in range(0, seq, Q_TILE):
            rows = slice(r0, r0 + Q_TILE)
            cb_scores = _dot_nt(cg[rows], bg)
            ys = []
            for p in range(R_B // 2):
                weights, inputs = [], []
                for d in range(2):
                    k0, k1 = _key_range(d, r0, seq)
                    for i in range(2):
                        c = d * NH_B + g * R_B + 2 * p + i
                        expo = _causal_exponent(cols[rows, c:c + 1] - key_shift[c:c + 1, k0:k1], r0, k0, d == 1)
                        weights.append((cb_scores[:, k0:k1] * jnp.exp(expo)).astype(BF16))
                        inputs.append(xbd[p][i * seq + k0:i * seq + k1])
                yp = _dot(jnp.concatenate(weights, axis=1), jnp.concatenate(inputs, axis=0))
                if has_state:
                    h0 = g * R_B + 2 * p
                    for d in range(2):
                        c = d * NH_B + h0
                        carry = jnp.where(_pair_lanes((Q_TILE, LANES)), cols[rows, n_ch + c:n_ch + c + 1],
                                          cols[rows, n_ch + c + 1:n_ch + c + 2])
                        s0_pair = s0_ref[d, h0:h0 + 2].reshape(2 * HP_B, DSTATE).astype(BF16)
                        yp = yp + carry * _dot_nt(cg[rows], s0_pair)
                ys.append(yp)
            y = jnp.concatenate(ys, axis=1) + dsk_ref[:, cx] * xg[rows]
            y = y * _silu(z_ref[rows, cx])
            yb_ref[rows, cx] = _rms(y, bnw_ref[:, cx])
        if emit_state:
            for d in range(2):
                c0 = n_ch + d * NH_B + g * R_B
                spread = jnp.concatenate([jnp.where(first, cols[:, c0 + 2 * p:c0 + 2 * p + 1],
                                                    cols[:, c0 + 2 * p + 1:c0 + 2 * p + 2])
                                          for p in range(R_B // 2)], axis=1)
                sn = _dot((xg * spread).T.astype(BF16), bg)
                for r in range(R_B):
                    sn_ref[d, g * R_B + r] = sn[r * HP_B:(r + 1) * HP_B, :]


def _ssd(xbc, z, gates, params, seq, sub_layer, state=None, carried=None):
    n = xbc.shape[0]
    nseq = n // seq
    has_state = state is not None
    emit_state = not has_state
    aliases = {}
    row = lambda s: (s, 0)
    in_specs = [pl.BlockSpec((seq, B_XBC), row), pl.BlockSpec((seq, B_INNER), row),
                pl.BlockSpec((GATE_ROWS, seq), lambda s: (0, s))]
    in_specs += [_layer_spec(p, sub_layer) for p in params]
    args = [xbc, z, gates, *params]
    out_specs = [pl.BlockSpec((seq, B_INNER), row)]
    out_shape = [jax.ShapeDtypeStruct((n, B_INNER), F32)]
    state_spec = pl.BlockSpec((None, None, 2, NH_B, HP_B, DSTATE), lambda s: (s, sub_layer, 0, 0, 0, 0))
    if has_state:
        in_specs.append(state_spec)
        args.append(state)
    if emit_state:
        out_specs.append(state_spec if carried is not None else
                         pl.BlockSpec((None, N_EVEN, 2, NH_B, HP_B, DSTATE), lambda s: (s, 0, 0, 0, 0, 0)))
        out_shape.append(jax.ShapeDtypeStruct((nseq, N_EVEN, 2, NH_B, HP_B, DSTATE), F32))
        if carried is not None:
            aliases = {len(args): 1}
            in_specs.append(pl.BlockSpec(memory_space=pl.ANY))
            args.append(carried)
    return pl.pallas_call(
        functools.partial(_ssd_kernel, seq=seq, has_state=has_state, emit_state=emit_state, n_carried=len(aliases),
                          slot=sub_layer),
        grid=(nseq,), in_specs=in_specs, out_specs=out_specs, out_shape=out_shape,
        input_output_aliases=aliases, compiler_params=_params("arbitrary"), name="ssd",
    )(*args)


def _shared_split(x, x_swapped, kh):
    first = _pair_lanes(x.shape)
    zero = jnp.zeros_like(x)
    top, bottom = (x, x_swapped) if kh == 0 else (x_swapped, x)
    return jnp.concatenate([jnp.where(first, top, zero), jnp.where(first, zero, bottom)], axis=0)


def _pair_attention(q, kbd, vbd, sinks=None, valid=None):
    s = _dot_nt(q, kbd)
    n_keys = kbd.shape[0] // 2
    probs, maxes = [], []
    for i in range(2):
        si = s[:, i * n_keys:(i + 1) * n_keys]
        if valid is not None:
            si = jnp.where(valid, si, -jnp.inf)
        m = jnp.max(si, axis=1, keepdims=True)
        if sinks is not None:
            m = jnp.maximum(m, sinks[i])
        probs.append(jnp.exp(si - m))
        maxes.append(m)
    o = _dot(jnp.concatenate(probs, axis=1).astype(BF16), vbd)
    den = o[:, LANES:]
    if sinks is not None:
        den = den + jnp.where(_pair_lanes(den.shape), jnp.exp(sinks[0] - maxes[0]), jnp.exp(sinks[1] - maxes[1]))
    return o[:, :LANES] / den


def _pair_sinks(sink_ref, n):
    return sink_ref[0:1, n:n + 1], sink_ref[0:1, n + 1:n + 2]


def _mla_queries(qa_ref, qan_ref, wqb_ref):
    return _dot(_rms(qa_ref[...], qan_ref[...]).astype(BF16), wqb_ref[...]) * MLA_SCALE


def _attn_ctx_kernel(qc_ref, kc_ref, vc_ref, qa_ref, kva_ref, kpe_ref, sink_ref, qan_ref, kvn_ref, wqb_ref, wkvb_ref,
                     *rest, seq, n_carried, slot):
    oc_ref, od_ref, ckv_ref = rest[n_carried:]
    if n_carried == 0:
        for other in range(N_ODD):
            if other != slot:
                ckv_ref[other] = jnp.zeros(ckv_ref.shape[1:], F32)
        ckv_ref = ckv_ref.at[slot]
    ones_bd = _pair_split(jnp.ones((seq, LANES), F32))
    kc, vc = kc_ref[...], vc_ref[...]
    kc_sw, vc_sw = pltpu.roll(kc, HD_C, 1), pltpu.roll(vc, HD_C, 1)
    for kh in range(NKV_C):
        kbd = _shared_split(kc, kc_sw, kh).astype(BF16)
        vbd = jnp.concatenate([_shared_split(vc, vc_sw, kh), ones_bd], axis=1).astype(BF16)
        for n in range(kh * G_C, (kh + 1) * G_C, 2):
            cols = slice(n * HD_C, (n + 2) * HD_C)
            q = (qc_ref[:, cols] * (HD_C ** -0.5)).astype(BF16)
            oc_ref[:, cols] = _pair_attention(q, kbd, vbd, sinks=_pair_sinks(sink_ref, n))

    qd = _mla_queries(qa_ref, qan_ref, wqb_ref)
    ckv = _rms(kva_ref[...], kvn_ref[...])
    ckv_ref[...] = ckv
    kv = _dot(ckv.astype(BF16), wkvb_ref[...])
    kpe = kpe_ref[...]
    kpe_bd = jnp.concatenate([kpe, pltpu.roll(kpe, ROPE_D, 1)], axis=0)
    nope_w = NH_D * NOPE_D
    for i in range(NH_D // 2):
        cols = slice(i * LANES, (i + 1) * LANES)
        vcols = slice(nope_w + i * LANES, nope_w + (i + 1) * LANES)
        q = jnp.concatenate([qd[:, cols], qd[:, vcols]], axis=1).astype(BF16)
        kbd = jnp.concatenate([_pair_split(kv[:, cols]), kpe_bd], axis=1).astype(BF16)
        vbd = jnp.concatenate([_pair_split(kv[:, vcols]), ones_bd], axis=1).astype(BF16)
        od_ref[:, cols] = _pair_attention(q, kbd, vbd)


def _attn_ctx(proj, params, seq, sub_layer, carried=None):
    n = proj[0].shape[0]
    nseq = n // seq
    row = lambda s: (s, 0)
    in_specs = [pl.BlockSpec((seq, wd), row) for wd in ODD_WIDTHS]
    in_specs += [_layer_spec(p, sub_layer) for p in params]
    args = [*proj, *params]
    half = NH_C * HD_C
    aliases = {}
    if carried is None:
        ckv_spec = pl.BlockSpec((None, N_ODD, seq, KV_RANK), lambda s: (s, 0, 0, 0))
    else:
        ckv_spec = pl.BlockSpec((None, None, seq, KV_RANK), lambda s: (s, sub_layer, 0, 0))
        aliases = {len(args): 2}
        in_specs.append(pl.BlockSpec(memory_space=pl.ANY))
        args.append(carried)
    return pl.pallas_call(
        functools.partial(_attn_ctx_kernel, seq=seq, n_carried=len(aliases), slot=sub_layer),
        grid=(nseq,), in_specs=in_specs,
        out_specs=[pl.BlockSpec((seq, half), row), pl.BlockSpec((seq, half), row), ckv_spec],
        out_shape=[jax.ShapeDtypeStruct((n, half), F32), jax.ShapeDtypeStruct((n, half), F32),
                   jax.ShapeDtypeStruct((nseq, N_ODD, seq, KV_RANK), F32)],
        input_output_aliases=aliases, compiler_params=_params("arbitrary"), name="attn_ctx",
    )(*args)


def _rope(x, cos, sin, half):
    parts = []
    lane = _lane_iota((x.shape[0], LANES))
    first = (lane & (2 * half - 1)) < half
    for i in range(x.shape[1] // LANES):
        xi = x[:, i * LANES:(i + 1) * LANES]
        partner = jnp.where(first, -pltpu.roll(xi, LANES - half, 1), pltpu.roll(xi, half, 1))
        parts.append(xi * cos + partner * sin)
    return parts[0] if len(parts) == 1 else jnp.concatenate(parts, axis=1)


def _attn_lat_kernel(qc_ref, qa_ref, ropeq_ref, kc_ref, vc_ref, kva_ref, kpe_ref, kctx_ref, vctx_ref, ckvctx_ref,
                     kpectx_ref, rope_ref, sink_ref, qan_ref, kvn_ref, wqb_ref, wkvb_ref, oc_ref, od_ref,
                     kwin_s, vwin_s, kext_s, vext_s, *, seq, past):
    qi = pl.program_id(1)
    nope_w = NH_D * NOPE_D
    n_all = past + seq
    ctx0 = 2 * WINDOW + seq

    @pl.when(qi == 0)
    def _():
        zeros = jnp.zeros((WINDOW, LANES), BF16)
        for ref, lat, ctx in ((kwin_s, _rope(kc_ref[...], rope_ref[0], rope_ref[1], HD_C // 2), kctx_ref[...]),
                              (vwin_s, vc_ref[...], vctx_ref[...])):
            lat_sw, ctx_sw = pltpu.roll(lat, HD_C, 1), pltpu.roll(ctx, HD_C, 1)
            for kh in range(NKV_C):
                lat_bd = _shared_split(lat, lat_sw, kh).astype(BF16)
                ctx_bd = _shared_split(ctx, ctx_sw, kh).astype(BF16)
                for i in range(2):
                    ref[kh, i, 0:WINDOW, :] = zeros
                    ref[kh, i, WINDOW:WINDOW + seq, :] = lat_bd[i * seq:(i + 1) * seq]
                    ref[kh, i, WINDOW + seq:ctx0, :] = zeros
                    ref[kh, i, ctx0:, :] = ctx_bd[i * past:(i + 1) * past]
        ckv = _rms(kva_ref[...], kvn_ref[...])
        kv = jnp.concatenate([_dot(ckvctx_ref[...].astype(BF16), wkvb_ref[...]),
                              _dot(ckv.astype(BF16), wkvb_ref[...])], axis=0)
        kpe = jnp.concatenate([kpectx_ref[...], _rope(kpe_ref[...], rope_ref[2], rope_ref[3], ROPE_D // 2)], axis=0)
        kpe_bd = jnp.concatenate([kpe, pltpu.roll(kpe, ROPE_D, 1)], axis=0).astype(BF16)
        ones_bd = _pair_split(jnp.ones((n_all, LANES), F32)).astype(BF16)
        for i in range(NH_D // 2):
            kext_s[i, :, 0:LANES] = _pair_split(kv[:, i * LANES:(i + 1) * LANES]).astype(BF16)
            kext_s[i, :, LANES:] = kpe_bd
            vext_s[i, :, 0:LANES] = _pair_split(kv[:, nope_w + i * LANES:nope_w + (i + 1) * LANES]).astype(BF16)
            vext_s[i, :, LANES:] = ones_bd

    r0 = pl.multiple_of(qi * Q_TILE, Q_TILE)
    nloc = Q_TILE + 2 * WINDOW
    n_keys = nloc + past
    qr = _rope(qc_ref[...], ropeq_ref[0], ropeq_ref[1], HD_C // 2) * (HD_C ** -0.5)
    ti = r0 + _row_iota((Q_TILE, n_keys))
    col = _lane_iota((Q_TILE, n_keys))
    pos = r0 - WINDOW + col
    valid = (col >= nloc) | ((jnp.abs(ti - pos) <= WINDOW) & (pos >= 0) & (pos < seq))
    ones_bd = _pair_split(jnp.ones((n_keys, LANES), F32)).astype(BF16)
    for kh in range(NKV_C):
        kbd = jnp.concatenate([kwin_s[kh, 0, pl.ds(r0, nloc), :], kwin_s[kh, 0, ctx0:, :],
                               kwin_s[kh, 1, pl.ds(r0, nloc), :], kwin_s[kh, 1, ctx0:, :]], axis=0)
        vals = jnp.concatenate([vwin_s[kh, 0, pl.ds(r0, nloc), :], vwin_s[kh, 0, ctx0:, :],
                                vwin_s[kh, 1, pl.ds(r0, nloc), :], vwin_s[kh, 1, ctx0:, :]], axis=0)
        vbd = jnp.concatenate([vals, ones_bd], axis=1)
        for n in range(kh * G_C, (kh + 1) * G_C, 2):
            cols = slice(n * HD_C, (n + 2) * HD_C)
            oc_ref[:, cols] = _pair_attention(qr[:, cols].astype(BF16), kbd, vbd, sinks=_pair_sinks(sink_ref, n),
                                              valid=valid)

    qd = _mla_queries(qa_ref, qan_ref, wqb_ref)
    q_pe = _rope(qd[:, nope_w:], ropeq_ref[2], ropeq_ref[3], ROPE_D // 2)
    for i in range(NH_D // 2):
        cols = slice(i * LANES, (i + 1) * LANES)
        q = jnp.concatenate([qd[:, cols], q_pe[:, cols]], axis=1).astype(BF16)
        od_ref[:, cols] = _pair_attention(q, kext_s[i], vext_s[i])


def _attn_lat(proj, caches, rope, params, seq, sub_layer):
    qc, kc, vc, qa, kva, kpe = proj
    n = qc.shape[0]
    past = caches[0].shape[2]
    nq = seq // Q_TILE
    qrow = lambda b, q: (b * nq + q, 0)
    krow = lambda b, q: (b, 0)
    kvw = NKV_C * HD_C
    in_specs = [pl.BlockSpec((Q_TILE, NH_C * HD_C), qrow), pl.BlockSpec((Q_TILE, Q_RANK), qrow),
                pl.BlockSpec((4, Q_TILE, LANES), lambda b, q: (0, q, 0)),
                pl.BlockSpec((seq, kvw), krow), pl.BlockSpec((seq, kvw), krow),
                pl.BlockSpec((seq, KV_RANK), krow), pl.BlockSpec((seq, LANES), krow)]
    in_specs += [pl.BlockSpec((None, None, past, LANES), lambda b, q: (b, sub_layer, 0, 0)) for _ in caches]
    in_specs += [pl.BlockSpec(rope.shape, lambda b, q: (0, 0, 0))]
    in_specs += [_layer_spec(p, sub_layer) for p in params]
    half = NH_C * HD_C
    win_rows = 2 * WINDOW + seq + past
    return pl.pallas_call(
        functools.partial(_attn_lat_kernel, seq=seq, past=past),
        grid=(n // seq, nq), in_specs=in_specs,
        out_specs=[pl.BlockSpec((Q_TILE, half), qrow), pl.BlockSpec((Q_TILE, half), qrow)],
        out_shape=[jax.ShapeDtypeStruct((n, half), F32), jax.ShapeDtypeStruct((n, half), F32)],
        scratch_shapes=[pltpu.VMEM((NKV_C, 2, win_rows, LANES), BF16), pltpu.VMEM((NKV_C, 2, win_rows, LANES), BF16),
                        pltpu.VMEM((NH_D // 2, 2 * (past + seq), 2 * LANES), BF16),
                        pltpu.VMEM((NH_D // 2, 2 * (past + seq), 2 * LANES), BF16)],
        compiler_params=_params("arbitrary", "arbitrary"), name="attn_lat",
    )(qc, qa, rope, kc, vc, kva, kpe, *caches, rope, *params)


def _pad_lanes(x, width=LANES):
    return jnp.pad(x, [(0, 0)] * (x.ndim - 1) + [(0, width - x.shape[-1])])


def _on_lanes(x):
    return jnp.broadcast_to(x[..., None], x.shape + (LANES,))


def _even_in_weights(w):
    o0 = 2 * A_QK + 2 * A_V
    z0 = o0 + 4 * NH_A
    d0 = z0 + B_INNER + B_XBC
    gates_t = jnp.swapaxes(jnp.concatenate([w[..., o0:z0], w[..., d0:]], axis=-1), 1, 2)
    return jnp.concatenate([w[..., :o0], w[..., z0:d0]], axis=-1).astype(BF16), gates_t.astype(BF16)


def _mla_query_weights(w):
    lead = w.shape[:-1]
    w4 = w.reshape(lead + (NH_D // 2, 2, NOPE_D + ROPE_D))
    nope = w4[..., :NOPE_D].reshape(lead + (NH_D * NOPE_D,))
    pe = _pad_lanes(w4[..., NOPE_D:].reshape(lead + (NH_D // 2, 2 * ROPE_D)))
    return jnp.concatenate([nope, pe.reshape(lead + (NH_D // 2 * LANES,))], axis=-1).astype(BF16)


def _mla_kv_weights(w):
    lead = w.shape[:-1]
    w3 = w.reshape(lead + (NH_D, NOPE_D + V_D))
    return jnp.concatenate([w3[..., :NOPE_D].reshape(lead + (NH_D * NOPE_D,)),
                            w3[..., NOPE_D:].reshape(lead + (NH_D * V_D,))], axis=-1).astype(BF16)


def _rope_tables(rows):
    def table(rot_dim):
        quarter = rot_dim // 4
        inv = ROPE_BASE ** (-jnp.arange(quarter, dtype=F32) / quarter)
        r = jnp.repeat(jnp.arange(rows, dtype=F32), GRID_W)
        col = jnp.tile(jnp.arange(GRID_W, dtype=F32), rows)
        ang = jnp.concatenate([r[:, None] * inv, col[:, None] * inv], axis=-1)
        reps = LANES // (rot_dim // 2)
        return jnp.tile(jnp.cos(ang), (1, reps)), jnp.tile(jnp.sin(ang), (1, reps))
    cos_c, sin_c = table(HD_C)
    cos_d, sin_d = table(ROPE_D)
    return jnp.stack([cos_c, sin_c, cos_d, sin_d])


def kernel(x_prompt, x_sample, c, state_mlstm_C, state_mlstm_n, state_mlstm_m, state_ssd, cache_gqa_k, cache_gqa_v,
           cache_mla_ckv, cache_mla_kpe, c_ctx, w_ada, b_ada, norm_g, w_up, w_down, w_in_even, conv_a_w, conv_a_b,
           conv_b_w, conv_b_b, gate_b, a_norm_w, dt_bias, a_log, d_skip, b_norm_w, w_out_even, w_in_odd, sink,
           q_a_norm, kv_a_norm, w_q_b, w_kv_b, w_out_odd):
    xp = x_prompt.reshape(BATCH * SEQ, D_MODEL)
    xs = x_sample.reshape(DEC_BATCH * DEC_SEQ, D_MODEL)
    cond = jnp.concatenate([c_ctx[None, :], c, jnp.zeros((MOD_ROWS - 1 - DEC_BATCH, D_MODEL), F32)], axis=0)
    mods = _modulations(cond, w_ada, b_ada)
    rope = _rope_tables(DEC_SEQ // GRID_W)

    w_even, w_gates_t = _even_in_weights(w_in_even)
    a_params = (conv_a_w, conv_a_b[:, None, :], _on_lanes(gate_b), a_norm_w[:, None, :])
    b_params = (conv_b_w, conv_b_b[:, None, :], _on_lanes(dt_bias.reshape(N_EVEN, 2 * NH_B)),
                _on_lanes(a_log.reshape(N_EVEN, 2 * NH_B)), jnp.repeat(d_skip, HP_B, axis=1)[:, None, :],
                b_norm_w[:, None, :])
    n0 = state_mlstm_n[..., None]
    mem_in = (jnp.concatenate([state_mlstm_C, jnp.broadcast_to(n0, n0.shape[:-1] + (LANES,))], axis=-1),
              _on_lanes(state_mlstm_m.reshape(DEC_BATCH, N_EVEN, 2 * NH_A)))
    w_odd = _pad_lanes(w_in_odd, sum(ODD_WIDTHS)).astype(BF16)
    o_params = (_pad_lanes(sink)[:, None, :], q_a_norm[:, None, :], kv_a_norm[:, None, :],
                _mla_query_weights(w_q_b), _mla_kv_weights(w_kv_b))
    caches = (cache_gqa_k.reshape(DEC_BATCH, N_ODD, PAST_LEN, NKV_C * HD_C),
              cache_gqa_v.reshape(DEC_BATCH, N_ODD, PAST_LEN, NKV_C * HD_C),
              cache_mla_ckv, _pad_lanes(cache_mla_kpe))

    new_k, new_v, new_kpe = [], [], []
    mem_state, ssd_state, ckv_state = None, None, None
    for l in range(DEPTH):
        j = l // 2
        if l % 2 == 0:
            qk, v, o, z, xbc, g = _project(xp, mods, l, False, norm_g, w_even, EVEN_WIDTHS, w_gates_t)
            a1p, *mem_state = _mlstm(qk, v, o, g, a_params, SEQ, j, carried=mem_state)
            a2p, ssd_state = _ssd(xbc, z, g, b_params, SEQ, j, carried=ssd_state)
            qk, v, o, z, xbc, g = _project(xs, mods, l, True, norm_g, w_even, EVEN_WIDTHS, w_gates_t)
            a1s, = _mlstm(qk, v, o, g, a_params, DEC_SEQ, j, state=mem_in)
            a2s, = _ssd(xbc, z, g, b_params, DEC_SEQ, j, state=state_ssd)
            w_out = w_out_even
        else:
            proj = _project(xp, mods, l, False, norm_g, w_odd, ODD_WIDTHS)
            a1p, a2p, ckv_state = _attn_ctx(proj, o_params, SEQ, j, carried=ckv_state)
            new_k.append(proj[1].reshape(BATCH, SEQ, NKV_C, HD_C))
            new_v.append(proj[2].reshape(BATCH, SEQ, NKV_C, HD_C))
            new_kpe.append(proj[5][:, :ROPE_D].reshape(BATCH, SEQ, ROPE_D))
            proj = _project(xs, mods, l, True, norm_g, w_odd, ODD_WIDTHS)
            a1s, a2s = _attn_lat(proj, caches, rope, o_params, DEC_SEQ, j)
            w_out = w_out_odd
        xp = _channel(a1p, a2p, xp, mods, l, False, norm_g, w_out, w_up, w_down)
        xs = _channel(a1s, a2s, xs, mods, l, True, norm_g, w_out, w_up, w_down)

    new_c, new_n, new_m = mem_state
    return (xp.reshape(BATCH, SEQ, D_MODEL), xs.reshape(DEC_BATCH, DEC_SEQ, D_MODEL),
            new_c, new_n, new_m[..., 0].reshape(BATCH, N_EVEN, 2, NH_A), ssd_state,
            jnp.stack(new_k, axis=1), jnp.stack(new_v, axis=1), ckv_state, jnp.stack(new_kpe, axis=1))
```

```python
import functools

import jax
import jax.numpy as jnp
from jax import lax
from jax.experimental import pallas as pl
from jax.experimental.pallas import tpu as pltpu

F32 = jnp.float32
BF16 = jnp.bfloat16

D_MODEL = 1024
BATCH = 32
SEQ = 256
DEPTH = 4
DEC_BATCH = 2
DEC_SEQ = 1024
PAST_LEN = 256
GRID_W = 64
N_EVEN = (DEPTH + 1) // 2
N_ODD = DEPTH // 2
EPS = 1e-6
CONV_K = 5
NH_A = 4
DK_A = 128
DV_A = 128
A_QK = NH_A * DK_A
A_V = NH_A * DV_A
NH_B = 8
HP_B = 64
DSTATE = 128
NG_B = 2
R_B = NH_B // NG_B
B_INNER = NH_B * HP_B
B_BC = NG_B * DSTATE
B_XBC = B_INNER + 2 * B_BC
NH_C = 8
NKV_C = 2
G_C = NH_C // NKV_C
HD_C = 64
WINDOW = 128
NH_D = 8
Q_RANK = 256
KV_RANK = 128
NOPE_D = 64
ROPE_D = 32
V_D = 64
MLA_SCALE = (NOPE_D + ROPE_D) ** -0.5
D_FF = 4 * D_MODEL
ROPE_BASE = 10000.0

LANES = 128
VMEM_LIMIT_BYTES = 56 * 1024 * 1024
ROW_TILE = 512
FF_TILE = 1024
SUB_ROWS = 512
STAGE_ROWS = 512
Q_TILE = 256
ADA_TILE = 1536
MOD_ROWS = 8
GATE_ROWS = 4 * NH_A + 2 * NH_B

EVEN_WIDTHS = (2 * A_QK, A_V, A_V, B_INNER, B_XBC)
ODD_WIDTHS = (NH_C * HD_C, NKV_C * HD_C, NKV_C * HD_C, Q_RANK, KV_RANK, LANES)

_NT = (((1,), (1,)), ((), ()))


def _params(*sem):
    return pltpu.CompilerParams(dimension_semantics=sem, vmem_limit_bytes=VMEM_LIMIT_BYTES)


def _rms(x, g):
    return x * lax.rsqrt(jnp.mean(x * x, axis=-1, keepdims=True) + EPS) * g


def _silu(x):
    return x * jax.nn.sigmoid(x)


def _softplus(x):
    return jnp.maximum(x, 0.0) + jnp.log1p(jnp.exp(-jnp.abs(x)))


def _dot(a, b):
    return jnp.dot(a, b, preferred_element_type=F32)


def _dot_nt(a, b):
    return lax.dot_general(a, b, _NT, preferred_element_type=F32)


def _layer_spec(arr, layer):
    tail = arr.shape[1:]
    zeros = (0,) * len(tail)
    return pl.BlockSpec((None,) + tail, lambda *_: (layer,) + zeros)


def _ada_kernel(c_ref, w_ref, b_ref, o_ref):
    s = _silu(c_ref[...]).astype(BF16)
    o_ref[...] = _dot(s, w_ref[...].astype(BF16)) + b_ref[...]


def _modulations(cond, w_ada, b_ada):
    out = pl.pallas_call(
        _ada_kernel,
        grid=(DEPTH, 6 * D_MODEL // ADA_TILE),
        in_specs=[pl.BlockSpec((MOD_ROWS, D_MODEL), lambda l, n: (0, 0)),
                  pl.BlockSpec((None, D_MODEL, ADA_TILE), lambda l, n: (l, 0, n)),
                  pl.BlockSpec((None, 1, ADA_TILE), lambda l, n: (l, 0, n))],
        out_specs=pl.BlockSpec((None, MOD_ROWS, ADA_TILE), lambda l, n: (l, 0, n)),
        out_shape=jax.ShapeDtypeStruct((DEPTH, MOD_ROWS, 6 * D_MODEL), F32),
        compiler_params=_params("arbitrary", "arbitrary"),
        name="ada",
    )(cond, w_ada, b_ada.reshape(DEPTH, 1, 6 * D_MODEL))
    return out.reshape(DEPTH, MOD_ROWS, 6, D_MODEL)


def _mod_spec(layer, latent):
    if latent:
        per_seq = DEC_SEQ // ROW_TILE
        return pl.BlockSpec((None, None, 6, D_MODEL), lambda i, *_: (layer, 1 + i // per_seq, 0, 0))
    return pl.BlockSpec((None, None, 6, D_MODEL), lambda i, *_: (layer, 0, 0, 0))


def _proj_kernel(x_ref, mod_ref, g_ref, w_ref, *rest, widths, has_gates):
    h = _rms(x_ref[...], g_ref[0:1, :]) * (1.0 + mod_ref[1:2, :]) + mod_ref[0:1, :]
    hb = h.astype(BF16)
    o_refs = rest[1:] if has_gates else rest
    off = 0
    for o_ref, wd in zip(o_refs, widths):
        o_ref[...] = _dot(hb, w_ref[:, off:off + wd])
        off += wd
    if has_gates:
        o_refs[-1][...] = _dot_nt(rest[0][...], hb)


def _project(x, mods, layer, latent, gains, w_all, widths, w_gates_t=None):
    n = x.shape[0]
    in_specs = [pl.BlockSpec((ROW_TILE, D_MODEL), lambda i: (i, 0)), _mod_spec(layer, latent),
                _layer_spec(gains, layer), _layer_spec(w_all, layer // 2)]
    args = [x, mods, gains, w_all]
    out_specs = [pl.BlockSpec((ROW_TILE, wd), lambda i: (i, 0)) for wd in widths]
    out_shape = [jax.ShapeDtypeStruct((n, wd), F32) for wd in widths]
    if w_gates_t is not None:
        gate_rows = w_gates_t.shape[1]
        in_specs.append(_layer_spec(w_gates_t, layer // 2))
        args.append(w_gates_t)
        out_specs.append(pl.BlockSpec((gate_rows, ROW_TILE), lambda i: (0, i)))
        out_shape.append(jax.ShapeDtypeStruct((gate_rows, n), F32))
    return pl.pallas_call(
        functools.partial(_proj_kernel, widths=widths, has_gates=w_gates_t is not None),
        grid=(n // ROW_TILE,), in_specs=in_specs, out_specs=out_specs, out_shape=out_shape,
        compiler_params=_params("arbitrary"),
        name="proj",
    )(*args)


def _weight_chunks(layer, sub_layer, wo_hbm, wu_hbm, wd_hbm, wo_s, wu_s, wd_s):
    chunks = []
    for r in range(0, D_MODEL, STAGE_ROWS):
        chunks.append((wo_hbm.at[sub_layer, pl.ds(r, STAGE_ROWS), :], wo_s.at[pl.ds(r, STAGE_ROWS), :]))
    for r in range(0, D_MODEL, STAGE_ROWS):
        for c in range(0, D_FF, D_MODEL):
            chunks.append((wu_hbm.at[layer, pl.ds(r, STAGE_ROWS), pl.ds(c, D_MODEL)],
                           wu_s.at[pl.ds(r, STAGE_ROWS), pl.ds(c, D_MODEL)]))
    for r in range(0, D_FF, STAGE_ROWS):
        chunks.append((wd_hbm.at[layer, pl.ds(r, STAGE_ROWS), :], wd_s.at[pl.ds(r, STAGE_ROWS), :]))
    return chunks


def _channel_kernel(a1p_ref, a2p_ref, xp_ref, a1s_ref, a2s_ref, xs_ref, mod_ref, g_ref, wo_hbm, wu_hbm, wd_hbm,
                    op_ref, os_ref, wo_s, wu_s, wd_s, stage, sem, *, layer, sub_layer, prompt_steps):
    step = pl.program_id(0)

    @pl.when(step == 0)
    def _():
        chunks = _weight_chunks(layer, sub_layer, wo_hbm, wu_hbm, wd_hbm, wo_s, wu_s, wd_s)
        copies = [pltpu.make_async_copy(src, stage.at[k % 2], sem.at[k % 2]) for k, (src, _) in enumerate(chunks)]
        copies[0].start()
        for k, (_, dst) in enumerate(chunks):
            if k + 1 < len(chunks):
                copies[k + 1].start()
            copies[k].wait()
            dst[...] = stage[k % 2].astype(BF16)

    def rows_block(a1_ref, a2_ref, x_ref, o_ref):
        half = a1_ref.shape[1]
        for r0 in range(0, ROW_TILE, SUB_ROWS):
            rows = slice(r0, r0 + SUB_ROWS)
            y = (_dot(a1_ref[rows, :].astype(BF16), wo_s[0:half, :])
                 + _dot(a2_ref[rows, :].astype(BF16), wo_s[half:, :]))
            x1 = x_ref[rows, :] + mod_ref[2:3, :] * _rms(y, g_ref[1:2, :])
            h = (_rms(x1, g_ref[2:3, :]) * (1.0 + mod_ref[4:5, :]) + mod_ref[3:4, :]).astype(BF16)
            acc = None
            for c in range(0, D_FF, FF_TILE):
                u = jnp.square(jnp.maximum(_dot(h, wu_s[:, c:c + FF_TILE]), 0.0)).astype(BF16)
                part = _dot(u, wd_s[c:c + FF_TILE, :])
                acc = part if acc is None else acc + part
            o_ref[rows, :] = x1 + mod_ref[5:6, :] * _rms(acc, g_ref[3:4, :])

    @pl.when(step < prompt_steps)
    def _():
        rows_block(a1p_ref, a2p_ref, xp_ref, op_ref)

    @pl.when(step >= prompt_steps)
    def _():
        rows_block(a1s_ref, a2s_ref, xs_ref, os_ref)


def _channel(prompt, latent, mods, layer, gains, w_out, w_up, w_down):
    n_p, n_s = prompt[2].shape[0], latent[2].shape[0]
    steps_p, steps_s = n_p // ROW_TILE, n_s // ROW_TILE
    per_seq = DEC_SEQ // ROW_TILE
    row_p = lambda i: (jnp.minimum(i, steps_p - 1), 0)
    row_s = lambda i: (jnp.maximum(i - steps_p, 0), 0)
    mod_spec = pl.BlockSpec((None, None, 6, D_MODEL),
                            lambda i: (layer, jnp.where(i < steps_p, 0, 1 + (i - steps_p) // per_seq), 0, 0))
    hbm = pl.BlockSpec(memory_space=pl.ANY)
    specs = lambda arrs, row: [pl.BlockSpec((ROW_TILE, a.shape[1]), row) for a in arrs]
    return pl.pallas_call(
        functools.partial(_channel_kernel, layer=layer, sub_layer=layer // 2, prompt_steps=steps_p),
        grid=(steps_p + steps_s,),
        in_specs=specs(prompt, row_p) + specs(latent, row_s) + [mod_spec, _layer_spec(gains, layer), hbm, hbm, hbm],
        out_specs=[pl.BlockSpec((ROW_TILE, D_MODEL), row_p), pl.BlockSpec((ROW_TILE, D_MODEL), row_s)],
        out_shape=[jax.ShapeDtypeStruct((n_p, D_MODEL), F32), jax.ShapeDtypeStruct((n_s, D_MODEL), F32)],
        scratch_shapes=[pltpu.VMEM((D_MODEL, D_MODEL), BF16), pltpu.VMEM((D_MODEL, D_FF), BF16),
                        pltpu.VMEM((D_FF, D_MODEL), BF16), pltpu.VMEM((2, STAGE_ROWS, D_MODEL), F32),
                        pltpu.SemaphoreType.DMA((2,))],
        compiler_params=_params("arbitrary"),
        name="channel",
    )(*prompt, *latent, mods, gains, w_out, w_up, w_down)


def _row_iota(shape):
    return lax.broadcasted_iota(jnp.int32, shape, 0)


def _lane_iota(shape):
    return lax.broadcasted_iota(jnp.int32, shape, 1)


def _pair_lanes(shape):
    return _lane_iota(shape) < LANES // 2


def _cumsum_lanes(x, n_fwd):
    t = x.shape[1]
    si, ti = _row_iota((t, t)), _lane_iota((t, t))
    upper = jnp.where(si <= ti, 1.0, 0.0).astype(BF16)
    lower = jnp.where(si >= ti, 1.0, 0.0).astype(BF16)
    hi = x.astype(BF16)
    rest = x - hi.astype(F32)
    mid = rest.astype(BF16)
    lo = (rest - mid.astype(F32)).astype(BF16)
    pre = _dot(hi, upper) + _dot(mid, upper) + _dot(lo, upper)
    suf = _dot(hi, lower) + _dot(mid, lower) + _dot(lo, lower)
    return jnp.where(_row_iota(x.shape) < n_fwd, pre, suf)


def _cummax_lanes(x, n_fwd):
    t = x.shape[1]
    lane = _lane_iota(x.shape)
    pre, suf = x, x
    k = 1
    while k < t:
        pre = jnp.maximum(pre, jnp.where(lane >= k, pltpu.roll(pre, k, 1), -jnp.inf))
        suf = jnp.maximum(suf, jnp.where(lane < t - k, pltpu.roll(suf, t - k, 1), -jnp.inf))
        k *= 2
    return jnp.where(_row_iota(x.shape) < n_fwd, pre, suf)


def _columns(row_arrays):
    t = row_arrays[0].shape[1]
    used = sum(a.shape[0] for a in row_arrays)
    return jnp.concatenate(list(row_arrays) + [jnp.zeros((LANES - used, t), F32)], axis=0).T


def _dwconv_silu(x, w, b):
    t = x.shape[0]
    row = _row_iota(x.shape)
    acc = x * w[CONV_K // 2:CONV_K // 2 + 1, :] + b
    for j in range(CONV_K):
        d = j - CONV_K // 2
        if d == 0:
            continue
        shifted = pltpu.roll(x, (-d) % t, 0)
        valid = (row >= -d) if d < 0 else (row < t - d)
        acc = acc + jnp.where(valid, shifted, 0.0) * w[j:j + 1, :]
    return _silu(acc)


def _causal_exponent(expo, r0, k0, reverse):
    ti = r0 + _row_iota(expo.shape)
    si = k0 + _lane_iota(expo.shape)
    keep = (si >= ti) if reverse else (si <= ti)
    return jnp.where(keep, expo, -jnp.inf)


def _pair_split(x):
    first = _pair_lanes(x.shape)
    zero = jnp.zeros_like(x)
    return jnp.concatenate([jnp.where(first, x, zero), jnp.where(first, zero, x)], axis=0)


def _key_range(d, r0, seq):
    return (0, r0 + Q_TILE) if d == 0 else (r0, seq)


def _mlstm_kernel(*refs, seq, has_state, emit_state, n_carried, slot):
    assert not (has_state and emit_state)
    it = iter(refs)
    qk_ref, v_ref, o_ref, g_ref, cw_ref, cb_ref, gb_ref, anw_ref = (next(it) for _ in range(8))
    if has_state:
        c0_ref, m0_ref = next(it), next(it)
    for _ in range(n_carried):
        next(it)
    ha_ref = next(it)
    if emit_state:
        cn_ref, nn_ref, mn_ref = next(it), next(it), next(it)
        if n_carried == 0:
            for other in range(N_EVEN):
                if other != slot:
                    cn_ref[other] = jnp.zeros(cn_ref.shape[1:], F32)
                    nn_ref[other] = jnp.zeros(nn_ref.shape[1:], F32)
                    mn_ref[other] = jnp.zeros(mn_ref.shape[1:], F32)
            cn_ref, nn_ref, mn_ref = cn_ref.at[slot], nn_ref.at[slot], mn_ref.at[slot]

    n_ch = 2 * NH_A
    log_i = g_ref[0:n_ch, :] + gb_ref[0:n_ch, 0:1]
    f_pre = g_ref[n_ch:2 * n_ch, :] + gb_ref[n_ch:2 * n_ch, 0:1]
    log_f = jnp.minimum(f_pre, 0.0) - jnp.log1p(jnp.exp(-jnp.abs(f_pre)))
    b = _cumsum_lanes(log_f, NH_A)
    a = log_i - b
    m_run = _cummax_lanes(a, NH_A)
    if has_state:
        m0 = m0_ref[:, 0:1]
        m_run = jnp.maximum(m_run, m0)
    else:
        m_run = jnp.maximum(m_run, 0.0)
    by_time = [m_run, jnp.exp(-(b + m_run))]
    if has_state:
        by_time.append(jnp.exp(m0 - m_run))
    if emit_state:
        fwd = _row_iota((n_ch, 1)) < NH_A
        b_last = jnp.where(fwd, b[:, seq - 1:seq], b[:, 0:1])
        m_last = jnp.where(fwd, m_run[:, seq - 1:seq], m_run[:, 0:1])
        mn_ref[...] = jnp.broadcast_to(b_last + m_last, (n_ch, LANES))
        by_time.append(jnp.exp(a - m_last))
    cols = _columns(by_time)

    ones = jnp.ones((seq, LANES), F32)
    for h in range(NH_A):
        cq = slice(h * DK_A, (h + 1) * DK_A)
        ck = slice(A_QK + h * DK_A, A_QK + (h + 1) * DK_A)
        cv = slice(h * DV_A, (h + 1) * DV_A)
        q = _dwconv_silu(qk_ref[:, cq], cw_ref[:, cq], cb_ref[:, cq])
        k = _dwconv_silu(qk_ref[:, ck], cw_ref[:, ck], cb_ref[:, ck]) * (DK_A ** -0.5)
        qb = q.astype(BF16)
        kb = k.astype(BF16)
        vh = v_ref[:, cv]
        vaug = jnp.concatenate([vh, ones], axis=1).astype(BF16)
        for r0 in range(0, seq, Q_TILE):
            rows = slice(r0, r0 + Q_TILE)
            s = _dot_nt(qb[rows], kb)
            hsum = None
            for d in range(2):
                c = d * NH_A + h
                k0, k1 = _key_range(d, r0, seq)
                expo = _causal_exponent(a[c:c + 1, k0:k1] - cols[rows, c:c + 1], r0, k0, d == 1)
                p = (s[:, k0:k1] * jnp.exp(expo)).astype(BF16)
                acc = _dot(p, vaug[k0:k1])
                if has_state:
                    acc = acc + cols[rows, 2 * n_ch + c:2 * n_ch + c + 1] * _dot(qb[rows], c0_ref[d, h].astype(BF16))
                hd = acc[:, 0:DV_A] / jnp.maximum(jnp.abs(acc[:, DV_A:]), cols[rows, n_ch + c:n_ch + c + 1])
                hsum = hd if hsum is None else hsum + hd
            og = jax.nn.sigmoid(o_ref[rows, cv]) * hsum
            ha_ref[rows, cv] = _rms(og, anw_ref[:, cv])
        if emit_state:
            for d in range(2):
                c = d * NH_A + h
                kw = k * cols[:, 2 * n_ch + c:2 * n_ch + c + 1]
                cn_ref[d, h] = _dot(kw.T.astype(BF16), vh.astype(BF16))
                nn_ref[d, h:h + 1, :] = jnp.sum(kw, axis=0, keepdims=True)


def _mlstm(qk, v, o, gates, params, seq, sub_layer, state=None, carried=None):
    n = qk.shape[0]
    nseq = n // seq
    has_state = state is not None
    emit_state = not has_state
    aliases = {}
    row = lambda s: (s, 0)
    in_specs = [pl.BlockSpec((seq, 2 * A_QK), row), pl.BlockSpec((seq, A_V), row), pl.BlockSpec((seq, A_V), row),
                pl.BlockSpec((GATE_ROWS, seq), lambda s: (0, s))]
    in_specs += [_layer_spec(p, sub_layer) for p in params]
    args = [qk, v, o, gates, *params]
    out_specs = [pl.BlockSpec((seq, A_V), row)]
    out_shape = [jax.ShapeDtypeStruct((n, A_V), F32)]
    if has_state:
        c0_aug, m0 = state
        in_specs += [pl.BlockSpec((None, None) + c0_aug.shape[2:], lambda s: (s, sub_layer, 0, 0, 0, 0)),
                     pl.BlockSpec((None, None) + m0.shape[2:], lambda s: (s, sub_layer, 0, 0))]
        args += [c0_aug, m0]
    if emit_state:
        shapes = [(2, NH_A, DK_A, DV_A), (2, NH_A, DK_A), (2 * NH_A, LANES)]
        for shp in shapes:
            zeros = (0,) * len(shp)
            if carried is None:
                out_specs.append(pl.BlockSpec((None, N_EVEN) + shp, lambda s, z=zeros: (s, 0) + z))
            else:
                out_specs.append(pl.BlockSpec((None, None) + shp, lambda s, z=zeros: (s, sub_layer) + z))
            out_shape.append(jax.ShapeDtypeStruct((nseq, N_EVEN) + shp, F32))
        if carried is not None:
            aliases = {len(args) + i: 1 + i for i in range(len(shapes))}
            in_specs += [pl.BlockSpec(memory_space=pl.ANY)] * len(shapes)
            args += list(carried)
    return pl.pallas_call(
        functools.partial(_mlstm_kernel, seq=seq, has_state=has_state, emit_state=emit_state,
                          n_carried=len(aliases), slot=sub_layer),
        grid=(nseq,), in_specs=in_specs, out_specs=out_specs, out_shape=out_shape,
        input_output_aliases=aliases, compiler_params=_params("arbitrary"), name="mlstm",
    )(*args)


def _ssd_kernel(*refs, seq, has_state, emit_state, n_carried, slot):
    assert not (has_state and emit_state)
    it = iter(refs)
    xbc_ref, z_ref, g_ref, cw_ref, cb_ref, dtb_ref, alog_ref, dsk_ref, bnw_ref = (next(it) for _ in range(9))
    if has_state:
        s0_ref = next(it)
    for _ in range(n_carried):
        next(it)
    yb_ref = next(it)
    if emit_state:
        sn_ref = next(it)
        if n_carried == 0:
            for other in range(N_EVEN):
                if other != slot:
                    sn_ref[other] = jnp.zeros(sn_ref.shape[1:], F32)
            sn_ref = sn_ref.at[slot]

    n_ch = 2 * NH_B
    dt = _softplus(g_ref[4 * NH_A:4 * NH_A + n_ch, :] + dtb_ref[:, 0:1])
    acum = _cumsum_lanes(dt * (-jnp.exp(alog_ref[:, 0:1])), NH_B)
    key_shift = acum - jnp.log(dt)
    by_time = [acum]
    if has_state:
        by_time.append(jnp.exp(acum))
    if emit_state:
        fwd = _row_iota((n_ch, 1)) < NH_B
        a_last = jnp.where(fwd, acum[:, seq - 1:seq], acum[:, 0:1])
        by_time.append(jnp.exp(a_last - acum) * dt)
    cols = _columns(by_time)

    gw = R_B * HP_B
    first = _pair_lanes((seq, LANES))
    for g in range(NG_B):
        cx = slice(g * gw, (g + 1) * gw)
        cb_ = slice(B_INNER + g * DSTATE, B_INNER + (g + 1) * DSTATE)
        cc = slice(B_INNER + B_BC + g * DSTATE, B_INNER + B_BC + (g + 1) * DSTATE)
        xg = _dwconv_silu(xbc_ref[:, cx], cw_ref[:, cx], cb_ref[:, cx])
        bg = _dwconv_silu(xbc_ref[:, cb_], cw_ref[:, cb_], cb_ref[:, cb_]).astype(BF16)
        cg = _dwconv_silu(xbc_ref[:, cc], cw_ref[:, cc], cb_ref[:, cc]).astype(BF16)
        xbd = [_pair_split(xg[:, p * LANES:(p + 1) * LANES]).astype(BF16) for p in range(R_B // 2)]
        for r0 in range(0, seq, Q_TILE):
            rows = slice(r0, r0 + Q_TILE)
            cb_scores = _dot_nt(cg[rows], bg)
            ys = []
            for p in range(R_B // 2):
                weights, inputs = [], []
                for d in range(2):
                    k0, k1 = _key_range(d, r0, seq)
                    for i in range(2):
                        c = d * NH_B + g * R_B + 2 * p + i
                        expo = _causal_exponent(cols[rows, c:c + 1] - key_shift[c:c + 1, k0:k1], r0, k0, d == 1)
                        weights.append((cb_scores[:, k0:k1] * jnp.exp(expo)).astype(BF16))
                        inputs.append(xbd[p][i * seq + k0:i * seq + k1])
                yp = _dot(jnp.concatenate(weights, axis=1), jnp.concatenate(inputs, axis=0))
                if has_state:
                    h0 = g * R_B + 2 * p
                    for d in range(2):
                        c = d * NH_B + h0
                        carry = jnp.where(_pair_lanes((Q_TILE, LANES)), cols[rows, n_ch + c:n_ch + c + 1],
                                          cols[rows, n_ch + c + 1:n_ch + c + 2])
                        s0_pair = s0_ref[d, h0:h0 + 2].reshape(2 * HP_B, DSTATE).astype(BF16)
                        yp = yp + carry * _dot_nt(cg[rows], s0_pair)
                ys.append(yp)
            y = jnp.concatenate(ys, axis=1) + dsk_ref[:, cx] * xg[rows]
            y = y * _silu(z_ref[rows, cx])
            yb_ref[rows, cx] = _rms(y, bnw_ref[:, cx])
        if emit_state:
            for d in range(2):
                c0 = n_ch + d * NH_B + g * R_B
                spread = jnp.concatenate([jnp.where(first, cols[:, c0 + 2 * p:c0 + 2 * p + 1],
                                                    cols[:, c0 + 2 * p + 1:c0 + 2 * p + 2])
                                          for p in range(R_B // 2)], axis=1)
                sn = _dot((xg * spread).T.astype(BF16), bg)
                for r in range(R_B):
                    sn_ref[d, g * R_B + r] = sn[r * HP_B:(r + 1) * HP_B, :]


def _ssd(xbc, z, gates, params, seq, sub_layer, state=None, carried=None):
    n = xbc.shape[0]
    nseq = n // seq
    has_state = state is not None
    emit_state = not has_state
    aliases = {}
    row = lambda s: (s, 0)
    in_specs = [pl.BlockSpec((seq, B_XBC), row), pl.BlockSpec((seq, B_INNER), row),
                pl.BlockSpec((GATE_ROWS, seq), lambda s: (0, s))]
    in_specs += [_layer_spec(p, sub_layer) for p in params]
    args = [xbc, z, gates, *params]
    out_specs = [pl.BlockSpec((seq, B_INNER), row)]
    out_shape = [jax.ShapeDtypeStruct((n, B_INNER), F32)]
    state_spec = pl.BlockSpec((None, None, 2, NH_B, HP_B, DSTATE), lambda s: (s, sub_layer, 0, 0, 0, 0))
    if has_state:
        in_specs.append(state_spec)
        args.append(state)
    if emit_state:
        out_specs.append(state_spec if carried is not None else
                         pl.BlockSpec((None, N_EVEN, 2, NH_B, HP_B, DSTATE), lambda s: (s, 0, 0, 0, 0, 0)))
        out_shape.append(jax.ShapeDtypeStruct((nseq, N_EVEN, 2, NH_B, HP_B, DSTATE), F32))
        if carried is not None:
            aliases = {len(args): 1}
            in_specs.append(pl.BlockSpec(memory_space=pl.ANY))
            args.append(carried)
    return pl.pallas_call(
        functools.partial(_ssd_kernel, seq=seq, has_state=has_state, emit_state=emit_state, n_carried=len(aliases),
                          slot=sub_layer),
        grid=(nseq,), in_specs=in_specs, out_specs=out_specs, out_shape=out_shape,
        input_output_aliases=aliases, compiler_params=_params("arbitrary"), name="ssd",
    )(*args)


def _shared_split(x, x_swapped, kh):
    first = _pair_lanes(x.shape)
    zero = jnp.zeros_like(x)
    top, bottom = (x, x_swapped) if kh == 0 else (x_swapped, x)
    return jnp.concatenate([jnp.where(first, top, zero), jnp.where(first, zero, bottom)], axis=0)


def _pair_attention(q, kbd, vbd, sinks=None, valid=None):
    s = _dot_nt(q, kbd)
    n_keys = kbd.shape[0] // 2
    probs, maxes = [], []
    for i in range(2):
        si = s[:, i * n_keys:(i + 1) * n_keys]
        if valid is not None:
            si = jnp.where(valid, si, -jnp.inf)
        m = jnp.max(si, axis=1, keepdims=True)
        if sinks is not None:
            m = jnp.maximum(m, sinks[i])
        probs.append(jnp.exp(si - m))
        maxes.append(m)
    o = _dot(jnp.concatenate(probs, axis=1).astype(BF16), vbd)
    den = o[:, LANES:]
    if sinks is not None:
        den = den + jnp.where(_pair_lanes(den.shape), jnp.exp(sinks[0] - maxes[0]), jnp.exp(sinks[1] - maxes[1]))
    return o[:, :LANES] / den


def _pair_sinks(sink_ref, n):
    return sink_ref[0:1, n:n + 1], sink_ref[0:1, n + 1:n + 2]


def _mla_queries(qa_ref, qan_ref, wqb_ref):
    return _dot(_rms(qa_ref[...], qan_ref[...]).astype(BF16), wqb_ref[...]) * MLA_SCALE


def _attn_ctx_kernel(qc_ref, kc_ref, vc_ref, qa_ref, kva_ref, kpe_ref, sink_ref, qan_ref, kvn_ref, wqb_ref, wkvb_ref,
                     *rest, seq, n_carried, slot):
    oc_ref, od_ref, ckv_ref = rest[n_carried:]
    if n_carried == 0:
        for other in range(N_ODD):
            if other != slot:
                ckv_ref[other] = jnp.zeros(ckv_ref.shape[1:], F32)
        ckv_ref = ckv_ref.at[slot]
    ones_bd = _pair_split(jnp.ones((seq, LANES), F32))
    kc, vc = kc_ref[...], vc_ref[...]
    kc_sw, vc_sw = pltpu.roll(kc, HD_C, 1), pltpu.roll(vc, HD_C, 1)
    for kh in range(NKV_C):
        kbd = _shared_split(kc, kc_sw, kh).astype(BF16)
        vbd = jnp.concatenate([_shared_split(vc, vc_sw, kh), ones_bd], axis=1).astype(BF16)
        for n in range(kh * G_C, (kh + 1) * G_C, 2):
            cols = slice(n * HD_C, (n + 2) * HD_C)
            q = (qc_ref[:, cols] * (HD_C ** -0.5)).astype(BF16)
            oc_ref[:, cols] = _pair_attention(q, kbd, vbd, sinks=_pair_sinks(sink_ref, n))

    qd = _mla_queries(qa_ref, qan_ref, wqb_ref)
    ckv = _rms(kva_ref[...], kvn_ref[...])
    ckv_ref[...] = ckv
    kv = _dot(ckv.astype(BF16), wkvb_ref[...])
    kpe = kpe_ref[...]
    kpe_bd = jnp.concatenate([kpe, pltpu.roll(kpe, ROPE_D, 1)], axis=0)
    nope_w = NH_D * NOPE_D
    for i in range(NH_D // 2):
        cols = slice(i * LANES, (i + 1) * LANES)
        vcols = slice(nope_w + i * LANES, nope_w + (i + 1) * LANES)
        q = jnp.concatenate([qd[:, cols], qd[:, vcols]], axis=1).astype(BF16)
        kbd = jnp.concatenate([_pair_split(kv[:, cols]), kpe_bd], axis=1).astype(BF16)
        vbd = jnp.concatenate([_pair_split(kv[:, vcols]), ones_bd], axis=1).astype(BF16)
        od_ref[:, cols] = _pair_attention(q, kbd, vbd)


def _attn_ctx(proj, params, seq, sub_layer, carried=None):
    n = proj[0].shape[0]
    nseq = n // seq
    row = lambda s: (s, 0)
    in_specs = [pl.BlockSpec((seq, wd), row) for wd in ODD_WIDTHS]
    in_specs += [_layer_spec(p, sub_layer) for p in params]
    args = [*proj, *params]
    half = NH_C * HD_C
    aliases = {}
    if carried is None:
        ckv_spec = pl.BlockSpec((None, N_ODD, seq, KV_RANK), lambda s: (s, 0, 0, 0))
    else:
        ckv_spec = pl.BlockSpec((None, None, seq, KV_RANK), lambda s: (s, sub_layer, 0, 0))
        aliases = {len(args): 2}
        in_specs.append(pl.BlockSpec(memory_space=pl.ANY))
        args.append(carried)
    return pl.pallas_call(
        functools.partial(_attn_ctx_kernel, seq=seq, n_carried=len(aliases), slot=sub_layer),
        grid=(nseq,), in_specs=in_specs,
        out_specs=[pl.BlockSpec((seq, half), row), pl.BlockSpec((seq, half), row), ckv_spec],
        out_shape=[jax.ShapeDtypeStruct((n, half), F32), jax.ShapeDtypeStruct((n, half), F32),
                   jax.ShapeDtypeStruct((nseq, N_ODD, seq, KV_RANK), F32)],
        input_output_aliases=aliases, compiler_params=_params("arbitrary"), name="attn_ctx",
    )(*args)


def _rope(x, cos, sin, half):
    parts = []
    lane = _lane_iota((x.shape[0], LANES))
    first = (lane & (2 * half - 1)) < half
    for i in range(x.shape[1] // LANES):
        xi = x[:, i * LANES:(i + 1) * LANES]
        partner = jnp.where(first, -pltpu.roll(xi, LANES - half, 1), pltpu.roll(xi, half, 1))
        parts.append(xi * cos + partner * sin)
    return parts[0] if len(parts) == 1 else jnp.concatenate(parts, axis=1)


def _attn_lat_kernel(qc_ref, qa_ref, ropeq_ref, kc_ref, vc_ref, kva_ref, kpe_ref, kctx_ref, vctx_ref, ckvctx_ref,
                     kpectx_ref, rope_ref, sink_ref, qan_ref, kvn_ref, wqb_ref, wkvb_ref, oc_ref, od_ref,
                     kwin_s, vwin_s, kext_s, vext_s, *, seq, past):
    qi = pl.program_id(1)
    nope_w = NH_D * NOPE_D
    n_all = past + seq
    ctx0 = 2 * WINDOW + seq

    @pl.when(qi == 0)
    def _():
        zeros = jnp.zeros((WINDOW, LANES), BF16)
        for ref, lat, ctx in ((kwin_s, _rope(kc_ref[...], rope_ref[0], rope_ref[1], HD_C // 2), kctx_ref[...]),
                              (vwin_s, vc_ref[...], vctx_ref[...])):
            lat_sw, ctx_sw = pltpu.roll(lat, HD_C, 1), pltpu.roll(ctx, HD_C, 1)
            for kh in range(NKV_C):
                lat_bd = _shared_split(lat, lat_sw, kh).astype(BF16)
                ctx_bd = _shared_split(ctx, ctx_sw, kh).astype(BF16)
                for i in range(2):
                    ref[kh, i, 0:WINDOW, :] = zeros
                    ref[kh, i, WINDOW:WINDOW + seq, :] = lat_bd[i * seq:(i + 1) * seq]
                    ref[kh, i, WINDOW + seq:ctx0, :] = zeros
                    ref[kh, i, ctx0:, :] = ctx_bd[i * past:(i + 1) * past]
        ckv = _rms(kva_ref[...], kvn_ref[...])
        kv = jnp.concatenate([_dot(ckvctx_ref[...].astype(BF16), wkvb_ref[...]),
                              _dot(ckv.astype(BF16), wkvb_ref[...])], axis=0)
        kpe = jnp.concatenate([kpectx_ref[...], _rope(kpe_ref[...], rope_ref[2], rope_ref[3], ROPE_D // 2)], axis=0)
        kpe_bd = jnp.concatenate([kpe, pltpu.roll(kpe, ROPE_D, 1)], axis=0).astype(BF16)
        ones_bd = _pair_split(jnp.ones((n_all, LANES), F32)).astype(BF16)
        for i in range(NH_D // 2):
            kext_s[i, :, 0:LANES] = _pair_split(kv[:, i * LANES:(i + 1) * LANES]).astype(BF16)
            kext_s[i, :, LANES:] = kpe_bd
            vext_s[i, :, 0:LANES] = _pair_split(kv[:, nope_w + i * LANES:nope_w + (i + 1) * LANES]).astype(BF16)
            vext_s[i, :, LANES:] = ones_bd

    r0 = pl.multiple_of(qi * Q_TILE, Q_TILE)
    nloc = Q_TILE + 2 * WINDOW
    n_keys = nloc + past
    qr = _rope(qc_ref[...], ropeq_ref[0], ropeq_ref[1], HD_C // 2) * (HD_C ** -0.5)
    ti = r0 + _row_iota((Q_TILE, n_keys))
    col = _lane_iota((Q_TILE, n_keys))
    pos = r0 - WINDOW + col
    valid = (col >= nloc) | ((jnp.abs(ti - pos) <= WINDOW) & (pos >= 0) & (pos < seq))
    ones_bd = _pair_split(jnp.ones((n_keys, LANES), F32)).astype(BF16)
    for kh in range(NKV_C):
        kbd = jnp.concatenate([kwin_s[kh, 0, pl.ds(r0, nloc), :], kwin_s[kh, 0, ctx0:, :],
                               kwin_s[kh, 1, pl.ds(r0, nloc), :], kwin_s[kh, 1, ctx0:, :]], axis=0)
        vals = jnp.concatenate([vwin_s[kh, 0, pl.ds(r0, nloc), :], vwin_s[kh, 0, ctx0:, :],
                                vwin_s[kh, 1, pl.ds(r0, nloc), :], vwin_s[kh, 1, ctx0:, :]], axis=0)
        vbd = jnp.concatenate([vals, ones_bd], axis=1)
        for n in range(kh * G_C, (kh + 1) * G_C, 2):
            cols = slice(n * HD_C, (n + 2) * HD_C)
            oc_ref[:, cols] = _pair_attention(qr[:, cols].astype(BF16), kbd, vbd, sinks=_pair_sinks(sink_ref, n),
                                              valid=valid)

    qd = _mla_queries(qa_ref, qan_ref, wqb_ref)
    q_pe = _rope(qd[:, nope_w:], ropeq_ref[2], ropeq_ref[3], ROPE_D // 2)
    for i in range(NH_D // 2):
        cols = slice(i * LANES, (i + 1) * LANES)
        q = jnp.concatenate([qd[:, cols], q_pe[:, cols]], axis=1).astype(BF16)
        od_ref[:, cols] = _pair_attention(q, kext_s[i], vext_s[i])


def _attn_lat(proj, caches, rope, params, seq, sub_layer):
    qc, kc, vc, qa, kva, kpe = proj
    n = qc.shape[0]
    past = caches[0].shape[2]
    nq = seq // Q_TILE
    qrow = lambda b, q: (b * nq + q, 0)
    krow = lambda b, q: (b, 0)
    kvw = NKV_C * HD_C
    in_specs = [pl.BlockSpec((Q_TILE, NH_C * HD_C), qrow), pl.BlockSpec((Q_TILE, Q_RANK), qrow),
                pl.BlockSpec((4, Q_TILE, LANES), lambda b, q: (0, q, 0)),
                pl.BlockSpec((seq, kvw), krow), pl.BlockSpec((seq, kvw), krow),
                pl.BlockSpec((seq, KV_RANK), krow), pl.BlockSpec((seq, LANES), krow)]
    in_specs += [pl.BlockSpec((None, None, past, LANES), lambda b, q: (b, sub_layer, 0, 0)) for _ in caches]
    in_specs += [pl.BlockSpec(rope.shape, lambda b, q: (0, 0, 0))]
    in_specs += [_layer_spec(p, sub_layer) for p in params]
    half = NH_C * HD_C
    win_rows = 2 * WINDOW + seq + past
    return pl.pallas_call(
        functools.partial(_attn_lat_kernel, seq=seq, past=past),
        grid=(n // seq, nq), in_specs=in_specs,
        out_specs=[pl.BlockSpec((Q_TILE, half), qrow), pl.BlockSpec((Q_TILE, half), qrow)],
        out_shape=[jax.ShapeDtypeStruct((n, half), F32), jax.ShapeDtypeStruct((n, half), F32)],
        scratch_shapes=[pltpu.VMEM((NKV_C, 2, win_rows, LANES), BF16), pltpu.VMEM((NKV_C, 2, win_rows, LANES), BF16),
                        pltpu.VMEM((NH_D // 2, 2 * (past + seq), 2 * LANES), BF16),
                        pltpu.VMEM((NH_D // 2, 2 * (past + seq), 2 * LANES), BF16)],
        compiler_params=_params("arbitrary", "arbitrary"), name="attn_lat",
    )(qc, qa, rope, kc, vc, kva, kpe, *caches, rope, *params)


def _pad_lanes(x, width=LANES):
    return jnp.pad(x, [(0, 0)] * (x.ndim - 1) + [(0, width - x.shape[-1])])


def _on_lanes(x):
    return jnp.broadcast_to(x[..., None], x.shape + (LANES,))


def _even_weights_kernel(w_ref, o_ref, gt_ref):
    o0 = 2 * A_QK + 2 * A_V
    z0 = o0 + 4 * NH_A
    d0 = z0 + B_INNER + B_XBC
    o_ref[:, 0:o0] = w_ref[:, 0:o0].astype(BF16)
    o_ref[:, o0:] = w_ref[:, z0:d0].astype(BF16)
    rows = w_ref.shape[0]
    gates = jnp.concatenate([w_ref[:, o0:z0], w_ref[:, d0:], jnp.zeros((rows, LANES - GATE_ROWS), F32)], axis=1)
    gt_ref[...] = gates.T[0:GATE_ROWS, :].astype(BF16)


def _even_in_weights(w):
    layers, d, cols = w.shape
    return pl.pallas_call(
        _even_weights_kernel,
        grid=(layers, d // Q_TILE),
        in_specs=[pl.BlockSpec((None, Q_TILE, cols), lambda l, i: (l, i, 0))],
        out_specs=[pl.BlockSpec((None, Q_TILE, sum(EVEN_WIDTHS)), lambda l, i: (l, i, 0)),
                   pl.BlockSpec((None, GATE_ROWS, Q_TILE), lambda l, i: (l, 0, i))],
        out_shape=[jax.ShapeDtypeStruct((layers, d, sum(EVEN_WIDTHS)), BF16),
                   jax.ShapeDtypeStruct((layers, GATE_ROWS, d), BF16)],
        compiler_params=_params("arbitrary", "arbitrary"), name="even_weights",
    )(w)


def _mla_query_weights(w):
    lead = w.shape[:-1]
    w4 = w.reshape(lead + (NH_D // 2, 2, NOPE_D + ROPE_D))
    nope = w4[..., :NOPE_D].reshape(lead + (NH_D * NOPE_D,))
    pe = _pad_lanes(w4[..., NOPE_D:].reshape(lead + (NH_D // 2, 2 * ROPE_D)))
    return jnp.concatenate([nope, pe.reshape(lead + (NH_D // 2 * LANES,))], axis=-1).astype(BF16)


def _mla_kv_weights(w):
    lead = w.shape[:-1]
    w3 = w.reshape(lead + (NH_D, NOPE_D + V_D))
    return jnp.concatenate([w3[..., :NOPE_D].reshape(lead + (NH_D * NOPE_D,)),
                            w3[..., NOPE_D:].reshape(lead + (NH_D * V_D,))], axis=-1).astype(BF16)


def _rope_tables(rows):
    def table(rot_dim):
        quarter = rot_dim // 4
        inv = ROPE_BASE ** (-jnp.arange(quarter, dtype=F32) / quarter)
        r = jnp.repeat(jnp.arange(rows, dtype=F32), GRID_W)
        col = jnp.tile(jnp.arange(GRID_W, dtype=F32), rows)
        ang = jnp.concatenate([r[:, None] * inv, col[:, None] * inv], axis=-1)
        reps = LANES // (rot_dim // 2)
        return jnp.tile(jnp.cos(ang), (1, reps)), jnp.tile(jnp.sin(ang), (1, reps))
    cos_c, sin_c = table(HD_C)
    cos_d, sin_d = table(ROPE_D)
    return jnp.stack([cos_c, sin_c, cos_d, sin_d])


def kernel(x_prompt, x_sample, c, state_mlstm_C, state_mlstm_n, state_mlstm_m, state_ssd, cache_gqa_k, cache_gqa_v,
           cache_mla_ckv, cache_mla_kpe, c_ctx, w_ada, b_ada, norm_g, w_up, w_down, w_in_even, conv_a_w, conv_a_b,
           conv_b_w, conv_b_b, gate_b, a_norm_w, dt_bias, a_log, d_skip, b_norm_w, w_out_even, w_in_odd, sink,
           q_a_norm, kv_a_norm, w_q_b, w_kv_b, w_out_odd):
    xp = x_prompt.reshape(BATCH * SEQ, D_MODEL)
    xs = x_sample.reshape(DEC_BATCH * DEC_SEQ, D_MODEL)
    cond = jnp.concatenate([c_ctx[None, :], c, jnp.zeros((MOD_ROWS - 1 - DEC_BATCH, D_MODEL), F32)], axis=0)
    mods = _modulations(cond, w_ada, b_ada)
    rope = _rope_tables(DEC_SEQ // GRID_W)

    w_even, w_gates_t = _even_in_weights(w_in_even)
    a_params = (conv_a_w, conv_a_b[:, None, :], _on_lanes(gate_b), a_norm_w[:, None, :])
    b_params = (conv_b_w, conv_b_b[:, None, :], _on_lanes(dt_bias.reshape(N_EVEN, 2 * NH_B)),
                _on_lanes(a_log.reshape(N_EVEN, 2 * NH_B)), jnp.repeat(d_skip, HP_B, axis=1)[:, None, :],
                b_norm_w[:, None, :])
    n0 = state_mlstm_n[..., None]
    mem_in = (jnp.concatenate([state_mlstm_C, jnp.broadcast_to(n0, n0.shape[:-1] + (LANES,))], axis=-1),
              _on_lanes(state_mlstm_m.reshape(DEC_BATCH, N_EVEN, 2 * NH_A)))
    w_odd = _pad_lanes(w_in_odd, sum(ODD_WIDTHS)).astype(BF16)
    o_params = (_pad_lanes(sink)[:, None, :], q_a_norm[:, None, :], kv_a_norm[:, None, :],
                _mla_query_weights(w_q_b), _mla_kv_weights(w_kv_b))
    caches = (cache_gqa_k.reshape(DEC_BATCH, N_ODD, PAST_LEN, NKV_C * HD_C),
              cache_gqa_v.reshape(DEC_BATCH, N_ODD, PAST_LEN, NKV_C * HD_C),
              cache_mla_ckv, _pad_lanes(cache_mla_kpe))

    new_k, new_v, new_kpe = [], [], []
    mem_state, ssd_state, ckv_state = None, None, None
    for l in range(DEPTH):
        j = l // 2
        if l % 2 == 0:
            qk, v, o, z, xbc, g = _project(xp, mods, l, False, norm_g, w_even, EVEN_WIDTHS, w_gates_t)
            a1p, *mem_state = _mlstm(qk, v, o, g, a_params, SEQ, j, carried=mem_state)
            a2p, ssd_state = _ssd(xbc, z, g, b_params, SEQ, j, carried=ssd_state)
            qk, v, o, z, xbc, g = _project(xs, mods, l, True, norm_g, w_even, EVEN_WIDTHS, w_gates_t)
            a1s, = _mlstm(qk, v, o, g, a_params, DEC_SEQ, j, state=mem_in)
            a2s, = _ssd(xbc, z, g, b_params, DEC_SEQ, j, state=state_ssd)
            w_out = w_out_even
        else:
            proj = _project(xp, mods, l, False, norm_g, w_odd, ODD_WIDTHS)
            a1p, a2p, ckv_state = _attn_ctx(proj, o_params, SEQ, j, carried=ckv_state)
            new_k.append(proj[1].reshape(BATCH, SEQ, NKV_C, HD_C))
            new_v.append(proj[2].reshape(BATCH, SEQ, NKV_C, HD_C))
            new_kpe.append(proj[5][:, :ROPE_D].reshape(BATCH, SEQ, ROPE_D))
            proj = _project(xs, mods, l, True, norm_g, w_odd, ODD_WIDTHS)
            a1s, a2s = _attn_lat(proj, caches, rope, o_params, DEC_SEQ, j)
            w_out = w_out_odd
        xp, xs = _channel((a1p, a2p, xp), (a1s, a2s, xs), mods, l, norm_g, w_out, w_up, w_down)

    new_c, new_n, new_m = mem_state
    return (xp.reshape(BATCH, SEQ, D_MODEL), xs.reshape(DEC_BATCH, DEC_SEQ, D_MODEL),
            new_c, new_n, new_m[..., 0].reshape(BATCH, N_EVEN, 2, NH_A), ssd_state,
            jnp.stack(new_k, axis=1), jnp.stack(new_v, axis=1), ckv_state, jnp.stack(new_kpe, axis=1))
```

```python
import functools

import jax
import jax.numpy as jnp
from jax import lax
from jax.experimental import pallas as pl
from jax.experimental.pallas import tpu as pltpu

F32 = jnp.float32
BF16 = jnp.bfloat16

D_MODEL = 1024
BATCH = 32
SEQ = 256
DEPTH = 4
DEC_BATCH = 2
DEC_SEQ = 1024
PAST_LEN = 256
GRID_W = 64
N_EVEN = (DEPTH + 1) // 2
N_ODD = DEPTH // 2
EPS = 1e-6
CONV_K = 5
NH_A = 4
DK_A = 128
DV_A = 128
A_QK = NH_A * DK_A
A_V = NH_A * DV_A
NH_B = 8
HP_B = 64
DSTATE = 128
NG_B = 2
R_B = NH_B // NG_B
B_INNER = NH_B * HP_B
B_BC = NG_B * DSTATE
B_XBC = B_INNER + 2 * B_BC
NH_C = 8
NKV_C = 2
G_C = NH_C // NKV_C
HD_C = 64
WINDOW = 128
NH_D = 8
Q_RANK = 256
KV_RANK = 128
NOPE_D = 64
ROPE_D = 32
V_D = 64
MLA_SCALE = (NOPE_D + ROPE_D) ** -0.5
D_FF = 4 * D_MODEL
ROPE_BASE = 10000.0

LANES = 128
VMEM_LIMIT_BYTES = 56 * 1024 * 1024
ROW_TILE = 512
FF_TILE = 1024
SUB_ROWS = 512
STAGE_ROWS = 512
Q_TILE = 256
ADA_TILE = 1536
MOD_ROWS = 8
GATE_ROWS = 4 * NH_A + 2 * NH_B

EVEN_WIDTHS = (2 * A_QK, A_V, A_V, B_INNER, B_XBC)
ODD_WIDTHS = (NH_C * HD_C, NKV_C * HD_C, NKV_C * HD_C, Q_RANK, KV_RANK, LANES)

_NT = (((1,), (1,)), ((), ()))


def _params(*sem):
    return pltpu.CompilerParams(dimension_semantics=sem, vmem_limit_bytes=VMEM_LIMIT_BYTES)


def _rms(x, g):
    return x * lax.rsqrt(jnp.mean(x * x, axis=-1, keepdims=True) + EPS) * g


def _silu(x):
    return x * jax.nn.sigmoid(x)


def _softplus(x):
    return jnp.maximum(x, 0.0) + jnp.log1p(jnp.exp(-jnp.abs(x)))


def _dot(a, b):
    return jnp.dot(a, b, preferred_element_type=F32)


def _dot_nt(a, b):
    return lax.dot_general(a, b, _NT, preferred_element_type=F32)


def _layer_spec(arr, layer):
    tail = arr.shape[1:]
    zeros = (0,) * len(tail)
    return pl.BlockSpec((None,) + tail, lambda *_: (layer,) + zeros)


def _ada_kernel(c_ref, w_ref, b_ref, o_ref):
    s = _silu(c_ref[...]).astype(BF16)
    o_ref[...] = _dot(s, w_ref[...].astype(BF16)) + b_ref[...]


def _modulations(cond, w_ada, b_ada):
    out = pl.pallas_call(
        _ada_kernel,
        grid=(DEPTH, 6 * D_MODEL // ADA_TILE),
        in_specs=[pl.BlockSpec((MOD_ROWS, D_MODEL), lambda l, n: (0, 0)),
                  pl.BlockSpec((None, D_MODEL, ADA_TILE), lambda l, n: (l, 0, n)),
                  pl.BlockSpec((None, 1, ADA_TILE), lambda l, n: (l, 0, n))],
        out_specs=pl.BlockSpec((None, MOD_ROWS, ADA_TILE), lambda l, n: (l, 0, n)),
        out_shape=jax.ShapeDtypeStruct((DEPTH, MOD_ROWS, 6 * D_MODEL), F32),
        compiler_params=_params("arbitrary", "arbitrary"),
        name="ada",
    )(cond, w_ada, b_ada.reshape(DEPTH, 1, 6 * D_MODEL))
    return out.reshape(DEPTH, MOD_ROWS, 6, D_MODEL)


def _mod_spec(layer, latent):
    if latent:
        per_seq = DEC_SEQ // ROW_TILE
        return pl.BlockSpec((None, None, 6, D_MODEL), lambda i, *_: (layer, 1 + i // per_seq, 0, 0))
    return pl.BlockSpec((None, None, 6, D_MODEL), lambda i, *_: (layer, 0, 0, 0))


def _proj_kernel(x_ref, mod_ref, g_ref, w_ref, *rest, widths, has_gates):
    h = _rms(x_ref[...], g_ref[0:1, :]) * (1.0 + mod_ref[1:2, :]) + mod_ref[0:1, :]
    hb = h.astype(BF16)
    o_refs = rest[1:] if has_gates else rest
    off = 0
    for o_ref, wd in zip(o_refs, widths):
        o_ref[...] = _dot(hb, w_ref[:, off:off + wd])
        off += wd
    if has_gates:
        o_refs[-1][...] = _dot_nt(rest[0][...], hb)


def _project(x, mods, layer, latent, gains, w, widths, w_gates_t=None):
    n = x.shape[0]
    fixed = lambda i: (0, 0)
    in_specs = [pl.BlockSpec((ROW_TILE, D_MODEL), lambda i: (i, 0)), _mod_spec(layer, latent),
                _layer_spec(gains, layer), pl.BlockSpec(w.shape, fixed)]
    args = [x, mods, gains, w]
    out_specs = [pl.BlockSpec((ROW_TILE, wd), lambda i: (i, 0)) for wd in widths]
    out_shape = [jax.ShapeDtypeStruct((n, wd), F32) for wd in widths]
    if w_gates_t is not None:
        gate_rows = w_gates_t.shape[0]
        in_specs.append(pl.BlockSpec(w_gates_t.shape, fixed))
        args.append(w_gates_t)
        out_specs.append(pl.BlockSpec((gate_rows, ROW_TILE), lambda i: (0, i)))
        out_shape.append(jax.ShapeDtypeStruct((gate_rows, n), F32))
    return pl.pallas_call(
        functools.partial(_proj_kernel, widths=widths, has_gates=w_gates_t is not None),
        grid=(n // ROW_TILE,), in_specs=in_specs, out_specs=out_specs, out_shape=out_shape,
        compiler_params=_params("arbitrary"),
        name="proj",
    )(*args)


def _weight_chunks(layer, sub_layer, wo_hbm, wu_hbm, wd_hbm, wo_s, wu_s, wd_s):
    chunks = []
    for r in range(0, D_MODEL, STAGE_ROWS):
        chunks.append((wo_hbm.at[sub_layer, pl.ds(r, STAGE_ROWS), :], wo_s.at[pl.ds(r, STAGE_ROWS), :]))
    for r in range(0, D_MODEL, STAGE_ROWS):
        for c in range(0, D_FF, D_MODEL):
            chunks.append((wu_hbm.at[layer, pl.ds(r, STAGE_ROWS), pl.ds(c, D_MODEL)],
                           wu_s.at[pl.ds(r, STAGE_ROWS), pl.ds(c, D_MODEL)]))
    for r in range(0, D_FF, STAGE_ROWS):
        chunks.append((wd_hbm.at[layer, pl.ds(r, STAGE_ROWS), :], wd_s.at[pl.ds(r, STAGE_ROWS), :]))
    return chunks


def _channel_kernel(a1p_ref, a2p_ref, xp_ref, a1s_ref, a2s_ref, xs_ref, mod_ref, g_ref, wo_hbm, wu_hbm, wd_hbm,
                    op_ref, os_ref, wo_s, wu_s, wd_s, stage, sem, *, layer, sub_layer, prompt_steps):
    step = pl.program_id(0)

    @pl.when(step == 0)
    def _():
        chunks = _weight_chunks(layer, sub_layer, wo_hbm, wu_hbm, wd_hbm, wo_s, wu_s, wd_s)
        copies = [pltpu.make_async_copy(src, stage.at[k % 2], sem.at[k % 2]) for k, (src, _) in enumerate(chunks)]
        copies[0].start()
        for k, (_, dst) in enumerate(chunks):
            if k + 1 < len(chunks):
                copies[k + 1].start()
            copies[k].wait()
            dst[...] = stage[k % 2].astype(BF16)

    def rows_block(a1_ref, a2_ref, x_ref, o_ref):
        half = a1_ref.shape[1]
        for r0 in range(0, ROW_TILE, SUB_ROWS):
            rows = slice(r0, r0 + SUB_ROWS)
            y = (_dot(a1_ref[rows, :].astype(BF16), wo_s[0:half, :])
                 + _dot(a2_ref[rows, :].astype(BF16), wo_s[half:, :]))
            x1 = x_ref[rows, :] + mod_ref[2:3, :] * _rms(y, g_ref[1:2, :])
            h = (_rms(x1, g_ref[2:3, :]) * (1.0 + mod_ref[4:5, :]) + mod_ref[3:4, :]).astype(BF16)
            acc = None
            for c in range(0, D_FF, FF_TILE):
                u = jnp.square(jnp.maximum(_dot(h, wu_s[:, c:c + FF_TILE]), 0.0)).astype(BF16)
                part = _dot(u, wd_s[c:c + FF_TILE, :])
                acc = part if acc is None else acc + part
            o_ref[rows, :] = x1 + mod_ref[5:6, :] * _rms(acc, g_ref[3:4, :])

    @pl.when(step < prompt_steps)
    def _():
        rows_block(a1p_ref, a2p_ref, xp_ref, op_ref)

    @pl.when(step >= prompt_steps)
    def _():
        rows_block(a1s_ref, a2s_ref, xs_ref, os_ref)


def _channel(prompt, latent, mods, layer, gains, w_out, w_up, w_down):
    n_p, n_s = prompt[2].shape[0], latent[2].shape[0]
    steps_p, steps_s = n_p // ROW_TILE, n_s // ROW_TILE
    per_seq = DEC_SEQ // ROW_TILE
    row_p = lambda i: (jnp.minimum(i, steps_p - 1), 0)
    row_s = lambda i: (jnp.maximum(i - steps_p, 0), 0)
    mod_spec = pl.BlockSpec((None, None, 6, D_MODEL),
                            lambda i: (layer, jnp.where(i < steps_p, 0, 1 + (i - steps_p) // per_seq), 0, 0))
    hbm = pl.BlockSpec(memory_space=pl.ANY)
    specs = lambda arrs, row: [pl.BlockSpec((ROW_TILE, a.shape[1]), row) for a in arrs]
    return pl.pallas_call(
        functools.partial(_channel_kernel, layer=layer, sub_layer=layer // 2, prompt_steps=steps_p),
        grid=(steps_p + steps_s,),
        in_specs=specs(prompt, row_p) + specs(latent, row_s) + [mod_spec, _layer_spec(gains, layer), hbm, hbm, hbm],
        out_specs=[pl.BlockSpec((ROW_TILE, D_MODEL), row_p), pl.BlockSpec((ROW_TILE, D_MODEL), row_s)],
        out_shape=[jax.ShapeDtypeStruct((n_p, D_MODEL), F32), jax.ShapeDtypeStruct((n_s, D_MODEL), F32)],
        scratch_shapes=[pltpu.VMEM((D_MODEL, D_MODEL), BF16), pltpu.VMEM((D_MODEL, D_FF), BF16),
                        pltpu.VMEM((D_FF, D_MODEL), BF16), pltpu.VMEM((2, STAGE_ROWS, D_MODEL), F32),
                        pltpu.SemaphoreType.DMA((2,))],
        compiler_params=_params("arbitrary"),
        name="channel",
    )(*prompt, *latent, mods, gains, w_out, w_up, w_down)


def _row_iota(shape):
    return lax.broadcasted_iota(jnp.int32, shape, 0)


def _lane_iota(shape):
    return lax.broadcasted_iota(jnp.int32, shape, 1)


def _pair_lanes(shape):
    return _lane_iota(shape) < LANES // 2


def _cumsum_lanes(x, n_fwd):
    t = x.shape[1]
    si, ti = _row_iota((t, t)), _lane_iota((t, t))
    upper = jnp.where(si <= ti, 1.0, 0.0).astype(BF16)
    lower = jnp.where(si >= ti, 1.0, 0.0).astype(BF16)
    hi = x.astype(BF16)
    rest = x - hi.astype(F32)
    mid = rest.astype(BF16)
    lo = (rest - mid.astype(F32)).astype(BF16)
    pre = _dot(hi, upper) + _dot(mid, upper) + _dot(lo, upper)
    suf = _dot(hi, lower) + _dot(mid, lower) + _dot(lo, lower)
    return jnp.where(_row_iota(x.shape) < n_fwd, pre, suf)


def _cummax_lanes(x, n_fwd):
    t = x.shape[1]
    lane = _lane_iota(x.shape)
    pre, suf = x, x
    k = 1
    while k < t:
        pre = jnp.maximum(pre, jnp.where(lane >= k, pltpu.roll(pre, k, 1), -jnp.inf))
        suf = jnp.maximum(suf, jnp.where(lane < t - k, pltpu.roll(suf, t - k, 1), -jnp.inf))
        k *= 2
    return jnp.where(_row_iota(x.shape) < n_fwd, pre, suf)


def _columns(row_arrays):
    t = row_arrays[0].shape[1]
    used = sum(a.shape[0] for a in row_arrays)
    return jnp.concatenate(list(row_arrays) + [jnp.zeros((LANES - used, t), F32)], axis=0).T


def _dwconv_silu(x, w, b):
    t = x.shape[0]
    row = _row_iota(x.shape)
    acc = x * w[CONV_K // 2:CONV_K // 2 + 1, :] + b
    for j in range(CONV_K):
        d = j - CONV_K // 2
        if d == 0:
            continue
        shifted = pltpu.roll(x, (-d) % t, 0)
        valid = (row >= -d) if d < 0 else (row < t - d)
        acc = acc + jnp.where(valid, shifted, 0.0) * w[j:j + 1, :]
    return _silu(acc)


def _causal_exponent(expo, r0, k0, reverse):
    ti = r0 + _row_iota(expo.shape)
    si = k0 + _lane_iota(expo.shape)
    keep = (si >= ti) if reverse else (si <= ti)
    return jnp.where(keep, expo, -jnp.inf)


def _pair_split(x):
    first = _pair_lanes(x.shape)
    zero = jnp.zeros_like(x)
    return jnp.concatenate([jnp.where(first, x, zero), jnp.where(first, zero, x)], axis=0)


def _key_range(d, r0, seq):
    return (0, r0 + Q_TILE) if d == 0 else (r0, seq)


def _mlstm_kernel(*refs, seq, has_state, emit_state, n_carried, slot):
    assert not (has_state and emit_state)
    it = iter(refs)
    qk_ref, v_ref, o_ref, g_ref, cw_ref, cb_ref, gb_ref, anw_ref = (next(it) for _ in range(8))
    if has_state:
        c0_ref, m0_ref = next(it), next(it)
    for _ in range(n_carried):
        next(it)
    ha_ref = next(it)
    if emit_state:
        cn_ref, nn_ref, mn_ref = next(it), next(it), next(it)
        if n_carried == 0:
            for other in range(N_EVEN):
                if other != slot:
                    cn_ref[other] = jnp.zeros(cn_ref.shape[1:], F32)
                    nn_ref[other] = jnp.zeros(nn_ref.shape[1:], F32)
                    mn_ref[other] = jnp.zeros(mn_ref.shape[1:], F32)
            cn_ref, nn_ref, mn_ref = cn_ref.at[slot], nn_ref.at[slot], mn_ref.at[slot]

    n_ch = 2 * NH_A
    log_i = g_ref[0:n_ch, :] + gb_ref[0:n_ch, 0:1]
    f_pre = g_ref[n_ch:2 * n_ch, :] + gb_ref[n_ch:2 * n_ch, 0:1]
    log_f = jnp.minimum(f_pre, 0.0) - jnp.log1p(jnp.exp(-jnp.abs(f_pre)))
    b = _cumsum_lanes(log_f, NH_A)
    a = log_i - b
    m_run = _cummax_lanes(a, NH_A)
    if has_state:
        m0 = m0_ref[:, 0:1]
        m_run = jnp.maximum(m_run, m0)
    else:
        m_run = jnp.maximum(m_run, 0.0)
    by_time = [m_run, jnp.exp(-(b + m_run))]
    if has_state:
        by_time.append(jnp.exp(m0 - m_run))
    if emit_state:
        fwd = _row_iota((n_ch, 1)) < NH_A
        b_last = jnp.where(fwd, b[:, seq - 1:seq], b[:, 0:1])
        m_last = jnp.where(fwd, m_run[:, seq - 1:seq], m_run[:, 0:1])
        mn_ref[...] = jnp.broadcast_to(b_last + m_last, (n_ch, LANES))
        by_time.append(jnp.exp(a - m_last))
    cols = _columns(by_time)

    ones = jnp.ones((seq, LANES), F32)
    for h in range(NH_A):
        cq = slice(h * DK_A, (h + 1) * DK_A)
        ck = slice(A_QK + h * DK_A, A_QK + (h + 1) * DK_A)
        cv = slice(h * DV_A, (h + 1) * DV_A)
        q = _dwconv_silu(qk_ref[:, cq], cw_ref[:, cq], cb_ref[:, cq])
        k = _dwconv_silu(qk_ref[:, ck], cw_ref[:, ck], cb_ref[:, ck]) * (DK_A ** -0.5)
        qb = q.astype(BF16)
        kb = k.astype(BF16)
        vh = v_ref[:, cv]
        vaug = jnp.concatenate([vh, ones], axis=1).astype(BF16)
        for r0 in range(0, seq, Q_TILE):
            rows = slice(r0, r0 + Q_TILE)
            s = _dot_nt(qb[rows], kb)
            hsum = None
            for d in range(2):
                c = d * NH_A + h
                k0, k1 = _key_range(d, r0, seq)
                expo = _causal_exponent(a[c:c + 1, k0:k1] - cols[rows, c:c + 1], r0, k0, d == 1)
                p = (s[:, k0:k1] * jnp.exp(expo)).astype(BF16)
                acc = _dot(p, vaug[k0:k1])
                if has_state:
                    acc = acc + cols[rows, 2 * n_ch + c:2 * n_ch + c + 1] * _dot(qb[rows], c0_ref[d, h].astype(BF16))
                hd = acc[:, 0:DV_A] / jnp.maximum(jnp.abs(acc[:, DV_A:]), cols[rows, n_ch + c:n_ch + c + 1])
                hsum = hd if hsum is None else hsum + hd
            og = jax.nn.sigmoid(o_ref[rows, cv]) * hsum
            ha_ref[rows, cv] = _rms(og, anw_ref[:, cv])
        if emit_state:
            for d in range(2):
                c = d * NH_A + h
                kw = k * cols[:, 2 * n_ch + c:2 * n_ch + c + 1]
                cn_ref[d, h] = _dot(kw.T.astype(BF16), vh.astype(BF16))
                nn_ref[d, h:h + 1, :] = jnp.sum(kw, axis=0, keepdims=True)


def _mlstm(qk, v, o, gates, params, seq, sub_layer, state=None, carried=None):
    n = qk.shape[0]
    nseq = n // seq
    has_state = state is not None
    emit_state = not has_state
    aliases = {}
    row = lambda s: (s, 0)
    in_specs = [pl.BlockSpec((seq, 2 * A_QK), row), pl.BlockSpec((seq, A_V), row), pl.BlockSpec((seq, A_V), row),
                pl.BlockSpec((GATE_ROWS, seq), lambda s: (0, s))]
    in_specs += [_layer_spec(p, sub_layer) for p in params]
    args = [qk, v, o, gates, *params]
    out_specs = [pl.BlockSpec((seq, A_V), row)]
    out_shape = [jax.ShapeDtypeStruct((n, A_V), F32)]
    if has_state:
        c0_aug, m0 = state
        in_specs += [pl.BlockSpec((None, None) + c0_aug.shape[2:], lambda s: (s, sub_layer, 0, 0, 0, 0)),
                     pl.BlockSpec((None, None) + m0.shape[2:], lambda s: (s, sub_layer, 0, 0))]
        args += [c0_aug, m0]
    if emit_state:
        shapes = [(2, NH_A, DK_A, DV_A), (2, NH_A, DK_A), (2 * NH_A, LANES)]
        for shp in shapes:
            zeros = (0,) * len(shp)
            if carried is None:
                out_specs.append(pl.BlockSpec((None, N_EVEN) + shp, lambda s, z=zeros: (s, 0) + z))
            else:
                out_specs.append(pl.BlockSpec((None, None) + shp, lambda s, z=zeros: (s, sub_layer) + z))
            out_shape.append(jax.ShapeDtypeStruct((nseq, N_EVEN) + shp, F32))
        if carried is not None:
            aliases = {len(args) + i: 1 + i for i in range(len(shapes))}
            in_specs += [pl.BlockSpec(memory_space=pl.ANY)] * len(shapes)
            args += list(carried)
    return pl.pallas_call(
        functools.partial(_mlstm_kernel, seq=seq, has_state=has_state, emit_state=emit_state,
                          n_carried=len(aliases), slot=sub_layer),
        grid=(nseq,), in_specs=in_specs, out_specs=out_specs, out_shape=out_shape,
        input_output_aliases=aliases, compiler_params=_params("arbitrary"), name="mlstm",
    )(*args)


def _ssd_kernel(*refs, seq, has_state, emit_state, n_carried, slot):
    assert not (has_state and emit_state)
    it = iter(refs)
    xbc_ref, z_ref, g_ref, cw_ref, cb_ref, dtb_ref, alog_ref, dsk_ref, bnw_ref = (next(it) for _ in range(9))
    if has_state:
        s0_ref = next(it)
    for _ in range(n_carried):
        next(it)
    yb_ref = next(it)
    if emit_state:
        sn_ref = next(it)
        if n_carried == 0:
            for other in range(N_EVEN):
                if other != slot:
                    sn_ref[other] = jnp.zeros(sn_ref.shape[1:], F32)
            sn_ref = sn_ref.at[slot]

    n_ch = 2 * NH_B
    dt = _softplus(g_ref[4 * NH_A:4 * NH_A + n_ch, :] + dtb_ref[:, 0:1])
    acum = _cumsum_lanes(dt * (-jnp.exp(alog_ref[:, 0:1])), NH_B)
    key_shift = acum - jnp.log(dt)
    by_time = [acum]
    if has_state:
        by_time.append(jnp.exp(acum))
    if emit_state:
        fwd = _row_iota((n_ch, 1)) < NH_B
        a_last = jnp.where(fwd, acum[:, seq - 1:seq], acum[:, 0:1])
        by_time.append(jnp.exp(a_last - acum) * dt)
    cols = _columns(by_time)

    gw = R_B * HP_B
    first = _pair_lanes((seq, LANES))
    for g in range(NG_B):
        cx = slice(g * gw, (g + 1) * gw)
        cb_ = slice(B_INNER + g * DSTATE, B_INNER + (g + 1) * DSTATE)
        cc = slice(B_INNER + B_BC + g * DSTATE, B_INNER + B_BC + (g + 1) * DSTATE)
        xg = _dwconv_silu(xbc_ref[:, cx], cw_ref[:, cx], cb_ref[:, cx])
        bg = _dwconv_silu(xbc_ref[:, cb_], cw_ref[:, cb_], cb_ref[:, cb_]).astype(BF16)
        cg = _dwconv_silu(xbc_ref[:, cc], cw_ref[:, cc], cb_ref[:, cc]).astype(BF16)
        xbd = [_pair_split(xg[:, p * LANES:(p + 1) * LANES]).astype(BF16) for p in range(R_B // 2)]
        for r0 in range(0, seq, Q_TILE):
            rows = slice(r0, r0 + Q_TILE)
            cb_scores = _dot_nt(cg[rows], bg)
            ys = []
            for p in range(R_B // 2):
                weights, inputs = [], []
                for d in range(2):
                    k0, k1 = _key_range(d, r0, seq)
                    for i in range(2):
                        c = d * NH_B + g * R_B + 2 * p + i
                        expo = _causal_exponent(cols[rows, c:c + 1] - key_shift[c:c + 1, k0:k1], r0, k0, d == 1)
                        weights.append((cb_scores[:, k0:k1] * jnp.exp(expo)).astype(BF16))
                        inputs.append(xbd[p][i * seq + k0:i * seq + k1])
                yp = _dot(jnp.concatenate(weights, axis=1), jnp.concatenate(inputs, axis=0))
                if has_state:
                    h0 = g * R_B + 2 * p
                    for d in range(2):
                        c = d * NH_B + h0
                        carry = jnp.where(_pair_lanes((Q_TILE, LANES)), cols[rows, n_ch + c:n_ch + c + 1],
                                          cols[rows, n_ch + c + 1:n_ch + c + 2])
                        s0_pair = s0_ref[d, h0:h0 + 2].reshape(2 * HP_B, DSTATE).astype(BF16)
                        yp = yp + carry * _dot_nt(cg[rows], s0_pair)
                ys.append(yp)
            y = jnp.concatenate(ys, axis=1) + dsk_ref[:, cx] * xg[rows]
            y = y * _silu(z_ref[rows, cx])
            yb_ref[rows, cx] = _rms(y, bnw_ref[:, cx])
        if emit_state:
            for d in range(2):
                c0 = n_ch + d * NH_B + g * R_B
                spread = jnp.concatenate([jnp.where(first, cols[:, c0 + 2 * p:c0 + 2 * p + 1],
                                                    cols[:, c0 + 2 * p + 1:c0 + 2 * p + 2])
                                          for p in range(R_B // 2)], axis=1)
                sn = _dot((xg * spread).T.astype(BF16), bg)
                for r in range(R_B):
                    sn_ref[d, g * R_B + r] = sn[r * HP_B:(r + 1) * HP_B, :]


def _ssd(xbc, z, gates, params, seq, sub_layer, state=None, carried=None):
    n = xbc.shape[0]
    nseq = n // seq
    has_state = state is not None
    emit_state = not has_state
    aliases = {}
    row = lambda s: (s, 0)
    in_specs = [pl.BlockSpec((seq, B_XBC), row), pl.BlockSpec((seq, B_INNER), row),
                pl.BlockSpec((GATE_ROWS, seq), lambda s: (0, s))]
    in_specs += [_layer_spec(p, sub_layer) for p in params]
    args = [xbc, z, gates, *params]
    out_specs = [pl.BlockSpec((seq, B_INNER), row)]
    out_shape = [jax.ShapeDtypeStruct((n, B_INNER), F32)]
    state_spec = pl.BlockSpec((None, None, 2, NH_B, HP_B, DSTATE), lambda s: (s, sub_layer, 0, 0, 0, 0))
    if has_state:
        in_specs.append(state_spec)
        args.append(state)
    if emit_state:
        out_specs.append(state_spec if carried is not None else
                         pl.BlockSpec((None, N_EVEN, 2, NH_B, HP_B, DSTATE), lambda s: (s, 0, 0, 0, 0, 0)))
        out_shape.append(jax.ShapeDtypeStruct((nseq, N_EVEN, 2, NH_B, HP_B, DSTATE), F32))
        if carried is not None:
            aliases = {len(args): 1}
            in_specs.append(pl.BlockSpec(memory_space=pl.ANY))
            args.append(carried)
    return pl.pallas_call(
        functools.partial(_ssd_kernel, seq=seq, has_state=has_state, emit_state=emit_state, n_carried=len(aliases),
                          slot=sub_layer),
        grid=(nseq,), in_specs=in_specs, out_specs=out_specs, out_shape=out_shape,
        input_output_aliases=aliases, compiler_params=_params("arbitrary"), name="ssd",
    )(*args)


def _shared_split(x, x_swapped, kh):
    first = _pair_lanes(x.shape)
    zero = jnp.zeros_like(x)
    top, bottom = (x, x_swapped) if kh == 0 else (x_swapped, x)
    return jnp.concatenate([jnp.where(first, top, zero), jnp.where(first, zero, bottom)], axis=0)


def _pair_probs(s, sinks=None, valid=None):
    n_keys = s.shape[1] // 2
    probs, maxes = [], []
    for i in range(2):
        si = s[:, i * n_keys:(i + 1) * n_keys]
        if valid is not None:
            si = jnp.where(valid, si, -jnp.inf)
        m = jnp.max(si, axis=1, keepdims=True)
        if sinks is not None:
            m = jnp.maximum(m, sinks[i])
        probs.append(jnp.exp(si - m))
        maxes.append(m)
    return jnp.concatenate(probs, axis=1).astype(BF16), maxes


def _pair_output(p, maxes, vbd, sinks=None):
    o = _dot(p, vbd)
    den = o[:, LANES:]
    if sinks is not None:
        den = den + jnp.where(_pair_lanes(den.shape), jnp.exp(sinks[0] - maxes[0]), jnp.exp(sinks[1] - maxes[1]))
    return o[:, :LANES] / den


def _run_pairs(items, valid=None):
    s_next = items[0][0]()
    for idx, (_, values, sinks, out_ref, cols) in enumerate(items):
        s_cur = s_next
        if idx + 1 < len(items):
            s_next = items[idx + 1][0]()
        p, maxes = _pair_probs(s_cur, sinks, valid if sinks is not None else None)
        out_ref[:, cols] = _pair_output(p, maxes, values(), sinks)


def _pair_sinks(sink_ref, n):
    return sink_ref[0:1, n:n + 1], sink_ref[0:1, n + 1:n + 2]


def _mla_queries(qa_ref, qan_ref, wqb_ref):
    return _dot(_rms(qa_ref[...], qan_ref[...]).astype(BF16), wqb_ref[...]) * MLA_SCALE


def _attn_ctx_kernel(qc_ref, kc_ref, vc_ref, qa_ref, kva_ref, kpe_ref, sink_ref, qan_ref, kvn_ref, wqb_ref, wkvb_ref,
                     *rest, seq, n_carried, slot):
    oc_ref, od_ref, ckv_ref = rest[n_carried:]
    if n_carried == 0:
        for other in range(N_ODD):
            if other != slot:
                ckv_ref[other] = jnp.zeros(ckv_ref.shape[1:], F32)
        ckv_ref = ckv_ref.at[slot]
    ones_bd = _pair_split(jnp.ones((seq, LANES), F32))
    kc, vc = kc_ref[...], vc_ref[...]
    kc_sw, vc_sw = pltpu.roll(kc, HD_C, 1), pltpu.roll(vc, HD_C, 1)
    qd = _mla_queries(qa_ref, qan_ref, wqb_ref)
    ckv = _rms(kva_ref[...], kvn_ref[...])
    ckv_ref[...] = ckv
    kv = _dot(ckv.astype(BF16), wkvb_ref[...])
    kpe = kpe_ref[...]
    kpe_bd = jnp.concatenate([kpe, pltpu.roll(kpe, ROPE_D, 1)], axis=0)
    nope_w = NH_D * NOPE_D
    items = []
    for kh in range(NKV_C):
        for n in range(kh * G_C, (kh + 1) * G_C, 2):
            cols = slice(n * HD_C, (n + 2) * HD_C)
            items.append((lambda cols=cols, kh=kh: _dot_nt((qc_ref[:, cols] * (HD_C ** -0.5)).astype(BF16),
                                                           _shared_split(kc, kc_sw, kh).astype(BF16)),
                          lambda kh=kh: jnp.concatenate([_shared_split(vc, vc_sw, kh), ones_bd], axis=1).astype(BF16),
                          _pair_sinks(sink_ref, n), oc_ref, cols))
    for i in range(NH_D // 2):
        cols = slice(i * LANES, (i + 1) * LANES)
        vcols = slice(nope_w + i * LANES, nope_w + (i + 1) * LANES)
        items.append((lambda cols=cols, vcols=vcols: _dot_nt(
                          jnp.concatenate([qd[:, cols], qd[:, vcols]], axis=1).astype(BF16),
                          jnp.concatenate([_pair_split(kv[:, cols]), kpe_bd], axis=1).astype(BF16)),
                      lambda vcols=vcols: jnp.concatenate([_pair_split(kv[:, vcols]), ones_bd], axis=1).astype(BF16),
                      None, od_ref, cols))
    _run_pairs(items)


def _attn_ctx(proj, params, seq, sub_layer, carried=None):
    n = proj[0].shape[0]
    nseq = n // seq
    row = lambda s: (s, 0)
    in_specs = [pl.BlockSpec((seq, wd), row) for wd in ODD_WIDTHS]
    in_specs += [_layer_spec(p, sub_layer) for p in params]
    args = [*proj, *params]
    half = NH_C * HD_C
    aliases = {}
    if carried is None:
        ckv_spec = pl.BlockSpec((None, N_ODD, seq, KV_RANK), lambda s: (s, 0, 0, 0))
    else:
        ckv_spec = pl.BlockSpec((None, None, seq, KV_RANK), lambda s: (s, sub_layer, 0, 0))
        aliases = {len(args): 2}
        in_specs.append(pl.BlockSpec(memory_space=pl.ANY))
        args.append(carried)
    return pl.pallas_call(
        functools.partial(_attn_ctx_kernel, seq=seq, n_carried=len(aliases), slot=sub_layer),
        grid=(nseq,), in_specs=in_specs,
        out_specs=[pl.BlockSpec((seq, half), row), pl.BlockSpec((seq, half), row), ckv_spec],
        out_shape=[jax.ShapeDtypeStruct((n, half), F32), jax.ShapeDtypeStruct((n, half), F32),
                   jax.ShapeDtypeStruct((nseq, N_ODD, seq, KV_RANK), F32)],
        input_output_aliases=aliases, compiler_params=_params("arbitrary"), name="attn_ctx",
    )(*args)


def _rope(x, cos, sin, half):
    parts = []
    lane = _lane_iota((x.shape[0], LANES))
    first = (lane & (2 * half - 1)) < half
    for i in range(x.shape[1] // LANES):
        xi = x[:, i * LANES:(i + 1) * LANES]
        partner = jnp.where(first, -pltpu.roll(xi, LANES - half, 1), pltpu.roll(xi, half, 1))
        parts.append(xi * cos + partner * sin)
    return parts[0] if len(parts) == 1 else jnp.concatenate(parts, axis=1)


def _attn_lat_kernel(qc_ref, qa_ref, ropeq_ref, kc_ref, vc_ref, kva_ref, kpe_ref, kctx_ref, vctx_ref, ckvctx_ref,
                     kpectx_ref, rope_ref, sink_ref, qan_ref, kvn_ref, wqb_ref, wkvb_ref, oc_ref, od_ref,
                     kwin_s, vwin_s, kext_s, vext_s, *, seq, past):
    qi = pl.program_id(1)
    nope_w = NH_D * NOPE_D
    n_all = past + seq
    ctx0 = 2 * WINDOW + seq

    @pl.when(qi == 0)
    def _():
        zeros = jnp.zeros((WINDOW, LANES), BF16)
        for ref, lat, ctx in ((kwin_s, _rope(kc_ref[...], rope_ref[0], rope_ref[1], HD_C // 2), kctx_ref[...]),
                              (vwin_s, vc_ref[...], vctx_ref[...])):
            lat_sw, ctx_sw = pltpu.roll(lat, HD_C, 1), pltpu.roll(ctx, HD_C, 1)
            for kh in range(NKV_C):
                lat_bd = _shared_split(lat, lat_sw, kh).astype(BF16)
                ctx_bd = _shared_split(ctx, ctx_sw, kh).astype(BF16)
                for i in range(2):
                    ref[kh, i, 0:WINDOW, :] = zeros
                    ref[kh, i, WINDOW:WINDOW + seq, :] = lat_bd[i * seq:(i + 1) * seq]
                    ref[kh, i, WINDOW + seq:ctx0, :] = zeros
                    ref[kh, i, ctx0:, :] = ctx_bd[i * past:(i + 1) * past]
        ckv = _rms(kva_ref[...], kvn_ref[...])
        kv = jnp.concatenate([_dot(ckvctx_ref[...].astype(BF16), wkvb_ref[...]),
                              _dot(ckv.astype(BF16), wkvb_ref[...])], axis=0)
        kpe = jnp.concatenate([kpectx_ref[...], _rope(kpe_ref[...], rope_ref[2], rope_ref[3], ROPE_D // 2)], axis=0)
        kpe_bd = jnp.concatenate([kpe, pltpu.roll(kpe, ROPE_D, 1)], axis=0).astype(BF16)
        ones_bd = _pair_split(jnp.ones((n_all, LANES), F32)).astype(BF16)
        for i in range(NH_D // 2):
            kext_s[i, :, 0:LANES] = _pair_split(kv[:, i * LANES:(i + 1) * LANES]).astype(BF16)
            kext_s[i, :, LANES:] = kpe_bd
            vext_s[i, :, 0:LANES] = _pair_split(kv[:, nope_w + i * LANES:nope_w + (i + 1) * LANES]).astype(BF16)
            vext_s[i, :, LANES:] = ones_bd

    r0 = pl.multiple_of(qi * Q_TILE, Q_TILE)
    nloc = Q_TILE + 2 * WINDOW
    n_keys = nloc + past
    qr = _rope(qc_ref[...], ropeq_ref[0], ropeq_ref[1], HD_C // 2) * (HD_C ** -0.5)
    ti = r0 + _row_iota((Q_TILE, n_keys))
    col = _lane_iota((Q_TILE, n_keys))
    pos = r0 - WINDOW + col
    valid = (col >= nloc) | ((jnp.abs(ti - pos) <= WINDOW) & (pos >= 0) & (pos < seq))
    ones_bd = _pair_split(jnp.ones((n_keys, LANES), F32)).astype(BF16)
    qd = _mla_queries(qa_ref, qan_ref, wqb_ref)
    q_pe = _rope(qd[:, nope_w:], ropeq_ref[2], ropeq_ref[3], ROPE_D // 2)

    def banded(ref, kh):
        return jnp.concatenate([ref[kh, 0, pl.ds(r0, nloc), :], ref[kh, 0, ctx0:, :],
                                ref[kh, 1, pl.ds(r0, nloc), :], ref[kh, 1, ctx0:, :]], axis=0)

    items = []
    for kh in range(NKV_C):
        for n in range(kh * G_C, (kh + 1) * G_C, 2):
            cols = slice(n * HD_C, (n + 2) * HD_C)
            items.append((lambda cols=cols, kh=kh: _dot_nt(qr[:, cols].astype(BF16), banded(kwin_s, kh)),
                          lambda kh=kh: jnp.concatenate([banded(vwin_s, kh), ones_bd], axis=1),
                          _pair_sinks(sink_ref, n), oc_ref, cols))
    for i in range(NH_D // 2):
        cols = slice(i * LANES, (i + 1) * LANES)
        items.append((lambda cols=cols, i=i: _dot_nt(jnp.concatenate([qd[:, cols], q_pe[:, cols]], axis=1).astype(BF16),
                                                     kext_s[i]),
                      lambda i=i: vext_s[i], None, od_ref, cols))
    _run_pairs(items, valid)


def _attn_lat(proj, caches, rope, params, seq, sub_layer):
    qc, kc, vc, qa, kva, kpe = proj
    n = qc.shape[0]
    past = caches[0].shape[2]
    nq = seq // Q_TILE
    qrow = lambda b, q: (b * nq + q, 0)
    krow = lambda b, q: (b, 0)
    kvw = NKV_C * HD_C
    in_specs = [pl.BlockSpec((Q_TILE, NH_C * HD_C), qrow), pl.BlockSpec((Q_TILE, Q_RANK), qrow),
                pl.BlockSpec((4, Q_TILE, LANES), lambda b, q: (0, q, 0)),
                pl.BlockSpec((seq, kvw), krow), pl.BlockSpec((seq, kvw), krow),
                pl.BlockSpec((seq, KV_RANK), krow), pl.BlockSpec((seq, LANES), krow)]
    in_specs += [pl.BlockSpec((None, None, past, LANES), lambda b, q: (b, sub_layer, 0, 0)) for _ in caches]
    in_specs += [pl.BlockSpec(rope.shape, lambda b, q: (0, 0, 0))]
    in_specs += [_layer_spec(p, sub_layer) for p in params]
    half = NH_C * HD_C
    win_rows = 2 * WINDOW + seq + past
    return pl.pallas_call(
        functools.partial(_attn_lat_kernel, seq=seq, past=past),
        grid=(n // seq, nq), in_specs=in_specs,
        out_specs=[pl.BlockSpec((Q_TILE, half), qrow), pl.BlockSpec((Q_TILE, half), qrow)],
        out_shape=[jax.ShapeDtypeStruct((n, half), F32), jax.ShapeDtypeStruct((n, half), F32)],
        scratch_shapes=[pltpu.VMEM((NKV_C, 2, win_rows, LANES), BF16), pltpu.VMEM((NKV_C, 2, win_rows, LANES), BF16),
                        pltpu.VMEM((NH_D // 2, 2 * (past + seq), 2 * LANES), BF16),
                        pltpu.VMEM((NH_D // 2, 2 * (past + seq), 2 * LANES), BF16)],
        compiler_params=_params("arbitrary", "arbitrary"), name="attn_lat",
    )(qc, qa, rope, kc, vc, kva, kpe, *caches, rope, *params)


def _pad_lanes(x, width=LANES):
    return jnp.pad(x, [(0, 0)] * (x.ndim - 1) + [(0, width - x.shape[-1])])


def _on_lanes(x):
    return jnp.broadcast_to(x[..., None], x.shape + (LANES,))


def _even_in_weight(w):
    o0 = 2 * A_QK + 2 * A_V
    z0 = o0 + 4 * NH_A
    d0 = z0 + B_INNER + B_XBC
    gates_t = jnp.concatenate([w[:, o0:z0], w[:, d0:]], axis=1).T
    return jnp.concatenate([w[:, :o0], w[:, z0:d0]], axis=1).astype(BF16), gates_t.astype(BF16)


def _mla_query_weights(w):
    lead = w.shape[:-1]
    w4 = w.reshape(lead + (NH_D // 2, 2, NOPE_D + ROPE_D))
    nope = w4[..., :NOPE_D].reshape(lead + (NH_D * NOPE_D,))
    pe = _pad_lanes(w4[..., NOPE_D:].reshape(lead + (NH_D // 2, 2 * ROPE_D)))
    return jnp.concatenate([nope, pe.reshape(lead + (NH_D // 2 * LANES,))], axis=-1).astype(BF16)


def _mla_kv_weights(w):
    lead = w.shape[:-1]
    w3 = w.reshape(lead + (NH_D, NOPE_D + V_D))
    return jnp.concatenate([w3[..., :NOPE_D].reshape(lead + (NH_D * NOPE_D,)),
                            w3[..., NOPE_D:].reshape(lead + (NH_D * V_D,))], axis=-1).astype(BF16)


def _rope_tables(rows):
    def table(rot_dim):
        quarter = rot_dim // 4
        inv = ROPE_BASE ** (-jnp.arange(quarter, dtype=F32) / quarter)
        r = jnp.repeat(jnp.arange(rows, dtype=F32), GRID_W)
        col = jnp.tile(jnp.arange(GRID_W, dtype=F32), rows)
        ang = jnp.concatenate([r[:, None] * inv, col[:, None] * inv], axis=-1)
        reps = LANES // (rot_dim // 2)
        return jnp.tile(jnp.cos(ang), (1, reps)), jnp.tile(jnp.sin(ang), (1, reps))
    cos_c, sin_c = table(HD_C)
    cos_d, sin_d = table(ROPE_D)
    return jnp.stack([cos_c, sin_c, cos_d, sin_d])


def kernel(x_prompt, x_sample, c, state_mlstm_C, state_mlstm_n, state_mlstm_m, state_ssd, cache_gqa_k, cache_gqa_v,
           cache_mla_ckv, cache_mla_kpe, c_ctx, w_ada, b_ada, norm_g, w_up, w_down, w_in_even, conv_a_w, conv_a_b,
           conv_b_w, conv_b_b, gate_b, a_norm_w, dt_bias, a_log, d_skip, b_norm_w, w_out_even, w_in_odd, sink,
           q_a_norm, kv_a_norm, w_q_b, w_kv_b, w_out_odd):
    xp = x_prompt.reshape(BATCH * SEQ, D_MODEL)
    xs = x_sample.reshape(DEC_BATCH * DEC_SEQ, D_MODEL)
    cond = jnp.concatenate([c_ctx[None, :], c, jnp.zeros((MOD_ROWS - 1 - DEC_BATCH, D_MODEL), F32)], axis=0)
    mods = _modulations(cond, w_ada, b_ada)
    rope = _rope_tables(DEC_SEQ // GRID_W)

    w_even = [_even_in_weight(w_in_even[j]) for j in range(N_EVEN)]
    a_params = (conv_a_w, conv_a_b[:, None, :], _on_lanes(gate_b), a_norm_w[:, None, :])
    b_params = (conv_b_w, conv_b_b[:, None, :], _on_lanes(dt_bias.reshape(N_EVEN, 2 * NH_B)),
                _on_lanes(a_log.reshape(N_EVEN, 2 * NH_B)), jnp.repeat(d_skip, HP_B, axis=1)[:, None, :],
                b_norm_w[:, None, :])
    n0 = state_mlstm_n[..., None]
    mem_in = (jnp.concatenate([state_mlstm_C, jnp.broadcast_to(n0, n0.shape[:-1] + (LANES,))], axis=-1),
              _on_lanes(state_mlstm_m.reshape(DEC_BATCH, N_EVEN, 2 * NH_A)))
    w_odd = [_pad_lanes(w_in_odd[j], sum(ODD_WIDTHS)).astype(BF16) for j in range(N_ODD)]
    o_params = (_pad_lanes(sink)[:, None, :], q_a_norm[:, None, :], kv_a_norm[:, None, :],
                _mla_query_weights(w_q_b), _mla_kv_weights(w_kv_b))
    caches = (cache_gqa_k.reshape(DEC_BATCH, N_ODD, PAST_LEN, NKV_C * HD_C),
              cache_gqa_v.reshape(DEC_BATCH, N_ODD, PAST_LEN, NKV_C * HD_C),
              cache_mla_ckv, _pad_lanes(cache_mla_kpe))

    new_k, new_v, new_kpe = [], [], []
    mem_state, ssd_state, ckv_state = None, None, None
    for l in range(DEPTH):
        j = l // 2
        if l % 2 == 0:
            qk, v, o, z, xbc, g = _project(xp, mods, l, False, norm_g, w_even[j][0], EVEN_WIDTHS, w_even[j][1])
            a1p, *mem_state = _mlstm(qk, v, o, g, a_params, SEQ, j, carried=mem_state)
            a2p, ssd_state = _ssd(xbc, z, g, b_params, SEQ, j, carried=ssd_state)
            qk, v, o, z, xbc, g = _project(xs, mods, l, True, norm_g, w_even[j][0], EVEN_WIDTHS, w_even[j][1])
            a1s, = _mlstm(qk, v, o, g, a_params, DEC_SEQ, j, state=mem_in)
            a2s, = _ssd(xbc, z, g, b_params, DEC_SEQ, j, state=state_ssd)
            w_out = w_out_even
        else:
            proj = _project(xp, mods, l, False, norm_g, w_odd[j], ODD_WIDTHS)
            a1p, a2p, ckv_state = _attn_ctx(proj, o_params, SEQ, j, carried=ckv_state)
            new_k.append(proj[1].reshape(BATCH, SEQ, NKV_C, HD_C))
            new_v.append(proj[2].reshape(BATCH, SEQ, NKV_C, HD_C))
            new_kpe.append(proj[5][:, :ROPE_D].reshape(BATCH, SEQ, ROPE_D))
            proj = _project(xs, mods, l, True, norm_g, w_odd[j], ODD_WIDTHS)
            a1s, a2s = _attn_lat(proj, caches, rope, o_params, DEC_SEQ, j)
            w_out = w_out_odd
        xp, xs = _channel((a1p, a2p, xp), (a1s, a2s, xs), mods, l, norm_g, w_out, w_up, w_down)

    new_c, new_n, new_m = mem_state
    return (xp.reshape(BATCH, SEQ, D_MODEL), xs.reshape(DEC_BATCH, DEC_SEQ, D_MODEL),
            new_c, new_n, new_m[..., 0].reshape(BATCH, N_EVEN, 2, NH_A), ssd_state,
            jnp.stack(new_k, axis=1), jnp.stack(new_v, axis=1), ckv_state, jnp.stack(new_kpe, axis=1))
```

```python
import functools

import jax
import jax.numpy as jnp
from jax import lax
from jax.experimental import pallas as pl
from jax.experimental.pallas import tpu as pltpu

F32 = jnp.float32
BF16 = jnp.bfloat16

D_MODEL = 1024
BATCH = 32
SEQ = 256
DEPTH = 4
DEC_BATCH = 2
DEC_SEQ = 1024
PAST_LEN = 256
GRID_W = 64
N_EVEN = (DEPTH + 1) // 2
N_ODD = DEPTH // 2
EPS = 1e-6
CONV_K = 5
NH_A = 4
DK_A = 128
DV_A = 128
A_QK = NH_A * DK_A
A_V = NH_A * DV_A
NH_B = 8
HP_B = 64
DSTATE = 128
NG_B = 2
R_B = NH_B // NG_B
B_INNER = NH_B * HP_B
B_BC = NG_B * DSTATE
B_XBC = B_INNER + 2 * B_BC
NH_C = 8
NKV_C = 2
G_C = NH_C // NKV_C
HD_C = 64
WINDOW = 128
NH_D = 8
Q_RANK = 256
KV_RANK = 128
NOPE_D = 64
ROPE_D = 32
V_D = 64
MLA_SCALE = (NOPE_D + ROPE_D) ** -0.5
D_FF = 4 * D_MODEL
ROPE_BASE = 10000.0

LANES = 128
VMEM_LIMIT_BYTES = 56 * 1024 * 1024
ROW_TILE = 512
FF_TILE = 1024
SUB_ROWS = 512
STAGE_ROWS = 512
Q_TILE = 256
ADA_TILE = 1536
MOD_ROWS = 8
GATE_ROWS = 4 * NH_A + 2 * NH_B

EVEN_WIDTHS = (2 * A_QK, A_V, A_V, B_INNER, B_XBC)
ODD_WIDTHS = (NH_C * HD_C, NKV_C * HD_C, NKV_C * HD_C, Q_RANK, KV_RANK, LANES)
_GATES_LO = 2 * A_QK + 2 * A_V
_Z_LO = _GATES_LO + 4 * NH_A
_DT_LO = _Z_LO + B_INNER + B_XBC
EVEN_REGROUP = ((0, 0, _GATES_LO), (_GATES_LO, _Z_LO, B_INNER + B_XBC))
EVEN_GATE_COLS = ((_GATES_LO, _Z_LO), (_DT_LO, _DT_LO + 2 * NH_B))
ODD_IN = sum(ODD_WIDTHS) - LANES + ROPE_D
ODD_REGROUP = ((0, 0, ODD_IN), (ODD_IN, None, LANES - ROPE_D))

_NT = (((1,), (1,)), ((), ()))


def _params(*sem):
    return pltpu.CompilerParams(dimension_semantics=sem, vmem_limit_bytes=VMEM_LIMIT_BYTES)


def _rms(x, g):
    return x * lax.rsqrt(jnp.mean(x * x, axis=-1, keepdims=True) + EPS) * g


def _silu(x):
    return x * jax.nn.sigmoid(x)


def _softplus(x):
    return jnp.maximum(x, 0.0) + jnp.log1p(jnp.exp(-jnp.abs(x)))


def _dot(a, b):
    return jnp.dot(a, b, preferred_element_type=F32)


def _dot_nt(a, b):
    return lax.dot_general(a, b, _NT, preferred_element_type=F32)


def _layer_spec(arr, layer):
    tail = arr.shape[1:]
    zeros = (0,) * len(tail)
    return pl.BlockSpec((None,) + tail, lambda *_: (layer,) + zeros)


def _ada_kernel(c_ref, w_ref, b_ref, o_ref):
    s = _silu(c_ref[...]).astype(BF16)
    o_ref[...] = _dot(s, w_ref[...].astype(BF16)) + b_ref[...]


def _modulations(cond, w_ada, b_ada):
    out = pl.pallas_call(
        _ada_kernel,
        grid=(DEPTH, 6 * D_MODEL // ADA_TILE),
        in_specs=[pl.BlockSpec((MOD_ROWS, D_MODEL), lambda l, n: (0, 0)),
                  pl.BlockSpec((None, D_MODEL, ADA_TILE), lambda l, n: (l, 0, n)),
                  pl.BlockSpec((None, 1, ADA_TILE), lambda l, n: (l, 0, n))],
        out_specs=pl.BlockSpec((None, MOD_ROWS, ADA_TILE), lambda l, n: (l, 0, n)),
        out_shape=jax.ShapeDtypeStruct((DEPTH, MOD_ROWS, 6 * D_MODEL), F32),
        compiler_params=_params("arbitrary", "arbitrary"),
        name="ada",
    )(cond, w_ada, b_ada.reshape(DEPTH, 1, 6 * D_MODEL))
    return out.reshape(DEPTH, MOD_ROWS, 6, D_MODEL)


def _mod_spec(layer, latent):
    if latent:
        per_seq = DEC_SEQ // ROW_TILE
        return pl.BlockSpec((None, None, 6, D_MODEL), lambda i, *_: (layer, 1 + i // per_seq, 0, 0))
    return pl.BlockSpec((None, None, 6, D_MODEL), lambda i, *_: (layer, 0, 0, 0))


def _proj_kernel(x_ref, mod_ref, g_ref, w_ref, *rest, widths, regroup, gate_cols):
    n_out = len(widths) + (1 if gate_cols else 0)
    o_refs, w_s = rest[:n_out], rest[n_out]

    @pl.when(pl.program_id(0) == 0)
    def _():
        for dst, src, width in regroup:
            if src is None:
                w_s[:, dst:dst + width] = jnp.zeros((D_MODEL, width), BF16)
            else:
                w_s[:, dst:dst + width] = w_ref[:, src:src + width]
        if gate_cols:
            pieces = [w_ref[:, lo:hi].astype(F32) for lo, hi in gate_cols]
            n_gates = sum(hi - lo for lo, hi in gate_cols)
            gates = jnp.concatenate(pieces + [jnp.zeros((D_MODEL, LANES - n_gates), F32)], axis=1)
            rest[n_out + 1][...] = gates.T[0:n_gates, :].astype(BF16)

    h = _rms(x_ref[...], g_ref[0:1, :]) * (1.0 + mod_ref[1:2, :]) + mod_ref[0:1, :]
    hb = h.astype(BF16)
    off = 0
    for o_ref, wd in zip(o_refs, widths):
        o_ref[...] = _dot(hb, w_s[:, off:off + wd])
        off += wd
    if gate_cols:
        o_refs[-1][...] = _dot_nt(rest[n_out + 1][...], hb)


def _project(x, mods, layer, latent, gains, w_all, widths, regroup, gate_cols=()):
    n = x.shape[0]
    out_specs = [pl.BlockSpec((ROW_TILE, wd), lambda i: (i, 0)) for wd in widths]
    out_shape = [jax.ShapeDtypeStruct((n, wd), F32) for wd in widths]
    scratch = [pltpu.VMEM((D_MODEL, sum(widths)), BF16)]
    if gate_cols:
        n_gates = sum(hi - lo for lo, hi in gate_cols)
        out_specs.append(pl.BlockSpec((n_gates, ROW_TILE), lambda i: (0, i)))
        out_shape.append(jax.ShapeDtypeStruct((n_gates, n), F32))
        scratch.append(pltpu.VMEM((n_gates, D_MODEL), BF16))
    w_spec = pl.BlockSpec((None,) + w_all.shape[1:], lambda i: (layer // 2, 0, 0), pipeline_mode=pl.Buffered(1))
    return pl.pallas_call(
        functools.partial(_proj_kernel, widths=widths, regroup=regroup, gate_cols=gate_cols),
        grid=(n // ROW_TILE,),
        in_specs=[pl.BlockSpec((ROW_TILE, D_MODEL), lambda i: (i, 0)), _mod_spec(layer, latent),
                  _layer_spec(gains, layer), w_spec],
        out_specs=out_specs, out_shape=out_shape, scratch_shapes=scratch,
        compiler_params=_params("arbitrary"),
        name="proj",
    )(x, mods, gains, w_all)


def _weight_chunks(layer, sub_layer, wo_hbm, wu_hbm, wd_hbm, wo_s, wu_s, wd_s):
    chunks = []
    for r in range(0, D_MODEL, STAGE_ROWS):
        chunks.append((wo_hbm.at[sub_layer, pl.ds(r, STAGE_ROWS), :], wo_s.at[pl.ds(r, STAGE_ROWS), :]))
    for r in range(0, D_MODEL, STAGE_ROWS):
        for c in range(0, D_FF, D_MODEL):
            chunks.append((wu_hbm.at[layer, pl.ds(r, STAGE_ROWS), pl.ds(c, D_MODEL)],
                           wu_s.at[pl.ds(r, STAGE_ROWS), pl.ds(c, D_MODEL)]))
    for r in range(0, D_FF, STAGE_ROWS):
        chunks.append((wd_hbm.at[layer, pl.ds(r, STAGE_ROWS), :], wd_s.at[pl.ds(r, STAGE_ROWS), :]))
    return chunks


def _channel_kernel(a1p_ref, a2p_ref, xp_ref, a1s_ref, a2s_ref, xs_ref, mod_ref, g_ref, wo_hbm, wu_hbm, wd_hbm,
                    op_ref, os_ref, wo_s, wu_s, wd_s, stage, sem, *, layer, sub_layer, prompt_steps):
    step = pl.program_id(0)

    @pl.when(step == 0)
    def _():
        chunks = _weight_chunks(layer, sub_layer, wo_hbm, wu_hbm, wd_hbm, wo_s, wu_s, wd_s)
        copies = [pltpu.make_async_copy(src, stage.at[k % 2], sem.at[k % 2]) for k, (src, _) in enumerate(chunks)]
        copies[0].start()
        for k, (_, dst) in enumerate(chunks):
            if k + 1 < len(chunks):
                copies[k + 1].start()
            copies[k].wait()
            dst[...] = stage[k % 2].astype(BF16)

    def rows_block(a1_ref, a2_ref, x_ref, o_ref):
        half = a1_ref.shape[1]
        for r0 in range(0, ROW_TILE, SUB_ROWS):
            rows = slice(r0, r0 + SUB_ROWS)
            y = (_dot(a1_ref[rows, :].astype(BF16), wo_s[0:half, :])
                 + _dot(a2_ref[rows, :].astype(BF16), wo_s[half:, :]))
            x1 = x_ref[rows, :] + mod_ref[2:3, :] * _rms(y, g_ref[1:2, :])
            h = (_rms(x1, g_ref[2:3, :]) * (1.0 + mod_ref[4:5, :]) + mod_ref[3:4, :]).astype(BF16)
            acc = None
            for c in range(0, D_FF, FF_TILE):
                u = jnp.square(jnp.maximum(_dot(h, wu_s[:, c:c + FF_TILE]), 0.0)).astype(BF16)
                part = _dot(u, wd_s[c:c + FF_TILE, :])
                acc = part if acc is None else acc + part
            o_ref[rows, :] = x1 + mod_ref[5:6, :] * _rms(acc, g_ref[3:4, :])

    @pl.when(step < prompt_steps)
    def _():
        rows_block(a1p_ref, a2p_ref, xp_ref, op_ref)

    @pl.when(step >= prompt_steps)
    def _():
        rows_block(a1s_ref, a2s_ref, xs_ref, os_ref)


def _channel(prompt, latent, mods, layer, gains, w_out, w_up, w_down):
    n_p, n_s = prompt[2].shape[0], latent[2].shape[0]
    steps_p, steps_s = n_p // ROW_TILE, n_s // ROW_TILE
    per_seq = DEC_SEQ // ROW_TILE
    row_p = lambda i: (jnp.minimum(i, steps_p - 1), 0)
    row_s = lambda i: (jnp.maximum(i - steps_p, 0), 0)
    mod_spec = pl.BlockSpec((None, None, 6, D_MODEL),
                            lambda i: (layer, jnp.where(i < steps_p, 0, 1 + (i - steps_p) // per_seq), 0, 0))
    hbm = pl.BlockSpec(memory_space=pl.ANY)
    specs = lambda arrs, row: [pl.BlockSpec((ROW_TILE, a.shape[1]), row) for a in arrs]
    return pl.pallas_call(
        functools.partial(_channel_kernel, layer=layer, sub_layer=layer // 2, prompt_steps=steps_p),
        grid=(steps_p + steps_s,),
        in_specs=specs(prompt, row_p) + specs(latent, row_s) + [mod_spec, _layer_spec(gains, layer), hbm, hbm, hbm],
        out_specs=[pl.BlockSpec((ROW_TILE, D_MODEL), row_p), pl.BlockSpec((ROW_TILE, D_MODEL), row_s)],
        out_shape=[jax.ShapeDtypeStruct((n_p, D_MODEL), F32), jax.ShapeDtypeStruct((n_s, D_MODEL), F32)],
        scratch_shapes=[pltpu.VMEM((D_MODEL, D_MODEL), BF16), pltpu.VMEM((D_MODEL, D_FF), BF16),
                        pltpu.VMEM((D_FF, D_MODEL), BF16), pltpu.VMEM((2, STAGE_ROWS, D_MODEL), F32),
                        pltpu.SemaphoreType.DMA((2,))],
        compiler_params=_params("arbitrary"),
        name="channel",
    )(*prompt, *latent, mods, gains, w_out, w_up, w_down)


def _row_iota(shape):
    return lax.broadcasted_iota(jnp.int32, shape, 0)


def _lane_iota(shape):
    return lax.broadcasted_iota(jnp.int32, shape, 1)


def _pair_lanes(shape):
    return _lane_iota(shape) < LANES // 2


def _cumsum_lanes(x, n_fwd):
    t = x.shape[1]
    si, ti = _row_iota((t, t)), _lane_iota((t, t))
    upper = jnp.where(si <= ti, 1.0, 0.0).astype(BF16)
    lower = jnp.where(si >= ti, 1.0, 0.0).astype(BF16)
    hi = x.astype(BF16)
    rest = x - hi.astype(F32)
    mid = rest.astype(BF16)
    lo = (rest - mid.astype(F32)).astype(BF16)
    pre = _dot(hi, upper) + _dot(mid, upper) + _dot(lo, upper)
    suf = _dot(hi, lower) + _dot(mid, lower) + _dot(lo, lower)
    return jnp.where(_row_iota(x.shape) < n_fwd, pre, suf)


def _cummax_lanes(x, n_fwd):
    t = x.shape[1]
    lane = _lane_iota(x.shape)
    pre, suf = x, x
    k = 1
    while k < t:
        pre = jnp.maximum(pre, jnp.where(lane >= k, pltpu.roll(pre, k, 1), -jnp.inf))
        suf = jnp.maximum(suf, jnp.where(lane < t - k, pltpu.roll(suf, t - k, 1), -jnp.inf))
        k *= 2
    return jnp.where(_row_iota(x.shape) < n_fwd, pre, suf)


def _columns(row_arrays):
    t = row_arrays[0].shape[1]
    used = sum(a.shape[0] for a in row_arrays)
    return jnp.concatenate(list(row_arrays) + [jnp.zeros((LANES - used, t), F32)], axis=0).T


def _dwconv_silu(x, w, b):
    t = x.shape[0]
    row = _row_iota(x.shape)
    acc = x * w[CONV_K // 2:CONV_K // 2 + 1, :] + b
    for j in range(CONV_K):
        d = j - CONV_K // 2
        if d == 0:
            continue
        shifted = pltpu.roll(x, (-d) % t, 0)
        valid = (row >= -d) if d < 0 else (row < t - d)
        acc = acc + jnp.where(valid, shifted, 0.0) * w[j:j + 1, :]
    return _silu(acc)


def _causal_exponent(expo, r0, k0, reverse):
    ti = r0 + _row_iota(expo.shape)
    si = k0 + _lane_iota(expo.shape)
    keep = (si >= ti) if reverse else (si <= ti)
    return jnp.where(keep, expo, -jnp.inf)


def _pair_split(x):
    first = _pair_lanes(x.shape)
    zero = jnp.zeros_like(x)
    return jnp.concatenate([jnp.where(first, x, zero), jnp.where(first, zero, x)], axis=0)


def _key_range(d, r0, seq):
    return (0, r0 + Q_TILE) if d == 0 else (r0, seq)


def _mlstm_kernel(*refs, seq, has_state, emit_state, n_carried, slot):
    assert not (has_state and emit_state)
    it = iter(refs)
    qk_ref, v_ref, o_ref, g_ref, cw_ref, cb_ref, gb_ref, anw_ref = (next(it) for _ in range(8))
    if has_state:
        c0_ref, m0_ref = next(it), next(it)
    for _ in range(n_carried):
        next(it)
    ha_ref = next(it)
    if emit_state:
        cn_ref, nn_ref, mn_ref = next(it), next(it), next(it)
        if n_carried == 0:
            for other in range(N_EVEN):
                if other != slot:
                    cn_ref[other] = jnp.zeros(cn_ref.shape[1:], F32)
                    nn_ref[other] = jnp.zeros(nn_ref.shape[1:], F32)
                    mn_ref[other] = jnp.zeros(mn_ref.shape[1:], F32)
            cn_ref, nn_ref, mn_ref = cn_ref.at[slot], nn_ref.at[slot], mn_ref.at[slot]

    n_ch = 2 * NH_A
    log_i = g_ref[0:n_ch, :] + gb_ref[0:n_ch, 0:1]
    f_pre = g_ref[n_ch:2 * n_ch, :] + gb_ref[n_ch:2 * n_ch, 0:1]
    log_f = jnp.minimum(f_pre, 0.0) - jnp.log1p(jnp.exp(-jnp.abs(f_pre)))
    b = _cumsum_lanes(log_f, NH_A)
    a = log_i - b
    m_run = _cummax_lanes(a, NH_A)
    if has_state:
        m0 = m0_ref[:, 0:1]
        m_run = jnp.maximum(m_run, m0)
    else:
        m_run = jnp.maximum(m_run, 0.0)
    by_time = [m_run, jnp.exp(-(b + m_run))]
    if has_state:
        by_time.append(jnp.exp(m0 - m_run))
    if emit_state:
        fwd = _row_iota((n_ch, 1)) < NH_A
        b_last = jnp.where(fwd, b[:, seq - 1:seq], b[:, 0:1])
        m_last = jnp.where(fwd, m_run[:, seq - 1:seq], m_run[:, 0:1])
        mn_ref[...] = jnp.broadcast_to(b_last + m_last, (n_ch, LANES))
        by_time.append(jnp.exp(a - m_last))
    cols = _columns(by_time)

    ones = jnp.ones((seq, LANES), F32)
    for h in range(NH_A):
        cq = slice(h * DK_A, (h + 1) * DK_A)
        ck = slice(A_QK + h * DK_A, A_QK + (h + 1) * DK_A)
        cv = slice(h * DV_A, (h + 1) * DV_A)
        q = _dwconv_silu(qk_ref[:, cq], cw_ref[:, cq], cb_ref[:, cq])
        k = _dwconv_silu(qk_ref[:, ck], cw_ref[:, ck], cb_ref[:, ck]) * (DK_A ** -0.5)
        qb = q.astype(BF16)
        kb = k.astype(BF16)
        vh = v_ref[:, cv]
        vaug = jnp.concatenate([vh, ones], axis=1).astype(BF16)
        for r0 in range(0, seq, Q_TILE):
            rows = slice(r0, r0 + Q_TILE)
            s = _dot_nt(qb[rows], kb)
            hsum = None
            for d in range(2):
                c = d * NH_A + h
                k0, k1 = _key_range(d, r0, seq)
                expo = _causal_exponent(a[c:c + 1, k0:k1] - cols[rows, c:c + 1], r0, k0, d == 1)
                p = (s[:, k0:k1] * jnp.exp(expo)).astype(BF16)
                acc = _dot(p, vaug[k0:k1])
                if has_state:
                    acc = acc + cols[rows, 2 * n_ch + c:2 * n_ch + c + 1] * _dot(qb[rows], c0_ref[d, h].astype(BF16))
                hd = acc[:, 0:DV_A] / jnp.maximum(jnp.abs(acc[:, DV_A:]), cols[rows, n_ch + c:n_ch + c + 1])
                hsum = hd if hsum is None else hsum + hd
            og = jax.nn.sigmoid(o_ref[rows, cv]) * hsum
            ha_ref[rows, cv] = _rms(og, anw_ref[:, cv])
        if emit_state:
            for d in range(2):
                c = d * NH_A + h
                kw = k * cols[:, 2 * n_ch + c:2 * n_ch + c + 1]
                cn_ref[d, h] = _dot(kw.T.astype(BF16), vh.astype(BF16))
                nn_ref[d, h:h + 1, :] = jnp.sum(kw, axis=0, keepdims=True)


def _mlstm(qk, v, o, gates, params, seq, sub_layer, state=None, carried=None):
    n = qk.shape[0]
    nseq = n // seq
    has_state = state is not None
    emit_state = not has_state
    aliases = {}
    row = lambda s: (s, 0)
    in_specs = [pl.BlockSpec((seq, 2 * A_QK), row), pl.BlockSpec((seq, A_V), row), pl.BlockSpec((seq, A_V), row),
                pl.BlockSpec((GATE_ROWS, seq), lambda s: (0, s))]
    in_specs += [_layer_spec(p, sub_layer) for p in params]
    args = [qk, v, o, gates, *params]
    out_specs = [pl.BlockSpec((seq, A_V), row)]
    out_shape = [jax.ShapeDtypeStruct((n, A_V), F32)]
    if has_state:
        c0_aug, m0 = state
        in_specs += [pl.BlockSpec((None, None) + c0_aug.shape[2:], lambda s: (s, sub_layer, 0, 0, 0, 0)),
                     pl.BlockSpec((None, None) + m0.shape[2:], lambda s: (s, sub_layer, 0, 0))]
        args += [c0_aug, m0]
    if emit_state:
        shapes = [(2, NH_A, DK_A, DV_A), (2, NH_A, DK_A), (2 * NH_A, LANES)]
        for shp in shapes:
            zeros = (0,) * len(shp)
            if carried is None:
                out_specs.append(pl.BlockSpec((None, N_EVEN) + shp, lambda s, z=zeros: (s, 0) + z))
            else:
                out_specs.append(pl.BlockSpec((None, None) + shp, lambda s, z=zeros: (s, sub_layer) + z))
            out_shape.append(jax.ShapeDtypeStruct((nseq, N_EVEN) + shp, F32))
        if carried is not None:
            aliases = {len(args) + i: 1 + i for i in range(len(shapes))}
            in_specs += [pl.BlockSpec(memory_space=pl.ANY)] * len(shapes)
            args += list(carried)
    return pl.pallas_call(
        functools.partial(_mlstm_kernel, seq=seq, has_state=has_state, emit_state=emit_state,
                          n_carried=len(aliases), slot=sub_layer),
        grid=(nseq,), in_specs=in_specs, out_specs=out_specs, out_shape=out_shape,
        input_output_aliases=aliases, compiler_params=_params("arbitrary"), name="mlstm",
    )(*args)


def _ssd_kernel(*refs, seq, has_state, emit_state, n_carried, slot):
    assert not (has_state and emit_state)
    it = iter(refs)
    xbc_ref, z_ref, g_ref, cw_ref, cb_ref, dtb_ref, alog_ref, dsk_ref, bnw_ref = (next(it) for _ in range(9))
    if has_state:
        s0_ref = next(it)
    for _ in range(n_carried):
        next(it)
    yb_ref = next(it)
    if emit_state:
        sn_ref = next(it)
        if n_carried == 0:
            for other in range(N_EVEN):
                if other != slot:
                    sn_ref[other] = jnp.zeros(sn_ref.shape[1:], F32)
            sn_ref = sn_ref.at[slot]

    n_ch = 2 * NH_B
    dt = _softplus(g_ref[4 * NH_A:4 * NH_A + n_ch, :] + dtb_ref[:, 0:1])
    acum = _cumsum_lanes(dt * (-jnp.exp(alog_ref[:, 0:1])), NH_B)
    key_shift = acum - jnp.log(dt)
    by_time = [acum]
    if has_state:
        by_time.append(jnp.exp(acum))
    if emit_state:
        fwd = _row_iota((n_ch, 1)) < NH_B
        a_last = jnp.where(fwd, acum[:, seq - 1:seq], acum[:, 0:1])
        by_time.append(jnp.exp(a_last - acum) * dt)
    cols = _columns(by_time)

    gw = R_B * HP_B
    first = _pair_lanes((seq, LANES))
    for g in range(NG_B):
        cx = slice(g * gw, (g + 1) * gw)
        cb_ = slice(B_INNER + g * DSTATE, B_INNER + (g + 1) * DSTATE)
        cc = slice(B_INNER + B_BC + g * DSTATE, B_INNER + B_BC + (g + 1) * DSTATE)
        xg = _dwconv_silu(xbc_ref[:, cx], cw_ref[:, cx], cb_ref[:, cx])
        bg = _dwconv_silu(xbc_ref[:, cb_], cw_ref[:, cb_], cb_ref[:, cb_]).astype(BF16)
        cg = _dwconv_silu(xbc_ref[:, cc], cw_ref[:, cc], cb_ref[:, cc]).astype(BF16)
        xbd = [_pair_split(xg[:, p * LANES:(p + 1) * LANES]).astype(BF16) for p in range(R_B // 2)]
        for r0 in range(0, seq, Q_TILE):
            rows = slice(r0, r0 + Q_TILE)
            cb_scores = _dot_nt(cg[rows], bg)
            ys = []
            for p in range(R_B // 2):
                weights, inputs = [], []
                for d in range(2):
                    k0, k1 = _key_range(d, r0, seq)
                    for i in range(2):
                        c = d * NH_B + g * R_B + 2 * p + i
                        expo = _causal_exponent(cols[rows, c:c + 1] - key_shift[c:c + 1, k0:k1], r0, k0, d == 1)
                        weights.append((cb_scores[:, k0:k1] * jnp.exp(expo)).astype(BF16))
                        inputs.append(xbd[p][i * seq + k0:i * seq + k1])
                yp = _dot(jnp.concatenate(weights, axis=1), jnp.concatenate(inputs, axis=0))
                if has_state:
                    h0 = g * R_B + 2 * p
                    for d in range(2):
                        c = d * NH_B + h0
                        carry = jnp.where(_pair_lanes((Q_TILE, LANES)), cols[rows, n_ch + c:n_ch + c + 1],
                                          cols[rows, n_ch + c + 1:n_ch + c + 2])
                        s0_pair = s0_ref[d, h0:h0 + 2].reshape(2 * HP_B, DSTATE).astype(BF16)
                        yp = yp + carry * _dot_nt(cg[rows], s0_pair)
                ys.append(yp)
            y = jnp.concatenate(ys, axis=1) + dsk_ref[:, cx] * xg[rows]
            y = y * _silu(z_ref[rows, cx])
            yb_ref[rows, cx] = _rms(y, bnw_ref[:, cx])
        if emit_state:
            for d in range(2):
                c0 = n_ch + d * NH_B + g * R_B
                spread = jnp.concatenate([jnp.where(first, cols[:, c0 + 2 * p:c0 + 2 * p + 1],
                                                    cols[:, c0 + 2 * p + 1:c0 + 2 * p + 2])
                                          for p in range(R_B // 2)], axis=1)
                sn = _dot((xg * spread).T.astype(BF16), bg)
                for r in range(R_B):
                    sn_ref[d, g * R_B + r] = sn[r * HP_B:(r + 1) * HP_B, :]


def _ssd(xbc, z, gates, params, seq, sub_layer, state=None, carried=None):
    n = xbc.shape[0]
    nseq = n // seq
    has_state = state is not None
    emit_state = not has_state
    aliases = {}
    row = lambda s: (s, 0)
    in_specs = [pl.BlockSpec((seq, B_XBC), row), pl.BlockSpec((seq, B_INNER), row),
                pl.BlockSpec((GATE_ROWS, seq), lambda s: (0, s))]
    in_specs += [_layer_spec(p, sub_layer) for p in params]
    args = [xbc, z, gates, *params]
    out_specs = [pl.BlockSpec((seq, B_INNER), row)]
    out_shape = [jax.ShapeDtypeStruct((n, B_INNER), F32)]
    state_spec = pl.BlockSpec((None, None, 2, NH_B, HP_B, DSTATE), lambda s: (s, sub_layer, 0, 0, 0, 0))
    if has_state:
        in_specs.append(state_spec)
        args.append(state)
    if emit_state:
        out_specs.append(state_spec if carried is not None else
                         pl.BlockSpec((None, N_EVEN, 2, NH_B, HP_B, DSTATE), lambda s: (s, 0, 0, 0, 0, 0)))
        out_shape.append(jax.ShapeDtypeStruct((nseq, N_EVEN, 2, NH_B, HP_B, DSTATE), F32))
        if carried is not None:
            aliases = {len(args): 1}
            in_specs.append(pl.BlockSpec(memory_space=pl.ANY))
            args.append(carried)
    return pl.pallas_call(
        functools.partial(_ssd_kernel, seq=seq, has_state=has_state, emit_state=emit_state, n_carried=len(aliases),
                          slot=sub_layer),
        grid=(nseq,), in_specs=in_specs, out_specs=out_specs, out_shape=out_shape,
        input_output_aliases=aliases, compiler_params=_params("arbitrary"), name="ssd",
    )(*args)


def _shared_split(x, x_swapped, kh):
    first = _pair_lanes(x.shape)
    zero = jnp.zeros_like(x)
    top, bottom = (x, x_swapped) if kh == 0 else (x_swapped, x)
    return jnp.concatenate([jnp.where(first, top, zero), jnp.where(first, zero, bottom)], axis=0)


def _pair_probs(s, sinks=None, valid=None):
    n_keys = s.shape[1] // 2
    probs, maxes = [], []
    for i in range(2):
        si = s[:, i * n_keys:(i + 1) * n_keys]
        if valid is not None:
            si = jnp.where(valid, si, -jnp.inf)
        m = jnp.max(si, axis=1, keepdims=True)
        if sinks is not None:
            m = jnp.maximum(m, sinks[i])
        probs.append(jnp.exp(si - m))
        maxes.append(m)
    return jnp.concatenate(probs, axis=1).astype(BF16), maxes


def _pair_output(p, maxes, vbd, sinks=None):
    o = _dot(p, vbd)
    den = o[:, LANES:]
    if sinks is not None:
        den = den + jnp.where(_pair_lanes(den.shape), jnp.exp(sinks[0] - maxes[0]), jnp.exp(sinks[1] - maxes[1]))
    return o[:, :LANES] / den


def _run_pairs(items, valid=None):
    s_next = items[0][0]()
    for idx, (_, values, sinks, out_ref, cols) in enumerate(items):
        s_cur = s_next
        if idx + 1 < len(items):
            s_next = items[idx + 1][0]()
        p, maxes = _pair_probs(s_cur, sinks, valid if sinks is not None else None)
        out_ref[:, cols] = _pair_output(p, maxes, values(), sinks)


def _pair_sinks(sink_ref, n):
    return sink_ref[0:1, n:n + 1], sink_ref[0:1, n + 1:n + 2]


def _mla_queries(qa_ref, qan_ref, wqb_ref):
    return _dot(_rms(qa_ref[...], qan_ref[...]).astype(BF16), wqb_ref[...]) * MLA_SCALE


def _attn_ctx_kernel(qc_ref, kc_ref, vc_ref, qa_ref, kva_ref, kpe_ref, sink_ref, qan_ref, kvn_ref, wqb_ref, wkvb_ref,
                     *rest, seq, n_carried, slot):
    oc_ref, od_ref, ckv_ref = rest[n_carried:]
    if n_carried == 0:
        for other in range(N_ODD):
            if other != slot:
                ckv_ref[other] = jnp.zeros(ckv_ref.shape[1:], F32)
        ckv_ref = ckv_ref.at[slot]
    ones_bd = _pair_split(jnp.ones((seq, LANES), F32))
    kc, vc = kc_ref[...], vc_ref[...]
    kc_sw, vc_sw = pltpu.roll(kc, HD_C, 1), pltpu.roll(vc, HD_C, 1)
    qd = _mla_queries(qa_ref, qan_ref, wqb_ref)
    ckv = _rms(kva_ref[...], kvn_ref[...])
    ckv_ref[...] = ckv
    kv = _dot(ckv.astype(BF16), wkvb_ref[...])
    kpe = kpe_ref[...]
    kpe_bd = jnp.concatenate([kpe, pltpu.roll(kpe, ROPE_D, 1)], axis=0)
    nope_w = NH_D * NOPE_D
    items = []
    for kh in range(NKV_C):
        for n in range(kh * G_C, (kh + 1) * G_C, 2):
            cols = slice(n * HD_C, (n + 2) * HD_C)
            items.append((lambda cols=cols, kh=kh: _dot_nt((qc_ref[:, cols] * (HD_C ** -0.5)).astype(BF16),
                                                           _shared_split(kc, kc_sw, kh).astype(BF16)),
                          lambda kh=kh: jnp.concatenate([_shared_split(vc, vc_sw, kh), ones_bd], axis=1).astype(BF16),
                          _pair_sinks(sink_ref, n), oc_ref, cols))
    for i in range(NH_D // 2):
        cols = slice(i * LANES, (i + 1) * LANES)
        vcols = slice(nope_w + i * LANES, nope_w + (i + 1) * LANES)
        items.append((lambda cols=cols, vcols=vcols: _dot_nt(
                          jnp.concatenate([qd[:, cols], qd[:, vcols]], axis=1).astype(BF16),
                          jnp.concatenate([_pair_split(kv[:, cols]), kpe_bd], axis=1).astype(BF16)),
                      lambda vcols=vcols: jnp.concatenate([_pair_split(kv[:, vcols]), ones_bd], axis=1).astype(BF16),
                      None, od_ref, cols))
    _run_pairs(items)


def _attn_ctx(proj, params, seq, sub_layer, carried=None):
    n = proj[0].shape[0]
    nseq = n // seq
    row = lambda s: (s, 0)
    in_specs = [pl.BlockSpec((seq, wd), row) for wd in ODD_WIDTHS]
    in_specs += [_layer_spec(p, sub_layer) for p in params]
    args = [*proj, *params]
    half = NH_C * HD_C
    aliases = {}
    if carried is None:
        ckv_spec = pl.BlockSpec((None, N_ODD, seq, KV_RANK), lambda s: (s, 0, 0, 0))
    else:
        ckv_spec = pl.BlockSpec((None, None, seq, KV_RANK), lambda s: (s, sub_layer, 0, 0))
        aliases = {len(args): 2}
        in_specs.append(pl.BlockSpec(memory_space=pl.ANY))
        args.append(carried)
    return pl.pallas_call(
        functools.partial(_attn_ctx_kernel, seq=seq, n_carried=len(aliases), slot=sub_layer),
        grid=(nseq,), in_specs=in_specs,
        out_specs=[pl.BlockSpec((seq, half), row), pl.BlockSpec((seq, half), row), ckv_spec],
        out_shape=[jax.ShapeDtypeStruct((n, half), F32), jax.ShapeDtypeStruct((n, half), F32),
                   jax.ShapeDtypeStruct((nseq, N_ODD, seq, KV_RANK), F32)],
        input_output_aliases=aliases, compiler_params=_params("arbitrary"), name="attn_ctx",
    )(*args)


def _rope(x, cos, sin, half):
    parts = []
    lane = _lane_iota((x.shape[0], LANES))
    first = (lane & (2 * half - 1)) < half
    for i in range(x.shape[1] // LANES):
        xi = x[:, i * LANES:(i + 1) * LANES]
        partner = jnp.where(first, -pltpu.roll(xi, LANES - half, 1), pltpu.roll(xi, half, 1))
        parts.append(xi * cos + partner * sin)
    return parts[0] if len(parts) == 1 else jnp.concatenate(parts, axis=1)


def _attn_lat_kernel(qc_ref, qa_ref, ropeq_ref, kc_ref, vc_ref, kva_ref, kpe_ref, kctx_ref, vctx_ref, ckvctx_ref,
                     kpectx_ref, rope_ref, sink_ref, qan_ref, kvn_ref, wqb_ref, wkvb_ref, oc_ref, od_ref,
                     kwin_s, vwin_s, kext_s, vext_s, *, seq, past):
    qi = pl.program_id(1)
    nope_w = NH_D * NOPE_D
    n_all = past + seq
    ctx0 = 2 * WINDOW + seq

    @pl.when(qi == 0)
    def _():
        zeros = jnp.zeros((WINDOW, LANES), BF16)
        for ref, lat, ctx in ((kwin_s, _rope(kc_ref[...], rope_ref[0], rope_ref[1], HD_C // 2), kctx_ref[...]),
                              (vwin_s, vc_ref[...], vctx_ref[...])):
            lat_sw, ctx_sw = pltpu.roll(lat, HD_C, 1), pltpu.roll(ctx, HD_C, 1)
            for kh in range(NKV_C):
                lat_bd = _shared_split(lat, lat_sw, kh).astype(BF16)
                ctx_bd = _shared_split(ctx, ctx_sw, kh).astype(BF16)
                for i in range(2):
                    ref[kh, i, 0:WINDOW, :] = zeros
                    ref[kh, i, WINDOW:WINDOW + seq, :] = lat_bd[i * seq:(i + 1) * seq]
                    ref[kh, i, WINDOW + seq:ctx0, :] = zeros
                    ref[kh, i, ctx0:, :] = ctx_bd[i * past:(i + 1) * past]
        ckv = _rms(kva_ref[...], kvn_ref[...])
        kv = jnp.concatenate([_dot(ckvctx_ref[...].astype(BF16), wkvb_ref[...]),
                              _dot(ckv.astype(BF16), wkvb_ref[...])], axis=0)
        kpe = jnp.concatenate([kpectx_ref[...], _rope(kpe_ref[...], rope_ref[2], rope_ref[3], ROPE_D // 2)], axis=0)
        kpe_bd = jnp.concatenate([kpe, pltpu.roll(kpe, ROPE_D, 1)], axis=0).astype(BF16)
        ones_bd = _pair_split(jnp.ones((n_all, LANES), F32)).astype(BF16)
        for i in range(NH_D // 2):
            kext_s[i, :, 0:LANES] = _pair_split(kv[:, i * LANES:(i + 1) * LANES]).astype(BF16)
            kext_s[i, :, LANES:] = kpe_bd
            vext_s[i, :, 0:LANES] = _pair_split(kv[:, nope_w + i * LANES:nope_w + (i + 1) * LANES]).astype(BF16)
            vext_s[i, :, LANES:] = ones_bd

    r0 = pl.multiple_of(qi * Q_TILE, Q_TILE)
    nloc = Q_TILE + 2 * WINDOW
    n_keys = nloc + past
    qr = _rope(qc_ref[...], ropeq_ref[0], ropeq_ref[1], HD_C // 2) * (HD_C ** -0.5)
    ti = r0 + _row_iota((Q_TILE, n_keys))
    col = _lane_iota((Q_TILE, n_keys))
    pos = r0 - WINDOW + col
    valid = (col >= nloc) | ((jnp.abs(ti - pos) <= WINDOW) & (pos >= 0) & (pos < seq))
    ones_bd = _pair_split(jnp.ones((n_keys, LANES), F32)).astype(BF16)
    qd = _mla_queries(qa_ref, qan_ref, wqb_ref)
    q_pe = _rope(qd[:, nope_w:], ropeq_ref[2], ropeq_ref[3], ROPE_D // 2)

    def banded(ref, kh):
        return jnp.concatenate([ref[kh, 0, pl.ds(r0, nloc), :], ref[kh, 0, ctx0:, :],
                                ref[kh, 1, pl.ds(r0, nloc), :], ref[kh, 1, ctx0:, :]], axis=0)

    items = []
    for kh in range(NKV_C):
        for n in range(kh * G_C, (kh + 1) * G_C, 2):
            cols = slice(n * HD_C, (n + 2) * HD_C)
            items.append((lambda cols=cols, kh=kh: _dot_nt(qr[:, cols].astype(BF16), banded(kwin_s, kh)),
                          lambda kh=kh: jnp.concatenate([banded(vwin_s, kh), ones_bd], axis=1),
                          _pair_sinks(sink_ref, n), oc_ref, cols))
    for i in range(NH_D // 2):
        cols = slice(i * LANES, (i + 1) * LANES)
        items.append((lambda cols=cols, i=i: _dot_nt(jnp.concatenate([qd[:, cols], q_pe[:, cols]], axis=1).astype(BF16),
                                                     kext_s[i]),
                      lambda i=i: vext_s[i], None, od_ref, cols))
    _run_pairs(items, valid)


def _attn_lat(proj, caches, rope, params, seq, sub_layer):
    qc, kc, vc, qa, kva, kpe = proj
    n = qc.shape[0]
    past = caches[0].shape[2]
    nq = seq // Q_TILE
    qrow = lambda b, q: (b * nq + q, 0)
    krow = lambda b, q: (b, 0)
    kvw = NKV_C * HD_C
    in_specs = [pl.BlockSpec((Q_TILE, NH_C * HD_C), qrow), pl.BlockSpec((Q_TILE, Q_RANK), qrow),
                pl.BlockSpec((4, Q_TILE, LANES), lambda b, q: (0, q, 0)),
                pl.BlockSpec((seq, kvw), krow), pl.BlockSpec((seq, kvw), krow),
                pl.BlockSpec((seq, KV_RANK), krow), pl.BlockSpec((seq, LANES), krow)]
    in_specs += [pl.BlockSpec((None, None, past, LANES), lambda b, q: (b, sub_layer, 0, 0)) for _ in caches]
    in_specs += [pl.BlockSpec(rope.shape, lambda b, q: (0, 0, 0))]
    in_specs += [_layer_spec(p, sub_layer) for p in params]
    half = NH_C * HD_C
    win_rows = 2 * WINDOW + seq + past
    return pl.pallas_call(
        functools.partial(_attn_lat_kernel, seq=seq, past=past),
        grid=(n // seq, nq), in_specs=in_specs,
        out_specs=[pl.BlockSpec((Q_TILE, half), qrow), pl.BlockSpec((Q_TILE, half), qrow)],
        out_shape=[jax.ShapeDtypeStruct((n, half), F32), jax.ShapeDtypeStruct((n, half), F32)],
        scratch_shapes=[pltpu.VMEM((NKV_C, 2, win_rows, LANES), BF16), pltpu.VMEM((NKV_C, 2, win_rows, LANES), BF16),
                        pltpu.VMEM((NH_D // 2, 2 * (past + seq), 2 * LANES), BF16),
                        pltpu.VMEM((NH_D // 2, 2 * (past + seq), 2 * LANES), BF16)],
        compiler_params=_params("arbitrary", "arbitrary"), name="attn_lat",
    )(qc, qa, rope, kc, vc, kva, kpe, *caches, rope, *params)


def _pad_lanes(x, width=LANES):
    return jnp.pad(x, [(0, 0)] * (x.ndim - 1) + [(0, width - x.shape[-1])])


def _on_lanes(x):
    return jnp.broadcast_to(x[..., None], x.shape + (LANES,))


def _mla_query_weights(w):
    lead = w.shape[:-1]
    w4 = w.reshape(lead + (NH_D // 2, 2, NOPE_D + ROPE_D))
    nope = w4[..., :NOPE_D].reshape(lead + (NH_D * NOPE_D,))
    pe = _pad_lanes(w4[..., NOPE_D:].reshape(lead + (NH_D // 2, 2 * ROPE_D)))
    return jnp.concatenate([nope, pe.reshape(lead + (NH_D // 2 * LANES,))], axis=-1).astype(BF16)


def _mla_kv_weights(w):
    lead = w.shape[:-1]
    w3 = w.reshape(lead + (NH_D, NOPE_D + V_D))
    return jnp.concatenate([w3[..., :NOPE_D].reshape(lead + (NH_D * NOPE_D,)),
                            w3[..., NOPE_D:].reshape(lead + (NH_D * V_D,))], axis=-1).astype(BF16)


def _rope_tables(rows):
    def table(rot_dim):
        quarter = rot_dim // 4
        inv = ROPE_BASE ** (-jnp.arange(quarter, dtype=F32) / quarter)
        r = jnp.repeat(jnp.arange(rows, dtype=F32), GRID_W)
        col = jnp.tile(jnp.arange(GRID_W, dtype=F32), rows)
        ang = jnp.concatenate([r[:, None] * inv, col[:, None] * inv], axis=-1)
        reps = LANES // (rot_dim // 2)
        return jnp.tile(jnp.cos(ang), (1, reps)), jnp.tile(jnp.sin(ang), (1, reps))
    cos_c, sin_c = table(HD_C)
    cos_d, sin_d = table(ROPE_D)
    return jnp.stack([cos_c, sin_c, cos_d, sin_d])


def kernel(x_prompt, x_sample, c, state_mlstm_C, state_mlstm_n, state_mlstm_m, state_ssd, cache_gqa_k, cache_gqa_v,
           cache_mla_ckv, cache_mla_kpe, c_ctx, w_ada, b_ada, norm_g, w_up, w_down, w_in_even, conv_a_w, conv_a_b,
           conv_b_w, conv_b_b, gate_b, a_norm_w, dt_bias, a_log, d_skip, b_norm_w, w_out_even, w_in_odd, sink,
           q_a_norm, kv_a_norm, w_q_b, w_kv_b, w_out_odd):
    xp = x_prompt.reshape(BATCH * SEQ, D_MODEL)
    xs = x_sample.reshape(DEC_BATCH * DEC_SEQ, D_MODEL)
    cond = jnp.concatenate([c_ctx[None, :], c, jnp.zeros((MOD_ROWS - 1 - DEC_BATCH, D_MODEL), F32)], axis=0)
    mods = _modulations(cond, w_ada, b_ada)
    rope = _rope_tables(DEC_SEQ // GRID_W)

    w_even = w_in_even.astype(BF16)
    a_params = (conv_a_w, conv_a_b[:, None, :], _on_lanes(gate_b), a_norm_w[:, None, :])
    b_params = (conv_b_w, conv_b_b[:, None, :], _on_lanes(dt_bias.reshape(N_EVEN, 2 * NH_B)),
                _on_lanes(a_log.reshape(N_EVEN, 2 * NH_B)), jnp.repeat(d_skip, HP_B, axis=1)[:, None, :],
                b_norm_w[:, None, :])
    n0 = state_mlstm_n[..., None]
    mem_in = (jnp.concatenate([state_mlstm_C, jnp.broadcast_to(n0, n0.shape[:-1] + (LANES,))], axis=-1),
              _on_lanes(state_mlstm_m.reshape(DEC_BATCH, N_EVEN, 2 * NH_A)))
    w_odd = w_in_odd.astype(BF16)
    o_params = (_pad_lanes(sink)[:, None, :], q_a_norm[:, None, :], kv_a_norm[:, None, :],
                _mla_query_weights(w_q_b), _mla_kv_weights(w_kv_b))
    caches = (cache_gqa_k.reshape(DEC_BATCH, N_ODD, PAST_LEN, NKV_C * HD_C),
              cache_gqa_v.reshape(DEC_BATCH, N_ODD, PAST_LEN, NKV_C * HD_C),
              cache_mla_ckv, _pad_lanes(cache_mla_kpe))

    new_k, new_v, new_kpe = [], [], []
    mem_state, ssd_state, ckv_state = None, None, None
    for l in range(DEPTH):
        j = l // 2
        if l % 2 == 0:
            qk, v, o, z, xbc, g = _project(xp, mods, l, False, norm_g, w_even, EVEN_WIDTHS, EVEN_REGROUP, EVEN_GATE_COLS)
            a1p, *mem_state = _mlstm(qk, v, o, g, a_params, SEQ, j, carried=mem_state)
            a2p, ssd_state = _ssd(xbc, z, g, b_params, SEQ, j, carried=ssd_state)
            qk, v, o, z, xbc, g = _project(xs, mods, l, True, norm_g, w_even, EVEN_WIDTHS, EVEN_REGROUP, EVEN_GATE_COLS)
            a1s, = _mlstm(qk, v, o, g, a_params, DEC_SEQ, j, state=mem_in)
            a2s, = _ssd(xbc, z, g, b_params, DEC_SEQ, j, state=state_ssd)
            w_out = w_out_even
        else:
            proj = _project(xp, mods, l, False, norm_g, w_odd, ODD_WIDTHS, ODD_REGROUP)
            a1p, a2p, ckv_state = _attn_ctx(proj, o_params, SEQ, j, carried=ckv_state)
            new_k.append(proj[1].reshape(BATCH, SEQ, NKV_C, HD_C))
            new_v.append(proj[2].reshape(BATCH, SEQ, NKV_C, HD_C))
            new_kpe.append(proj[5][:, :ROPE_D].reshape(BATCH, SEQ, ROPE_D))
            proj = _project(xs, mods, l, True, norm_g, w_odd, ODD_WIDTHS, ODD_REGROUP)
            a1s, a2s = _attn_lat(proj, caches, rope, o_params, DEC_SEQ, j)
            w_out = w_out_odd
        xp, xs = _channel((a1p, a2p, xp), (a1s, a2s, xs), mods, l, norm_g, w_out, w_up, w_down)

    new_c, new_n, new_m = mem_state
    return (xp.reshape(BATCH, SEQ, D_MODEL), xs.reshape(DEC_BATCH, DEC_SEQ, D_MODEL),
            new_c, new_n, new_m[..., 0].reshape(BATCH, N_EVEN, 2, NH_A), ssd_state,
            jnp.stack(new_k, axis=1), jnp.stack(new_v, axis=1), ckv_state, jnp.stack(new_kpe, axis=1))
```

```python
import functools

import jax
import jax.numpy as jnp
from jax import lax
from jax.experimental import pallas as pl
from jax.experimental.pallas import tpu as pltpu

F32 = jnp.float32
BF16 = jnp.bfloat16

D_MODEL = 1024
BATCH = 32
SEQ = 256
DEPTH = 4
DEC_BATCH = 2
DEC_SEQ = 1024
PAST_LEN = 256
GRID_W = 64
N_EVEN = (DEPTH + 1) // 2
N_ODD = DEPTH // 2
EPS = 1e-6
CONV_K = 5
NH_A = 4
DK_A = 128
DV_A = 128
A_QK = NH_A * DK_A
A_V = NH_A * DV_A
NH_B = 8
HP_B = 64
DSTATE = 128
NG_B = 2
R_B = NH_B // NG_B
B_INNER = NH_B * HP_B
B_BC = NG_B * DSTATE
B_XBC = B_INNER + 2 * B_BC
NH_C = 8
NKV_C = 2
G_C = NH_C // NKV_C
HD_C = 64
WINDOW = 128
NH_D = 8
Q_RANK = 256
KV_RANK = 128
NOPE_D = 64
ROPE_D = 32
V_D = 64
MLA_SCALE = (NOPE_D + ROPE_D) ** -0.5
D_FF = 4 * D_MODEL
ROPE_BASE = 10000.0

LANES = 128
VMEM_LIMIT_BYTES = 56 * 1024 * 1024
ROW_TILE = 512
FF_TILE = 1024
SUB_ROWS = 512
STAGE_ROWS = 512
Q_TILE = 256
ADA_TILE = 1536
MOD_ROWS = 8
GATE_ROWS = 4 * NH_A + 2 * NH_B

EVEN_WIDTHS = (2 * A_QK, A_V, A_V, B_INNER, B_XBC)
ODD_WIDTHS = (NH_C * HD_C, NKV_C * HD_C, NKV_C * HD_C, Q_RANK, KV_RANK, LANES)
_GATES_LO = 2 * A_QK + 2 * A_V
_Z_LO = _GATES_LO + 4 * NH_A
_DT_LO = _Z_LO + B_INNER + B_XBC
EVEN_REGROUP = ((0, 0, _GATES_LO), (_GATES_LO, _Z_LO, B_INNER + B_XBC))
EVEN_GATE_COLS = ((_GATES_LO, _Z_LO), (_DT_LO, _DT_LO + 2 * NH_B))
ODD_IN = sum(ODD_WIDTHS) - LANES + ROPE_D
ODD_REGROUP = ((0, 0, ODD_IN), (ODD_IN, None, LANES - ROPE_D))

_NT = (((1,), (1,)), ((), ()))


def _params(*sem):
    return pltpu.CompilerParams(dimension_semantics=sem, vmem_limit_bytes=VMEM_LIMIT_BYTES)


def _rms(x, g):
    return x * lax.rsqrt(jnp.mean(x * x, axis=-1, keepdims=True) + EPS) * g


def _silu(x):
    return x * jax.nn.sigmoid(x)


def _softplus(x):
    return jnp.maximum(x, 0.0) + jnp.log1p(jnp.exp(-jnp.abs(x)))


def _dot(a, b):
    return jnp.dot(a, b, preferred_element_type=F32)


def _dot_nt(a, b):
    return lax.dot_general(a, b, _NT, preferred_element_type=F32)


def _layer_spec(arr, layer):
    tail = arr.shape[1:]
    zeros = (0,) * len(tail)
    return pl.BlockSpec((None,) + tail, lambda *_: (layer,) + zeros)


def _ada_kernel(c_ref, w_ref, b_ref, o_ref):
    s = _silu(c_ref[...]).astype(BF16)
    o_ref[...] = _dot(s, w_ref[...].astype(BF16)) + b_ref[...]


def _modulations(cond, w_ada, b_ada):
    out = pl.pallas_call(
        _ada_kernel,
        grid=(DEPTH, 6 * D_MODEL // ADA_TILE),
        in_specs=[pl.BlockSpec((MOD_ROWS, D_MODEL), lambda l, n: (0, 0)),
                  pl.BlockSpec((None, D_MODEL, ADA_TILE), lambda l, n: (l, 0, n)),
                  pl.BlockSpec((None, 1, ADA_TILE), lambda l, n: (l, 0, n))],
        out_specs=pl.BlockSpec((None, MOD_ROWS, ADA_TILE), lambda l, n: (l, 0, n)),
        out_shape=jax.ShapeDtypeStruct((DEPTH, MOD_ROWS, 6 * D_MODEL), F32),
        compiler_params=_params("arbitrary", "arbitrary"),
        name="ada",
    )(cond, w_ada, b_ada.reshape(DEPTH, 1, 6 * D_MODEL))
    return out.reshape(DEPTH, MOD_ROWS, 6, D_MODEL)


def _mod_spec(layer, latent):
    if latent:
        per_seq = DEC_SEQ // ROW_TILE
        return pl.BlockSpec((None, None, 6, D_MODEL), lambda i, *_: (layer, 1 + i // per_seq, 0, 0))
    return pl.BlockSpec((None, None, 6, D_MODEL), lambda i, *_: (layer, 0, 0, 0))


def _proj_kernel(x_ref, mod_ref, g_ref, w_ref, *rest, widths, regroup, gate_cols):
    n_out = len(widths) + (1 if gate_cols else 0)
    o_refs, w_s = rest[:n_out], rest[n_out]

    @pl.when(pl.program_id(0) == 0)
    def _():
        for dst, src, width in regroup:
            if src is None:
                w_s[:, dst:dst + width] = jnp.zeros((D_MODEL, width), BF16)
            else:
                w_s[:, dst:dst + width] = w_ref[:, src:src + width]
        if gate_cols:
            pieces = [w_ref[:, lo:hi].astype(F32) for lo, hi in gate_cols]
            n_gates = sum(hi - lo for lo, hi in gate_cols)
            gates = jnp.concatenate(pieces + [jnp.zeros((D_MODEL, LANES - n_gates), F32)], axis=1)
            rest[n_out + 1][...] = gates.T[0:n_gates, :].astype(BF16)

    h = _rms(x_ref[...], g_ref[0:1, :]) * (1.0 + mod_ref[1:2, :]) + mod_ref[0:1, :]
    hb = h.astype(BF16)
    off = 0
    for o_ref, wd in zip(o_refs, widths):
        o_ref[...] = _dot(hb, w_s[:, off:off + wd])
        off += wd
    if gate_cols:
        o_refs[-1][...] = _dot_nt(rest[n_out + 1][...], hb)


def _project(x, mods, layer, latent, gains, w_all, widths, regroup, gate_cols=()):
    n = x.shape[0]
    out_specs = [pl.BlockSpec((ROW_TILE, wd), lambda i: (i, 0)) for wd in widths]
    out_shape = [jax.ShapeDtypeStruct((n, wd), F32) for wd in widths]
    scratch = [pltpu.VMEM((D_MODEL, sum(widths)), BF16)]
    if gate_cols:
        n_gates = sum(hi - lo for lo, hi in gate_cols)
        out_specs.append(pl.BlockSpec((n_gates, ROW_TILE), lambda i: (0, i)))
        out_shape.append(jax.ShapeDtypeStruct((n_gates, n), F32))
        scratch.append(pltpu.VMEM((n_gates, D_MODEL), BF16))
    w_spec = pl.BlockSpec((None,) + w_all.shape[1:], lambda i: (layer // 2, 0, 0), pipeline_mode=pl.Buffered(1))
    return pl.pallas_call(
        functools.partial(_proj_kernel, widths=widths, regroup=regroup, gate_cols=gate_cols),
        grid=(n // ROW_TILE,),
        in_specs=[pl.BlockSpec((ROW_TILE, D_MODEL), lambda i: (i, 0)), _mod_spec(layer, latent),
                  _layer_spec(gains, layer), w_spec],
        out_specs=out_specs, out_shape=out_shape, scratch_shapes=scratch,
        compiler_params=_params("arbitrary"),
        name="proj",
    )(x, mods, gains, w_all)


def _weight_chunks(layer, sub_layer, wo_hbm, wu_hbm, wd_hbm, wo_s, wu_s, wd_s):
    chunks = []
    for r in range(0, D_MODEL, STAGE_ROWS):
        chunks.append((wo_hbm.at[sub_layer, pl.ds(r, STAGE_ROWS), :], wo_s.at[pl.ds(r, STAGE_ROWS), :]))
    for r in range(0, D_MODEL, STAGE_ROWS):
        for c in range(0, D_FF, D_MODEL):
            chunks.append((wu_hbm.at[layer, pl.ds(r, STAGE_ROWS), pl.ds(c, D_MODEL)],
                           wu_s.at[pl.ds(r, STAGE_ROWS), pl.ds(c, D_MODEL)]))
    for r in range(0, D_FF, STAGE_ROWS):
        chunks.append((wd_hbm.at[layer, pl.ds(r, STAGE_ROWS), :], wd_s.at[pl.ds(r, STAGE_ROWS), :]))
    return chunks


def _channel_kernel(a1p_ref, a2p_ref, xp_ref, a1s_ref, a2s_ref, xs_ref, mod_ref, g_ref, wo_hbm, wu_hbm, wd_hbm,
                    op_ref, os_ref, wo_s, wu_s, wd_s, stage, sem, *, layer, sub_layer, prompt_steps):
    step = pl.program_id(0)

    @pl.when(step == 0)
    def _():
        chunks = _weight_chunks(layer, sub_layer, wo_hbm, wu_hbm, wd_hbm, wo_s, wu_s, wd_s)
        copies = [pltpu.make_async_copy(src, stage.at[k % 2], sem.at[k % 2]) for k, (src, _) in enumerate(chunks)]
        copies[0].start()
        for k, (_, dst) in enumerate(chunks):
            if k + 1 < len(chunks):
                copies[k + 1].start()
            copies[k].wait()
            dst[...] = stage[k % 2].astype(BF16)

    def rows_block(a1_ref, a2_ref, x_ref, o_ref):
        half = a1_ref.shape[1]
        for r0 in range(0, ROW_TILE, SUB_ROWS):
            rows = slice(r0, r0 + SUB_ROWS)
            y = (_dot(a1_ref[rows, :].astype(BF16), wo_s[0:half, :])
                 + _dot(a2_ref[rows, :].astype(BF16), wo_s[half:, :]))
            x1 = x_ref[rows, :] + mod_ref[2:3, :] * _rms(y, g_ref[1:2, :])
            h = (_rms(x1, g_ref[2:3, :]) * (1.0 + mod_ref[4:5, :]) + mod_ref[3:4, :]).astype(BF16)
            acc = None
            for c in range(0, D_FF, FF_TILE):
                u = jnp.square(jnp.maximum(_dot(h, wu_s[:, c:c + FF_TILE]), 0.0)).astype(BF16)
                part = _dot(u, wd_s[c:c + FF_TILE, :])
                acc = part if acc is None else acc + part
            o_ref[rows, :] = x1 + mod_ref[5:6, :] * _rms(acc, g_ref[3:4, :])

    @pl.when(step < prompt_steps)
    def _():
        rows_block(a1p_ref, a2p_ref, xp_ref, op_ref)

    @pl.when(step >= prompt_steps)
    def _():
        rows_block(a1s_ref, a2s_ref, xs_ref, os_ref)


def _channel(prompt, latent, mods, layer, gains, w_out, w_up, w_down):
    n_p, n_s = prompt[2].shape[0], latent[2].shape[0]
    steps_p, steps_s = n_p // ROW_TILE, n_s // ROW_TILE
    per_seq = DEC_SEQ // ROW_TILE
    row_p = lambda i: (jnp.minimum(i, steps_p - 1), 0)
    row_s = lambda i: (jnp.maximum(i - steps_p, 0), 0)
    mod_spec = pl.BlockSpec((None, None, 6, D_MODEL),
                            lambda i: (layer, jnp.where(i < steps_p, 0, 1 + (i - steps_p) // per_seq), 0, 0))
    hbm = pl.BlockSpec(memory_space=pl.ANY)
    specs = lambda arrs, row: [pl.BlockSpec((ROW_TILE, a.shape[1]), row) for a in arrs]
    return pl.pallas_call(
        functools.partial(_channel_kernel, layer=layer, sub_layer=layer // 2, prompt_steps=steps_p),
        grid=(steps_p + steps_s,),
        in_specs=specs(prompt, row_p) + specs(latent, row_s) + [mod_spec, _layer_spec(gains, layer), hbm, hbm, hbm],
        out_specs=[pl.BlockSpec((ROW_TILE, D_MODEL), row_p), pl.BlockSpec((ROW_TILE, D_MODEL), row_s)],
        out_shape=[jax.ShapeDtypeStruct((n_p, D_MODEL), F32), jax.ShapeDtypeStruct((n_s, D_MODEL), F32)],
        scratch_shapes=[pltpu.VMEM((D_MODEL, D_MODEL), BF16), pltpu.VMEM((D_MODEL, D_FF), BF16),
                        pltpu.VMEM((D_FF, D_MODEL), BF16), pltpu.VMEM((2, STAGE_ROWS, D_MODEL), F32),
                        pltpu.SemaphoreType.DMA((2,))],
        compiler_params=_params("arbitrary"),
        name="channel",
    )(*prompt, *latent, mods, gains, w_out, w_up, w_down)


def _row_iota(shape):
    return lax.broadcasted_iota(jnp.int32, shape, 0)


def _lane_iota(shape):
    return lax.broadcasted_iota(jnp.int32, shape, 1)


def _pair_lanes(shape):
    return _lane_iota(shape) < LANES // 2


def _cumsum_lanes(x, n_fwd):
    t = x.shape[1]
    si, ti = _row_iota((t, t)), _lane_iota((t, t))
    upper = jnp.where(si <= ti, 1.0, 0.0).astype(BF16)
    lower = jnp.where(si >= ti, 1.0, 0.0).astype(BF16)
    hi = x.astype(BF16)
    rest = x - hi.astype(F32)
    mid = rest.astype(BF16)
    lo = (rest - mid.astype(F32)).astype(BF16)
    pre = _dot(hi, upper) + _dot(mid, upper) + _dot(lo, upper)
    suf = _dot(hi, lower) + _dot(mid, lower) + _dot(lo, lower)
    return jnp.where(_row_iota(x.shape) < n_fwd, pre, suf)


def _cummax_lanes(x, n_fwd):
    t = x.shape[1]
    lane = _lane_iota(x.shape)
    pre, suf = x, x
    k = 1
    while k < t:
        pre = jnp.maximum(pre, jnp.where(lane >= k, pltpu.roll(pre, k, 1), -jnp.inf))
        suf = jnp.maximum(suf, jnp.where(lane < t - k, pltpu.roll(suf, t - k, 1), -jnp.inf))
        k *= 2
    return jnp.where(_row_iota(x.shape) < n_fwd, pre, suf)


def _columns(row_arrays):
    t = row_arrays[0].shape[1]
    used = sum(a.shape[0] for a in row_arrays)
    return jnp.concatenate(list(row_arrays) + [jnp.zeros((LANES - used, t), F32)], axis=0).T


def _dwconv_silu(x, w, b):
    t = x.shape[0]
    row = _row_iota(x.shape)
    acc = x * w[CONV_K // 2:CONV_K // 2 + 1, :] + b
    for j in range(CONV_K):
        d = j - CONV_K // 2
        if d == 0:
            continue
        shifted = pltpu.roll(x, (-d) % t, 0)
        valid = (row >= -d) if d < 0 else (row < t - d)
        acc = acc + jnp.where(valid, shifted, 0.0) * w[j:j + 1, :]
    return _silu(acc)


def _causal_exponent(expo, r0, k0, reverse):
    ti = r0 + _row_iota(expo.shape)
    si = k0 + _lane_iota(expo.shape)
    keep = (si >= ti) if reverse else (si <= ti)
    return jnp.where(keep, expo, -jnp.inf)


def _pair_split(x):
    first = _pair_lanes(x.shape)
    zero = jnp.zeros_like(x)
    return jnp.concatenate([jnp.where(first, x, zero), jnp.where(first, zero, x)], axis=0)


def _key_range(d, r0, seq):
    return (0, r0 + Q_TILE) if d == 0 else (r0, seq)


N_FUSED_INPUTS = 5
PROJ_CHUNK = 512

MLSTM_PIECES = ((0, 2 * A_QK), (2 * A_QK, A_V), (2 * A_QK + A_V, A_V))
MLSTM_GATES = (_GATES_LO, _Z_LO)
MLSTM_W_COLS = _GATES_LO + LANES
SSD_PIECES = ((_Z_LO + B_INNER, B_XBC), (_Z_LO, B_INNER))
SSD_GATES = (_DT_LO, _DT_LO + 2 * NH_B)


def _fused_project(step, fused_refs, scratch, pieces, gate_cols, mix):
    x_cur_ref, x_next_ref, mod_ref, gain_ref, w_ref = fused_refs
    w_s, gt_s = scratch[0], scratch[1]
    n = len(pieces) + 1
    sets = (scratch[2:2 + n], scratch[2 + n:2 + 2 * n])

    def normed(x_ref):
        h = _rms(x_ref[...], gain_ref[0:1, :]) * (1.0 + mod_ref[1:2, :]) + mod_ref[0:1, :]
        return h.astype(BF16)

    def chunk_thunks(hb, dst):
        def gates():
            dst[0][...] = _dot_nt(gt_s[...], hb)

        def columns(ref, c, src):
            def run():
                ref[:, c:c + PROJ_CHUNK] = _dot(hb, w_s[:, src:src + PROJ_CHUNK])
            return run

        thunks = [gates]
        off = 0
        for ref, (_, width) in zip(dst[1:], pieces):
            thunks += [columns(ref, c, off + c) for c in range(0, width, PROJ_CHUNK)]
            off += width
        return thunks

    @pl.when(step == 0)
    def _():
        off = 0
        for src, width in pieces:
            w_s[:, off:off + width] = w_ref[:, src:src + width]
            off += width
        lo, hi = gate_cols
        gates = jnp.concatenate([w_ref[:, lo:hi].astype(F32), jnp.zeros((D_MODEL, LANES - (hi - lo)), F32)], axis=1)
        gt_s[...] = gates.T[0:hi - lo, :].astype(BF16)
        for thunk in chunk_thunks(normed(x_cur_ref), sets[0]):
            thunk()

    for parity in range(2):
        @pl.when(step % 2 == parity)
        def _(parity=parity):
            mix(*sets[parity][1:], sets[parity][0], chunk_thunks(normed(x_next_ref), sets[1 - parity]))


def _fused_specs(seq, nseq, layer, w_all, w_cols):
    nxt = lambda s: (jnp.minimum(s + 1, nseq - 1), 0)
    return [pl.BlockSpec((seq, D_MODEL), lambda s: (s, 0)), pl.BlockSpec((seq, D_MODEL), nxt),
            pl.BlockSpec((None, None, 6, D_MODEL), lambda s: (layer, 0, 0, 0)),
            pl.BlockSpec((None, 4, D_MODEL), lambda s: (layer, 0, 0)),
            pl.BlockSpec((None, D_MODEL, w_cols), lambda s: (layer // 2, 0, 0), pipeline_mode=pl.Buffered(1))]


def _fused_scratch(seq, pieces, n_gates):
    one_set = [pltpu.VMEM((n_gates, seq), F32)] + [pltpu.VMEM((seq, w), F32) for _, w in pieces]
    return ([pltpu.VMEM((D_MODEL, sum(w for _, w in pieces)), BF16), pltpu.VMEM((n_gates, D_MODEL), BF16)]
            + one_set + one_set)


def _run_share(side_work, stages_left):
    for _ in range(-(-len(side_work) // stages_left)):
        side_work.pop(0)()


def _mlstm_kernel(*refs, seq, has_state, emit_state, n_carried, slot, fused):
    assert not (has_state and emit_state)
    it = iter(refs)
    if fused:
        fused_refs = [next(it) for _ in range(N_FUSED_INPUTS)]
    else:
        qk_ref, v_ref, o_ref, g_ref = (next(it) for _ in range(4))
    cw_ref, cb_ref, gb_ref, anw_ref = (next(it) for _ in range(4))
    if has_state:
        c0_ref, m0_ref = next(it), next(it)
    for _ in range(n_carried):
        next(it)
    ha_ref = next(it)
    if emit_state:
        cn_ref, nn_ref, mn_ref = next(it), next(it), next(it)
        if n_carried == 0:
            for other in range(N_EVEN):
                if other != slot:
                    cn_ref[other] = jnp.zeros(cn_ref.shape[1:], F32)
                    nn_ref[other] = jnp.zeros(nn_ref.shape[1:], F32)
                    mn_ref[other] = jnp.zeros(mn_ref.shape[1:], F32)
            cn_ref, nn_ref, mn_ref = cn_ref.at[slot], nn_ref.at[slot], mn_ref.at[slot]

    def mix(qk_ref, v_ref, o_ref, g_ref, side_work=()):
        side_work = list(side_work)
        n_ch = 2 * NH_A
        log_i = g_ref[0:n_ch, :] + gb_ref[0:n_ch, 0:1]
        f_pre = g_ref[n_ch:2 * n_ch, :] + gb_ref[n_ch:2 * n_ch, 0:1]
        log_f = jnp.minimum(f_pre, 0.0) - jnp.log1p(jnp.exp(-jnp.abs(f_pre)))
        b = _cumsum_lanes(log_f, NH_A)
        a = log_i - b
        m_run = _cummax_lanes(a, NH_A)
        if has_state:
            m0 = m0_ref[:, 0:1]
            m_run = jnp.maximum(m_run, m0)
        else:
            m_run = jnp.maximum(m_run, 0.0)
        by_time = [m_run, jnp.exp(-(b + m_run))]
        if has_state:
            by_time.append(jnp.exp(m0 - m_run))
        if emit_state:
            fwd = _row_iota((n_ch, 1)) < NH_A
            b_last = jnp.where(fwd, b[:, seq - 1:seq], b[:, 0:1])
            m_last = jnp.where(fwd, m_run[:, seq - 1:seq], m_run[:, 0:1])
            mn_ref[...] = jnp.broadcast_to(b_last + m_last, (n_ch, LANES))
            by_time.append(jnp.exp(a - m_last))
        cols = _columns(by_time)

        ones = jnp.ones((seq, LANES), F32)
        for h in range(NH_A):
            _run_share(side_work, NH_A - h)
            cq = slice(h * DK_A, (h + 1) * DK_A)
            ck = slice(A_QK + h * DK_A, A_QK + (h + 1) * DK_A)
            cv = slice(h * DV_A, (h + 1) * DV_A)
            q = _dwconv_silu(qk_ref[:, cq], cw_ref[:, cq], cb_ref[:, cq])
            k = _dwconv_silu(qk_ref[:, ck], cw_ref[:, ck], cb_ref[:, ck]) * (DK_A ** -0.5)
            qb = q.astype(BF16)
            kb = k.astype(BF16)
            vh = v_ref[:, cv]
            vaug = jnp.concatenate([vh, ones], axis=1).astype(BF16)
            for r0 in range(0, seq, Q_TILE):
                rows = slice(r0, r0 + Q_TILE)
                s = _dot_nt(qb[rows], kb)
                hsum = None
                for d in range(2):
                    c = d * NH_A + h
                    k0, k1 = _key_range(d, r0, seq)
                    expo = _causal_exponent(a[c:c + 1, k0:k1] - cols[rows, c:c + 1], r0, k0, d == 1)
                    p = (s[:, k0:k1] * jnp.exp(expo)).astype(BF16)
                    acc = _dot(p, vaug[k0:k1])
                    if has_state:
                        acc = acc + (cols[rows, 2 * n_ch + c:2 * n_ch + c + 1]
                                     * _dot(qb[rows], c0_ref[d, h].astype(BF16)))
                    hd = acc[:, 0:DV_A] / jnp.maximum(jnp.abs(acc[:, DV_A:]), cols[rows, n_ch + c:n_ch + c + 1])
                    hsum = hd if hsum is None else hsum + hd
                og = jax.nn.sigmoid(o_ref[rows, cv]) * hsum
                ha_ref[rows, cv] = _rms(og, anw_ref[:, cv])
            if emit_state:
                for d in range(2):
                    c = d * NH_A + h
                    kw = k * cols[:, 2 * n_ch + c:2 * n_ch + c + 1]
                    cn_ref[d, h] = _dot(kw.T.astype(BF16), vh.astype(BF16))
                    nn_ref[d, h:h + 1, :] = jnp.sum(kw, axis=0, keepdims=True)

    if fused:
        _fused_project(pl.program_id(0), fused_refs, list(it), MLSTM_PIECES, MLSTM_GATES, mix)
    else:
        mix(qk_ref, v_ref, o_ref, g_ref)


def _state_out_specs(shapes, nseq, sub_layer, carried):
    out_specs, out_shape = [], []
    for shp in shapes:
        zeros = (0,) * len(shp)
        if carried is None:
            out_specs.append(pl.BlockSpec((None, N_EVEN) + shp, lambda s, z=zeros: (s, 0) + z))
        else:
            out_specs.append(pl.BlockSpec((None, None) + shp, lambda s, z=zeros: (s, sub_layer) + z))
        out_shape.append(jax.ShapeDtypeStruct((nseq, N_EVEN) + shp, F32))
    return out_specs, out_shape


MLSTM_STATE_SHAPES = ((2, NH_A, DK_A, DV_A), (2, NH_A, DK_A), (2 * NH_A, LANES))
SSD_STATE_SHAPES = ((2, NH_B, HP_B, DSTATE),)


def _fused_mixer(kernel_fn, name, x, mods, layer, gains, w_all, w_cols, params, seq, pieces, gate_cols,
                 out_width, state_shapes, carried):
    n = x.shape[0]
    nseq = n // seq
    sub_layer = layer // 2
    in_specs = _fused_specs(seq, nseq, layer, w_all, w_cols) + [_layer_spec(p, sub_layer) for p in params]
    args = [x, x, mods, gains, w_all, *params]
    state_specs, state_shape = _state_out_specs(state_shapes, nseq, sub_layer, carried)
    aliases = {}
    if carried is not None:
        aliases = {len(args) + i: 1 + i for i in range(len(state_shapes))}
        in_specs += [pl.BlockSpec(memory_space=pl.ANY)] * len(state_shapes)
        args += list(carried)
    return pl.pallas_call(
        functools.partial(kernel_fn, seq=seq, has_state=False, emit_state=True, n_carried=len(aliases),
                          slot=sub_layer, fused=True),
        grid=(nseq,), in_specs=in_specs,
        out_specs=[pl.BlockSpec((seq, out_width), lambda s: (s, 0))] + state_specs,
        out_shape=[jax.ShapeDtypeStruct((n, out_width), F32)] + state_shape,
        scratch_shapes=_fused_scratch(seq, pieces, gate_cols[1] - gate_cols[0]),
        input_output_aliases=aliases, compiler_params=_params("arbitrary"), name=name,
    )(*args)


def _mlstm(qk, v, o, gates, params, seq, sub_layer, state=None, carried=None):
    n = qk.shape[0]
    nseq = n // seq
    has_state = state is not None
    emit_state = not has_state
    aliases = {}
    row = lambda s: (s, 0)
    in_specs = [pl.BlockSpec((seq, 2 * A_QK), row), pl.BlockSpec((seq, A_V), row), pl.BlockSpec((seq, A_V), row),
                pl.BlockSpec((GATE_ROWS, seq), lambda s: (0, s))]
    in_specs += [_layer_spec(p, sub_layer) for p in params]
    args = [qk, v, o, gates, *params]
    out_specs = [pl.BlockSpec((seq, A_V), row)]
    out_shape = [jax.ShapeDtypeStruct((n, A_V), F32)]
    if has_state:
        c0_aug, m0 = state
        in_specs += [pl.BlockSpec((None, None) + c0_aug.shape[2:], lambda s: (s, sub_layer, 0, 0, 0, 0)),
                     pl.BlockSpec((None, None) + m0.shape[2:], lambda s: (s, sub_layer, 0, 0))]
        args += [c0_aug, m0]
    if emit_state:
        shapes = [(2, NH_A, DK_A, DV_A), (2, NH_A, DK_A), (2 * NH_A, LANES)]
        for shp in shapes:
            zeros = (0,) * len(shp)
            if carried is None:
                out_specs.append(pl.BlockSpec((None, N_EVEN) + shp, lambda s, z=zeros: (s, 0) + z))
            else:
                out_specs.append(pl.BlockSpec((None, None) + shp, lambda s, z=zeros: (s, sub_layer) + z))
            out_shape.append(jax.ShapeDtypeStruct((nseq, N_EVEN) + shp, F32))
        if carried is not None:
            aliases = {len(args) + i: 1 + i for i in range(len(shapes))}
            in_specs += [pl.BlockSpec(memory_space=pl.ANY)] * len(shapes)
            args += list(carried)
    return pl.pallas_call(
        functools.partial(_mlstm_kernel, seq=seq, has_state=has_state, emit_state=emit_state,
                          n_carried=len(aliases), slot=sub_layer, fused=False),
        grid=(nseq,), in_specs=in_specs, out_specs=out_specs, out_shape=out_shape,
        input_output_aliases=aliases, compiler_params=_params("arbitrary"), name="mlstm",
    )(*args)


def _ssd_kernel(*refs, seq, has_state, emit_state, n_carried, slot, fused):
    assert not (has_state and emit_state)
    it = iter(refs)
    if fused:
        fused_refs = [next(it) for _ in range(N_FUSED_INPUTS)]
    else:
        xbc_ref, z_ref, g_ref = (next(it) for _ in range(3))
    cw_ref, cb_ref, dtb_ref, alog_ref, dsk_ref, bnw_ref = (next(it) for _ in range(6))
    if has_state:
        s0_ref = next(it)
    for _ in range(n_carried):
        next(it)
    yb_ref = next(it)
    if emit_state:
        sn_ref = next(it)
        if n_carried == 0:
            for other in range(N_EVEN):
                if other != slot:
                    sn_ref[other] = jnp.zeros(sn_ref.shape[1:], F32)
            sn_ref = sn_ref.at[slot]

    gate_row0 = 0 if fused else 4 * NH_A

    def mix(xbc_ref, z_ref, g_ref, side_work=()):
        side_work = list(side_work)
        n_ch = 2 * NH_B
        dt = _softplus(g_ref[gate_row0:gate_row0 + n_ch, :] + dtb_ref[:, 0:1])
        acum = _cumsum_lanes(dt * (-jnp.exp(alog_ref[:, 0:1])), NH_B)
        key_shift = acum - jnp.log(dt)
        by_time = [acum]
        if has_state:
            by_time.append(jnp.exp(acum))
        if emit_state:
            fwd = _row_iota((n_ch, 1)) < NH_B
            a_last = jnp.where(fwd, acum[:, seq - 1:seq], acum[:, 0:1])
            by_time.append(jnp.exp(a_last - acum) * dt)
        cols = _columns(by_time)

        gw = R_B * HP_B
        first = _pair_lanes((seq, LANES))
        for g in range(NG_B):
            _run_share(side_work, NG_B - g)
            cx = slice(g * gw, (g + 1) * gw)
            cb_ = slice(B_INNER + g * DSTATE, B_INNER + (g + 1) * DSTATE)
            cc = slice(B_INNER + B_BC + g * DSTATE, B_INNER + B_BC + (g + 1) * DSTATE)
            xg = _dwconv_silu(xbc_ref[:, cx], cw_ref[:, cx], cb_ref[:, cx])
            bg = _dwconv_silu(xbc_ref[:, cb_], cw_ref[:, cb_], cb_ref[:, cb_]).astype(BF16)
            cg = _dwconv_silu(xbc_ref[:, cc], cw_ref[:, cc], cb_ref[:, cc]).astype(BF16)
            xbd = [_pair_split(xg[:, p * LANES:(p + 1) * LANES]).astype(BF16) for p in range(R_B // 2)]
            for r0 in range(0, seq, Q_TILE):
                rows = slice(r0, r0 + Q_TILE)
                cb_scores = _dot_nt(cg[rows], bg)
                ys = []
                for p in range(R_B // 2):
                    weights, inputs = [], []
                    for d in range(2):
                        k0, k1 = _key_range(d, r0, seq)
                        for i in range(2):
                            c = d * NH_B + g * R_B + 2 * p + i
                            expo = _causal_exponent(cols[rows, c:c + 1] - key_shift[c:c + 1, k0:k1], r0, k0, d == 1)
                            weights.append((cb_scores[:, k0:k1] * jnp.exp(expo)).astype(BF16))
                            inputs.append(xbd[p][i * seq + k0:i * seq + k1])
                    yp = _dot(jnp.concatenate(weights, axis=1), jnp.concatenate(inputs, axis=0))
                    if has_state:
                        h0 = g * R_B + 2 * p
                        for d in range(2):
                            c = d * NH_B + h0
                            carry = jnp.where(_pair_lanes((Q_TILE, LANES)), cols[rows, n_ch + c:n_ch + c + 1],
                                              cols[rows, n_ch + c + 1:n_ch + c + 2])
                            s0_pair = s0_ref[d, h0:h0 + 2].reshape(2 * HP_B, DSTATE).astype(BF16)
                            yp = yp + carry * _dot_nt(cg[rows], s0_pair)
                    ys.append(yp)
                y = jnp.concatenate(ys, axis=1) + dsk_ref[:, cx] * xg[rows]
                y = y * _silu(z_ref[rows, cx])
                yb_ref[rows, cx] = _rms(y, bnw_ref[:, cx])
            if emit_state:
                for d in range(2):
                    c0 = n_ch + d * NH_B + g * R_B
                    spread = jnp.concatenate([jnp.where(first, cols[:, c0 + 2 * p:c0 + 2 * p + 1],
                                                        cols[:, c0 + 2 * p + 1:c0 + 2 * p + 2])
                                              for p in range(R_B // 2)], axis=1)
                    sn = _dot((xg * spread).T.astype(BF16), bg)
                    for r in range(R_B):
                        sn_ref[d, g * R_B + r] = sn[r * HP_B:(r + 1) * HP_B, :]

    if fused:
        _fused_project(pl.program_id(0), fused_refs, list(it), SSD_PIECES, SSD_GATES, mix)
    else:
        mix(xbc_ref, z_ref, g_ref)


def _ssd(xbc, z, gates, params, seq, sub_layer, state=None, carried=None):
    n = xbc.shape[0]
    nseq = n // seq
    has_state = state is not None
    emit_state = not has_state
    aliases = {}
    row = lambda s: (s, 0)
    in_specs = [pl.BlockSpec((seq, B_XBC), row), pl.BlockSpec((seq, B_INNER), row),
                pl.BlockSpec((GATE_ROWS, seq), lambda s: (0, s))]
    in_specs += [_layer_spec(p, sub_layer) for p in params]
    args = [xbc, z, gates, *params]
    out_specs = [pl.BlockSpec((seq, B_INNER), row)]
    out_shape = [jax.ShapeDtypeStruct((n, B_INNER), F32)]
    state_spec = pl.BlockSpec((None, None, 2, NH_B, HP_B, DSTATE), lambda s: (s, sub_layer, 0, 0, 0, 0))
    if has_state:
        in_specs.append(state_spec)
        args.append(state)
    if emit_state:
        out_specs.append(state_spec if carried is not None else
                         pl.BlockSpec((None, N_EVEN, 2, NH_B, HP_B, DSTATE), lambda s: (s, 0, 0, 0, 0, 0)))
        out_shape.append(jax.ShapeDtypeStruct((nseq, N_EVEN, 2, NH_B, HP_B, DSTATE), F32))
        if carried is not None:
            aliases = {len(args): 1}
            in_specs.append(pl.BlockSpec(memory_space=pl.ANY))
            args.append(carried)
    return pl.pallas_call(
        functools.partial(_ssd_kernel, seq=seq, has_state=has_state, emit_state=emit_state, n_carried=len(aliases),
                          slot=sub_layer, fused=False),
        grid=(nseq,), in_specs=in_specs, out_specs=out_specs, out_shape=out_shape,
        input_output_aliases=aliases, compiler_params=_params("arbitrary"), name="ssd",
    )(*args)


def _shared_split(x, x_swapped, kh):
    first = _pair_lanes(x.shape)
    zero = jnp.zeros_like(x)
    top, bottom = (x, x_swapped) if kh == 0 else (x_swapped, x)
    return jnp.concatenate([jnp.where(first, top, zero), jnp.where(first, zero, bottom)], axis=0)


def _pair_probs(s, sinks=None, valid=None):
    n_keys = s.shape[1] // 2
    probs, maxes = [], []
    for i in range(2):
        si = s[:, i * n_keys:(i + 1) * n_keys]
        if valid is not None:
            si = jnp.where(valid, si, -jnp.inf)
        m = jnp.max(si, axis=1, keepdims=True)
        if sinks is not None:
            m = jnp.maximum(m, sinks[i])
        probs.append(jnp.exp(si - m))
        maxes.append(m)
    return jnp.concatenate(probs, axis=1).astype(BF16), maxes


def _pair_output(p, maxes, vbd, sinks=None):
    o = _dot(p, vbd)
    den = o[:, LANES:]
    if sinks is not None:
        den = den + jnp.where(_pair_lanes(den.shape), jnp.exp(sinks[0] - maxes[0]), jnp.exp(sinks[1] - maxes[1]))
    return o[:, :LANES] / den


def _run_pairs(items, valid=None):
    s_next = items[0][0]()
    for idx, (_, values, sinks, out_ref, cols) in enumerate(items):
        s_cur = s_next
        if idx + 1 < len(items):
            s_next = items[idx + 1][0]()
        p, maxes = _pair_probs(s_cur, sinks, valid if sinks is not None else None)
        out_ref[:, cols] = _pair_output(p, maxes, values(), sinks)


def _pair_sinks(sink_ref, n):
    return sink_ref[0:1, n:n + 1], sink_ref[0:1, n + 1:n + 2]


def _mla_queries(qa_ref, qan_ref, wqb_ref):
    return _dot(_rms(qa_ref[...], qan_ref[...]).astype(BF16), wqb_ref[...]) * MLA_SCALE


def _attn_ctx_kernel(qc_ref, kc_ref, vc_ref, qa_ref, kva_ref, kpe_ref, sink_ref, qan_ref, kvn_ref, wqb_ref, wkvb_ref,
                     *rest, seq, n_carried, slot):
    oc_ref, od_ref, ckv_ref = rest[n_carried:]
    if n_carried == 0:
        for other in range(N_ODD):
            if other != slot:
                ckv_ref[other] = jnp.zeros(ckv_ref.shape[1:], F32)
        ckv_ref = ckv_ref.at[slot]
    ones_bd = _pair_split(jnp.ones((seq, LANES), F32))
    kc, vc = kc_ref[...], vc_ref[...]
    kc_sw, vc_sw = pltpu.roll(kc, HD_C, 1), pltpu.roll(vc, HD_C, 1)
    qd = _mla_queries(qa_ref, qan_ref, wqb_ref)
    ckv = _rms(kva_ref[...], kvn_ref[...])
    ckv_ref[...] = ckv
    kv = _dot(ckv.astype(BF16), wkvb_ref[...])
    kpe = kpe_ref[...]
    kpe_bd = jnp.concatenate([kpe, pltpu.roll(kpe, ROPE_D, 1)], axis=0)
    nope_w = NH_D * NOPE_D
    items = []
    for kh in range(NKV_C):
        for n in range(kh * G_C, (kh + 1) * G_C, 2):
            cols = slice(n * HD_C, (n + 2) * HD_C)
            items.append((lambda cols=cols, kh=kh: _dot_nt((qc_ref[:, cols] * (HD_C ** -0.5)).astype(BF16),
                                                           _shared_split(kc, kc_sw, kh).astype(BF16)),
                          lambda kh=kh: jnp.concatenate([_shared_split(vc, vc_sw, kh), ones_bd], axis=1).astype(BF16),
                          _pair_sinks(sink_ref, n), oc_ref, cols))
    for i in range(NH_D // 2):
        cols = slice(i * LANES, (i + 1) * LANES)
        vcols = slice(nope_w + i * LANES, nope_w + (i + 1) * LANES)
        items.append((lambda cols=cols, vcols=vcols: _dot_nt(
                          jnp.concatenate([qd[:, cols], qd[:, vcols]], axis=1).astype(BF16),
                          jnp.concatenate([_pair_split(kv[:, cols]), kpe_bd], axis=1).astype(BF16)),
                      lambda vcols=vcols: jnp.concatenate([_pair_split(kv[:, vcols]), ones_bd], axis=1).astype(BF16),
                      None, od_ref, cols))
    _run_pairs(items)


def _attn_ctx(proj, params, seq, sub_layer, carried=None):
    n = proj[0].shape[0]
    nseq = n // seq
    row = lambda s: (s, 0)
    in_specs = [pl.BlockSpec((seq, wd), row) for wd in ODD_WIDTHS]
    in_specs += [_layer_spec(p, sub_layer) for p in params]
    args = [*proj, *params]
    half = NH_C * HD_C
    aliases = {}
    if carried is None:
        ckv_spec = pl.BlockSpec((None, N_ODD, seq, KV_RANK), lambda s: (s, 0, 0, 0))
    else:
        ckv_spec = pl.BlockSpec((None, None, seq, KV_RANK), lambda s: (s, sub_layer, 0, 0))
        aliases = {len(args): 2}
        in_specs.append(pl.BlockSpec(memory_space=pl.ANY))
        args.append(carried)
    return pl.pallas_call(
        functools.partial(_attn_ctx_kernel, seq=seq, n_carried=len(aliases), slot=sub_layer),
        grid=(nseq,), in_specs=in_specs,
        out_specs=[pl.BlockSpec((seq, half), row), pl.BlockSpec((seq, half), row), ckv_spec],
        out_shape=[jax.ShapeDtypeStruct((n, half), F32), jax.ShapeDtypeStruct((n, half), F32),
                   jax.ShapeDtypeStruct((nseq, N_ODD, seq, KV_RANK), F32)],
        input_output_aliases=aliases, compiler_params=_params("arbitrary"), name="attn_ctx",
    )(*args)


def _rope(x, cos, sin, half):
    parts = []
    lane = _lane_iota((x.shape[0], LANES))
    first = (lane & (2 * half - 1)) < half
    for i in range(x.shape[1] // LANES):
        xi = x[:, i * LANES:(i + 1) * LANES]
        partner = jnp.where(first, -pltpu.roll(xi, LANES - half, 1), pltpu.roll(xi, half, 1))
        parts.append(xi * cos + partner * sin)
    return parts[0] if len(parts) == 1 else jnp.concatenate(parts, axis=1)


def _attn_lat_kernel(qc_ref, qa_ref, ropeq_ref, kc_ref, vc_ref, kva_ref, kpe_ref, kctx_ref, vctx_ref, ckvctx_ref,
                     kpectx_ref, rope_ref, sink_ref, qan_ref, kvn_ref, wqb_ref, wkvb_ref, oc_ref, od_ref,
                     kwin_s, vwin_s, kext_s, vext_s, *, seq, past):
    qi = pl.program_id(1)
    nope_w = NH_D * NOPE_D
    n_all = past + seq
    ctx0 = 2 * WINDOW + seq

    @pl.when(qi == 0)
    def _():
        zeros = jnp.zeros((WINDOW, LANES), BF16)
        for ref, lat, ctx in ((kwin_s, _rope(kc_ref[...], rope_ref[0], rope_ref[1], HD_C // 2), kctx_ref[...]),
                              (vwin_s, vc_ref[...], vctx_ref[...])):
            lat_sw, ctx_sw = pltpu.roll(lat, HD_C, 1), pltpu.roll(ctx, HD_C, 1)
            for kh in range(NKV_C):
                lat_bd = _shared_split(lat, lat_sw, kh).astype(BF16)
                ctx_bd = _shared_split(ctx, ctx_sw, kh).astype(BF16)
                for i in range(2):
                    ref[kh, i, 0:WINDOW, :] = zeros
                    ref[kh, i, WINDOW:WINDOW + seq, :] = lat_bd[i * seq:(i + 1) * seq]
                    ref[kh, i, WINDOW + seq:ctx0, :] = zeros
                    ref[kh, i, ctx0:, :] = ctx_bd[i * past:(i + 1) * past]
        ckv = _rms(kva_ref[...], kvn_ref[...])
        kv = jnp.concatenate([_dot(ckvctx_ref[...].astype(BF16), wkvb_ref[...]),
                              _dot(ckv.astype(BF16), wkvb_ref[...])], axis=0)
        kpe = jnp.concatenate([kpectx_ref[...], _rope(kpe_ref[...], rope_ref[2], rope_ref[3], ROPE_D // 2)], axis=0)
        kpe_bd = jnp.concatenate([kpe, pltpu.roll(kpe, ROPE_D, 1)], axis=0).astype(BF16)
        ones_bd = _pair_split(jnp.ones((n_all, LANES), F32)).astype(BF16)
        for i in range(NH_D // 2):
            kext_s[i, :, 0:LANES] = _pair_split(kv[:, i * LANES:(i + 1) * LANES]).astype(BF16)
            kext_s[i, :, LANES:] = kpe_bd
            vext_s[i, :, 0:LANES] = _pair_split(kv[:, nope_w + i * LANES:nope_w + (i + 1) * LANES]).astype(BF16)
            vext_s[i, :, LANES:] = ones_bd

    r0 = pl.multiple_of(qi * Q_TILE, Q_TILE)
    nloc = Q_TILE + 2 * WINDOW
    n_keys = nloc + past
    qr = _rope(qc_ref[...], ropeq_ref[0], ropeq_ref[1], HD_C // 2) * (HD_C ** -0.5)
    ti = r0 + _row_iota((Q_TILE, n_keys))
    col = _lane_iota((Q_TILE, n_keys))
    pos = r0 - WINDOW + col
    valid = (col >= nloc) | ((jnp.abs(ti - pos) <= WINDOW) & (pos >= 0) & (pos < seq))
    ones_bd = _pair_split(jnp.ones((n_keys, LANES), F32)).astype(BF16)
    qd = _mla_queries(qa_ref, qan_ref, wqb_ref)
    q_pe = _rope(qd[:, nope_w:], ropeq_ref[2], ropeq_ref[3], ROPE_D // 2)

    def banded(ref, kh):
        return jnp.concatenate([ref[kh, 0, pl.ds(r0, nloc), :], ref[kh, 0, ctx0:, :],
                                ref[kh, 1, pl.ds(r0, nloc), :], ref[kh, 1, ctx0:, :]], axis=0)

    items = []
    for kh in range(NKV_C):
        for n in range(kh * G_C, (kh + 1) * G_C, 2):
            cols = slice(n * HD_C, (n + 2) * HD_C)
            items.append((lambda cols=cols, kh=kh: _dot_nt(qr[:, cols].astype(BF16), banded(kwin_s, kh)),
                          lambda kh=kh: jnp.concatenate([banded(vwin_s, kh), ones_bd], axis=1),
                          _pair_sinks(sink_ref, n), oc_ref, cols))
    for i in range(NH_D // 2):
        cols = slice(i * LANES, (i + 1) * LANES)
        items.append((lambda cols=cols, i=i: _dot_nt(jnp.concatenate([qd[:, cols], q_pe[:, cols]], axis=1).astype(BF16),
                                                     kext_s[i]),
                      lambda i=i: vext_s[i], None, od_ref, cols))
    _run_pairs(items, valid)


def _attn_lat(proj, caches, rope, params, seq, sub_layer):
    qc, kc, vc, qa, kva, kpe = proj
    n = qc.shape[0]
    past = caches[0].shape[2]
    nq = seq // Q_TILE
    qrow = lambda b, q: (b * nq + q, 0)
    krow = lambda b, q: (b, 0)
    kvw = NKV_C * HD_C
    in_specs = [pl.BlockSpec((Q_TILE, NH_C * HD_C), qrow), pl.BlockSpec((Q_TILE, Q_RANK), qrow),
                pl.BlockSpec((4, Q_TILE, LANES), lambda b, q: (0, q, 0)),
                pl.BlockSpec((seq, kvw), krow), pl.BlockSpec((seq, kvw), krow),
                pl.BlockSpec((seq, KV_RANK), krow), pl.BlockSpec((seq, LANES), krow)]
    in_specs += [pl.BlockSpec((None, None, past, LANES), lambda b, q: (b, sub_layer, 0, 0)) for _ in caches]
    in_specs += [pl.BlockSpec(rope.shape, lambda b, q: (0, 0, 0))]
    in_specs += [_layer_spec(p, sub_layer) for p in params]
    half = NH_C * HD_C
    win_rows = 2 * WINDOW + seq + past
    return pl.pallas_call(
        functools.partial(_attn_lat_kernel, seq=seq, past=past),
        grid=(n // seq, nq), in_specs=in_specs,
        out_specs=[pl.BlockSpec((Q_TILE, half), qrow), pl.BlockSpec((Q_TILE, half), qrow)],
        out_shape=[jax.ShapeDtypeStruct((n, half), F32), jax.ShapeDtypeStruct((n, half), F32)],
        scratch_shapes=[pltpu.VMEM((NKV_C, 2, win_rows, LANES), BF16), pltpu.VMEM((NKV_C, 2, win_rows, LANES), BF16),
                        pltpu.VMEM((NH_D // 2, 2 * (past + seq), 2 * LANES), BF16),
                        pltpu.VMEM((NH_D // 2, 2 * (past + seq), 2 * LANES), BF16)],
        compiler_params=_params("arbitrary", "arbitrary"), name="attn_lat",
    )(qc, qa, rope, kc, vc, kva, kpe, *caches, rope, *params)


def _pad_lanes(x, width=LANES):
    return jnp.pad(x, [(0, 0)] * (x.ndim - 1) + [(0, width - x.shape[-1])])


def _on_lanes(x):
    return jnp.broadcast_to(x[..., None], x.shape + (LANES,))


def _mla_query_weights(w):
    lead = w.shape[:-1]
    w4 = w.reshape(lead + (NH_D // 2, 2, NOPE_D + ROPE_D))
    nope = w4[..., :NOPE_D].reshape(lead + (NH_D * NOPE_D,))
    pe = _pad_lanes(w4[..., NOPE_D:].reshape(lead + (NH_D // 2, 2 * ROPE_D)))
    return jnp.concatenate([nope, pe.reshape(lead + (NH_D // 2 * LANES,))], axis=-1).astype(BF16)


def _mla_kv_weights(w):
    lead = w.shape[:-1]
    w3 = w.reshape(lead + (NH_D, NOPE_D + V_D))
    return jnp.concatenate([w3[..., :NOPE_D].reshape(lead + (NH_D * NOPE_D,)),
                            w3[..., NOPE_D:].reshape(lead + (NH_D * V_D,))], axis=-1).astype(BF16)


def _rope_tables(rows):
    def table(rot_dim):
        quarter = rot_dim // 4
        inv = ROPE_BASE ** (-jnp.arange(quarter, dtype=F32) / quarter)
        r = jnp.repeat(jnp.arange(rows, dtype=F32), GRID_W)
        col = jnp.tile(jnp.arange(GRID_W, dtype=F32), rows)
        ang = jnp.concatenate([r[:, None] * inv, col[:, None] * inv], axis=-1)
        reps = LANES // (rot_dim // 2)
        return jnp.tile(jnp.cos(ang), (1, reps)), jnp.tile(jnp.sin(ang), (1, reps))
    cos_c, sin_c = table(HD_C)
    cos_d, sin_d = table(ROPE_D)
    return jnp.stack([cos_c, sin_c, cos_d, sin_d])


def kernel(x_prompt, x_sample, c, state_mlstm_C, state_mlstm_n, state_mlstm_m, state_ssd, cache_gqa_k, cache_gqa_v,
           cache_mla_ckv, cache_mla_kpe, c_ctx, w_ada, b_ada, norm_g, w_up, w_down, w_in_even, conv_a_w, conv_a_b,
           conv_b_w, conv_b_b, gate_b, a_norm_w, dt_bias, a_log, d_skip, b_norm_w, w_out_even, w_in_odd, sink,
           q_a_norm, kv_a_norm, w_q_b, w_kv_b, w_out_odd):
    xp = x_prompt.reshape(BATCH * SEQ, D_MODEL)
    xs = x_sample.reshape(DEC_BATCH * DEC_SEQ, D_MODEL)
    cond = jnp.concatenate([c_ctx[None, :], c, jnp.zeros((MOD_ROWS - 1 - DEC_BATCH, D_MODEL), F32)], axis=0)
    mods = _modulations(cond, w_ada, b_ada)
    rope = _rope_tables(DEC_SEQ // GRID_W)

    w_even = w_in_even.astype(BF16)
    a_params = (conv_a_w, conv_a_b[:, None, :], _on_lanes(gate_b), a_norm_w[:, None, :])
    b_params = (conv_b_w, conv_b_b[:, None, :], _on_lanes(dt_bias.reshape(N_EVEN, 2 * NH_B)),
                _on_lanes(a_log.reshape(N_EVEN, 2 * NH_B)), jnp.repeat(d_skip, HP_B, axis=1)[:, None, :],
                b_norm_w[:, None, :])
    n0 = state_mlstm_n[..., None]
    mem_in = (jnp.concatenate([state_mlstm_C, jnp.broadcast_to(n0, n0.shape[:-1] + (LANES,))], axis=-1),
              _on_lanes(state_mlstm_m.reshape(DEC_BATCH, N_EVEN, 2 * NH_A)))
    w_odd = w_in_odd.astype(BF16)
    o_params = (_pad_lanes(sink)[:, None, :], q_a_norm[:, None, :], kv_a_norm[:, None, :],
                _mla_query_weights(w_q_b), _mla_kv_weights(w_kv_b))
    caches = (cache_gqa_k.reshape(DEC_BATCH, N_ODD, PAST_LEN, NKV_C * HD_C),
              cache_gqa_v.reshape(DEC_BATCH, N_ODD, PAST_LEN, NKV_C * HD_C),
              cache_mla_ckv, _pad_lanes(cache_mla_kpe))

    new_k, new_v, new_kpe = [], [], []
    mem_state, ssd_state, ckv_state = None, None, None
    for l in range(DEPTH):
        j = l // 2
        if l % 2 == 0:
            a1p, *mem_state = _fused_mixer(_mlstm_kernel, "mlstm", xp, mods, l, norm_g, w_even, MLSTM_W_COLS, a_params,
                                           SEQ, MLSTM_PIECES, MLSTM_GATES, A_V, MLSTM_STATE_SHAPES, mem_state)
            a2p, *ssd_state = _fused_mixer(_ssd_kernel, "ssd", xp, mods, l, norm_g, w_even, w_even.shape[2], b_params,
                                           SEQ, SSD_PIECES, SSD_GATES, B_INNER, SSD_STATE_SHAPES, ssd_state)
            qk, v, o, z, xbc, g = _project(xs, mods, l, True, norm_g, w_even, EVEN_WIDTHS, EVEN_REGROUP, EVEN_GATE_COLS)
            a1s, = _mlstm(qk, v, o, g, a_params, DEC_SEQ, j, state=mem_in)
            a2s, = _ssd(xbc, z, g, b_params, DEC_SEQ, j, state=state_ssd)
            w_out = w_out_even
        else:
            proj = _project(xp, mods, l, False, norm_g, w_odd, ODD_WIDTHS, ODD_REGROUP)
            a1p, a2p, ckv_state = _attn_ctx(proj, o_params, SEQ, j, carried=ckv_state)
            new_k.append(proj[1].reshape(BATCH, SEQ, NKV_C, HD_C))
            new_v.append(proj[2].reshape(BATCH, SEQ, NKV_C, HD_C))
            new_kpe.append(proj[5][:, :ROPE_D].reshape(BATCH, SEQ, ROPE_D))
            proj = _project(xs, mods, l, True, norm_g, w_odd, ODD_WIDTHS, ODD_REGROUP)
            a1s, a2s = _attn_lat(proj, caches, rope, o_params, DEC_SEQ, j)
            w_out = w_out_odd
        xp, xs = _channel((a1p, a2p, xp), (a1s, a2s, xs), mods, l, norm_g, w_out, w_up, w_down)

    new_c, new_n, new_m = mem_state
    return (xp.reshape(BATCH, SEQ, D_MODEL), xs.reshape(DEC_BATCH, DEC_SEQ, D_MODEL),
            new_c, new_n, new_m[..., 0].reshape(BATCH, N_EVEN, 2, NH_A), ssd_state[0],
            jnp.stack(new_k, axis=1), jnp.stack(new_v, axis=1), ckv_state, jnp.stack(new_kpe, axis=1))
```

```python
import functools

import jax
import jax.numpy as jnp
from jax import lax
from jax.experimental import pallas as pl
from jax.experimental.pallas import tpu as pltpu

F32 = jnp.float32
BF16 = jnp.bfloat16

D_MODEL = 1024
BATCH = 32
SEQ = 256
DEPTH = 4
DEC_BATCH = 2
DEC_SEQ = 1024
PAST_LEN = 256
GRID_W = 64
N_EVEN = (DEPTH + 1) // 2
N_ODD = DEPTH // 2
EPS = 1e-6
CONV_K = 5
NH_A = 4
DK_A = 128
DV_A = 128
A_QK = NH_A * DK_A
A_V = NH_A * DV_A
NH_B = 8
HP_B = 64
DSTATE = 128
NG_B = 2
R_B = NH_B // NG_B
B_INNER = NH_B * HP_B
B_BC = NG_B * DSTATE
B_XBC = B_INNER + 2 * B_BC
NH_C = 8
NKV_C = 2
G_C = NH_C // NKV_C
HD_C = 64
WINDOW = 128
NH_D = 8
Q_RANK = 256
KV_RANK = 128
NOPE_D = 64
ROPE_D = 32
V_D = 64
MLA_SCALE = (NOPE_D + ROPE_D) ** -0.5
D_FF = 4 * D_MODEL
ROPE_BASE = 10000.0

LANES = 128
VMEM_LIMIT_BYTES = 56 * 1024 * 1024
ROW_TILE = 512
FF_TILE = 1024
SUB_ROWS = 512
STAGE_ROWS = 512
Q_TILE = 256
ADA_TILE = 1536
MOD_ROWS = 8
GATE_ROWS = 4 * NH_A + 2 * NH_B

EVEN_WIDTHS = (2 * A_QK, A_V, A_V, B_INNER, B_XBC)
ODD_WIDTHS = (NH_C * HD_C, NKV_C * HD_C, NKV_C * HD_C, Q_RANK, KV_RANK, LANES)
_GATES_LO = 2 * A_QK + 2 * A_V
_Z_LO = _GATES_LO + 4 * NH_A
_DT_LO = _Z_LO + B_INNER + B_XBC
EVEN_REGROUP = ((0, 0, _GATES_LO), (_GATES_LO, _Z_LO, B_INNER + B_XBC))
EVEN_GATE_COLS = ((_GATES_LO, _Z_LO), (_DT_LO, _DT_LO + 2 * NH_B))
ODD_IN = sum(ODD_WIDTHS) - LANES + ROPE_D
ODD_REGROUP = ((0, 0, ODD_IN), (ODD_IN, None, LANES - ROPE_D))

_NT = (((1,), (1,)), ((), ()))


def _params(*sem):
    return pltpu.CompilerParams(dimension_semantics=sem, vmem_limit_bytes=VMEM_LIMIT_BYTES)


def _rms(x, g):
    return x * lax.rsqrt(jnp.mean(x * x, axis=-1, keepdims=True) + EPS) * g


def _silu(x):
    return x * jax.nn.sigmoid(x)


def _softplus(x):
    return jnp.maximum(x, 0.0) + jnp.log1p(jnp.exp(-jnp.abs(x)))


def _dot(a, b):
    return jnp.dot(a, b, preferred_element_type=F32)


def _dot_nt(a, b):
    return lax.dot_general(a, b, _NT, preferred_element_type=F32)


def _layer_spec(arr, layer):
    tail = arr.shape[1:]
    zeros = (0,) * len(tail)
    return pl.BlockSpec((None,) + tail, lambda *_: (layer,) + zeros)


def _ada_kernel(c_ref, w_ref, b_ref, o_ref):
    s = _silu(c_ref[...]).astype(BF16)
    o_ref[...] = _dot(s, w_ref[...].astype(BF16)) + b_ref[...]


def _modulations(cond, w_ada, b_ada):
    out = pl.pallas_call(
        _ada_kernel,
        grid=(DEPTH, 6 * D_MODEL // ADA_TILE),
        in_specs=[pl.BlockSpec((MOD_ROWS, D_MODEL), lambda l, n: (0, 0)),
                  pl.BlockSpec((None, D_MODEL, ADA_TILE), lambda l, n: (l, 0, n)),
                  pl.BlockSpec((None, 1, ADA_TILE), lambda l, n: (l, 0, n))],
        out_specs=pl.BlockSpec((None, MOD_ROWS, ADA_TILE), lambda l, n: (l, 0, n)),
        out_shape=jax.ShapeDtypeStruct((DEPTH, MOD_ROWS, 6 * D_MODEL), F32),
        compiler_params=_params("arbitrary", "arbitrary"),
        name="ada",
    )(cond, w_ada, b_ada.reshape(DEPTH, 1, 6 * D_MODEL))
    return out.reshape(DEPTH, MOD_ROWS, 6, D_MODEL)


def _mod_spec(layer, latent):
    if latent:
        per_seq = DEC_SEQ // ROW_TILE
        return pl.BlockSpec((None, None, 6, D_MODEL), lambda i, *_: (layer, 1 + i // per_seq, 0, 0))
    return pl.BlockSpec((None, None, 6, D_MODEL), lambda i, *_: (layer, 0, 0, 0))


def _proj_kernel(x_ref, mod_ref, g_ref, w_ref, *rest, widths, regroup, gate_cols):
    n_out = len(widths) + (1 if gate_cols else 0)
    o_refs, w_s = rest[:n_out], rest[n_out]

    @pl.when(pl.program_id(0) == 0)
    def _():
        for dst, src, width in regroup:
            if src is None:
                w_s[:, dst:dst + width] = jnp.zeros((D_MODEL, width), BF16)
            else:
                w_s[:, dst:dst + width] = w_ref[:, src:src + width]
        if gate_cols:
            pieces = [w_ref[:, lo:hi].astype(F32) for lo, hi in gate_cols]
            n_gates = sum(hi - lo for lo, hi in gate_cols)
            gates = jnp.concatenate(pieces + [jnp.zeros((D_MODEL, LANES - n_gates), F32)], axis=1)
            rest[n_out + 1][...] = gates.T[0:n_gates, :].astype(BF16)

    h = _rms(x_ref[...], g_ref[0:1, :]) * (1.0 + mod_ref[1:2, :]) + mod_ref[0:1, :]
    hb = h.astype(BF16)
    off = 0
    for o_ref, wd in zip(o_refs, widths):
        o_ref[...] = _dot(hb, w_s[:, off:off + wd])
        off += wd
    if gate_cols:
        o_refs[-1][...] = _dot_nt(rest[n_out + 1][...], hb)


def _project(x, mods, layer, latent, gains, w_all, widths, regroup, gate_cols=()):
    n = x.shape[0]
    out_specs = [pl.BlockSpec((ROW_TILE, wd), lambda i: (i, 0)) for wd in widths]
    out_shape = [jax.ShapeDtypeStruct((n, wd), F32) for wd in widths]
    scratch = [pltpu.VMEM((D_MODEL, sum(widths)), BF16)]
    if gate_cols:
        n_gates = sum(hi - lo for lo, hi in gate_cols)
        out_specs.append(pl.BlockSpec((n_gates, ROW_TILE), lambda i: (0, i)))
        out_shape.append(jax.ShapeDtypeStruct((n_gates, n), F32))
        scratch.append(pltpu.VMEM((n_gates, D_MODEL), BF16))
    w_spec = pl.BlockSpec((None,) + w_all.shape[1:], lambda i: (layer // 2, 0, 0), pipeline_mode=pl.Buffered(1))
    return pl.pallas_call(
        functools.partial(_proj_kernel, widths=widths, regroup=regroup, gate_cols=gate_cols),
        grid=(n // ROW_TILE,),
        in_specs=[pl.BlockSpec((ROW_TILE, D_MODEL), lambda i: (i, 0)), _mod_spec(layer, latent),
                  _layer_spec(gains, layer), w_spec],
        out_specs=out_specs, out_shape=out_shape, scratch_shapes=scratch,
        compiler_params=_params("arbitrary"),
        name="proj",
    )(x, mods, gains, w_all)


def _weight_chunks(layer, sub_layer, wo_hbm, wu_hbm, wd_hbm, wo_s, wu_s, wd_s):
    chunks = []
    for r in range(0, D_MODEL, STAGE_ROWS):
        chunks.append((wo_hbm.at[sub_layer, pl.ds(r, STAGE_ROWS), :], wo_s.at[pl.ds(r, STAGE_ROWS), :]))
    for r in range(0, D_MODEL, STAGE_ROWS):
        for c in range(0, D_FF, D_MODEL):
            chunks.append((wu_hbm.at[layer, pl.ds(r, STAGE_ROWS), pl.ds(c, D_MODEL)],
                           wu_s.at[pl.ds(r, STAGE_ROWS), pl.ds(c, D_MODEL)]))
    for r in range(0, D_FF, STAGE_ROWS):
        chunks.append((wd_hbm.at[layer, pl.ds(r, STAGE_ROWS), :], wd_s.at[pl.ds(r, STAGE_ROWS), :]))
    return chunks


def _channel_kernel(a1p_ref, a2p_ref, xp_ref, a1s_ref, a2s_ref, xs_ref, mod_ref, g_ref, wo_hbm, wu_hbm, wd_hbm,
                    op_ref, os_ref, wo_s, wu_s, wd_s, stage, sem, *, layer, sub_layer, prompt_steps):
    step = pl.program_id(0)

    @pl.when(step == 0)
    def _():
        chunks = _weight_chunks(layer, sub_layer, wo_hbm, wu_hbm, wd_hbm, wo_s, wu_s, wd_s)
        copies = [pltpu.make_async_copy(src, stage.at[k % 2], sem.at[k % 2]) for k, (src, _) in enumerate(chunks)]
        copies[0].start()
        for k, (_, dst) in enumerate(chunks):
            if k + 1 < len(chunks):
                copies[k + 1].start()
            copies[k].wait()
            dst[...] = stage[k % 2].astype(BF16)

    def rows_block(a1_ref, a2_ref, x_ref, o_ref):
        half = a1_ref.shape[1]
        for r0 in range(0, ROW_TILE, SUB_ROWS):
            rows = slice(r0, r0 + SUB_ROWS)
            y = (_dot(a1_ref[rows, :].astype(BF16), wo_s[0:half, :])
                 + _dot(a2_ref[rows, :].astype(BF16), wo_s[half:, :]))
            x1 = x_ref[rows, :] + mod_ref[2:3, :] * _rms(y, g_ref[1:2, :])
            h = (_rms(x1, g_ref[2:3, :]) * (1.0 + mod_ref[4:5, :]) + mod_ref[3:4, :]).astype(BF16)
            acc = None
            for c in range(0, D_FF, FF_TILE):
                u = jnp.square(jnp.maximum(_dot(h, wu_s[:, c:c + FF_TILE]), 0.0)).astype(BF16)
                part = _dot(u, wd_s[c:c + FF_TILE, :])
                acc = part if acc is None else acc + part
            o_ref[rows, :] = x1 + mod_ref[5:6, :] * _rms(acc, g_ref[3:4, :])

    @pl.when(step < prompt_steps)
    def _():
        rows_block(a1p_ref, a2p_ref, xp_ref, op_ref)

    @pl.when(step >= prompt_steps)
    def _():
        rows_block(a1s_ref, a2s_ref, xs_ref, os_ref)


def _channel(prompt, latent, mods, layer, gains, w_out, w_up, w_down):
    n_p, n_s = prompt[2].shape[0], latent[2].shape[0]
    steps_p, steps_s = n_p // ROW_TILE, n_s // ROW_TILE
    per_seq = DEC_SEQ // ROW_TILE
    row_p = lambda i: (jnp.minimum(i, steps_p - 1), 0)
    row_s = lambda i: (jnp.maximum(i - steps_p, 0), 0)
    mod_spec = pl.BlockSpec((None, None, 6, D_MODEL),
                            lambda i: (layer, jnp.where(i < steps_p, 0, 1 + (i - steps_p) // per_seq), 0, 0))
    hbm = pl.BlockSpec(memory_space=pl.ANY)
    specs = lambda arrs, row: [pl.BlockSpec((ROW_TILE, a.shape[1]), row) for a in arrs]
    return pl.pallas_call(
        functools.partial(_channel_kernel, layer=layer, sub_layer=layer // 2, prompt_steps=steps_p),
        grid=(steps_p + steps_s,),
        in_specs=specs(prompt, row_p) + specs(latent, row_s) + [mod_spec, _layer_spec(gains, layer), hbm, hbm, hbm],
        out_specs=[pl.BlockSpec((ROW_TILE, D_MODEL), row_p), pl.BlockSpec((ROW_TILE, D_MODEL), row_s)],
        out_shape=[jax.ShapeDtypeStruct((n_p, D_MODEL), F32), jax.ShapeDtypeStruct((n_s, D_MODEL), F32)],
        scratch_shapes=[pltpu.VMEM((D_MODEL, D_MODEL), BF16), pltpu.VMEM((D_MODEL, D_FF), BF16),
                        pltpu.VMEM((D_FF, D_MODEL), BF16), pltpu.VMEM((2, STAGE_ROWS, D_MODEL), F32),
                        pltpu.SemaphoreType.DMA((2,))],
        compiler_params=_params("arbitrary"),
        name="channel",
    )(*prompt, *latent, mods, gains, w_out, w_up, w_down)


def _row_iota(shape):
    return lax.broadcasted_iota(jnp.int32, shape, 0)


def _lane_iota(shape):
    return lax.broadcasted_iota(jnp.int32, shape, 1)


def _pair_lanes(shape):
    return _lane_iota(shape) < LANES // 2


def _cumsum_lanes(x, n_fwd):
    t = x.shape[1]
    si, ti = _row_iota((t, t)), _lane_iota((t, t))
    upper = jnp.where(si <= ti, 1.0, 0.0).astype(BF16)
    lower = jnp.where(si >= ti, 1.0, 0.0).astype(BF16)
    hi = x.astype(BF16)
    rest = x - hi.astype(F32)
    mid = rest.astype(BF16)
    lo = (rest - mid.astype(F32)).astype(BF16)
    pre = _dot(hi, upper) + _dot(mid, upper) + _dot(lo, upper)
    suf = _dot(hi, lower) + _dot(mid, lower) + _dot(lo, lower)
    return jnp.where(_row_iota(x.shape) < n_fwd, pre, suf)


def _cummax_lanes(x, n_fwd):
    t = x.shape[1]
    lane = _lane_iota(x.shape)
    pre, suf = x, x
    k = 1
    while k < t:
        pre = jnp.maximum(pre, jnp.where(lane >= k, pltpu.roll(pre, k, 1), -jnp.inf))
        suf = jnp.maximum(suf, jnp.where(lane < t - k, pltpu.roll(suf, t - k, 1), -jnp.inf))
        k *= 2
    return jnp.where(_row_iota(x.shape) < n_fwd, pre, suf)


def _columns(row_arrays):
    t = row_arrays[0].shape[1]
    used = sum(a.shape[0] for a in row_arrays)
    return jnp.concatenate(list(row_arrays) + [jnp.zeros((LANES - used, t), F32)], axis=0).T


def _dwconv_silu(x, w, b):
    t = x.shape[0]
    row = _row_iota(x.shape)
    acc = x * w[CONV_K // 2:CONV_K // 2 + 1, :] + b
    for j in range(CONV_K):
        d = j - CONV_K // 2
        if d == 0:
            continue
        shifted = pltpu.roll(x, (-d) % t, 0)
        valid = (row >= -d) if d < 0 else (row < t - d)
        acc = acc + jnp.where(valid, shifted, 0.0) * w[j:j + 1, :]
    return _silu(acc)


def _causal_exponent(expo, r0, k0, reverse):
    ti = r0 + _row_iota(expo.shape)
    si = k0 + _lane_iota(expo.shape)
    keep = (si >= ti) if reverse else (si <= ti)
    return jnp.where(keep, expo, -jnp.inf)


def _pair_split(x):
    first = _pair_lanes(x.shape)
    zero = jnp.zeros_like(x)
    return jnp.concatenate([jnp.where(first, x, zero), jnp.where(first, zero, x)], axis=0)


def _key_range(d, r0, seq):
    return (0, r0 + Q_TILE) if d == 0 else (r0, seq)


N_FUSED_INPUTS = 5
PROJ_CHUNK = 512

MLSTM_PIECES = ((0, 2 * A_QK), (2 * A_QK, A_V), (2 * A_QK + A_V, A_V))
MLSTM_GATES = (_GATES_LO, _Z_LO)
MLSTM_W_COLS = _GATES_LO + LANES
SSD_PIECES = ((_Z_LO + B_INNER, B_XBC), (_Z_LO, B_INNER))
SSD_GATES = (_DT_LO, _DT_LO + 2 * NH_B)
CTX_PIECES = tuple((sum(ODD_WIDTHS[:i]), wd) for i, wd in enumerate(ODD_WIDTHS[:-1])) + (
    (sum(ODD_WIDTHS[:-1]), LANES, ROPE_D),)


def _fused_project(step, fused_refs, scratch, pieces, gate_cols, mix):
    x_cur_ref, x_next_ref, mod_ref, gain_ref, w_ref = fused_refs
    w_s = scratch[0]
    n_head = 2 if gate_cols else 1
    n = len(pieces) + (1 if gate_cols else 0)
    sets = (scratch[n_head:n_head + n], scratch[n_head + n:n_head + 2 * n])

    def normed(x_ref):
        h = _rms(x_ref[...], gain_ref[0:1, :]) * (1.0 + mod_ref[1:2, :]) + mod_ref[0:1, :]
        return h.astype(BF16)

    def chunk_thunks(hb, dst):
        def gates():
            dst[0][...] = _dot_nt(scratch[1][...], hb)

        def columns(ref, c, src, width):
            def run():
                ref[:, c:c + width] = _dot(hb, w_s[:, src:src + width])
            return run

        thunks = [gates] if gate_cols else []
        off = 0
        for ref, piece in zip(dst[-len(pieces):], pieces):
            width = piece[1]
            thunks += [columns(ref, c, off + c, min(PROJ_CHUNK, width - c)) for c in range(0, width, PROJ_CHUNK)]
            off += width
        return thunks

    @pl.when(step == 0)
    def _():
        off = 0
        for piece in pieces:
            src, width = piece[0], piece[1]
            valid = piece[2] if len(piece) > 2 else width
            w_s[:, off:off + valid] = w_ref[:, src:src + valid]
            if valid < width:
                w_s[:, off + valid:off + width] = jnp.zeros((D_MODEL, width - valid), BF16)
            off += width
        if gate_cols:
            lo, hi = gate_cols
            gates = jnp.concatenate([w_ref[:, lo:hi].astype(F32), jnp.zeros((D_MODEL, LANES - (hi - lo)), F32)], axis=1)
            scratch[1][...] = gates.T[0:hi - lo, :].astype(BF16)
        for thunk in chunk_thunks(normed(x_cur_ref), sets[0]):
            thunk()

    for parity in range(2):
        @pl.when(step % 2 == parity)
        def _(parity=parity):
            cur = sets[parity]
            refs = list(cur[1:]) + [cur[0]] if gate_cols else list(cur)
            mix(*refs, chunk_thunks(normed(x_next_ref), sets[1 - parity]))


def _fused_specs(seq, nseq, layer, w_all, w_cols):
    nxt = lambda s: (jnp.minimum(s + 1, nseq - 1), 0)
    return [pl.BlockSpec((seq, D_MODEL), lambda s: (s, 0)), pl.BlockSpec((seq, D_MODEL), nxt),
            pl.BlockSpec((None, None, 6, D_MODEL), lambda s: (layer, 0, 0, 0)),
            pl.BlockSpec((None, 4, D_MODEL), lambda s: (layer, 0, 0)),
            pl.BlockSpec((None, D_MODEL, w_cols), lambda s: (layer // 2, 0, 0), pipeline_mode=pl.Buffered(1))]


def _fused_scratch(seq, pieces, n_gates):
    head = [pltpu.VMEM((D_MODEL, sum(p[1] for p in pieces)), BF16)]
    one_set = [pltpu.VMEM((seq, p[1]), F32) for p in pieces]
    if n_gates:
        head.append(pltpu.VMEM((n_gates, D_MODEL), BF16))
        one_set = [pltpu.VMEM((n_gates, seq), F32)] + one_set
    return head + one_set + one_set


def _run_share(side_work, stages_left):
    for _ in range(-(-len(side_work) // stages_left)):
        side_work.pop(0)()


def _mlstm_kernel(*refs, seq, has_state, emit_state, n_carried, slot, fused):
    assert not (has_state and emit_state)
    it = iter(refs)
    if fused:
        fused_refs = [next(it) for _ in range(N_FUSED_INPUTS)]
    else:
        qk_ref, v_ref, o_ref, g_ref = (next(it) for _ in range(4))
    cw_ref, cb_ref, gb_ref, anw_ref = (next(it) for _ in range(4))
    if has_state:
        c0_ref, m0_ref = next(it), next(it)
    for _ in range(n_carried):
        next(it)
    ha_ref = next(it)
    if emit_state:
        cn_ref, nn_ref, mn_ref = next(it), next(it), next(it)
        if n_carried == 0:
            for other in range(N_EVEN):
                if other != slot:
                    cn_ref[other] = jnp.zeros(cn_ref.shape[1:], F32)
                    nn_ref[other] = jnp.zeros(nn_ref.shape[1:], F32)
                    mn_ref[other] = jnp.zeros(mn_ref.shape[1:], F32)
            cn_ref, nn_ref, mn_ref = cn_ref.at[slot], nn_ref.at[slot], mn_ref.at[slot]

    def mix(qk_ref, v_ref, o_ref, g_ref, side_work=()):
        side_work = list(side_work)
        n_ch = 2 * NH_A
        log_i = g_ref[0:n_ch, :] + gb_ref[0:n_ch, 0:1]
        f_pre = g_ref[n_ch:2 * n_ch, :] + gb_ref[n_ch:2 * n_ch, 0:1]
        log_f = jnp.minimum(f_pre, 0.0) - jnp.log1p(jnp.exp(-jnp.abs(f_pre)))
        b = _cumsum_lanes(log_f, NH_A)
        a = log_i - b
        m_run = _cummax_lanes(a, NH_A)
        if has_state:
            m0 = m0_ref[:, 0:1]
            m_run = jnp.maximum(m_run, m0)
        else:
            m_run = jnp.maximum(m_run, 0.0)
        by_time = [m_run, jnp.exp(-(b + m_run))]
        if has_state:
            by_time.append(jnp.exp(m0 - m_run))
        if emit_state:
            fwd = _row_iota((n_ch, 1)) < NH_A
            b_last = jnp.where(fwd, b[:, seq - 1:seq], b[:, 0:1])
            m_last = jnp.where(fwd, m_run[:, seq - 1:seq], m_run[:, 0:1])
            mn_ref[...] = jnp.broadcast_to(b_last + m_last, (n_ch, LANES))
            by_time.append(jnp.exp(a - m_last))
        cols = _columns(by_time)

        ones = jnp.ones((seq, LANES), F32)
        for h in range(NH_A):
            _run_share(side_work, NH_A - h)
            cq = slice(h * DK_A, (h + 1) * DK_A)
            ck = slice(A_QK + h * DK_A, A_QK + (h + 1) * DK_A)
            cv = slice(h * DV_A, (h + 1) * DV_A)
            q = _dwconv_silu(qk_ref[:, cq], cw_ref[:, cq], cb_ref[:, cq])
            k = _dwconv_silu(qk_ref[:, ck], cw_ref[:, ck], cb_ref[:, ck]) * (DK_A ** -0.5)
            qb = q.astype(BF16)
            kb = k.astype(BF16)
            vh = v_ref[:, cv]
            vaug = jnp.concatenate([vh, ones], axis=1).astype(BF16)
            for r0 in range(0, seq, Q_TILE):
                rows = slice(r0, r0 + Q_TILE)
                s = _dot_nt(qb[rows], kb)
                hsum = None
                for d in range(2):
                    c = d * NH_A + h
                    k0, k1 = _key_range(d, r0, seq)
                    expo = _causal_exponent(a[c:c + 1, k0:k1] - cols[rows, c:c + 1], r0, k0, d == 1)
                    p = (s[:, k0:k1] * jnp.exp(expo)).astype(BF16)
                    acc = _dot(p, vaug[k0:k1])
                    if has_state:
                        acc = acc + (cols[rows, 2 * n_ch + c:2 * n_ch + c + 1]
                                     * _dot(qb[rows], c0_ref[d, h].astype(BF16)))
                    hd = acc[:, 0:DV_A] / jnp.maximum(jnp.abs(acc[:, DV_A:]), cols[rows, n_ch + c:n_ch + c + 1])
                    hsum = hd if hsum is None else hsum + hd
                og = jax.nn.sigmoid(o_ref[rows, cv]) * hsum
                ha_ref[rows, cv] = _rms(og, anw_ref[:, cv])
            if emit_state:
                for d in range(2):
                    c = d * NH_A + h
                    kw = k * cols[:, 2 * n_ch + c:2 * n_ch + c + 1]
                    cn_ref[d, h] = _dot(kw.T.astype(BF16), vh.astype(BF16))
                    nn_ref[d, h:h + 1, :] = jnp.sum(kw, axis=0, keepdims=True)

    if fused:
        _fused_project(pl.program_id(0), fused_refs, list(it), MLSTM_PIECES, MLSTM_GATES, mix)
    else:
        mix(qk_ref, v_ref, o_ref, g_ref)


def _state_out_specs(shapes, nseq, sub_layer, carried):
    out_specs, out_shape = [], []
    for shp in shapes:
        zeros = (0,) * len(shp)
        if carried is None:
            out_specs.append(pl.BlockSpec((None, N_EVEN) + shp, lambda s, z=zeros: (s, 0) + z))
        else:
            out_specs.append(pl.BlockSpec((None, None) + shp, lambda s, z=zeros: (s, sub_layer) + z))
        out_shape.append(jax.ShapeDtypeStruct((nseq, N_EVEN) + shp, F32))
    return out_specs, out_shape


MLSTM_STATE_SHAPES = ((2, NH_A, DK_A, DV_A), (2, NH_A, DK_A), (2 * NH_A, LANES))
SSD_STATE_SHAPES = ((2, NH_B, HP_B, DSTATE),)


def _fused_mixer(kernel_fn, name, x, mods, layer, gains, w_all, w_cols, params, seq, pieces, gate_cols,
                 out_width, state_shapes, carried):
    n = x.shape[0]
    nseq = n // seq
    sub_layer = layer // 2
    in_specs = _fused_specs(seq, nseq, layer, w_all, w_cols) + [_layer_spec(p, sub_layer) for p in params]
    args = [x, x, mods, gains, w_all, *params]
    state_specs, state_shape = _state_out_specs(state_shapes, nseq, sub_layer, carried)
    aliases = {}
    if carried is not None:
        aliases = {len(args) + i: 1 + i for i in range(len(state_shapes))}
        in_specs += [pl.BlockSpec(memory_space=pl.ANY)] * len(state_shapes)
        args += list(carried)
    return pl.pallas_call(
        functools.partial(kernel_fn, seq=seq, has_state=False, emit_state=True, n_carried=len(aliases),
                          slot=sub_layer, fused=True),
        grid=(nseq,), in_specs=in_specs,
        out_specs=[pl.BlockSpec((seq, out_width), lambda s: (s, 0))] + state_specs,
        out_shape=[jax.ShapeDtypeStruct((n, out_width), F32)] + state_shape,
        scratch_shapes=_fused_scratch(seq, pieces, gate_cols[1] - gate_cols[0]),
        input_output_aliases=aliases, compiler_params=_params("arbitrary"), name=name,
    )(*args)


def _mlstm(qk, v, o, gates, params, seq, sub_layer, state=None, carried=None):
    n = qk.shape[0]
    nseq = n // seq
    has_state = state is not None
    emit_state = not has_state
    aliases = {}
    row = lambda s: (s, 0)
    in_specs = [pl.BlockSpec((seq, 2 * A_QK), row), pl.BlockSpec((seq, A_V), row), pl.BlockSpec((seq, A_V), row),
                pl.BlockSpec((GATE_ROWS, seq), lambda s: (0, s))]
    in_specs += [_layer_spec(p, sub_layer) for p in params]
    args = [qk, v, o, gates, *params]
    out_specs = [pl.BlockSpec((seq, A_V), row)]
    out_shape = [jax.ShapeDtypeStruct((n, A_V), F32)]
    if has_state:
        c0_aug, m0 = state
        in_specs += [pl.BlockSpec((None, None) + c0_aug.shape[2:], lambda s: (s, sub_layer, 0, 0, 0, 0)),
                     pl.BlockSpec((None, None) + m0.shape[2:], lambda s: (s, sub_layer, 0, 0))]
        args += [c0_aug, m0]
    if emit_state:
        shapes = [(2, NH_A, DK_A, DV_A), (2, NH_A, DK_A), (2 * NH_A, LANES)]
        for shp in shapes:
            zeros = (0,) * len(shp)
            if carried is None:
                out_specs.append(pl.BlockSpec((None, N_EVEN) + shp, lambda s, z=zeros: (s, 0) + z))
            else:
                out_specs.append(pl.BlockSpec((None, None) + shp, lambda s, z=zeros: (s, sub_layer) + z))
            out_shape.append(jax.ShapeDtypeStruct((nseq, N_EVEN) + shp, F32))
        if carried is not None:
            aliases = {len(args) + i: 1 + i for i in range(len(shapes))}
            in_specs += [pl.BlockSpec(memory_space=pl.ANY)] * len(shapes)
            args += list(carried)
    return pl.pallas_call(
        functools.partial(_mlstm_kernel, seq=seq, has_state=has_state, emit_state=emit_state,
                          n_carried=len(aliases), slot=sub_layer, fused=False),
        grid=(nseq,), in_specs=in_specs, out_specs=out_specs, out_shape=out_shape,
        input_output_aliases=aliases, compiler_params=_params("arbitrary"), name="mlstm",
    )(*args)


def _ssd_kernel(*refs, seq, has_state, emit_state, n_carried, slot, fused):
    assert not (has_state and emit_state)
    it = iter(refs)
    if fused:
        fused_refs = [next(it) for _ in range(N_FUSED_INPUTS)]
    else:
        xbc_ref, z_ref, g_ref = (next(it) for _ in range(3))
    cw_ref, cb_ref, dtb_ref, alog_ref, dsk_ref, bnw_ref = (next(it) for _ in range(6))
    if has_state:
        s0_ref = next(it)
    for _ in range(n_carried):
        next(it)
    yb_ref = next(it)
    if emit_state:
        sn_ref = next(it)
        if n_carried == 0:
            for other in range(N_EVEN):
                if other != slot:
                    sn_ref[other] = jnp.zeros(sn_ref.shape[1:], F32)
            sn_ref = sn_ref.at[slot]

    gate_row0 = 0 if fused else 4 * NH_A

    def mix(xbc_ref, z_ref, g_ref, side_work=()):
        side_work = list(side_work)
        n_ch = 2 * NH_B
        dt = _softplus(g_ref[gate_row0:gate_row0 + n_ch, :] + dtb_ref[:, 0:1])
        acum = _cumsum_lanes(dt * (-jnp.exp(alog_ref[:, 0:1])), NH_B)
        key_shift = acum - jnp.log(dt)
        by_time = [acum]
        if has_state:
            by_time.append(jnp.exp(acum))
        if emit_state:
            fwd = _row_iota((n_ch, 1)) < NH_B
            a_last = jnp.where(fwd, acum[:, seq - 1:seq], acum[:, 0:1])
            by_time.append(jnp.exp(a_last - acum) * dt)
        cols = _columns(by_time)

        gw = R_B * HP_B
        first = _pair_lanes((seq, LANES))
        for g in range(NG_B):
            _run_share(side_work, NG_B - g)
            cx = slice(g * gw, (g + 1) * gw)
            cb_ = slice(B_INNER + g * DSTATE, B_INNER + (g + 1) * DSTATE)
            cc = slice(B_INNER + B_BC + g * DSTATE, B_INNER + B_BC + (g + 1) * DSTATE)
            xg = _dwconv_silu(xbc_ref[:, cx], cw_ref[:, cx], cb_ref[:, cx])
            bg = _dwconv_silu(xbc_ref[:, cb_], cw_ref[:, cb_], cb_ref[:, cb_]).astype(BF16)
            cg = _dwconv_silu(xbc_ref[:, cc], cw_ref[:, cc], cb_ref[:, cc]).astype(BF16)
            xbd = [_pair_split(xg[:, p * LANES:(p + 1) * LANES]).astype(BF16) for p in range(R_B // 2)]
            for r0 in range(0, seq, Q_TILE):
                rows = slice(r0, r0 + Q_TILE)
                cb_scores = _dot_nt(cg[rows], bg)
                ys = []
                for p in range(R_B // 2):
                    weights, inputs = [], []
                    for d in range(2):
                        k0, k1 = _key_range(d, r0, seq)
                        for i in range(2):
                            c = d * NH_B + g * R_B + 2 * p + i
                            expo = _causal_exponent(cols[rows, c:c + 1] - key_shift[c:c + 1, k0:k1], r0, k0, d == 1)
                            weights.append((cb_scores[:, k0:k1] * jnp.exp(expo)).astype(BF16))
                            inputs.append(xbd[p][i * seq + k0:i * seq + k1])
                    yp = _dot(jnp.concatenate(weights, axis=1), jnp.concatenate(inputs, axis=0))
                    if has_state:
                        h0 = g * R_B + 2 * p
                        for d in range(2):
                            c = d * NH_B + h0
                            carry = jnp.where(_pair_lanes((Q_TILE, LANES)), cols[rows, n_ch + c:n_ch + c + 1],
                                              cols[rows, n_ch + c + 1:n_ch + c + 2])
                            s0_pair = s0_ref[d, h0:h0 + 2].reshape(2 * HP_B, DSTATE).astype(BF16)
                            yp = yp + carry * _dot_nt(cg[rows], s0_pair)
                    ys.append(yp)
                y = jnp.concatenate(ys, axis=1) + dsk_ref[:, cx] * xg[rows]
                y = y * _silu(z_ref[rows, cx])
                yb_ref[rows, cx] = _rms(y, bnw_ref[:, cx])
            if emit_state:
                for d in range(2):
                    c0 = n_ch + d * NH_B + g * R_B
                    spread = jnp.concatenate([jnp.where(first, cols[:, c0 + 2 * p:c0 + 2 * p + 1],
                                                        cols[:, c0 + 2 * p + 1:c0 + 2 * p + 2])
                                              for p in range(R_B // 2)], axis=1)
                    sn = _dot((xg * spread).T.astype(BF16), bg)
                    for r in range(R_B):
                        sn_ref[d, g * R_B + r] = sn[r * HP_B:(r + 1) * HP_B, :]

    if fused:
        _fused_project(pl.program_id(0), fused_refs, list(it), SSD_PIECES, SSD_GATES, mix)
    else:
        mix(xbc_ref, z_ref, g_ref)


def _ssd(xbc, z, gates, params, seq, sub_layer, state=None, carried=None):
    n = xbc.shape[0]
    nseq = n // seq
    has_state = state is not None
    emit_state = not has_state
    aliases = {}
    row = lambda s: (s, 0)
    in_specs = [pl.BlockSpec((seq, B_XBC), row), pl.BlockSpec((seq, B_INNER), row),
                pl.BlockSpec((GATE_ROWS, seq), lambda s: (0, s))]
    in_specs += [_layer_spec(p, sub_layer) for p in params]
    args = [xbc, z, gates, *params]
    out_specs = [pl.BlockSpec((seq, B_INNER), row)]
    out_shape = [jax.ShapeDtypeStruct((n, B_INNER), F32)]
    state_spec = pl.BlockSpec((None, None, 2, NH_B, HP_B, DSTATE), lambda s: (s, sub_layer, 0, 0, 0, 0))
    if has_state:
        in_specs.append(state_spec)
        args.append(state)
    if emit_state:
        out_specs.append(state_spec if carried is not None else
                         pl.BlockSpec((None, N_EVEN, 2, NH_B, HP_B, DSTATE), lambda s: (s, 0, 0, 0, 0, 0)))
        out_shape.append(jax.ShapeDtypeStruct((nseq, N_EVEN, 2, NH_B, HP_B, DSTATE), F32))
        if carried is not None:
            aliases = {len(args): 1}
            in_specs.append(pl.BlockSpec(memory_space=pl.ANY))
            args.append(carried)
    return pl.pallas_call(
        functools.partial(_ssd_kernel, seq=seq, has_state=has_state, emit_state=emit_state, n_carried=len(aliases),
                          slot=sub_layer, fused=False),
        grid=(nseq,), in_specs=in_specs, out_specs=out_specs, out_shape=out_shape,
        input_output_aliases=aliases, compiler_params=_params("arbitrary"), name="ssd",
    )(*args)


def _shared_split(x, x_swapped, kh):
    first = _pair_lanes(x.shape)
    zero = jnp.zeros_like(x)
    top, bottom = (x, x_swapped) if kh == 0 else (x_swapped, x)
    return jnp.concatenate([jnp.where(first, top, zero), jnp.where(first, zero, bottom)], axis=0)


def _pair_probs(s, sinks=None, valid=None):
    n_keys = s.shape[1] // 2
    probs, maxes = [], []
    for i in range(2):
        si = s[:, i * n_keys:(i + 1) * n_keys]
        if valid is not None:
            si = jnp.where(valid, si, -jnp.inf)
        m = jnp.max(si, axis=1, keepdims=True)
        if sinks is not None:
            m = jnp.maximum(m, sinks[i])
        probs.append(jnp.exp(si - m))
        maxes.append(m)
    return jnp.concatenate(probs, axis=1).astype(BF16), maxes


def _pair_output(p, maxes, vbd, sinks=None):
    o = _dot(p, vbd)
    den = o[:, LANES:]
    if sinks is not None:
        den = den + jnp.where(_pair_lanes(den.shape), jnp.exp(sinks[0] - maxes[0]), jnp.exp(sinks[1] - maxes[1]))
    return o[:, :LANES] / den


def _run_pairs(items, valid=None, side_work=()):
    side_work = list(side_work)
    s_next = items[0][0]()
    for idx, (_, values, sinks, out_ref, cols) in enumerate(items):
        _run_share(side_work, len(items) - idx)
        s_cur = s_next
        if idx + 1 < len(items):
            s_next = items[idx + 1][0]()
        p, maxes = _pair_probs(s_cur, sinks, valid if sinks is not None else None)
        out_ref[:, cols] = _pair_output(p, maxes, values(), sinks)


def _pair_sinks(sink_ref, n):
    return sink_ref[0:1, n:n + 1], sink_ref[0:1, n + 1:n + 2]


def _mla_queries(qa_ref, qan_ref, wqb_ref):
    return _dot(_rms(qa_ref[...], qan_ref[...]).astype(BF16), wqb_ref[...]) * MLA_SCALE


def _attn_ctx_kernel(*refs, seq, n_carried, slot, fused):
    it = iter(refs)
    if fused:
        fused_refs = [next(it) for _ in range(N_FUSED_INPUTS)]
    else:
        proj_refs = [next(it) for _ in range(len(ODD_WIDTHS))]
    sink_ref, qan_ref, kvn_ref, wqb_ref, wkvb_ref = (next(it) for _ in range(5))
    for _ in range(n_carried):
        next(it)
    oc_ref, od_ref, ckv_ref = next(it), next(it), next(it)
    if fused:
        new_k_ref, new_v_ref, new_kpe_ref = next(it), next(it), next(it)
    if n_carried == 0:
        for other in range(N_ODD):
            if other != slot:
                ckv_ref[other] = jnp.zeros(ckv_ref.shape[1:], F32)
        ckv_ref = ckv_ref.at[slot]

    def mix(qc_ref, kc_ref, vc_ref, qa_ref, kva_ref, kpe_ref, side_work=()):
        if fused:
            new_k_ref[...] = kc_ref[...]
            new_v_ref[...] = vc_ref[...]
            new_kpe_ref[...] = kpe_ref[...]
        ones_bd = _pair_split(jnp.ones((seq, LANES), F32))
        kc, vc = kc_ref[...], vc_ref[...]
        kc_sw, vc_sw = pltpu.roll(kc, HD_C, 1), pltpu.roll(vc, HD_C, 1)
        qd = _mla_queries(qa_ref, qan_ref, wqb_ref)
        ckv = _rms(kva_ref[...], kvn_ref[...])
        ckv_ref[...] = ckv
        kv = _dot(ckv.astype(BF16), wkvb_ref[...])
        kpe = kpe_ref[...]
        kpe_bd = jnp.concatenate([kpe, pltpu.roll(kpe, ROPE_D, 1)], axis=0)
        nope_w = NH_D * NOPE_D
        items = []
        for kh in range(NKV_C):
            for n in range(kh * G_C, (kh + 1) * G_C, 2):
                cols = slice(n * HD_C, (n + 2) * HD_C)
                items.append((lambda cols=cols, kh=kh: _dot_nt((qc_ref[:, cols] * (HD_C ** -0.5)).astype(BF16),
                                                               _shared_split(kc, kc_sw, kh).astype(BF16)),
                              lambda kh=kh: jnp.concatenate([_shared_split(vc, vc_sw, kh), ones_bd], axis=1).astype(BF16),
                              _pair_sinks(sink_ref, n), oc_ref, cols))
        for i in range(NH_D // 2):
            cols = slice(i * LANES, (i + 1) * LANES)
            vcols = slice(nope_w + i * LANES, nope_w + (i + 1) * LANES)
            items.append((lambda cols=cols, vcols=vcols: _dot_nt(
                              jnp.concatenate([qd[:, cols], qd[:, vcols]], axis=1).astype(BF16),
                              jnp.concatenate([_pair_split(kv[:, cols]), kpe_bd], axis=1).astype(BF16)),
                          lambda vcols=vcols: jnp.concatenate([_pair_split(kv[:, vcols]), ones_bd], axis=1).astype(BF16),
                          None, od_ref, cols))
        _run_pairs(items, side_work=side_work)

    if fused:
        _fused_project(pl.program_id(0), fused_refs, list(it), CTX_PIECES, None, mix)
    else:
        mix(*proj_refs)


def _attn_ctx(x, mods, layer, gains, w_all, params, seq, carried=None):
    n = x.shape[0]
    nseq = n // seq
    sub_layer = layer // 2
    row = lambda s: (s, 0)
    in_specs = _fused_specs(seq, nseq, layer, w_all, w_all.shape[2]) + [_layer_spec(p, sub_layer) for p in params]
    args = [x, x, mods, gains, w_all, *params]
    half = NH_C * HD_C
    aliases = {}
    if carried is None:
        ckv_spec = pl.BlockSpec((None, N_ODD, seq, KV_RANK), lambda s: (s, 0, 0, 0))
    else:
        ckv_spec = pl.BlockSpec((None, None, seq, KV_RANK), lambda s: (s, sub_layer, 0, 0))
        aliases = {len(args): 2}
        in_specs.append(pl.BlockSpec(memory_space=pl.ANY))
        args.append(carried)
    kv_w = NKV_C * HD_C
    return pl.pallas_call(
        functools.partial(_attn_ctx_kernel, seq=seq, n_carried=len(aliases), slot=sub_layer, fused=True),
        grid=(nseq,), in_specs=in_specs,
        out_specs=[pl.BlockSpec((seq, half), row), pl.BlockSpec((seq, half), row), ckv_spec,
                   pl.BlockSpec((seq, kv_w), row), pl.BlockSpec((seq, kv_w), row), pl.BlockSpec((seq, LANES), row)],
        out_shape=[jax.ShapeDtypeStruct((n, half), F32), jax.ShapeDtypeStruct((n, half), F32),
                   jax.ShapeDtypeStruct((nseq, N_ODD, seq, KV_RANK), F32),
                   jax.ShapeDtypeStruct((n, kv_w), F32), jax.ShapeDtypeStruct((n, kv_w), F32),
                   jax.ShapeDtypeStruct((n, LANES), F32)],
        scratch_shapes=_fused_scratch(seq, CTX_PIECES, 0),
        input_output_aliases=aliases, compiler_params=_params("arbitrary"), name="attn_ctx",
    )(*args)


def _rope(x, cos, sin, half):
    parts = []
    lane = _lane_iota((x.shape[0], LANES))
    first = (lane & (2 * half - 1)) < half
    for i in range(x.shape[1] // LANES):
        xi = x[:, i * LANES:(i + 1) * LANES]
        partner = jnp.where(first, -pltpu.roll(xi, LANES - half, 1), pltpu.roll(xi, half, 1))
        parts.append(xi * cos + partner * sin)
    return parts[0] if len(parts) == 1 else jnp.concatenate(parts, axis=1)


def _attn_lat_kernel(qc_ref, qa_ref, ropeq_ref, kc_ref, vc_ref, kva_ref, kpe_ref, kctx_ref, vctx_ref, ckvctx_ref,
                     kpectx_ref, rope_ref, sink_ref, qan_ref, kvn_ref, wqb_ref, wkvb_ref, oc_ref, od_ref,
                     kwin_s, vwin_s, kext_s, vext_s, *, seq, past):
    qi = pl.program_id(1)
    nope_w = NH_D * NOPE_D
    n_all = past + seq
    ctx0 = 2 * WINDOW + seq

    @pl.when(qi == 0)
    def _():
        zeros = jnp.zeros((WINDOW, LANES), BF16)
        for ref, lat, ctx in ((kwin_s, _rope(kc_ref[...], rope_ref[0], rope_ref[1], HD_C // 2), kctx_ref[...]),
                              (vwin_s, vc_ref[...], vctx_ref[...])):
            lat_sw, ctx_sw = pltpu.roll(lat, HD_C, 1), pltpu.roll(ctx, HD_C, 1)
            for kh in range(NKV_C):
                lat_bd = _shared_split(lat, lat_sw, kh).astype(BF16)
                ctx_bd = _shared_split(ctx, ctx_sw, kh).astype(BF16)
                for i in range(2):
                    ref[kh, i, 0:WINDOW, :] = zeros
                    ref[kh, i, WINDOW:WINDOW + seq, :] = lat_bd[i * seq:(i + 1) * seq]
                    ref[kh, i, WINDOW + seq:ctx0, :] = zeros
                    ref[kh, i, ctx0:, :] = ctx_bd[i * past:(i + 1) * past]
        ckv = _rms(kva_ref[...], kvn_ref[...])
        kv = jnp.concatenate([_dot(ckvctx_ref[...].astype(BF16), wkvb_ref[...]),
                              _dot(ckv.astype(BF16), wkvb_ref[...])], axis=0)
        kpe = jnp.concatenate([kpectx_ref[...], _rope(kpe_ref[...], rope_ref[2], rope_ref[3], ROPE_D // 2)], axis=0)
        kpe_bd = jnp.concatenate([kpe, pltpu.roll(kpe, ROPE_D, 1)], axis=0).astype(BF16)
        ones_bd = _pair_split(jnp.ones((n_all, LANES), F32)).astype(BF16)
        for i in range(NH_D // 2):
            kext_s[i, :, 0:LANES] = _pair_split(kv[:, i * LANES:(i + 1) * LANES]).astype(BF16)
            kext_s[i, :, LANES:] = kpe_bd
            vext_s[i, :, 0:LANES] = _pair_split(kv[:, nope_w + i * LANES:nope_w + (i + 1) * LANES]).astype(BF16)
            vext_s[i, :, LANES:] = ones_bd

    r0 = pl.multiple_of(qi * Q_TILE, Q_TILE)
    nloc = Q_TILE + 2 * WINDOW
    n_keys = nloc + past
    qr = _rope(qc_ref[...], ropeq_ref[0], ropeq_ref[1], HD_C // 2) * (HD_C ** -0.5)
    ti = r0 + _row_iota((Q_TILE, n_keys))
    col = _lane_iota((Q_TILE, n_keys))
    pos = r0 - WINDOW + col
    valid = (col >= nloc) | ((jnp.abs(ti - pos) <= WINDOW) & (pos >= 0) & (pos < seq))
    ones_bd = _pair_split(jnp.ones((n_keys, LANES), F32)).astype(BF16)
    qd = _mla_queries(qa_ref, qan_ref, wqb_ref)
    q_pe = _rope(qd[:, nope_w:], ropeq_ref[2], ropeq_ref[3], ROPE_D // 2)

    def banded(ref, kh):
        return jnp.concatenate([ref[kh, 0, pl.ds(r0, nloc), :], ref[kh, 0, ctx0:, :],
                                ref[kh, 1, pl.ds(r0, nloc), :], ref[kh, 1, ctx0:, :]], axis=0)

    items = []
    for kh in range(NKV_C):
        for n in range(kh * G_C, (kh + 1) * G_C, 2):
            cols = slice(n * HD_C, (n + 2) * HD_C)
            items.append((lambda cols=cols, kh=kh: _dot_nt(qr[:, cols].astype(BF16), banded(kwin_s, kh)),
                          lambda kh=kh: jnp.concatenate([banded(vwin_s, kh), ones_bd], axis=1),
                          _pair_sinks(sink_ref, n), oc_ref, cols))
    for i in range(NH_D // 2):
        cols = slice(i * LANES, (i + 1) * LANES)
        items.append((lambda cols=cols, i=i: _dot_nt(jnp.concatenate([qd[:, cols], q_pe[:, cols]], axis=1).astype(BF16),
                                                     kext_s[i]),
                      lambda i=i: vext_s[i], None, od_ref, cols))
    _run_pairs(items, valid)


def _attn_lat(proj, caches, rope, params, seq, sub_layer):
    qc, kc, vc, qa, kva, kpe = proj
    n = qc.shape[0]
    past = caches[0].shape[2]
    nq = seq // Q_TILE
    qrow = lambda b, q: (b * nq + q, 0)
    krow = lambda b, q: (b, 0)
    kvw = NKV_C * HD_C
    in_specs = [pl.BlockSpec((Q_TILE, NH_C * HD_C), qrow), pl.BlockSpec((Q_TILE, Q_RANK), qrow),
                pl.BlockSpec((4, Q_TILE, LANES), lambda b, q: (0, q, 0)),
                pl.BlockSpec((seq, kvw), krow), pl.BlockSpec((seq, kvw), krow),
                pl.BlockSpec((seq, KV_RANK), krow), pl.BlockSpec((seq, LANES), krow)]
    in_specs += [pl.BlockSpec((None, None, past, LANES), lambda b, q: (b, sub_layer, 0, 0)) for _ in caches]
    in_specs += [pl.BlockSpec(rope.shape, lambda b, q: (0, 0, 0))]
    in_specs += [_layer_spec(p, sub_layer) for p in params]
    half = NH_C * HD_C
    win_rows = 2 * WINDOW + seq + past
    return pl.pallas_call(
        functools.partial(_attn_lat_kernel, seq=seq, past=past),
        grid=(n // seq, nq), in_specs=in_specs,
        out_specs=[pl.BlockSpec((Q_TILE, half), qrow), pl.BlockSpec((Q_TILE, half), qrow)],
        out_shape=[jax.ShapeDtypeStruct((n, half), F32), jax.ShapeDtypeStruct((n, half), F32)],
        scratch_shapes=[pltpu.VMEM((NKV_C, 2, win_rows, LANES), BF16), pltpu.VMEM((NKV_C, 2, win_rows, LANES), BF16),
                        pltpu.VMEM((NH_D // 2, 2 * (past + seq), 2 * LANES), BF16),
                        pltpu.VMEM((NH_D // 2, 2 * (past + seq), 2 * LANES), BF16)],
        compiler_params=_params("arbitrary", "arbitrary"), name="attn_lat",
    )(qc, qa, rope, kc, vc, kva, kpe, *caches, rope, *params)


def _pad_lanes(x, width=LANES):
    return jnp.pad(x, [(0, 0)] * (x.ndim - 1) + [(0, width - x.shape[-1])])


def _on_lanes(x):
    return jnp.broadcast_to(x[..., None], x.shape + (LANES,))


def _mla_query_weights(w):
    lead = w.shape[:-1]
    w4 = w.reshape(lead + (NH_D // 2, 2, NOPE_D + ROPE_D))
    nope = w4[..., :NOPE_D].reshape(lead + (NH_D * NOPE_D,))
    pe = _pad_lanes(w4[..., NOPE_D:].reshape(lead + (NH_D // 2, 2 * ROPE_D)))
    return jnp.concatenate([nope, pe.reshape(lead + (NH_D // 2 * LANES,))], axis=-1).astype(BF16)


def _mla_kv_weights(w):
    lead = w.shape[:-1]
    w3 = w.reshape(lead + (NH_D, NOPE_D + V_D))
    return jnp.concatenate([w3[..., :NOPE_D].reshape(lead + (NH_D * NOPE_D,)),
                            w3[..., NOPE_D:].reshape(lead + (NH_D * V_D,))], axis=-1).astype(BF16)


def _rope_tables(rows):
    def table(rot_dim):
        quarter = rot_dim // 4
        inv = ROPE_BASE ** (-jnp.arange(quarter, dtype=F32) / quarter)
        r = jnp.repeat(jnp.arange(rows, dtype=F32), GRID_W)
        col = jnp.tile(jnp.arange(GRID_W, dtype=F32), rows)
        ang = jnp.concatenate([r[:, None] * inv, col[:, None] * inv], axis=-1)
        reps = LANES // (rot_dim // 2)
        return jnp.tile(jnp.cos(ang), (1, reps)), jnp.tile(jnp.sin(ang), (1, reps))
    cos_c, sin_c = table(HD_C)
    cos_d, sin_d = table(ROPE_D)
    return jnp.stack([cos_c, sin_c, cos_d, sin_d])


def kernel(x_prompt, x_sample, c, state_mlstm_C, state_mlstm_n, state_mlstm_m, state_ssd, cache_gqa_k, cache_gqa_v,
           cache_mla_ckv, cache_mla_kpe, c_ctx, w_ada, b_ada, norm_g, w_up, w_down, w_in_even, conv_a_w, conv_a_b,
           conv_b_w, conv_b_b, gate_b, a_norm_w, dt_bias, a_log, d_skip, b_norm_w, w_out_even, w_in_odd, sink,
           q_a_norm, kv_a_norm, w_q_b, w_kv_b, w_out_odd):
    xp = x_prompt.reshape(BATCH * SEQ, D_MODEL)
    xs = x_sample.reshape(DEC_BATCH * DEC_SEQ, D_MODEL)
    cond = jnp.concatenate([c_ctx[None, :], c, jnp.zeros((MOD_ROWS - 1 - DEC_BATCH, D_MODEL), F32)], axis=0)
    mods = _modulations(cond, w_ada, b_ada)
    rope = _rope_tables(DEC_SEQ // GRID_W)

    w_even = w_in_even.astype(BF16)
    a_params = (conv_a_w, conv_a_b[:, None, :], _on_lanes(gate_b), a_norm_w[:, None, :])
    b_params = (conv_b_w, conv_b_b[:, None, :], _on_lanes(dt_bias.reshape(N_EVEN, 2 * NH_B)),
                _on_lanes(a_log.reshape(N_EVEN, 2 * NH_B)), jnp.repeat(d_skip, HP_B, axis=1)[:, None, :],
                b_norm_w[:, None, :])
    n0 = state_mlstm_n[..., None]
    mem_in = (jnp.concatenate([state_mlstm_C, jnp.broadcast_to(n0, n0.shape[:-1] + (LANES,))], axis=-1),
              _on_lanes(state_mlstm_m.reshape(DEC_BATCH, N_EVEN, 2 * NH_A)))
    w_odd = w_in_odd.astype(BF16)
    o_params = (_pad_lanes(sink)[:, None, :], q_a_norm[:, None, :], kv_a_norm[:, None, :],
                _mla_query_weights(w_q_b), _mla_kv_weights(w_kv_b))
    caches = (cache_gqa_k.reshape(DEC_BATCH, N_ODD, PAST_LEN, NKV_C * HD_C),
              cache_gqa_v.reshape(DEC_BATCH, N_ODD, PAST_LEN, NKV_C * HD_C),
              cache_mla_ckv, _pad_lanes(cache_mla_kpe))

    new_k, new_v, new_kpe = [], [], []
    mem_state, ssd_state, ckv_state = None, None, None
    for l in range(DEPTH):
        j = l // 2
        if l % 2 == 0:
            a1p, *mem_state = _fused_mixer(_mlstm_kernel, "mlstm", xp, mods, l, norm_g, w_even, MLSTM_W_COLS, a_params,
                                           SEQ, MLSTM_PIECES, MLSTM_GATES, A_V, MLSTM_STATE_SHAPES, mem_state)
            a2p, *ssd_state = _fused_mixer(_ssd_kernel, "ssd", xp, mods, l, norm_g, w_even, w_even.shape[2], b_params,
                                           SEQ, SSD_PIECES, SSD_GATES, B_INNER, SSD_STATE_SHAPES, ssd_state)
            qk, v, o, z, xbc, g = _project(xs, mods, l, True, norm_g, w_even, EVEN_WIDTHS, EVEN_REGROUP, EVEN_GATE_COLS)
            a1s, = _mlstm(qk, v, o, g, a_params, DEC_SEQ, j, state=mem_in)
            a2s, = _ssd(xbc, z, g, b_params, DEC_SEQ, j, state=state_ssd)
            w_out = w_out_even
        else:
            a1p, a2p, ckv_state, kc, vc, kpe = _attn_ctx(xp, mods, l, norm_g, w_odd, o_params, SEQ, carried=ckv_state)
            new_k.append(kc.reshape(BATCH, SEQ, NKV_C, HD_C))
            new_v.append(vc.reshape(BATCH, SEQ, NKV_C, HD_C))
            new_kpe.append(kpe[:, :ROPE_D].reshape(BATCH, SEQ, ROPE_D))
            proj = _project(xs, mods, l, True, norm_g, w_odd, ODD_WIDTHS, ODD_REGROUP)
            a1s, a2s = _attn_lat(proj, caches, rope, o_params, DEC_SEQ, j)
            w_out = w_out_odd
        xp, xs = _channel((a1p, a2p, xp), (a1s, a2s, xs), mods, l, norm_g, w_out, w_up, w_down)

    new_c, new_n, new_m = mem_state
    return (xp.reshape(BATCH, SEQ, D_MODEL), xs.reshape(DEC_BATCH, DEC_SEQ, D_MODEL),
            new_c, new_n, new_m[..., 0].reshape(BATCH, N_EVEN, 2, NH_A), ssd_state[0],
            jnp.stack(new_k, axis=1), jnp.stack(new_v, axis=1), ckv_state, jnp.stack(new_kpe, axis=1))
```

```python
import functools

import jax
import jax.numpy as jnp
from jax import lax
from jax.experimental import pallas as pl
from jax.experimental.pallas import tpu as pltpu

F32 = jnp.float32
BF16 = jnp.bfloat16

D_MODEL = 1024
BATCH = 32
SEQ = 256
DEPTH = 4
DEC_BATCH = 2
DEC_SEQ = 1024
PAST_LEN = 256
GRID_W = 64
N_EVEN = (DEPTH + 1) // 2
N_ODD = DEPTH // 2
EPS = 1e-6
CONV_K = 5
NH_A = 4
DK_A = 128
DV_A = 128
A_QK = NH_A * DK_A
A_V = NH_A * DV_A
NH_B = 8
HP_B = 64
DSTATE = 128
NG_B = 2
R_B = NH_B // NG_B
B_INNER = NH_B * HP_B
B_BC = NG_B * DSTATE
B_XBC = B_INNER + 2 * B_BC
NH_C = 8
NKV_C = 2
G_C = NH_C // NKV_C
HD_C = 64
WINDOW = 128
NH_D = 8
Q_RANK = 256
KV_RANK = 128
NOPE_D = 64
ROPE_D = 32
V_D = 64
MLA_SCALE = (NOPE_D + ROPE_D) ** -0.5
D_FF = 4 * D_MODEL
ROPE_BASE = 10000.0

LANES = 128
VMEM_LIMIT_BYTES = 56 * 1024 * 1024
ROW_TILE = 512
FF_TILE = 1024
SUB_ROWS = 256
STAGE_ROWS = 512
Q_TILE = 256
ADA_TILE = 1536
MOD_ROWS = 8
GATE_ROWS = 4 * NH_A + 2 * NH_B

EVEN_WIDTHS = (2 * A_QK, A_V, A_V, B_INNER, B_XBC)
ODD_WIDTHS = (NH_C * HD_C, NKV_C * HD_C, NKV_C * HD_C, Q_RANK, KV_RANK, LANES)
_GATES_LO = 2 * A_QK + 2 * A_V
_Z_LO = _GATES_LO + 4 * NH_A
_DT_LO = _Z_LO + B_INNER + B_XBC
EVEN_REGROUP = ((0, 0, _GATES_LO), (_GATES_LO, _Z_LO, B_INNER + B_XBC))
EVEN_GATE_COLS = ((_GATES_LO, _Z_LO), (_DT_LO, _DT_LO + 2 * NH_B))
ODD_IN = sum(ODD_WIDTHS) - LANES + ROPE_D
ODD_REGROUP = ((0, 0, ODD_IN), (ODD_IN, None, LANES - ROPE_D))

_NT = (((1,), (1,)), ((), ()))


def _params(*sem):
    return pltpu.CompilerParams(dimension_semantics=sem, vmem_limit_bytes=VMEM_LIMIT_BYTES)


def _rms(x, g):
    return x * lax.rsqrt(jnp.mean(x * x, axis=-1, keepdims=True) + EPS) * g


def _silu(x):
    return x * jax.nn.sigmoid(x)


def _softplus(x):
    return jnp.maximum(x, 0.0) + jnp.log1p(jnp.exp(-jnp.abs(x)))


def _dot(a, b):
    return jnp.dot(a, b, preferred_element_type=F32)


def _dot_nt(a, b):
    return lax.dot_general(a, b, _NT, preferred_element_type=F32)


def _layer_spec(arr, layer):
    tail = arr.shape[1:]
    zeros = (0,) * len(tail)
    return pl.BlockSpec((None,) + tail, lambda *_: (layer,) + zeros)


def _ada_kernel(c_ref, w_ref, b_ref, o_ref):
    s = _silu(c_ref[...]).astype(BF16)
    o_ref[...] = _dot(s, w_ref[...].astype(BF16)) + b_ref[...]


def _modulations(cond, w_ada, b_ada):
    out = pl.pallas_call(
        _ada_kernel,
        grid=(DEPTH, 6 * D_MODEL // ADA_TILE),
        in_specs=[pl.BlockSpec((MOD_ROWS, D_MODEL), lambda l, n: (0, 0)),
                  pl.BlockSpec((None, D_MODEL, ADA_TILE), lambda l, n: (l, 0, n)),
                  pl.BlockSpec((None, 1, ADA_TILE), lambda l, n: (l, 0, n))],
        out_specs=pl.BlockSpec((None, MOD_ROWS, ADA_TILE), lambda l, n: (l, 0, n)),
        out_shape=jax.ShapeDtypeStruct((DEPTH, MOD_ROWS, 6 * D_MODEL), F32),
        compiler_params=_params("arbitrary", "arbitrary"),
        name="ada",
    )(cond, w_ada, b_ada.reshape(DEPTH, 1, 6 * D_MODEL))
    return out.reshape(DEPTH, MOD_ROWS, 6, D_MODEL)


def _mod_spec(layer, latent):
    if latent:
        per_seq = DEC_SEQ // ROW_TILE
        return pl.BlockSpec((None, None, 6, D_MODEL), lambda i, *_: (layer, 1 + i // per_seq, 0, 0))
    return pl.BlockSpec((None, None, 6, D_MODEL), lambda i, *_: (layer, 0, 0, 0))


def _proj_kernel(x_ref, mod_ref, g_ref, w_ref, *rest, widths, regroup, gate_cols):
    n_out = len(widths) + (1 if gate_cols else 0)
    o_refs, w_s = rest[:n_out], rest[n_out]

    @pl.when(pl.program_id(0) == 0)
    def _():
        for dst, src, width in regroup:
            if src is None:
                w_s[:, dst:dst + width] = jnp.zeros((D_MODEL, width), BF16)
            else:
                w_s[:, dst:dst + width] = w_ref[:, src:src + width]
        if gate_cols:
            pieces = [w_ref[:, lo:hi].astype(F32) for lo, hi in gate_cols]
            n_gates = sum(hi - lo for lo, hi in gate_cols)
            gates = jnp.concatenate(pieces + [jnp.zeros((D_MODEL, LANES - n_gates), F32)], axis=1)
            rest[n_out + 1][...] = gates.T[0:n_gates, :].astype(BF16)

    h = _rms(x_ref[...], g_ref[0:1, :]) * (1.0 + mod_ref[1:2, :]) + mod_ref[0:1, :]
    hb = h.astype(BF16)
    off = 0
    for o_ref, wd in zip(o_refs, widths):
        o_ref[...] = _dot(hb, w_s[:, off:off + wd])
        off += wd
    if gate_cols:
        o_refs[-1][...] = _dot_nt(rest[n_out + 1][...], hb)


def _project(x, mods, layer, latent, gains, w_all, widths, regroup, gate_cols=()):
    n = x.shape[0]
    out_specs = [pl.BlockSpec((ROW_TILE, wd), lambda i: (i, 0)) for wd in widths]
    out_shape = [jax.ShapeDtypeStruct((n, wd), F32) for wd in widths]
    scratch = [pltpu.VMEM((D_MODEL, sum(widths)), BF16)]
    if gate_cols:
        n_gates = sum(hi - lo for lo, hi in gate_cols)
        out_specs.append(pl.BlockSpec((n_gates, ROW_TILE), lambda i: (0, i)))
        out_shape.append(jax.ShapeDtypeStruct((n_gates, n), F32))
        scratch.append(pltpu.VMEM((n_gates, D_MODEL), BF16))
    w_spec = pl.BlockSpec((None,) + w_all.shape[1:], lambda i: (layer // 2, 0, 0), pipeline_mode=pl.Buffered(1))
    return pl.pallas_call(
        functools.partial(_proj_kernel, widths=widths, regroup=regroup, gate_cols=gate_cols),
        grid=(n // ROW_TILE,),
        in_specs=[pl.BlockSpec((ROW_TILE, D_MODEL), lambda i: (i, 0)), _mod_spec(layer, latent),
                  _layer_spec(gains, layer), w_spec],
        out_specs=out_specs, out_shape=out_shape, scratch_shapes=scratch,
        compiler_params=_params("arbitrary"),
        name="proj",
    )(x, mods, gains, w_all)


def _weight_chunks(layer, sub_layer, wo_hbm, wu_hbm, wd_hbm, wo_s, wu_s, wd_s):
    chunks = []
    for r in range(0, D_MODEL, STAGE_ROWS):
        chunks.append((wo_hbm.at[sub_layer, pl.ds(r, STAGE_ROWS), :], wo_s.at[pl.ds(r, STAGE_ROWS), :]))
    for r in range(0, D_MODEL, STAGE_ROWS):
        for c in range(0, D_FF, D_MODEL):
            chunks.append((wu_hbm.at[layer, pl.ds(r, STAGE_ROWS), pl.ds(c, D_MODEL)],
                           wu_s.at[pl.ds(r, STAGE_ROWS), pl.ds(c, D_MODEL)]))
    for r in range(0, D_FF, STAGE_ROWS):
        chunks.append((wd_hbm.at[layer, pl.ds(r, STAGE_ROWS), :], wd_s.at[pl.ds(r, STAGE_ROWS), :]))
    return chunks


def _channel_kernel(a1p_ref, a2p_ref, xp_ref, a1s_ref, a2s_ref, xs_ref, mod_ref, g_ref, wo_hbm, wu_hbm, wd_hbm,
                    op_ref, os_ref, wo_s, wu_s, wd_s, stage, sem, *, layer, sub_layer, prompt_steps):
    step = pl.program_id(0)

    @pl.when(step == 0)
    def _():
        chunks = _weight_chunks(layer, sub_layer, wo_hbm, wu_hbm, wd_hbm, wo_s, wu_s, wd_s)
        copies = [pltpu.make_async_copy(src, stage.at[k % 2], sem.at[k % 2]) for k, (src, _) in enumerate(chunks)]
        copies[0].start()
        for k, (_, dst) in enumerate(chunks):
            if k + 1 < len(chunks):
                copies[k + 1].start()
            copies[k].wait()
            dst[...] = stage[k % 2].astype(BF16)

    def rows_block(a1_ref, a2_ref, x_ref, o_ref):
        half = a1_ref.shape[1]
        tiles = range(0, D_FF, FF_TILE)
        blocks = [slice(r0, r0 + SUB_ROWS) for r0 in range(0, ROW_TILE, SUB_ROWS)]

        def prologue(rows):
            y = (_dot(a1_ref[rows, :].astype(BF16), wo_s[0:half, :])
                 + _dot(a2_ref[rows, :].astype(BF16), wo_s[half:, :]))
            x1 = x_ref[rows, :] + mod_ref[2:3, :] * _rms(y, g_ref[1:2, :])
            h = (_rms(x1, g_ref[2:3, :]) * (1.0 + mod_ref[4:5, :]) + mod_ref[3:4, :]).astype(BF16)
            return x1, h

        def mlp_tile(h, c):
            u = jnp.square(jnp.maximum(_dot(h, wu_s[:, c:c + FF_TILE]), 0.0)).astype(BF16)
            return _dot(u, wd_s[c:c + FF_TILE, :])

        def epilogue(rows, x1, acc):
            o_ref[rows, :] = x1 + mod_ref[5:6, :] * _rms(acc, g_ref[3:4, :])

        ready = {0: prologue(blocks[0])}
        done = None
        for b, rows in enumerate(blocks):
            x1, h = ready.pop(b)
            acc = None
            for t, c in enumerate(tiles):
                part = mlp_tile(h, c)
                acc = part if acc is None else acc + part
                if t == 0 and b + 1 < len(blocks):
                    ready[b + 1] = prologue(blocks[b + 1])
                if t == 0 and done is not None:
                    epilogue(*done)
                    done = None
            done = (rows, x1, acc)
        epilogue(*done)

    @pl.when(step < prompt_steps)
    def _():
        rows_block(a1p_ref, a2p_ref, xp_ref, op_ref)

    @pl.when(step >= prompt_steps)
    def _():
        rows_block(a1s_ref, a2s_ref, xs_ref, os_ref)


def _channel(prompt, latent, mods, layer, gains, w_out, w_up, w_down):
    n_p, n_s = prompt[2].shape[0], latent[2].shape[0]
    steps_p, steps_s = n_p // ROW_TILE, n_s // ROW_TILE
    per_seq = DEC_SEQ // ROW_TILE
    row_p = lambda i: (jnp.minimum(i, steps_p - 1), 0)
    row_s = lambda i: (jnp.maximum(i - steps_p, 0), 0)
    mod_spec = pl.BlockSpec((None, None, 6, D_MODEL),
                            lambda i: (layer, jnp.where(i < steps_p, 0, 1 + (i - steps_p) // per_seq), 0, 0))
    hbm = pl.BlockSpec(memory_space=pl.ANY)
    specs = lambda arrs, row: [pl.BlockSpec((ROW_TILE, a.shape[1]), row) for a in arrs]
    return pl.pallas_call(
        functools.partial(_channel_kernel, layer=layer, sub_layer=layer // 2, prompt_steps=steps_p),
        grid=(steps_p + steps_s,),
        in_specs=specs(prompt, row_p) + specs(latent, row_s) + [mod_spec, _layer_spec(gains, layer), hbm, hbm, hbm],
        out_specs=[pl.BlockSpec((ROW_TILE, D_MODEL), row_p), pl.BlockSpec((ROW_TILE, D_MODEL), row_s)],
        out_shape=[jax.ShapeDtypeStruct((n_p, D_MODEL), F32), jax.ShapeDtypeStruct((n_s, D_MODEL), F32)],
        scratch_shapes=[pltpu.VMEM((D_MODEL, D_MODEL), BF16), pltpu.VMEM((D_MODEL, D_FF), BF16),
                        pltpu.VMEM((D_FF, D_MODEL), BF16), pltpu.VMEM((2, STAGE_ROWS, D_MODEL), F32),
                        pltpu.SemaphoreType.DMA((2,))],
        compiler_params=_params("arbitrary"),
        name="channel",
    )(*prompt, *latent, mods, gains, w_out, w_up, w_down)


def _row_iota(shape):
    return lax.broadcasted_iota(jnp.int32, shape, 0)


def _lane_iota(shape):
    return lax.broadcasted_iota(jnp.int32, shape, 1)


def _pair_lanes(shape):
    return _lane_iota(shape) < LANES // 2


def _cumsum_lanes(x, n_fwd):
    t = x.shape[1]
    si, ti = _row_iota((t, t)), _lane_iota((t, t))
    upper = jnp.where(si <= ti, 1.0, 0.0).astype(BF16)
    lower = jnp.where(si >= ti, 1.0, 0.0).astype(BF16)
    hi = x.astype(BF16)
    rest = x - hi.astype(F32)
    mid = rest.astype(BF16)
    lo = (rest - mid.astype(F32)).astype(BF16)
    pre = _dot(hi, upper) + _dot(mid, upper) + _dot(lo, upper)
    suf = _dot(hi, lower) + _dot(mid, lower) + _dot(lo, lower)
    return jnp.where(_row_iota(x.shape) < n_fwd, pre, suf)


def _cummax_lanes(x, n_fwd):
    t = x.shape[1]
    lane = _lane_iota(x.shape)
    pre, suf = x, x
    k = 1
    while k < t:
        pre = jnp.maximum(pre, jnp.where(lane >= k, pltpu.roll(pre, k, 1), -jnp.inf))
        suf = jnp.maximum(suf, jnp.where(lane < t - k, pltpu.roll(suf, t - k, 1), -jnp.inf))
        k *= 2
    return jnp.where(_row_iota(x.shape) < n_fwd, pre, suf)


def _columns(row_arrays):
    t = row_arrays[0].shape[1]
    used = sum(a.shape[0] for a in row_arrays)
    return jnp.concatenate(list(row_arrays) + [jnp.zeros((LANES - used, t), F32)], axis=0).T


def _dwconv_silu(x, w, b):
    t = x.shape[0]
    row = _row_iota(x.shape)
    acc = x * w[CONV_K // 2:CONV_K // 2 + 1, :] + b
    for j in range(CONV_K):
        d = j - CONV_K // 2
        if d == 0:
            continue
        shifted = pltpu.roll(x, (-d) % t, 0)
        valid = (row >= -d) if d < 0 else (row < t - d)
        acc = acc + jnp.where(valid, shifted, 0.0) * w[j:j + 1, :]
    return _silu(acc)


def _causal_exponent(expo, r0, k0, reverse):
    ti = r0 + _row_iota(expo.shape)
    si = k0 + _lane_iota(expo.shape)
    keep = (si >= ti) if reverse else (si <= ti)
    return jnp.where(keep, expo, -jnp.inf)


def _pair_split(x):
    first = _pair_lanes(x.shape)
    zero = jnp.zeros_like(x)
    return jnp.concatenate([jnp.where(first, x, zero), jnp.where(first, zero, x)], axis=0)


def _key_range(d, r0, seq):
    return (0, r0 + Q_TILE) if d == 0 else (r0, seq)


N_FUSED_INPUTS = 5
PROJ_CHUNK = 512

MLSTM_PIECES = ((0, 2 * A_QK), (2 * A_QK, A_V), (2 * A_QK + A_V, A_V))
MLSTM_GATES = (_GATES_LO, _Z_LO)
MLSTM_W_COLS = _GATES_LO + LANES
SSD_PIECES = ((_Z_LO + B_INNER, B_XBC), (_Z_LO, B_INNER))
SSD_GATES = (_DT_LO, _DT_LO + 2 * NH_B)
CTX_PIECES = tuple((sum(ODD_WIDTHS[:i]), wd) for i, wd in enumerate(ODD_WIDTHS[:-1])) + (
    (sum(ODD_WIDTHS[:-1]), LANES, ROPE_D),)


def _fused_project(step, fused_refs, scratch, pieces, gate_cols, mix):
    x_cur_ref, x_next_ref, mod_ref, gain_ref, w_ref = fused_refs
    w_s = scratch[0]
    n_head = 2 if gate_cols else 1
    n = len(pieces) + (1 if gate_cols else 0)
    sets = (scratch[n_head:n_head + n], scratch[n_head + n:n_head + 2 * n])

    def normed(x_ref):
        h = _rms(x_ref[...], gain_ref[0:1, :]) * (1.0 + mod_ref[1:2, :]) + mod_ref[0:1, :]
        return h.astype(BF16)

    def chunk_thunks(hb, dst):
        def gates():
            dst[0][...] = _dot_nt(scratch[1][...], hb)

        def columns(ref, c, src, width):
            def run():
                ref[:, c:c + width] = _dot(hb, w_s[:, src:src + width])
            return run

        thunks = [gates] if gate_cols else []
        off = 0
        for ref, piece in zip(dst[-len(pieces):], pieces):
            width = piece[1]
            thunks += [columns(ref, c, off + c, min(PROJ_CHUNK, width - c)) for c in range(0, width, PROJ_CHUNK)]
            off += width
        return thunks

    @pl.when(step == 0)
    def _():
        off = 0
        for piece in pieces:
            src, width = piece[0], piece[1]
            valid = piece[2] if len(piece) > 2 else width
            w_s[:, off:off + valid] = w_ref[:, src:src + valid]
            if valid < width:
                w_s[:, off + valid:off + width] = jnp.zeros((D_MODEL, width - valid), BF16)
            off += width
        if gate_cols:
            lo, hi = gate_cols
            gates = jnp.concatenate([w_ref[:, lo:hi].astype(F32), jnp.zeros((D_MODEL, LANES - (hi - lo)), F32)], axis=1)
            scratch[1][...] = gates.T[0:hi - lo, :].astype(BF16)
        for thunk in chunk_thunks(normed(x_cur_ref), sets[0]):
            thunk()

    for parity in range(2):
        @pl.when(step % 2 == parity)
        def _(parity=parity):
            cur = sets[parity]
            refs = list(cur[1:]) + [cur[0]] if gate_cols else list(cur)
            mix(*refs, chunk_thunks(normed(x_next_ref), sets[1 - parity]))


def _fused_specs(seq, nseq, layer, w_all, w_cols):
    nxt = lambda s: (jnp.minimum(s + 1, nseq - 1), 0)
    return [pl.BlockSpec((seq, D_MODEL), lambda s: (s, 0)), pl.BlockSpec((seq, D_MODEL), nxt),
            pl.BlockSpec((None, None, 6, D_MODEL), lambda s: (layer, 0, 0, 0)),
            pl.BlockSpec((None, 4, D_MODEL), lambda s: (layer, 0, 0)),
            pl.BlockSpec((None, D_MODEL, w_cols), lambda s: (layer // 2, 0, 0), pipeline_mode=pl.Buffered(1))]


def _fused_scratch(seq, pieces, n_gates):
    head = [pltpu.VMEM((D_MODEL, sum(p[1] for p in pieces)), BF16)]
    one_set = [pltpu.VMEM((seq, p[1]), F32) for p in pieces]
    if n_gates:
        head.append(pltpu.VMEM((n_gates, D_MODEL), BF16))
        one_set = [pltpu.VMEM((n_gates, seq), F32)] + one_set
    return head + one_set + one_set


def _run_share(side_work, stages_left):
    for _ in range(-(-len(side_work) // stages_left)):
        side_work.pop(0)()


def _mlstm_kernel(*refs, seq, has_state, emit_state, n_carried, slot, fused):
    assert not (has_state and emit_state)
    it = iter(refs)
    if fused:
        fused_refs = [next(it) for _ in range(N_FUSED_INPUTS)]
    else:
        qk_ref, v_ref, o_ref, g_ref = (next(it) for _ in range(4))
    cw_ref, cb_ref, gb_ref, anw_ref = (next(it) for _ in range(4))
    if has_state:
        c0_ref, m0_ref = next(it), next(it)
    for _ in range(n_carried):
        next(it)
    ha_ref = next(it)
    if emit_state:
        cn_ref, nn_ref, mn_ref = next(it), next(it), next(it)
        if n_carried == 0:
            for other in range(N_EVEN):
                if other != slot:
                    cn_ref[other] = jnp.zeros(cn_ref.shape[1:], F32)
                    nn_ref[other] = jnp.zeros(nn_ref.shape[1:], F32)
                    mn_ref[other] = jnp.zeros(mn_ref.shape[1:], F32)
            cn_ref, nn_ref, mn_ref = cn_ref.at[slot], nn_ref.at[slot], mn_ref.at[slot]

    def mix(qk_ref, v_ref, o_ref, g_ref, side_work=()):
        side_work = list(side_work)
        n_ch = 2 * NH_A
        log_i = g_ref[0:n_ch, :] + gb_ref[0:n_ch, 0:1]
        f_pre = g_ref[n_ch:2 * n_ch, :] + gb_ref[n_ch:2 * n_ch, 0:1]
        log_f = jnp.minimum(f_pre, 0.0) - jnp.log1p(jnp.exp(-jnp.abs(f_pre)))
        b = _cumsum_lanes(log_f, NH_A)
        a = log_i - b
        m_run = _cummax_lanes(a, NH_A)
        if has_state:
            m0 = m0_ref[:, 0:1]
            m_run = jnp.maximum(m_run, m0)
        else:
            m_run = jnp.maximum(m_run, 0.0)
        by_time = [m_run, jnp.exp(-(b + m_run))]
        if has_state:
            by_time.append(jnp.exp(m0 - m_run))
        if emit_state:
            fwd = _row_iota((n_ch, 1)) < NH_A
            b_last = jnp.where(fwd, b[:, seq - 1:seq], b[:, 0:1])
            m_last = jnp.where(fwd, m_run[:, seq - 1:seq], m_run[:, 0:1])
            mn_ref[...] = jnp.broadcast_to(b_last + m_last, (n_ch, LANES))
            by_time.append(jnp.exp(a - m_last))
        cols = _columns(by_time)

        ones = jnp.ones((seq, LANES), F32)
        for h in range(NH_A):
            _run_share(side_work, NH_A - h)
            cq = slice(h * DK_A, (h + 1) * DK_A)
            ck = slice(A_QK + h * DK_A, A_QK + (h + 1) * DK_A)
            cv = slice(h * DV_A, (h + 1) * DV_A)
            q = _dwconv_silu(qk_ref[:, cq], cw_ref[:, cq], cb_ref[:, cq])
            k = _dwconv_silu(qk_ref[:, ck], cw_ref[:, ck], cb_ref[:, ck]) * (DK_A ** -0.5)
            qb = q.astype(BF16)
            kb = k.astype(BF16)
            vh = v_ref[:, cv]
            vaug = jnp.concatenate([vh, ones], axis=1).astype(BF16)
            for r0 in range(0, seq, Q_TILE):
                rows = slice(r0, r0 + Q_TILE)
                s = _dot_nt(qb[rows], kb)
                hsum = None
                for d in range(2):
                    c = d * NH_A + h
                    k0, k1 = _key_range(d, r0, seq)
                    expo = _causal_exponent(a[c:c + 1, k0:k1] - cols[rows, c:c + 1], r0, k0, d == 1)
                    p = (s[:, k0:k1] * jnp.exp(expo)).astype(BF16)
                    acc = _dot(p, vaug[k0:k1])
                    if has_state:
                        acc = acc + (cols[rows, 2 * n_ch + c:2 * n_ch + c + 1]
                                     * _dot(qb[rows], c0_ref[d, h].astype(BF16)))
                    hd = acc[:, 0:DV_A] / jnp.maximum(jnp.abs(acc[:, DV_A:]), cols[rows, n_ch + c:n_ch + c + 1])
                    hsum = hd if hsum is None else hsum + hd
                og = jax.nn.sigmoid(o_ref[rows, cv]) * hsum
                ha_ref[rows, cv] = _rms(og, anw_ref[:, cv])
            if emit_state:
                for d in range(2):
                    c = d * NH_A + h
                    kw = k * cols[:, 2 * n_ch + c:2 * n_ch + c + 1]
                    cn_ref[d, h] = _dot(kw.T.astype(BF16), vh.astype(BF16))
                    nn_ref[d, h:h + 1, :] = jnp.sum(kw, axis=0, keepdims=True)

    if fused:
        _fused_project(pl.program_id(0), fused_refs, list(it), MLSTM_PIECES, MLSTM_GATES, mix)
    else:
        mix(qk_ref, v_ref, o_ref, g_ref)


def _state_out_specs(shapes, nseq, sub_layer, carried):
    out_specs, out_shape = [], []
    for shp in shapes:
        zeros = (0,) * len(shp)
        if carried is None:
            out_specs.append(pl.BlockSpec((None, N_EVEN) + shp, lambda s, z=zeros: (s, 0) + z))
        else:
            out_specs.append(pl.BlockSpec((None, None) + shp, lambda s, z=zeros: (s, sub_layer) + z))
        out_shape.append(jax.ShapeDtypeStruct((nseq, N_EVEN) + shp, F32))
    return out_specs, out_shape


MLSTM_STATE_SHAPES = ((2, NH_A, DK_A, DV_A), (2, NH_A, DK_A), (2 * NH_A, LANES))
SSD_STATE_SHAPES = ((2, NH_B, HP_B, DSTATE),)


def _fused_mixer(kernel_fn, name, x, mods, layer, gains, w_all, w_cols, params, seq, pieces, gate_cols,
                 out_width, state_shapes, carried):
    n = x.shape[0]
    nseq = n // seq
    sub_layer = layer // 2
    in_specs = _fused_specs(seq, nseq, layer, w_all, w_cols) + [_layer_spec(p, sub_layer) for p in params]
    args = [x, x, mods, gains, w_all, *params]
    state_specs, state_shape = _state_out_specs(state_shapes, nseq, sub_layer, carried)
    aliases = {}
    if carried is not None:
        aliases = {len(args) + i: 1 + i for i in range(len(state_shapes))}
        in_specs += [pl.BlockSpec(memory_space=pl.ANY)] * len(state_shapes)
        args += list(carried)
    return pl.pallas_call(
        functools.partial(kernel_fn, seq=seq, has_state=False, emit_state=True, n_carried=len(aliases),
                          slot=sub_layer, fused=True),
        grid=(nseq,), in_specs=in_specs,
        out_specs=[pl.BlockSpec((seq, out_width), lambda s: (s, 0))] + state_specs,
        out_shape=[jax.ShapeDtypeStruct((n, out_width), F32)] + state_shape,
        scratch_shapes=_fused_scratch(seq, pieces, gate_cols[1] - gate_cols[0]),
        input_output_aliases=aliases, compiler_params=_params("arbitrary"), name=name,
    )(*args)


def _mlstm(qk, v, o, gates, params, seq, sub_layer, state=None, carried=None):
    n = qk.shape[0]
    nseq = n // seq
    has_state = state is not None
    emit_state = not has_state
    aliases = {}
    row = lambda s: (s, 0)
    in_specs = [pl.BlockSpec((seq, 2 * A_QK), row), pl.BlockSpec((seq, A_V), row), pl.BlockSpec((seq, A_V), row),
                pl.BlockSpec((GATE_ROWS, seq), lambda s: (0, s))]
    in_specs += [_layer_spec(p, sub_layer) for p in params]
    args = [qk, v, o, gates, *params]
    out_specs = [pl.BlockSpec((seq, A_V), row)]
    out_shape = [jax.ShapeDtypeStruct((n, A_V), F32)]
    if has_state:
        c0_aug, m0 = state
        in_specs += [pl.BlockSpec((None, None) + c0_aug.shape[2:], lambda s: (s, sub_layer, 0, 0, 0, 0)),
                     pl.BlockSpec((None, None) + m0.shape[2:], lambda s: (s, sub_layer, 0, 0))]
        args += [c0_aug, m0]
    if emit_state:
        shapes = [(2, NH_A, DK_A, DV_A), (2, NH_A, DK_A), (2 * NH_A, LANES)]
        for shp in shapes:
            zeros = (0,) * len(shp)
            if carried is None:
                out_specs.append(pl.BlockSpec((None, N_EVEN) + shp, lambda s, z=zeros: (s, 0) + z))
            else:
                out_specs.append(pl.BlockSpec((None, None) + shp, lambda s, z=zeros: (s, sub_layer) + z))
            out_shape.append(jax.ShapeDtypeStruct((nseq, N_EVEN) + shp, F32))
        if carried is not None:
            aliases = {len(args) + i: 1 + i for i in range(len(shapes))}
            in_specs += [pl.BlockSpec(memory_space=pl.ANY)] * len(shapes)
            args += list(carried)
    return pl.pallas_call(
        functools.partial(_mlstm_kernel, seq=seq, has_state=has_state, emit_state=emit_state,
                          n_carried=len(aliases), slot=sub_layer, fused=False),
        grid=(nseq,), in_specs=in_specs, out_specs=out_specs, out_shape=out_shape,
        input_output_aliases=aliases, compiler_params=_params("arbitrary"), name="mlstm",
    )(*args)


def _ssd_kernel(*refs, seq, has_state, emit_state, n_carried, slot, fused):
    assert not (has_state and emit_state)
    it = iter(refs)
    if fused:
        fused_refs = [next(it) for _ in range(N_FUSED_INPUTS)]
    else:
        xbc_ref, z_ref, g_ref = (next(it) for _ in range(3))
    cw_ref, cb_ref, dtb_ref, alog_ref, dsk_ref, bnw_ref = (next(it) for _ in range(6))
    if has_state:
        s0_ref = next(it)
    for _ in range(n_carried):
        next(it)
    yb_ref = next(it)
    if emit_state:
        sn_ref = next(it)
        if n_carried == 0:
            for other in range(N_EVEN):
                if other != slot:
                    sn_ref[other] = jnp.zeros(sn_ref.shape[1:], F32)
            sn_ref = sn_ref.at[slot]

    gate_row0 = 0 if fused else 4 * NH_A

    def mix(xbc_ref, z_ref, g_ref, side_work=()):
        side_work = list(side_work)
        n_ch = 2 * NH_B
        dt = _softplus(g_ref[gate_row0:gate_row0 + n_ch, :] + dtb_ref[:, 0:1])
        acum = _cumsum_lanes(dt * (-jnp.exp(alog_ref[:, 0:1])), NH_B)
        key_shift = acum - jnp.log(dt)
        by_time = [acum]
        if has_state:
            by_time.append(jnp.exp(acum))
        if emit_state:
            fwd = _row_iota((n_ch, 1)) < NH_B
            a_last = jnp.where(fwd, acum[:, seq - 1:seq], acum[:, 0:1])
            by_time.append(jnp.exp(a_last - acum) * dt)
        cols = _columns(by_time)

        gw = R_B * HP_B
        first = _pair_lanes((seq, LANES))
        for g in range(NG_B):
            _run_share(side_work, NG_B - g)
            cx = slice(g * gw, (g + 1) * gw)
            cb_ = slice(B_INNER + g * DSTATE, B_INNER + (g + 1) * DSTATE)
            cc = slice(B_INNER + B_BC + g * DSTATE, B_INNER + B_BC + (g + 1) * DSTATE)
            xg = _dwconv_silu(xbc_ref[:, cx], cw_ref[:, cx], cb_ref[:, cx])
            bg = _dwconv_silu(xbc_ref[:, cb_], cw_ref[:, cb_], cb_ref[:, cb_]).astype(BF16)
            cg = _dwconv_silu(xbc_ref[:, cc], cw_ref[:, cc], cb_ref[:, cc]).astype(BF16)
            xbd = [_pair_split(xg[:, p * LANES:(p + 1) * LANES]).astype(BF16) for p in range(R_B // 2)]
            for r0 in range(0, seq, Q_TILE):
                rows = slice(r0, r0 + Q_TILE)
                cb_scores = _dot_nt(cg[rows], bg)
                ys = []
                for p in range(R_B // 2):
                    weights, inputs = [], []
                    for d in range(2):
                        k0, k1 = _key_range(d, r0, seq)
                        for i in range(2):
                            c = d * NH_B + g * R_B + 2 * p + i
                            expo = _causal_exponent(cols[rows, c:c + 1] - key_shift[c:c + 1, k0:k1], r0, k0, d == 1)
                            weights.append((cb_scores[:, k0:k1] * jnp.exp(expo)).astype(BF16))
                            inputs.append(xbd[p][i * seq + k0:i * seq + k1])
                    yp = _dot(jnp.concatenate(weights, axis=1), jnp.concatenate(inputs, axis=0))
                    if has_state:
                        h0 = g * R_B + 2 * p
                        for d in range(2):
                            c = d * NH_B + h0
                            carry = jnp.where(_pair_lanes((Q_TILE, LANES)), cols[rows, n_ch + c:n_ch + c + 1],
                                              cols[rows, n_ch + c + 1:n_ch + c + 2])
                            s0_pair = s0_ref[d, h0:h0 + 2].reshape(2 * HP_B, DSTATE).astype(BF16)
                            yp = yp + carry * _dot_nt(cg[rows], s0_pair)
                    ys.append(yp)
                y = jnp.concatenate(ys, axis=1) + dsk_ref[:, cx] * xg[rows]
                y = y * _silu(z_ref[rows, cx])
                yb_ref[rows, cx] = _rms(y, bnw_ref[:, cx])
            if emit_state:
                for d in range(2):
                    c0 = n_ch + d * NH_B + g * R_B
                    spread = jnp.concatenate([jnp.where(first, cols[:, c0 + 2 * p:c0 + 2 * p + 1],
                                                        cols[:, c0 + 2 * p + 1:c0 + 2 * p + 2])
                                              for p in range(R_B // 2)], axis=1)
                    sn = _dot((xg * spread).T.astype(BF16), bg)
                    for r in range(R_B):
                        sn_ref[d, g * R_B + r] = sn[r * HP_B:(r + 1) * HP_B, :]

    if fused:
        _fused_project(pl.program_id(0), fused_refs, list(it), SSD_PIECES, SSD_GATES, mix)
    else:
        mix(xbc_ref, z_ref, g_ref)


def _ssd(xbc, z, gates, params, seq, sub_layer, state=None, carried=None):
    n = xbc.shape[0]
    nseq = n // seq
    has_state = state is not None
    emit_state = not has_state
    aliases = {}
    row = lambda s: (s, 0)
    in_specs = [pl.BlockSpec((seq, B_XBC), row), pl.BlockSpec((seq, B_INNER), row),
                pl.BlockSpec((GATE_ROWS, seq), lambda s: (0, s))]
    in_specs += [_layer_spec(p, sub_layer) for p in params]
    args = [xbc, z, gates, *params]
    out_specs = [pl.BlockSpec((seq, B_INNER), row)]
    out_shape = [jax.ShapeDtypeStruct((n, B_INNER), F32)]
    state_spec = pl.BlockSpec((None, None, 2, NH_B, HP_B, DSTATE), lambda s: (s, sub_layer, 0, 0, 0, 0))
    if has_state:
        in_specs.append(state_spec)
        args.append(state)
    if emit_state:
        out_specs.append(state_spec if carried is not None else
                         pl.BlockSpec((None, N_EVEN, 2, NH_B, HP_B, DSTATE), lambda s: (s, 0, 0, 0, 0, 0)))
        out_shape.append(jax.ShapeDtypeStruct((nseq, N_EVEN, 2, NH_B, HP_B, DSTATE), F32))
        if carried is not None:
            aliases = {len(args): 1}
            in_specs.append(pl.BlockSpec(memory_space=pl.ANY))
            args.append(carried)
    return pl.pallas_call(
        functools.partial(_ssd_kernel, seq=seq, has_state=has_state, emit_state=emit_state, n_carried=len(aliases),
                          slot=sub_layer, fused=False),
        grid=(nseq,), in_specs=in_specs, out_specs=out_specs, out_shape=out_shape,
        input_output_aliases=aliases, compiler_params=_params("arbitrary"), name="ssd",
    )(*args)


def _shared_split(x, x_swapped, kh):
    first = _pair_lanes(x.shape)
    zero = jnp.zeros_like(x)
    top, bottom = (x, x_swapped) if kh == 0 else (x_swapped, x)
    return jnp.concatenate([jnp.where(first, top, zero), jnp.where(first, zero, bottom)], axis=0)


def _pair_probs(s, sinks=None, valid=None):
    n_keys = s.shape[1] // 2
    probs, maxes = [], []
    for i in range(2):
        si = s[:, i * n_keys:(i + 1) * n_keys]
        if valid is not None:
            si = jnp.where(valid, si, -jnp.inf)
        m = jnp.max(si, axis=1, keepdims=True)
        if sinks is not None:
            m = jnp.maximum(m, sinks[i])
        probs.append(jnp.exp(si - m))
        maxes.append(m)
    return jnp.concatenate(probs, axis=1).astype(BF16), maxes


def _pair_output(p, maxes, vbd, sinks=None):
    o = _dot(p, vbd)
    den = o[:, LANES:]
    if sinks is not None:
        den = den + jnp.where(_pair_lanes(den.shape), jnp.exp(sinks[0] - maxes[0]), jnp.exp(sinks[1] - maxes[1]))
    return o[:, :LANES] / den


def _run_pairs(items, valid=None, side_work=()):
    side_work = list(side_work)
    s_next = items[0][0]()
    for idx, (_, values, sinks, out_ref, cols) in enumerate(items):
        _run_share(side_work, len(items) - idx)
        s_cur = s_next
        if idx + 1 < len(items):
            s_next = items[idx + 1][0]()
        p, maxes = _pair_probs(s_cur, sinks, valid if sinks is not None else None)
        out_ref[:, cols] = _pair_output(p, maxes, values(), sinks)


def _pair_sinks(sink_ref, n):
    return sink_ref[0:1, n:n + 1], sink_ref[0:1, n + 1:n + 2]


def _mla_queries(qa_ref, qan_ref, wqb_ref):
    return _dot(_rms(qa_ref[...], qan_ref[...]).astype(BF16), wqb_ref[...]) * MLA_SCALE


def _attn_ctx_kernel(*refs, seq, n_carried, slot, fused):
    it = iter(refs)
    if fused:
        fused_refs = [next(it) for _ in range(N_FUSED_INPUTS)]
    else:
        proj_refs = [next(it) for _ in range(len(ODD_WIDTHS))]
    sink_ref, qan_ref, kvn_ref, wqb_ref, wkvb_ref = (next(it) for _ in range(5))
    for _ in range(n_carried):
        next(it)
    oc_ref, od_ref, ckv_ref = next(it), next(it), next(it)
    if fused:
        new_k_ref, new_v_ref, new_kpe_ref = next(it), next(it), next(it)
    if n_carried == 0:
        for other in range(N_ODD):
            if other != slot:
                ckv_ref[other] = jnp.zeros(ckv_ref.shape[1:], F32)
        ckv_ref = ckv_ref.at[slot]

    def mix(qc_ref, kc_ref, vc_ref, qa_ref, kva_ref, kpe_ref, side_work=()):
        if fused:
            new_k_ref[...] = kc_ref[...]
            new_v_ref[...] = vc_ref[...]
            new_kpe_ref[...] = kpe_ref[...]
        ones_bd = _pair_split(jnp.ones((seq, LANES), F32))
        kc, vc = kc_ref[...], vc_ref[...]
        kc_sw, vc_sw = pltpu.roll(kc, HD_C, 1), pltpu.roll(vc, HD_C, 1)
        qd = _mla_queries(qa_ref, qan_ref, wqb_ref)
        ckv = _rms(kva_ref[...], kvn_ref[...])
        ckv_ref[...] = ckv
        kv = _dot(ckv.astype(BF16), wkvb_ref[...])
        kpe = kpe_ref[...]
        kpe_bd = jnp.concatenate([kpe, pltpu.roll(kpe, ROPE_D, 1)], axis=0)
        nope_w = NH_D * NOPE_D
        items = []
        for kh in range(NKV_C):
            for n in range(kh * G_C, (kh + 1) * G_C, 2):
                cols = slice(n * HD_C, (n + 2) * HD_C)
                items.append((lambda cols=cols, kh=kh: _dot_nt((qc_ref[:, cols] * (HD_C ** -0.5)).astype(BF16),
                                                               _shared_split(kc, kc_sw, kh).astype(BF16)),
                              lambda kh=kh: jnp.concatenate([_shared_split(vc, vc_sw, kh), ones_bd], axis=1).astype(BF16),
                              _pair_sinks(sink_ref, n), oc_ref, cols))
        for i in range(NH_D // 2):
            cols = slice(i * LANES, (i + 1) * LANES)
            vcols = slice(nope_w + i * LANES, nope_w + (i + 1) * LANES)
            items.append((lambda cols=cols, vcols=vcols: _dot_nt(
                              jnp.concatenate([qd[:, cols], qd[:, vcols]], axis=1).astype(BF16),
                              jnp.concatenate([_pair_split(kv[:, cols]), kpe_bd], axis=1).astype(BF16)),
                          lambda vcols=vcols: jnp.concatenate([_pair_split(kv[:, vcols]), ones_bd], axis=1).astype(BF16),
                          None, od_ref, cols))
        _run_pairs(items, side_work=side_work)

    if fused:
        _fused_project(pl.program_id(0), fused_refs, list(it), CTX_PIECES, None, mix)
    else:
        mix(*proj_refs)


def _attn_ctx(x, mods, layer, gains, w_all, params, seq, carried=None):
    n = x.shape[0]
    nseq = n // seq
    sub_layer = layer // 2
    row = lambda s: (s, 0)
    in_specs = _fused_specs(seq, nseq, layer, w_all, w_all.shape[2]) + [_layer_spec(p, sub_layer) for p in params]
    args = [x, x, mods, gains, w_all, *params]
    half = NH_C * HD_C
    aliases = {}
    if carried is None:
        ckv_spec = pl.BlockSpec((None, N_ODD, seq, KV_RANK), lambda s: (s, 0, 0, 0))
    else:
        ckv_spec = pl.BlockSpec((None, None, seq, KV_RANK), lambda s: (s, sub_layer, 0, 0))
        aliases = {len(args): 2}
        in_specs.append(pl.BlockSpec(memory_space=pl.ANY))
        args.append(carried)
    kv_w = NKV_C * HD_C
    return pl.pallas_call(
        functools.partial(_attn_ctx_kernel, seq=seq, n_carried=len(aliases), slot=sub_layer, fused=True),
        grid=(nseq,), in_specs=in_specs,
        out_specs=[pl.BlockSpec((seq, half), row), pl.BlockSpec((seq, half), row), ckv_spec,
                   pl.BlockSpec((seq, kv_w), row), pl.BlockSpec((seq, kv_w), row), pl.BlockSpec((seq, LANES), row)],
        out_shape=[jax.ShapeDtypeStruct((n, half), F32), jax.ShapeDtypeStruct((n, half), F32),
                   jax.ShapeDtypeStruct((nseq, N_ODD, seq, KV_RANK), F32),
                   jax.ShapeDtypeStruct((n, kv_w), F32), jax.ShapeDtypeStruct((n, kv_w), F32),
                   jax.ShapeDtypeStruct((n, LANES), F32)],
        scratch_shapes=_fused_scratch(seq, CTX_PIECES, 0),
        input_output_aliases=aliases, compiler_params=_params("arbitrary"), name="attn_ctx",
    )(*args)


def _rope(x, cos, sin, half):
    parts = []
    lane = _lane_iota((x.shape[0], LANES))
    first = (lane & (2 * half - 1)) < half
    for i in range(x.shape[1] // LANES):
        xi = x[:, i * LANES:(i + 1) * LANES]
        partner = jnp.where(first, -pltpu.roll(xi, LANES - half, 1), pltpu.roll(xi, half, 1))
        parts.append(xi * cos + partner * sin)
    return parts[0] if len(parts) == 1 else jnp.concatenate(parts, axis=1)


def _attn_lat_kernel(qc_ref, qa_ref, ropeq_ref, kc_ref, vc_ref, kva_ref, kpe_ref, kctx_ref, vctx_ref, ckvctx_ref,
                     kpectx_ref, rope_ref, sink_ref, qan_ref, kvn_ref, wqb_ref, wkvb_ref, oc_ref, od_ref,
                     kwin_s, vwin_s, kext_s, vext_s, *, seq, past):
    qi = pl.program_id(1)
    nope_w = NH_D * NOPE_D
    n_all = past + seq
    ctx0 = 2 * WINDOW + seq

    @pl.when(qi == 0)
    def _():
        zeros = jnp.zeros((WINDOW, LANES), BF16)
        for ref, lat, ctx in ((kwin_s, _rope(kc_ref[...], rope_ref[0], rope_ref[1], HD_C // 2), kctx_ref[...]),
                              (vwin_s, vc_ref[...], vctx_ref[...])):
            lat_sw, ctx_sw = pltpu.roll(lat, HD_C, 1), pltpu.roll(ctx, HD_C, 1)
            for kh in range(NKV_C):
                lat_bd = _shared_split(lat, lat_sw, kh).astype(BF16)
                ctx_bd = _shared_split(ctx, ctx_sw, kh).astype(BF16)
                for i in range(2):
                    ref[kh, i, 0:WINDOW, :] = zeros
                    ref[kh, i, WINDOW:WINDOW + seq, :] = lat_bd[i * seq:(i + 1) * seq]
                    ref[kh, i, WINDOW + seq:ctx0, :] = zeros
                    ref[kh, i, ctx0:, :] = ctx_bd[i * past:(i + 1) * past]
        ckv = _rms(kva_ref[...], kvn_ref[...])
        kv = jnp.concatenate([_dot(ckvctx_ref[...].astype(BF16), wkvb_ref[...]),
                              _dot(ckv.astype(BF16), wkvb_ref[...])], axis=0)
        kpe = jnp.concatenate([kpectx_ref[...], _rope(kpe_ref[...], rope_ref[2], rope_ref[3], ROPE_D // 2)], axis=0)
        kpe_bd = jnp.concatenate([kpe, pltpu.roll(kpe, ROPE_D, 1)], axis=0).astype(BF16)
        ones_bd = _pair_split(jnp.ones((n_all, LANES), F32)).astype(BF16)
        for i in range(NH_D // 2):
            kext_s[i, :, 0:LANES] = _pair_split(kv[:, i * LANES:(i + 1) * LANES]).astype(BF16)
            kext_s[i, :, LANES:] = kpe_bd
            vext_s[i, :, 0:LANES] = _pair_split(kv[:, nope_w + i * LANES:nope_w + (i + 1) * LANES]).astype(BF16)
            vext_s[i, :, LANES:] = ones_bd

    r0 = pl.multiple_of(qi * Q_TILE, Q_TILE)
    nloc = Q_TILE + 2 * WINDOW
    n_keys = nloc + past
    qr = _rope(qc_ref[...], ropeq_ref[0], ropeq_ref[1], HD_C // 2) * (HD_C ** -0.5)
    ti = r0 + _row_iota((Q_TILE, n_keys))
    col = _lane_iota((Q_TILE, n_keys))
    pos = r0 - WINDOW + col
    valid = (col >= nloc) | ((jnp.abs(ti - pos) <= WINDOW) & (pos >= 0) & (pos < seq))
    ones_bd = _pair_split(jnp.ones((n_keys, LANES), F32)).astype(BF16)
    qd = _mla_queries(qa_ref, qan_ref, wqb_ref)
    q_pe = _rope(qd[:, nope_w:], ropeq_ref[2], ropeq_ref[3], ROPE_D // 2)

    def banded(ref, kh):
        return jnp.concatenate([ref[kh, 0, pl.ds(r0, nloc), :], ref[kh, 0, ctx0:, :],
                                ref[kh, 1, pl.ds(r0, nloc), :], ref[kh, 1, ctx0:, :]], axis=0)

    items = []
    for kh in range(NKV_C):
        for n in range(kh * G_C, (kh + 1) * G_C, 2):
            cols = slice(n * HD_C, (n + 2) * HD_C)
            items.append((lambda cols=cols, kh=kh: _dot_nt(qr[:, cols].astype(BF16), banded(kwin_s, kh)),
                          lambda kh=kh: jnp.concatenate([banded(vwin_s, kh), ones_bd], axis=1),
                          _pair_sinks(sink_ref, n), oc_ref, cols))
    for i in range(NH_D // 2):
        cols = slice(i * LANES, (i + 1) * LANES)
        items.append((lambda cols=cols, i=i: _dot_nt(jnp.concatenate([qd[:, cols], q_pe[:, cols]], axis=1).astype(BF16),
                                                     kext_s[i]),
                      lambda i=i: vext_s[i], None, od_ref, cols))
    _run_pairs(items, valid)


def _attn_lat(proj, caches, rope, params, seq, sub_layer):
    qc, kc, vc, qa, kva, kpe = proj
    n = qc.shape[0]
    past = caches[0].shape[2]
    nq = seq // Q_TILE
    qrow = lambda b, q: (b * nq + q, 0)
    krow = lambda b, q: (b, 0)
    kvw = NKV_C * HD_C
    in_specs = [pl.BlockSpec((Q_TILE, NH_C * HD_C), qrow), pl.BlockSpec((Q_TILE, Q_RANK), qrow),
                pl.BlockSpec((4, Q_TILE, LANES), lambda b, q: (0, q, 0)),
                pl.BlockSpec((seq, kvw), krow), pl.BlockSpec((seq, kvw), krow),
                pl.BlockSpec((seq, KV_RANK), krow), pl.BlockSpec((seq, LANES), krow)]
    in_specs += [pl.BlockSpec((None, None, past, LANES), lambda b, q: (b, sub_layer, 0, 0)) for _ in caches]
    in_specs += [pl.BlockSpec(rope.shape, lambda b, q: (0, 0, 0))]
    in_specs += [_layer_spec(p, sub_layer) for p in params]
    half = NH_C * HD_C
    win_rows = 2 * WINDOW + seq + past
    return pl.pallas_call(
        functools.partial(_attn_lat_kernel, seq=seq, past=past),
        grid=(n // seq, nq), in_specs=in_specs,
        out_specs=[pl.BlockSpec((Q_TILE, half), qrow), pl.BlockSpec((Q_TILE, half), qrow)],
        out_shape=[jax.ShapeDtypeStruct((n, half), F32), jax.ShapeDtypeStruct((n, half), F32)],
        scratch_shapes=[pltpu.VMEM((NKV_C, 2, win_rows, LANES), BF16), pltpu.VMEM((NKV_C, 2, win_rows, LANES), BF16),
                        pltpu.VMEM((NH_D // 2, 2 * (past + seq), 2 * LANES), BF16),
                        pltpu.VMEM((NH_D // 2, 2 * (past + seq), 2 * LANES), BF16)],
        compiler_params=_params("arbitrary", "arbitrary"), name="attn_lat",
    )(qc, qa, rope, kc, vc, kva, kpe, *caches, rope, *params)


def _pad_lanes(x, width=LANES):
    return jnp.pad(x, [(0, 0)] * (x.ndim - 1) + [(0, width - x.shape[-1])])


def _on_lanes(x):
    return jnp.broadcast_to(x[..., None], x.shape + (LANES,))


def _mla_query_weights(w):
    lead = w.shape[:-1]
    w4 = w.reshape(lead + (NH_D // 2, 2, NOPE_D + ROPE_D))
    nope = w4[..., :NOPE_D].reshape(lead + (NH_D * NOPE_D,))
    pe = _pad_lanes(w4[..., NOPE_D:].reshape(lead + (NH_D // 2, 2 * ROPE_D)))
    return jnp.concatenate([nope, pe.reshape(lead + (NH_D // 2 * LANES,))], axis=-1).astype(BF16)


def _mla_kv_weights(w):
    lead = w.shape[:-1]
    w3 = w.reshape(lead + (NH_D, NOPE_D + V_D))
    return jnp.concatenate([w3[..., :NOPE_D].reshape(lead + (NH_D * NOPE_D,)),
                            w3[..., NOPE_D:].reshape(lead + (NH_D * V_D,))], axis=-1).astype(BF16)


def _rope_tables(rows):
    def table(rot_dim):
        quarter = rot_dim // 4
        inv = ROPE_BASE ** (-jnp.arange(quarter, dtype=F32) / quarter)
        r = jnp.repeat(jnp.arange(rows, dtype=F32), GRID_W)
        col = jnp.tile(jnp.arange(GRID_W, dtype=F32), rows)
        ang = jnp.concatenate([r[:, None] * inv, col[:, None] * inv], axis=-1)
        reps = LANES // (rot_dim // 2)
        return jnp.tile(jnp.cos(ang), (1, reps)), jnp.tile(jnp.sin(ang), (1, reps))
    cos_c, sin_c = table(HD_C)
    cos_d, sin_d = table(ROPE_D)
    return jnp.stack([cos_c, sin_c, cos_d, sin_d])


def kernel(x_prompt, x_sample, c, state_mlstm_C, state_mlstm_n, state_mlstm_m, state_ssd, cache_gqa_k, cache_gqa_v,
           cache_mla_ckv, cache_mla_kpe, c_ctx, w_ada, b_ada, norm_g, w_up, w_down, w_in_even, conv_a_w, conv_a_b,
           conv_b_w, conv_b_b, gate_b, a_norm_w, dt_bias, a_log, d_skip, b_norm_w, w_out_even, w_in_odd, sink,
           q_a_norm, kv_a_norm, w_q_b, w_kv_b, w_out_odd):
    xp = x_prompt.reshape(BATCH * SEQ, D_MODEL)
    xs = x_sample.reshape(DEC_BATCH * DEC_SEQ, D_MODEL)
    cond = jnp.concatenate([c_ctx[None, :], c, jnp.zeros((MOD_ROWS - 1 - DEC_BATCH, D_MODEL), F32)], axis=0)
    mods = _modulations(cond, w_ada, b_ada)
    rope = _rope_tables(DEC_SEQ // GRID_W)

    w_even = w_in_even.astype(BF16)
    a_params = (conv_a_w, conv_a_b[:, None, :], _on_lanes(gate_b), a_norm_w[:, None, :])
    b_params = (conv_b_w, conv_b_b[:, None, :], _on_lanes(dt_bias.reshape(N_EVEN, 2 * NH_B)),
                _on_lanes(a_log.reshape(N_EVEN, 2 * NH_B)), jnp.repeat(d_skip, HP_B, axis=1)[:, None, :],
                b_norm_w[:, None, :])
    n0 = state_mlstm_n[..., None]
    mem_in = (jnp.concatenate([state_mlstm_C, jnp.broadcast_to(n0, n0.shape[:-1] + (LANES,))], axis=-1),
              _on_lanes(state_mlstm_m.reshape(DEC_BATCH, N_EVEN, 2 * NH_A)))
    w_odd = w_in_odd.astype(BF16)
    o_params = (_pad_lanes(sink)[:, None, :], q_a_norm[:, None, :], kv_a_norm[:, None, :],
                _mla_query_weights(w_q_b), _mla_kv_weights(w_kv_b))
    caches = (cache_gqa_k.reshape(DEC_BATCH, N_ODD, PAST_LEN, NKV_C * HD_C),
              cache_gqa_v.reshape(DEC_BATCH, N_ODD, PAST_LEN, NKV_C * HD_C),
              cache_mla_ckv, _pad_lanes(cache_mla_kpe))

    new_k, new_v, new_kpe = [], [], []
    mem_state, ssd_state, ckv_state = None, None, None
    for l in range(DEPTH):
        j = l // 2
        if l % 2 == 0:
            a1p, *mem_state = _fused_mixer(_mlstm_kernel, "mlstm", xp, mods, l, norm_g, w_even, MLSTM_W_COLS, a_params,
                                           SEQ, MLSTM_PIECES, MLSTM_GATES, A_V, MLSTM_STATE_SHAPES, mem_state)
            a2p, *ssd_state = _fused_mixer(_ssd_kernel, "ssd", xp, mods, l, norm_g, w_even, w_even.shape[2], b_params,
                                           SEQ, SSD_PIECES, SSD_GATES, B_INNER, SSD_STATE_SHAPES, ssd_state)
            qk, v, o, z, xbc, g = _project(xs, mods, l, True, norm_g, w_even, EVEN_WIDTHS, EVEN_REGROUP, EVEN_GATE_COLS)
            a1s, = _mlstm(qk, v, o, g, a_params, DEC_SEQ, j, state=mem_in)
            a2s, = _ssd(xbc, z, g, b_params, DEC_SEQ, j, state=state_ssd)
            w_out = w_out_even
        else:
            a1p, a2p, ckv_state, kc, vc, kpe = _attn_ctx(xp, mods, l, norm_g, w_odd, o_params, SEQ, carried=ckv_state)
            new_k.append(kc.reshape(BATCH, SEQ, NKV_C, HD_C))
            new_v.append(vc.reshape(BATCH, SEQ, NKV_C, HD_C))
            new_kpe.append(kpe[:, :ROPE_D].reshape(BATCH, SEQ, ROPE_D))
            proj = _project(xs, mods, l, True, norm_g, w_odd, ODD_WIDTHS, ODD_REGROUP)
            a1s, a2s = _attn_lat(proj, caches, rope, o_params, DEC_SEQ, j)
            w_out = w_out_odd
        xp, xs = _channel((a1p, a2p, xp), (a1s, a2s, xs), mods, l, norm_g, w_out, w_up, w_down)

    new_c, new_n, new_m = mem_state
    return (xp.reshape(BATCH, SEQ, D_MODEL), xs.reshape(DEC_BATCH, DEC_SEQ, D_MODEL),
            new_c, new_n, new_m[..., 0].reshape(BATCH, N_EVEN, 2, NH_A), ssd_state[0],
            jnp.stack(new_k, axis=1), jnp.stack(new_v, axis=1), ckv_state, jnp.stack(new_kpe, axis=1))
```

```python
import functools

import jax
import jax.numpy as jnp
from jax import lax
from jax.experimental import pallas as pl
from jax.experimental.pallas import tpu as pltpu

F32 = jnp.float32
BF16 = jnp.bfloat16

D_MODEL = 1024
BATCH = 32
SEQ = 256
DEPTH = 4
DEC_BATCH = 2
DEC_SEQ = 1024
PAST_LEN = 256
GRID_W = 64
N_EVEN = (DEPTH + 1) // 2
N_ODD = DEPTH // 2
EPS = 1e-6
CONV_K = 5
NH_A = 4
DK_A = 128
DV_A = 128
A_QK = NH_A * DK_A
A_V = NH_A * DV_A
NH_B = 8
HP_B = 64
DSTATE = 128
NG_B = 2
R_B = NH_B // NG_B
B_INNER = NH_B * HP_B
B_BC = NG_B * DSTATE
B_XBC = B_INNER + 2 * B_BC
NH_C = 8
NKV_C = 2
G_C = NH_C // NKV_C
HD_C = 64
WINDOW = 128
NH_D = 8
Q_RANK = 256
KV_RANK = 128
NOPE_D = 64
ROPE_D = 32
V_D = 64
MLA_SCALE = (NOPE_D + ROPE_D) ** -0.5
D_FF = 4 * D_MODEL
ROPE_BASE = 10000.0

LANES = 128
VMEM_LIMIT_BYTES = 56 * 1024 * 1024
ROW_TILE = 512
FF_TILE = 1024
SUB_ROWS = 256
STAGE_ROWS = 512
Q_TILE = 256
ADA_TILE = 1536
MOD_ROWS = 8
GATE_ROWS = 4 * NH_A + 2 * NH_B

EVEN_WIDTHS = (2 * A_QK, A_V, A_V, B_INNER, B_XBC)
ODD_WIDTHS = (NH_C * HD_C, NKV_C * HD_C, NKV_C * HD_C, Q_RANK, KV_RANK, LANES)
_GATES_LO = 2 * A_QK + 2 * A_V
_Z_LO = _GATES_LO + 4 * NH_A
_DT_LO = _Z_LO + B_INNER + B_XBC
EVEN_REGROUP = ((0, 0, _GATES_LO), (_GATES_LO, _Z_LO, B_INNER + B_XBC))
EVEN_GATE_COLS = ((_GATES_LO, _Z_LO), (_DT_LO, _DT_LO + 2 * NH_B))
ODD_IN = sum(ODD_WIDTHS) - LANES + ROPE_D
ODD_REGROUP = ((0, 0, ODD_IN), (ODD_IN, None, LANES - ROPE_D))

_NT = (((1,), (1,)), ((), ()))


def _params(*sem):
    return pltpu.CompilerParams(dimension_semantics=sem, vmem_limit_bytes=VMEM_LIMIT_BYTES)


def _rms(x, g):
    return x * lax.rsqrt(jnp.mean(x * x, axis=-1, keepdims=True) + EPS) * g


def _silu(x):
    return x * jax.nn.sigmoid(x)


def _softplus(x):
    return jnp.maximum(x, 0.0) + jnp.log1p(jnp.exp(-jnp.abs(x)))


def _dot(a, b):
    return jnp.dot(a, b, preferred_element_type=F32)


def _dot_nt(a, b):
    return lax.dot_general(a, b, _NT, preferred_element_type=F32)


def _layer_spec(arr, layer):
    tail = arr.shape[1:]
    zeros = (0,) * len(tail)
    return pl.BlockSpec((None,) + tail, lambda *_: (layer,) + zeros)


def _ada_kernel(c_ref, w_ref, b_ref, o_ref):
    s = _silu(c_ref[...]).astype(BF16)
    o_ref[...] = _dot(s, w_ref[...].astype(BF16)) + b_ref[...]


def _modulations(cond, w_ada, b_ada):
    out = pl.pallas_call(
        _ada_kernel,
        grid=(DEPTH, 6 * D_MODEL // ADA_TILE),
        in_specs=[pl.BlockSpec((MOD_ROWS, D_MODEL), lambda l, n: (0, 0)),
                  pl.BlockSpec((None, D_MODEL, ADA_TILE), lambda l, n: (l, 0, n)),
                  pl.BlockSpec((None, 1, ADA_TILE), lambda l, n: (l, 0, n))],
        out_specs=pl.BlockSpec((None, MOD_ROWS, ADA_TILE), lambda l, n: (l, 0, n)),
        out_shape=jax.ShapeDtypeStruct((DEPTH, MOD_ROWS, 6 * D_MODEL), F32),
        compiler_params=_params("arbitrary", "arbitrary"),
        name="ada",
    )(cond, w_ada, b_ada.reshape(DEPTH, 1, 6 * D_MODEL))
    return out.reshape(DEPTH, MOD_ROWS, 6, D_MODEL)


def _mod_spec(layer, latent):
    if latent:
        per_seq = DEC_SEQ // ROW_TILE
        return pl.BlockSpec((None, None, 6, D_MODEL), lambda i, *_: (layer, 1 + i // per_seq, 0, 0))
    return pl.BlockSpec((None, None, 6, D_MODEL), lambda i, *_: (layer, 0, 0, 0))


def _proj_kernel(x_ref, mod_ref, g_ref, w_ref, *rest, widths, regroup, gate_cols):
    n_out = len(widths) + (1 if gate_cols else 0)
    o_refs, w_s = rest[:n_out], rest[n_out]

    @pl.when(pl.program_id(0) == 0)
    def _():
        for dst, src, width in regroup:
            if src is None:
                w_s[:, dst:dst + width] = jnp.zeros((D_MODEL, width), BF16)
            else:
                w_s[:, dst:dst + width] = w_ref[:, src:src + width]
        if gate_cols:
            pieces = [w_ref[:, lo:hi].astype(F32) for lo, hi in gate_cols]
            n_gates = sum(hi - lo for lo, hi in gate_cols)
            gates = jnp.concatenate(pieces + [jnp.zeros((D_MODEL, LANES - n_gates), F32)], axis=1)
            rest[n_out + 1][...] = gates.T[0:n_gates, :].astype(BF16)

    h = _rms(x_ref[...], g_ref[0:1, :]) * (1.0 + mod_ref[1:2, :]) + mod_ref[0:1, :]
    hb = h.astype(BF16)
    off = 0
    for o_ref, wd in zip(o_refs, widths):
        o_ref[...] = _dot(hb, w_s[:, off:off + wd])
        off += wd
    if gate_cols:
        o_refs[-1][...] = _dot_nt(rest[n_out + 1][...], hb)


def _project(x, mods, layer, latent, gains, w_all, widths, regroup, gate_cols=()):
    n = x.shape[0]
    out_specs = [pl.BlockSpec((ROW_TILE, wd), lambda i: (i, 0)) for wd in widths]
    out_shape = [jax.ShapeDtypeStruct((n, wd), F32) for wd in widths]
    scratch = [pltpu.VMEM((D_MODEL, sum(widths)), BF16)]
    if gate_cols:
        n_gates = sum(hi - lo for lo, hi in gate_cols)
        out_specs.append(pl.BlockSpec((n_gates, ROW_TILE), lambda i: (0, i)))
        out_shape.append(jax.ShapeDtypeStruct((n_gates, n), F32))
        scratch.append(pltpu.VMEM((n_gates, D_MODEL), BF16))
    w_spec = pl.BlockSpec((None,) + w_all.shape[1:], lambda i: (layer // 2, 0, 0), pipeline_mode=pl.Buffered(1))
    return pl.pallas_call(
        functools.partial(_proj_kernel, widths=widths, regroup=regroup, gate_cols=gate_cols),
        grid=(n // ROW_TILE,),
        in_specs=[pl.BlockSpec((ROW_TILE, D_MODEL), lambda i: (i, 0)), _mod_spec(layer, latent),
                  _layer_spec(gains, layer), w_spec],
        out_specs=out_specs, out_shape=out_shape, scratch_shapes=scratch,
        compiler_params=_params("arbitrary"),
        name="proj",
    )(x, mods, gains, w_all)


def _weight_chunks(layer, sub_layer, wo_hbm, wu_hbm, wd_hbm, wo_s, wu_s, wd_s):
    chunks = []
    for r in range(0, D_MODEL, STAGE_ROWS):
        chunks.append((wo_hbm.at[sub_layer, pl.ds(r, STAGE_ROWS), :], wo_s.at[pl.ds(r, STAGE_ROWS), :]))
    for r in range(0, D_MODEL, STAGE_ROWS):
        for c in range(0, D_FF, D_MODEL):
            chunks.append((wu_hbm.at[layer, pl.ds(r, STAGE_ROWS), pl.ds(c, D_MODEL)],
                           wu_s.at[pl.ds(r, STAGE_ROWS), pl.ds(c, D_MODEL)]))
    for r in range(0, D_FF, STAGE_ROWS):
        chunks.append((wd_hbm.at[layer, pl.ds(r, STAGE_ROWS), :], wd_s.at[pl.ds(r, STAGE_ROWS), :]))
    return chunks


def _channel_kernel(a1p_ref, a2p_ref, xp_ref, a1s_ref, a2s_ref, xs_ref, mod_ref, g_ref, wo_hbm, wu_hbm, wd_hbm,
                    op_ref, os_ref, wo_s, wu_s, wd_s, stage, sem, *, layer, sub_layer, prompt_steps):
    step = pl.program_id(0)

    @pl.when(step == 0)
    def _():
        chunks = _weight_chunks(layer, sub_layer, wo_hbm, wu_hbm, wd_hbm, wo_s, wu_s, wd_s)
        copies = [pltpu.make_async_copy(src, stage.at[k % 2], sem.at[k % 2]) for k, (src, _) in enumerate(chunks)]
        copies[0].start()
        for k, (_, dst) in enumerate(chunks):
            if k + 1 < len(chunks):
                copies[k + 1].start()
            copies[k].wait()
            dst[...] = stage[k % 2].astype(BF16)

    def rows_block(a1_ref, a2_ref, x_ref, o_ref):
        half = a1_ref.shape[1]
        tiles = range(0, D_FF, FF_TILE)
        blocks = [slice(r0, r0 + SUB_ROWS) for r0 in range(0, ROW_TILE, SUB_ROWS)]

        def prologue(rows):
            y = (_dot(a1_ref[rows, :].astype(BF16), wo_s[0:half, :])
                 + _dot(a2_ref[rows, :].astype(BF16), wo_s[half:, :]))
            x1 = x_ref[rows, :] + mod_ref[2:3, :] * _rms(y, g_ref[1:2, :])
            h = (_rms(x1, g_ref[2:3, :]) * (1.0 + mod_ref[4:5, :]) + mod_ref[3:4, :]).astype(BF16)
            return x1, h

        def mlp_tile(h, c):
            u = jnp.square(jnp.maximum(_dot(h, wu_s[:, c:c + FF_TILE]), 0.0)).astype(BF16)
            return _dot(u, wd_s[c:c + FF_TILE, :])

        def epilogue(rows, x1, acc):
            o_ref[rows, :] = x1 + mod_ref[5:6, :] * _rms(acc, g_ref[3:4, :])

        ready = {0: prologue(blocks[0])}
        done = None
        for b, rows in enumerate(blocks):
            x1, h = ready.pop(b)
            acc = None
            for t, c in enumerate(tiles):
                part = mlp_tile(h, c)
                acc = part if acc is None else acc + part
                if t == 0 and b + 1 < len(blocks):
                    ready[b + 1] = prologue(blocks[b + 1])
                if t == 0 and done is not None:
                    epilogue(*done)
                    done = None
            done = (rows, x1, acc)
        epilogue(*done)

    @pl.when(step < prompt_steps)
    def _():
        rows_block(a1p_ref, a2p_ref, xp_ref, op_ref)

    @pl.when(step >= prompt_steps)
    def _():
        rows_block(a1s_ref, a2s_ref, xs_ref, os_ref)


def _channel(prompt, latent, mods, layer, gains, w_out, w_up, w_down):
    n_p, n_s = prompt[2].shape[0], latent[2].shape[0]
    steps_p, steps_s = n_p // ROW_TILE, n_s // ROW_TILE
    per_seq = DEC_SEQ // ROW_TILE
    row_p = lambda i: (jnp.minimum(i, steps_p - 1), 0)
    row_s = lambda i: (jnp.maximum(i - steps_p, 0), 0)
    mod_spec = pl.BlockSpec((None, None, 6, D_MODEL),
                            lambda i: (layer, jnp.where(i < steps_p, 0, 1 + (i - steps_p) // per_seq), 0, 0))
    hbm = pl.BlockSpec(memory_space=pl.ANY)
    specs = lambda arrs, row: [pl.BlockSpec((ROW_TILE, a.shape[1]), row) for a in arrs]
    return pl.pallas_call(
        functools.partial(_channel_kernel, layer=layer, sub_layer=layer // 2, prompt_steps=steps_p),
        grid=(steps_p + steps_s,),
        in_specs=specs(prompt, row_p) + specs(latent, row_s) + [mod_spec, _layer_spec(gains, layer), hbm, hbm, hbm],
        out_specs=[pl.BlockSpec((ROW_TILE, D_MODEL), row_p), pl.BlockSpec((ROW_TILE, D_MODEL), row_s)],
        out_shape=[jax.ShapeDtypeStruct((n_p, D_MODEL), F32), jax.ShapeDtypeStruct((n_s, D_MODEL), F32)],
        scratch_shapes=[pltpu.VMEM((D_MODEL, D_MODEL), BF16), pltpu.VMEM((D_MODEL, D_FF), BF16),
                        pltpu.VMEM((D_FF, D_MODEL), BF16), pltpu.VMEM((2, STAGE_ROWS, D_MODEL), F32),
                        pltpu.SemaphoreType.DMA((2,))],
        compiler_params=_params("arbitrary"),
        name="channel",
    )(*prompt, *latent, mods, gains, w_out, w_up, w_down)


def _row_iota(shape):
    return lax.broadcasted_iota(jnp.int32, shape, 0)


def _lane_iota(shape):
    return lax.broadcasted_iota(jnp.int32, shape, 1)


def _pair_lanes(shape):
    return _lane_iota(shape) < LANES // 2


def _cumsum_lanes(x, n_fwd):
    t = x.shape[1]
    si, ti = _row_iota((t, t)), _lane_iota((t, t))
    upper = jnp.where(si <= ti, 1.0, 0.0).astype(BF16)
    lower = jnp.where(si >= ti, 1.0, 0.0).astype(BF16)
    hi = x.astype(BF16)
    rest = x - hi.astype(F32)
    mid = rest.astype(BF16)
    lo = (rest - mid.astype(F32)).astype(BF16)
    pre = _dot(hi, upper) + _dot(mid, upper) + _dot(lo, upper)
    suf = _dot(hi, lower) + _dot(mid, lower) + _dot(lo, lower)
    return jnp.where(_row_iota(x.shape) < n_fwd, pre, suf)


def _cummax_lanes(x, n_fwd):
    t = x.shape[1]
    lane = _lane_iota(x.shape)
    pre, suf = x, x
    k = 1
    while k < t:
        pre = jnp.maximum(pre, jnp.where(lane >= k, pltpu.roll(pre, k, 1), -jnp.inf))
        suf = jnp.maximum(suf, jnp.where(lane < t - k, pltpu.roll(suf, t - k, 1), -jnp.inf))
        k *= 2
    return jnp.where(_row_iota(x.shape) < n_fwd, pre, suf)


def _columns(row_arrays):
    t = row_arrays[0].shape[1]
    used = sum(a.shape[0] for a in row_arrays)
    return jnp.concatenate(list(row_arrays) + [jnp.zeros((LANES - used, t), F32)], axis=0).T


def _dwconv_silu(x, w, b):
    t = x.shape[0]
    row = _row_iota(x.shape)
    acc = x * w[CONV_K // 2:CONV_K // 2 + 1, :] + b
    for j in range(CONV_K):
        d = j - CONV_K // 2
        if d == 0:
            continue
        shifted = pltpu.roll(x, (-d) % t, 0)
        valid = (row >= -d) if d < 0 else (row < t - d)
        acc = acc + jnp.where(valid, shifted, 0.0) * w[j:j + 1, :]
    return _silu(acc)


def _causal_exponent(expo, r0, k0, reverse):
    ti = r0 + _row_iota(expo.shape)
    si = k0 + _lane_iota(expo.shape)
    keep = (si >= ti) if reverse else (si <= ti)
    return jnp.where(keep, expo, -jnp.inf)


def _pair_split(x):
    first = _pair_lanes(x.shape)
    zero = jnp.zeros_like(x)
    return jnp.concatenate([jnp.where(first, x, zero), jnp.where(first, zero, x)], axis=0)


def _key_range(d, r0, seq):
    return (0, r0 + Q_TILE) if d == 0 else (r0, seq)


N_FUSED_INPUTS = 5
PROJ_CHUNK = 512

MLSTM_PIECES = ((0, 2 * A_QK), (2 * A_QK, A_V), (2 * A_QK + A_V, A_V))
MLSTM_GATES = (_GATES_LO, _Z_LO)
MLSTM_W_COLS = _GATES_LO + LANES
SSD_PIECES = ((_Z_LO + B_INNER, B_XBC), (_Z_LO, B_INNER))
SSD_GATES = (_DT_LO, _DT_LO + 2 * NH_B)
CTX_PIECES = tuple((sum(ODD_WIDTHS[:i]), wd) for i, wd in enumerate(ODD_WIDTHS[:-1])) + (
    (sum(ODD_WIDTHS[:-1]), LANES, ROPE_D),)


def _fused_project(step, fused_refs, scratch, pieces, gate_cols, mix):
    x_cur_ref, x_next_ref, mod_ref, gain_ref, w_ref = fused_refs
    w_s = scratch[0]
    n_head = 2 if gate_cols else 1
    n = len(pieces) + (1 if gate_cols else 0)
    sets = (scratch[n_head:n_head + n], scratch[n_head + n:n_head + 2 * n])

    def normed(x_ref):
        h = _rms(x_ref[...], gain_ref[0:1, :]) * (1.0 + mod_ref[1:2, :]) + mod_ref[0:1, :]
        return h.astype(BF16)

    def chunk_thunks(hb, dst):
        def gates():
            dst[0][...] = _dot_nt(scratch[1][...], hb)

        def columns(ref, c, src, width):
            def run():
                ref[:, c:c + width] = _dot(hb, w_s[:, src:src + width])
            return run

        thunks = [gates] if gate_cols else []
        off = 0
        for ref, piece in zip(dst[-len(pieces):], pieces):
            width = piece[1]
            thunks += [columns(ref, c, off + c, min(PROJ_CHUNK, width - c)) for c in range(0, width, PROJ_CHUNK)]
            off += width
        return thunks

    @pl.when(step == 0)
    def _():
        off = 0
        for piece in pieces:
            src, width = piece[0], piece[1]
            valid = piece[2] if len(piece) > 2 else width
            w_s[:, off:off + valid] = w_ref[:, src:src + valid]
            if valid < width:
                w_s[:, off + valid:off + width] = jnp.zeros((D_MODEL, width - valid), BF16)
            off += width
        if gate_cols:
            lo, hi = gate_cols
            gates = jnp.concatenate([w_ref[:, lo:hi].astype(F32), jnp.zeros((D_MODEL, LANES - (hi - lo)), F32)], axis=1)
            scratch[1][...] = gates.T[0:hi - lo, :].astype(BF16)
        for thunk in chunk_thunks(normed(x_cur_ref), sets[0]):
            thunk()

    for parity in range(2):
        @pl.when(step % 2 == parity)
        def _(parity=parity):
            cur = sets[parity]
            refs = list(cur[1:]) + [cur[0]] if gate_cols else list(cur)
            mix(*refs, chunk_thunks(normed(x_next_ref), sets[1 - parity]))


def _fused_specs(seq, nseq, layer, w_all, w_cols):
    nxt = lambda s: (jnp.minimum(s + 1, nseq - 1), 0)
    return [pl.BlockSpec((seq, D_MODEL), lambda s: (s, 0)), pl.BlockSpec((seq, D_MODEL), nxt),
            pl.BlockSpec((None, None, 6, D_MODEL), lambda s: (layer, 0, 0, 0)),
            pl.BlockSpec((None, 4, D_MODEL), lambda s: (layer, 0, 0)),
            pl.BlockSpec((None, D_MODEL, w_cols), lambda s: (layer // 2, 0, 0), pipeline_mode=pl.Buffered(1))]


def _fused_scratch(seq, pieces, n_gates):
    head = [pltpu.VMEM((D_MODEL, sum(p[1] for p in pieces)), BF16)]
    one_set = [pltpu.VMEM((seq, p[1]), F32) for p in pieces]
    if n_gates:
        head.append(pltpu.VMEM((n_gates, D_MODEL), BF16))
        one_set = [pltpu.VMEM((n_gates, seq), F32)] + one_set
    return head + one_set + one_set


def _run_share(side_work, stages_left):
    for _ in range(-(-len(side_work) // stages_left)):
        side_work.pop(0)()


def _mlstm_kernel(*refs, seq, has_state, emit_state, n_carried, slot, fused):
    assert not (has_state and emit_state)
    it = iter(refs)
    if fused:
        fused_refs = [next(it) for _ in range(N_FUSED_INPUTS)]
    else:
        qk_ref, v_ref, o_ref, g_ref = (next(it) for _ in range(4))
    cw_ref, cb_ref, gb_ref, anw_ref = (next(it) for _ in range(4))
    if has_state:
        c0_ref, m0_ref = next(it), next(it)
    for _ in range(n_carried):
        next(it)
    ha_ref = next(it)
    if emit_state:
        cn_ref, nn_ref, mn_ref = next(it), next(it), next(it)
        if n_carried == 0:
            for other in range(N_EVEN):
                if other != slot:
                    cn_ref[other] = jnp.zeros(cn_ref.shape[1:], F32)
                    nn_ref[other] = jnp.zeros(nn_ref.shape[1:], F32)
                    mn_ref[other] = jnp.zeros(mn_ref.shape[1:], F32)
            cn_ref, nn_ref, mn_ref = cn_ref.at[slot], nn_ref.at[slot], mn_ref.at[slot]

    def mix(qk_ref, v_ref, o_ref, g_ref, side_work=()):
        side_work = list(side_work)
        n_ch = 2 * NH_A
        log_i = g_ref[0:n_ch, :] + gb_ref[0:n_ch, 0:1]
        f_pre = g_ref[n_ch:2 * n_ch, :] + gb_ref[n_ch:2 * n_ch, 0:1]
        log_f = jnp.minimum(f_pre, 0.0) - jnp.log1p(jnp.exp(-jnp.abs(f_pre)))
        b = _cumsum_lanes(log_f, NH_A)
        a = log_i - b
        m_run = _cummax_lanes(a, NH_A)
        if has_state:
            m0 = m0_ref[:, 0:1]
            m_run = jnp.maximum(m_run, m0)
        else:
            m_run = jnp.maximum(m_run, 0.0)
        by_time = [m_run, jnp.exp(-(b + m_run))]
        if has_state:
            by_time.append(jnp.exp(m0 - m_run))
        if emit_state:
            fwd = _row_iota((n_ch, 1)) < NH_A
            b_last = jnp.where(fwd, b[:, seq - 1:seq], b[:, 0:1])
            m_last = jnp.where(fwd, m_run[:, seq - 1:seq], m_run[:, 0:1])
            mn_ref[...] = jnp.broadcast_to(b_last + m_last, (n_ch, LANES))
            by_time.append(jnp.exp(a - m_last))
        cols = _columns(by_time)

        ones = jnp.ones((seq, LANES), F32)
        for h in range(NH_A):
            _run_share(side_work, NH_A - h)
            cq = slice(h * DK_A, (h + 1) * DK_A)
            ck = slice(A_QK + h * DK_A, A_QK + (h + 1) * DK_A)
            cv = slice(h * DV_A, (h + 1) * DV_A)
            q = _dwconv_silu(qk_ref[:, cq], cw_ref[:, cq], cb_ref[:, cq])
            k = _dwconv_silu(qk_ref[:, ck], cw_ref[:, ck], cb_ref[:, ck]) * (DK_A ** -0.5)
            qb = q.astype(BF16)
            kb = k.astype(BF16)
            vh = v_ref[:, cv]
            vaug = jnp.concatenate([vh, ones], axis=1).astype(BF16)
            for r0 in range(0, seq, Q_TILE):
                rows = slice(r0, r0 + Q_TILE)
                s = _dot_nt(qb[rows], kb)
                hsum = None
                for d in range(2):
                    c = d * NH_A + h
                    k0, k1 = _key_range(d, r0, seq)
                    expo = _causal_exponent(a[c:c + 1, k0:k1] - cols[rows, c:c + 1], r0, k0, d == 1)
                    p = (s[:, k0:k1] * jnp.exp(expo)).astype(BF16)
                    acc = _dot(p, vaug[k0:k1])
                    if has_state:
                        acc = acc + (cols[rows, 2 * n_ch + c:2 * n_ch + c + 1]
                                     * _dot(qb[rows], c0_ref[d, h].astype(BF16)))
                    hd = acc[:, 0:DV_A] / jnp.maximum(jnp.abs(acc[:, DV_A:]), cols[rows, n_ch + c:n_ch + c + 1])
                    hsum = hd if hsum is None else hsum + hd
                og = jax.nn.sigmoid(o_ref[rows, cv]) * hsum
                ha_ref[rows, cv] = _rms(og, anw_ref[:, cv])
            if emit_state:
                for d in range(2):
                    c = d * NH_A + h
                    kw = k * cols[:, 2 * n_ch + c:2 * n_ch + c + 1]
                    cn_ref[d, h] = _dot(kw.T.astype(BF16), vh.astype(BF16))
                    nn_ref[d, h:h + 1, :] = jnp.sum(kw, axis=0, keepdims=True)

    if fused:
        _fused_project(pl.program_id(0), fused_refs, list(it), MLSTM_PIECES, MLSTM_GATES, mix)
    else:
        mix(qk_ref, v_ref, o_ref, g_ref)


def _state_out_specs(shapes, nseq, sub_layer, carried):
    out_specs, out_shape = [], []
    for shp in shapes:
        zeros = (0,) * len(shp)
        if carried is None:
            out_specs.append(pl.BlockSpec((None, N_EVEN) + shp, lambda s, z=zeros: (s, 0) + z))
        else:
            out_specs.append(pl.BlockSpec((None, None) + shp, lambda s, z=zeros: (s, sub_layer) + z))
        out_shape.append(jax.ShapeDtypeStruct((nseq, N_EVEN) + shp, F32))
    return out_specs, out_shape


MLSTM_STATE_SHAPES = ((2, NH_A, DK_A, DV_A), (2, NH_A, DK_A), (2 * NH_A, LANES))
SSD_STATE_SHAPES = ((2, NH_B, HP_B, DSTATE),)


def _fused_mixer(kernel_fn, name, x, mods, layer, gains, w_all, w_cols, params, seq, pieces, gate_cols,
                 out_width, state_shapes, carried):
    n = x.shape[0]
    nseq = n // seq
    sub_layer = layer // 2
    in_specs = _fused_specs(seq, nseq, layer, w_all, w_cols) + [_layer_spec(p, sub_layer) for p in params]
    args = [x, x, mods, gains, w_all, *params]
    state_specs, state_shape = _state_out_specs(state_shapes, nseq, sub_layer, carried)
    aliases = {}
    if carried is not None:
        aliases = {len(args) + i: 1 + i for i in range(len(state_shapes))}
        in_specs += [pl.BlockSpec(memory_space=pl.ANY)] * len(state_shapes)
        args += list(carried)
    return pl.pallas_call(
        functools.partial(kernel_fn, seq=seq, has_state=False, emit_state=True, n_carried=len(aliases),
                          slot=sub_layer, fused=True),
        grid=(nseq,), in_specs=in_specs,
        out_specs=[pl.BlockSpec((seq, out_width), lambda s: (s, 0))] + state_specs,
        out_shape=[jax.ShapeDtypeStruct((n, out_width), F32)] + state_shape,
        scratch_shapes=_fused_scratch(seq, pieces, gate_cols[1] - gate_cols[0]),
        input_output_aliases=aliases, compiler_params=_params("arbitrary"), name=name,
    )(*args)


def _mlstm_latent(qk, v, o, gates, params, seq, sub_layer, state):
    n = qk.shape[0]
    row = lambda s: (s, 0)
    c0_aug, m0 = state
    in_specs = [pl.BlockSpec((seq, 2 * A_QK), row), pl.BlockSpec((seq, A_V), row), pl.BlockSpec((seq, A_V), row),
                pl.BlockSpec((GATE_ROWS, seq), lambda s: (0, s))]
    in_specs += [_layer_spec(p, sub_layer) for p in params]
    in_specs += [pl.BlockSpec((None, None) + c0_aug.shape[2:], lambda s: (s, sub_layer, 0, 0, 0, 0)),
                 pl.BlockSpec((None, None) + m0.shape[2:], lambda s: (s, sub_layer, 0, 0))]
    return pl.pallas_call(
        functools.partial(_mlstm_kernel, seq=seq, has_state=True, emit_state=False, n_carried=0, slot=sub_layer,
                          fused=False),
        grid=(n // seq,), in_specs=in_specs, out_specs=pl.BlockSpec((seq, A_V), row),
        out_shape=jax.ShapeDtypeStruct((n, A_V), F32),
        compiler_params=_params("arbitrary"), name="mlstm",
    )(qk, v, o, gates, *params, c0_aug, m0)


def _ssd_kernel(*refs, seq, has_state, emit_state, n_carried, slot, fused):
    assert not (has_state and emit_state)
    it = iter(refs)
    if fused:
        fused_refs = [next(it) for _ in range(N_FUSED_INPUTS)]
    else:
        xbc_ref, z_ref, g_ref = (next(it) for _ in range(3))
    cw_ref, cb_ref, dtb_ref, alog_ref, dsk_ref, bnw_ref = (next(it) for _ in range(6))
    if has_state:
        s0_ref = next(it)
    for _ in range(n_carried):
        next(it)
    yb_ref = next(it)
    if emit_state:
        sn_ref = next(it)
        if n_carried == 0:
            for other in range(N_EVEN):
                if other != slot:
                    sn_ref[other] = jnp.zeros(sn_ref.shape[1:], F32)
            sn_ref = sn_ref.at[slot]

    gate_row0 = 0 if fused else 4 * NH_A

    def mix(xbc_ref, z_ref, g_ref, side_work=()):
        side_work = list(side_work)
        n_ch = 2 * NH_B
        dt = _softplus(g_ref[gate_row0:gate_row0 + n_ch, :] + dtb_ref[:, 0:1])
        acum = _cumsum_lanes(dt * (-jnp.exp(alog_ref[:, 0:1])), NH_B)
        key_shift = acum - jnp.log(dt)
        by_time = [acum]
        if has_state:
            by_time.append(jnp.exp(acum))
        if emit_state:
            fwd = _row_iota((n_ch, 1)) < NH_B
            a_last = jnp.where(fwd, acum[:, seq - 1:seq], acum[:, 0:1])
            by_time.append(jnp.exp(a_last - acum) * dt)
        cols = _columns(by_time)

        gw = R_B * HP_B
        first = _pair_lanes((seq, LANES))
        for g in range(NG_B):
            _run_share(side_work, NG_B - g)
            cx = slice(g * gw, (g + 1) * gw)
            cb_ = slice(B_INNER + g * DSTATE, B_INNER + (g + 1) * DSTATE)
            cc = slice(B_INNER + B_BC + g * DSTATE, B_INNER + B_BC + (g + 1) * DSTATE)
            xg = _dwconv_silu(xbc_ref[:, cx], cw_ref[:, cx], cb_ref[:, cx])
            bg = _dwconv_silu(xbc_ref[:, cb_], cw_ref[:, cb_], cb_ref[:, cb_]).astype(BF16)
            cg = _dwconv_silu(xbc_ref[:, cc], cw_ref[:, cc], cb_ref[:, cc]).astype(BF16)
            xbd = [_pair_split(xg[:, p * LANES:(p + 1) * LANES]).astype(BF16) for p in range(R_B // 2)]
            for r0 in range(0, seq, Q_TILE):
                rows = slice(r0, r0 + Q_TILE)
                cb_scores = _dot_nt(cg[rows], bg)
                ys = []
                for p in range(R_B // 2):
                    weights, inputs = [], []
                    for d in range(2):
                        k0, k1 = _key_range(d, r0, seq)
                        for i in range(2):
                            c = d * NH_B + g * R_B + 2 * p + i
                            expo = _causal_exponent(cols[rows, c:c + 1] - key_shift[c:c + 1, k0:k1], r0, k0, d == 1)
                            weights.append((cb_scores[:, k0:k1] * jnp.exp(expo)).astype(BF16))
                            inputs.append(xbd[p][i * seq + k0:i * seq + k1])
                    yp = _dot(jnp.concatenate(weights, axis=1), jnp.concatenate(inputs, axis=0))
                    if has_state:
                        h0 = g * R_B + 2 * p
                        for d in range(2):
                            c = d * NH_B + h0
                            carry = jnp.where(_pair_lanes((Q_TILE, LANES)), cols[rows, n_ch + c:n_ch + c + 1],
                                              cols[rows, n_ch + c + 1:n_ch + c + 2])
                            s0_pair = s0_ref[d, h0:h0 + 2].reshape(2 * HP_B, DSTATE).astype(BF16)
                            yp = yp + carry * _dot_nt(cg[rows], s0_pair)
                    ys.append(yp)
                y = jnp.concatenate(ys, axis=1) + dsk_ref[:, cx] * xg[rows]
                y = y * _silu(z_ref[rows, cx])
                yb_ref[rows, cx] = _rms(y, bnw_ref[:, cx])
            if emit_state:
                for d in range(2):
                    c0 = n_ch + d * NH_B + g * R_B
                    spread = jnp.concatenate([jnp.where(first, cols[:, c0 + 2 * p:c0 + 2 * p + 1],
                                                        cols[:, c0 + 2 * p + 1:c0 + 2 * p + 2])
                                              for p in range(R_B // 2)], axis=1)
                    sn = _dot((xg * spread).T.astype(BF16), bg)
                    for r in range(R_B):
                        sn_ref[d, g * R_B + r] = sn[r * HP_B:(r + 1) * HP_B, :]

    if fused:
        _fused_project(pl.program_id(0), fused_refs, list(it), SSD_PIECES, SSD_GATES, mix)
    else:
        mix(xbc_ref, z_ref, g_ref)


def _ssd_latent(xbc, z, gates, params, seq, sub_layer, state):
    n = xbc.shape[0]
    row = lambda s: (s, 0)
    in_specs = [pl.BlockSpec((seq, B_XBC), row), pl.BlockSpec((seq, B_INNER), row),
                pl.BlockSpec((GATE_ROWS, seq), lambda s: (0, s))]
    in_specs += [_layer_spec(p, sub_layer) for p in params]
    in_specs += [pl.BlockSpec((None, None, 2, NH_B, HP_B, DSTATE), lambda s: (s, sub_layer, 0, 0, 0, 0))]
    return pl.pallas_call(
        functools.partial(_ssd_kernel, seq=seq, has_state=True, emit_state=False, n_carried=0, slot=sub_layer,
                          fused=False),
        grid=(n // seq,), in_specs=in_specs, out_specs=pl.BlockSpec((seq, B_INNER), row),
        out_shape=jax.ShapeDtypeStruct((n, B_INNER), F32),
        compiler_params=_params("arbitrary"), name="ssd",
    )(xbc, z, gates, *params, state)


def _shared_split(x, x_swapped, kh):
    first = _pair_lanes(x.shape)
    zero = jnp.zeros_like(x)
    top, bottom = (x, x_swapped) if kh == 0 else (x_swapped, x)
    return jnp.concatenate([jnp.where(first, top, zero), jnp.where(first, zero, bottom)], axis=0)


def _pair_probs(s, sinks=None, valid=None):
    n_keys = s.shape[1] // 2
    probs, maxes = [], []
    for i in range(2):
        si = s[:, i * n_keys:(i + 1) * n_keys]
        if valid is not None:
            si = jnp.where(valid, si, -jnp.inf)
        m = jnp.max(si, axis=1, keepdims=True)
        if sinks is not None:
            m = jnp.maximum(m, sinks[i])
        probs.append(jnp.exp(si - m))
        maxes.append(m)
    return jnp.concatenate(probs, axis=1).astype(BF16), maxes


def _pair_output(p, maxes, vbd, sinks=None):
    o = _dot(p, vbd)
    den = o[:, LANES:]
    if sinks is not None:
        den = den + jnp.where(_pair_lanes(den.shape), jnp.exp(sinks[0] - maxes[0]), jnp.exp(sinks[1] - maxes[1]))
    return o[:, :LANES] / den


def _run_pairs(items, valid=None, side_work=()):
    side_work = list(side_work)
    s_next = items[0][0]()
    for idx, (_, values, sinks, out_ref, cols) in enumerate(items):
        _run_share(side_work, len(items) - idx)
        s_cur = s_next
        if idx + 1 < len(items):
            s_next = items[idx + 1][0]()
        p, maxes = _pair_probs(s_cur, sinks, valid if sinks is not None else None)
        out_ref[:, cols] = _pair_output(p, maxes, values(), sinks)


def _pair_sinks(sink_ref, n):
    return sink_ref[0:1, n:n + 1], sink_ref[0:1, n + 1:n + 2]


def _mla_queries(qa_ref, qan_ref, wqb_ref):
    return _dot(_rms(qa_ref[...], qan_ref[...]).astype(BF16), wqb_ref[...]) * MLA_SCALE


def _attn_ctx_kernel(*refs, seq, n_carried, slot, fused):
    it = iter(refs)
    if fused:
        fused_refs = [next(it) for _ in range(N_FUSED_INPUTS)]
    else:
        proj_refs = [next(it) for _ in range(len(ODD_WIDTHS))]
    sink_ref, qan_ref, kvn_ref, wqb_ref, wkvb_ref = (next(it) for _ in range(5))
    for _ in range(n_carried):
        next(it)
    oc_ref, od_ref, ckv_ref = next(it), next(it), next(it)
    if fused:
        new_k_ref, new_v_ref, new_kpe_ref = next(it), next(it), next(it)
    if n_carried == 0:
        for other in range(N_ODD):
            if other != slot:
                ckv_ref[other] = jnp.zeros(ckv_ref.shape[1:], F32)
        ckv_ref = ckv_ref.at[slot]

    def mix(qc_ref, kc_ref, vc_ref, qa_ref, kva_ref, kpe_ref, side_work=()):
        if fused:
            new_k_ref[...] = kc_ref[...]
            new_v_ref[...] = vc_ref[...]
            new_kpe_ref[...] = kpe_ref[...]
        ones_bd = _pair_split(jnp.ones((seq, LANES), F32))
        kc, vc = kc_ref[...], vc_ref[...]
        kc_sw, vc_sw = pltpu.roll(kc, HD_C, 1), pltpu.roll(vc, HD_C, 1)
        qd = _mla_queries(qa_ref, qan_ref, wqb_ref)
        ckv = _rms(kva_ref[...], kvn_ref[...])
        ckv_ref[...] = ckv
        kv = _dot(ckv.astype(BF16), wkvb_ref[...])
        kpe = kpe_ref[...]
        kpe_bd = jnp.concatenate([kpe, pltpu.roll(kpe, ROPE_D, 1)], axis=0)
        nope_w = NH_D * NOPE_D
        items = []
        for kh in range(NKV_C):
            for n in range(kh * G_C, (kh + 1) * G_C, 2):
                cols = slice(n * HD_C, (n + 2) * HD_C)
                items.append((lambda cols=cols, kh=kh: _dot_nt((qc_ref[:, cols] * (HD_C ** -0.5)).astype(BF16),
                                                               _shared_split(kc, kc_sw, kh).astype(BF16)),
                              lambda kh=kh: jnp.concatenate([_shared_split(vc, vc_sw, kh), ones_bd], axis=1).astype(BF16),
                              _pair_sinks(sink_ref, n), oc_ref, cols))
        for i in range(NH_D // 2):
            cols = slice(i * LANES, (i + 1) * LANES)
            vcols = slice(nope_w + i * LANES, nope_w + (i + 1) * LANES)
            items.append((lambda cols=cols, vcols=vcols: _dot_nt(
                              jnp.concatenate([qd[:, cols], qd[:, vcols]], axis=1).astype(BF16),
                              jnp.concatenate([_pair_split(kv[:, cols]), kpe_bd], axis=1).astype(BF16)),
                          lambda vcols=vcols: jnp.concatenate([_pair_split(kv[:, vcols]), ones_bd], axis=1).astype(BF16),
                          None, od_ref, cols))
        _run_pairs(items, side_work=side_work)

    if fused:
        _fused_project(pl.program_id(0), fused_refs, list(it), CTX_PIECES, None, mix)
    else:
        mix(*proj_refs)


def _attn_ctx(x, mods, layer, gains, w_all, params, seq, carried=None):
    n = x.shape[0]
    nseq = n // seq
    sub_layer = layer // 2
    row = lambda s: (s, 0)
    in_specs = _fused_specs(seq, nseq, layer, w_all, w_all.shape[2]) + [_layer_spec(p, sub_layer) for p in params]
    args = [x, x, mods, gains, w_all, *params]
    half = NH_C * HD_C
    aliases = {}
    if carried is None:
        ckv_spec = pl.BlockSpec((None, N_ODD, seq, KV_RANK), lambda s: (s, 0, 0, 0))
    else:
        ckv_spec = pl.BlockSpec((None, None, seq, KV_RANK), lambda s: (s, sub_layer, 0, 0))
        aliases = {len(args): 2}
        in_specs.append(pl.BlockSpec(memory_space=pl.ANY))
        args.append(carried)
    kv_w = NKV_C * HD_C
    return pl.pallas_call(
        functools.partial(_attn_ctx_kernel, seq=seq, n_carried=len(aliases), slot=sub_layer, fused=True),
        grid=(nseq,), in_specs=in_specs,
        out_specs=[pl.BlockSpec((seq, half), row), pl.BlockSpec((seq, half), row), ckv_spec,
                   pl.BlockSpec((seq, kv_w), row), pl.BlockSpec((seq, kv_w), row), pl.BlockSpec((seq, LANES), row)],
        out_shape=[jax.ShapeDtypeStruct((n, half), F32), jax.ShapeDtypeStruct((n, half), F32),
                   jax.ShapeDtypeStruct((nseq, N_ODD, seq, KV_RANK), F32),
                   jax.ShapeDtypeStruct((n, kv_w), F32), jax.ShapeDtypeStruct((n, kv_w), F32),
                   jax.ShapeDtypeStruct((n, LANES), F32)],
        scratch_shapes=_fused_scratch(seq, CTX_PIECES, 0),
        input_output_aliases=aliases, compiler_params=_params("arbitrary"), name="attn_ctx",
    )(*args)


def _rope(x, cos, sin, half):
    parts = []
    lane = _lane_iota((x.shape[0], LANES))
    first = (lane & (2 * half - 1)) < half
    for i in range(x.shape[1] // LANES):
        xi = x[:, i * LANES:(i + 1) * LANES]
        partner = jnp.where(first, -pltpu.roll(xi, LANES - half, 1), pltpu.roll(xi, half, 1))
        parts.append(xi * cos + partner * sin)
    return parts[0] if len(parts) == 1 else jnp.concatenate(parts, axis=1)


def _attn_lat_kernel(qc_ref, qa_ref, ropeq_ref, kc_ref, vc_ref, kva_ref, kpe_ref, kctx_ref, vctx_ref, ckvctx_ref,
                     kpectx_ref, rope_ref, sink_ref, qan_ref, kvn_ref, wqb_ref, wkvb_ref, oc_ref, od_ref,
                     kwin_s, vwin_s, kext_s, vext_s, *, seq, past):
    qi = pl.program_id(1)
    nope_w = NH_D * NOPE_D
    n_all = past + seq
    ctx0 = 2 * WINDOW + seq

    @pl.when(qi == 0)
    def _():
        zeros = jnp.zeros((WINDOW, LANES), BF16)
        for ref, lat, ctx in ((kwin_s, _rope(kc_ref[...], rope_ref[0], rope_ref[1], HD_C // 2), kctx_ref[...]),
                              (vwin_s, vc_ref[...], vctx_ref[...])):
            lat_sw, ctx_sw = pltpu.roll(lat, HD_C, 1), pltpu.roll(ctx, HD_C, 1)
            for kh in range(NKV_C):
                lat_bd = _shared_split(lat, lat_sw, kh).astype(BF16)
                ctx_bd = _shared_split(ctx, ctx_sw, kh).astype(BF16)
                for i in range(2):
                    ref[kh, i, 0:WINDOW, :] = zeros
                    ref[kh, i, WINDOW:WINDOW + seq, :] = lat_bd[i * seq:(i + 1) * seq]
                    ref[kh, i, WINDOW + seq:ctx0, :] = zeros
                    ref[kh, i, ctx0:, :] = ctx_bd[i * past:(i + 1) * past]
        ckv = _rms(kva_ref[...], kvn_ref[...])
        kv = jnp.concatenate([_dot(ckvctx_ref[...].astype(BF16), wkvb_ref[...]),
                              _dot(ckv.astype(BF16), wkvb_ref[...])], axis=0)
        kpe = jnp.concatenate([kpectx_ref[...], _rope(kpe_ref[...], rope_ref[2], rope_ref[3], ROPE_D // 2)], axis=0)
        kpe_bd = jnp.concatenate([kpe, pltpu.roll(kpe, ROPE_D, 1)], axis=0).astype(BF16)
        ones_bd = _pair_split(jnp.ones((n_all, LANES), F32)).astype(BF16)
        for i in range(NH_D // 2):
            kext_s[i, :, 0:LANES] = _pair_split(kv[:, i * LANES:(i + 1) * LANES]).astype(BF16)
            kext_s[i, :, LANES:] = kpe_bd
            vext_s[i, :, 0:LANES] = _pair_split(kv[:, nope_w + i * LANES:nope_w + (i + 1) * LANES]).astype(BF16)
            vext_s[i, :, LANES:] = ones_bd

    r0 = pl.multiple_of(qi * Q_TILE, Q_TILE)
    nloc = Q_TILE + 2 * WINDOW
    n_keys = nloc + past
    qr = _rope(qc_ref[...], ropeq_ref[0], ropeq_ref[1], HD_C // 2) * (HD_C ** -0.5)
    ti = r0 + _row_iota((Q_TILE, n_keys))
    col = _lane_iota((Q_TILE, n_keys))
    pos = r0 - WINDOW + col
    valid = (col >= nloc) | ((jnp.abs(ti - pos) <= WINDOW) & (pos >= 0) & (pos < seq))
    ones_bd = _pair_split(jnp.ones((n_keys, LANES), F32)).astype(BF16)
    qd = _mla_queries(qa_ref, qan_ref, wqb_ref)
    q_pe = _rope(qd[:, nope_w:], ropeq_ref[2], ropeq_ref[3], ROPE_D // 2)

    def banded(ref, kh):
        return jnp.concatenate([ref[kh, 0, pl.ds(r0, nloc), :], ref[kh, 0, ctx0:, :],
                                ref[kh, 1, pl.ds(r0, nloc), :], ref[kh, 1, ctx0:, :]], axis=0)

    items = []
    for kh in range(NKV_C):
        for n in range(kh * G_C, (kh + 1) * G_C, 2):
            cols = slice(n * HD_C, (n + 2) * HD_C)
            items.append((lambda cols=cols, kh=kh: _dot_nt(qr[:, cols].astype(BF16), banded(kwin_s, kh)),
                          lambda kh=kh: jnp.concatenate([banded(vwin_s, kh), ones_bd], axis=1),
                          _pair_sinks(sink_ref, n), oc_ref, cols))
    for i in range(NH_D // 2):
        cols = slice(i * LANES, (i + 1) * LANES)
        items.append((lambda cols=cols, i=i: _dot_nt(jnp.concatenate([qd[:, cols], q_pe[:, cols]], axis=1).astype(BF16),
                                                     kext_s[i]),
                      lambda i=i: vext_s[i], None, od_ref, cols))
    _run_pairs(items, valid)


def _attn_lat(proj, caches, rope, params, seq, sub_layer):
    qc, kc, vc, qa, kva, kpe = proj
    n = qc.shape[0]
    past = caches[0].shape[2]
    nq = seq // Q_TILE
    qrow = lambda b, q: (b * nq + q, 0)
    krow = lambda b, q: (b, 0)
    kvw = NKV_C * HD_C
    in_specs = [pl.BlockSpec((Q_TILE, NH_C * HD_C), qrow), pl.BlockSpec((Q_TILE, Q_RANK), qrow),
                pl.BlockSpec((4, Q_TILE, LANES), lambda b, q: (0, q, 0)),
                pl.BlockSpec((seq, kvw), krow), pl.BlockSpec((seq, kvw), krow),
                pl.BlockSpec((seq, KV_RANK), krow), pl.BlockSpec((seq, LANES), krow)]
    in_specs += [pl.BlockSpec((None, None, past, LANES), lambda b, q: (b, sub_layer, 0, 0)) for _ in caches]
    in_specs += [pl.BlockSpec(rope.shape, lambda b, q: (0, 0, 0))]
    in_specs += [_layer_spec(p, sub_layer) for p in params]
    half = NH_C * HD_C
    win_rows = 2 * WINDOW + seq + past
    return pl.pallas_call(
        functools.partial(_attn_lat_kernel, seq=seq, past=past),
        grid=(n // seq, nq), in_specs=in_specs,
        out_specs=[pl.BlockSpec((Q_TILE, half), qrow), pl.BlockSpec((Q_TILE, half), qrow)],
        out_shape=[jax.ShapeDtypeStruct((n, half), F32), jax.ShapeDtypeStruct((n, half), F32)],
        scratch_shapes=[pltpu.VMEM((NKV_C, 2, win_rows, LANES), BF16), pltpu.VMEM((NKV_C, 2, win_rows, LANES), BF16),
                        pltpu.VMEM((NH_D // 2, 2 * (past + seq), 2 * LANES), BF16),
                        pltpu.VMEM((NH_D // 2, 2 * (past + seq), 2 * LANES), BF16)],
        compiler_params=_params("arbitrary", "arbitrary"), name="attn_lat",
    )(qc, qa, rope, kc, vc, kva, kpe, *caches, rope, *params)


def _pad_lanes(x, width=LANES):
    return jnp.pad(x, [(0, 0)] * (x.ndim - 1) + [(0, width - x.shape[-1])])


def _on_lanes(x):
    return jnp.broadcast_to(x[..., None], x.shape + (LANES,))


def _mla_query_weights(w):
    lead = w.shape[:-1]
    w4 = w.reshape(lead + (NH_D // 2, 2, NOPE_D + ROPE_D))
    nope = w4[..., :NOPE_D].reshape(lead + (NH_D * NOPE_D,))
    pe = _pad_lanes(w4[..., NOPE_D:].reshape(lead + (NH_D // 2, 2 * ROPE_D)))
    return jnp.concatenate([nope, pe.reshape(lead + (NH_D // 2 * LANES,))], axis=-1).astype(BF16)


def _mla_kv_weights(w):
    lead = w.shape[:-1]
    w3 = w.reshape(lead + (NH_D, NOPE_D + V_D))
    return jnp.concatenate([w3[..., :NOPE_D].reshape(lead + (NH_D * NOPE_D,)),
                            w3[..., NOPE_D:].reshape(lead + (NH_D * V_D,))], axis=-1).astype(BF16)


def _rope_tables(rows):
    def table(rot_dim):
        quarter = rot_dim // 4
        inv = ROPE_BASE ** (-jnp.arange(quarter, dtype=F32) / quarter)
        r = jnp.repeat(jnp.arange(rows, dtype=F32), GRID_W)
        col = jnp.tile(jnp.arange(GRID_W, dtype=F32), rows)
        ang = jnp.concatenate([r[:, None] * inv, col[:, None] * inv], axis=-1)
        reps = LANES // (rot_dim // 2)
        return jnp.tile(jnp.cos(ang), (1, reps)), jnp.tile(jnp.sin(ang), (1, reps))
    cos_c, sin_c = table(HD_C)
    cos_d, sin_d = table(ROPE_D)
    return jnp.stack([cos_c, sin_c, cos_d, sin_d])


def kernel(x_prompt, x_sample, c, state_mlstm_C, state_mlstm_n, state_mlstm_m, state_ssd, cache_gqa_k, cache_gqa_v,
           cache_mla_ckv, cache_mla_kpe, c_ctx, w_ada, b_ada, norm_g, w_up, w_down, w_in_even, conv_a_w, conv_a_b,
           conv_b_w, conv_b_b, gate_b, a_norm_w, dt_bias, a_log, d_skip, b_norm_w, w_out_even, w_in_odd, sink,
           q_a_norm, kv_a_norm, w_q_b, w_kv_b, w_out_odd):
    xp = x_prompt.reshape(BATCH * SEQ, D_MODEL)
    xs = x_sample.reshape(DEC_BATCH * DEC_SEQ, D_MODEL)
    cond = jnp.concatenate([c_ctx[None, :], c, jnp.zeros((MOD_ROWS - 1 - DEC_BATCH, D_MODEL), F32)], axis=0)
    mods = _modulations(cond, w_ada, b_ada)
    rope = _rope_tables(DEC_SEQ // GRID_W)

    w_even = w_in_even.astype(BF16)
    a_params = (conv_a_w, conv_a_b[:, None, :], _on_lanes(gate_b), a_norm_w[:, None, :])
    b_params = (conv_b_w, conv_b_b[:, None, :], _on_lanes(dt_bias.reshape(N_EVEN, 2 * NH_B)),
                _on_lanes(a_log.reshape(N_EVEN, 2 * NH_B)), jnp.repeat(d_skip, HP_B, axis=1)[:, None, :],
                b_norm_w[:, None, :])
    n0 = state_mlstm_n[..., None]
    mem_in = (jnp.concatenate([state_mlstm_C, jnp.broadcast_to(n0, n0.shape[:-1] + (LANES,))], axis=-1),
              _on_lanes(state_mlstm_m.reshape(DEC_BATCH, N_EVEN, 2 * NH_A)))
    w_odd = w_in_odd.astype(BF16)
    o_params = (_pad_lanes(sink)[:, None, :], q_a_norm[:, None, :], kv_a_norm[:, None, :],
                _mla_query_weights(w_q_b), _mla_kv_weights(w_kv_b))
    caches = (cache_gqa_k.reshape(DEC_BATCH, N_ODD, PAST_LEN, NKV_C * HD_C),
              cache_gqa_v.reshape(DEC_BATCH, N_ODD, PAST_LEN, NKV_C * HD_C),
              cache_mla_ckv, _pad_lanes(cache_mla_kpe))

    new_k, new_v, new_kpe = [], [], []
    mem_state, ssd_state, ckv_state = None, None, None
    for l in range(DEPTH):
        j = l // 2
        if l % 2 == 0:
            a1p, *mem_state = _fused_mixer(_mlstm_kernel, "mlstm", xp, mods, l, norm_g, w_even, MLSTM_W_COLS, a_params,
                                           SEQ, MLSTM_PIECES, MLSTM_GATES, A_V, MLSTM_STATE_SHAPES, mem_state)
            a2p, *ssd_state = _fused_mixer(_ssd_kernel, "ssd", xp, mods, l, norm_g, w_even, w_even.shape[2], b_params,
                                           SEQ, SSD_PIECES, SSD_GATES, B_INNER, SSD_STATE_SHAPES, ssd_state)
            qk, v, o, z, xbc, g = _project(xs, mods, l, True, norm_g, w_even, EVEN_WIDTHS, EVEN_REGROUP, EVEN_GATE_COLS)
            a1s = _mlstm_latent(qk, v, o, g, a_params, DEC_SEQ, j, mem_in)
            a2s = _ssd_latent(xbc, z, g, b_params, DEC_SEQ, j, state_ssd)
            w_out = w_out_even
        else:
            a1p, a2p, ckv_state, kc, vc, kpe = _attn_ctx(xp, mods, l, norm_g, w_odd, o_params, SEQ, carried=ckv_state)
            new_k.append(kc.reshape(BATCH, SEQ, NKV_C, HD_C))
            new_v.append(vc.reshape(BATCH, SEQ, NKV_C, HD_C))
            new_kpe.append(kpe[:, :ROPE_D].reshape(BATCH, SEQ, ROPE_D))
            proj = _project(xs, mods, l, True, norm_g, w_odd, ODD_WIDTHS, ODD_REGROUP)
            a1s, a2s = _attn_lat(proj, caches, rope, o_params, DEC_SEQ, j)
            w_out = w_out_odd
        xp, xs = _channel((a1p, a2p, xp), (a1s, a2s, xs), mods, l, norm_g, w_out, w_up, w_down)

    new_c, new_n, new_m = mem_state
    return (xp.reshape(BATCH, SEQ, D_MODEL), xs.reshape(DEC_BATCH, DEC_SEQ, D_MODEL),
            new_c, new_n, new_m[..., 0].reshape(BATCH, N_EVEN, 2, NH_A), ssd_state[0],
            jnp.stack(new_k, axis=1), jnp.stack(new_v, axis=1), ckv_state, jnp.stack(new_kpe, axis=1))
```

```python
import functools

import jax
import jax.numpy as jnp
from jax import lax
from jax.experimental import pallas as pl
from jax.experimental.pallas import tpu as pltpu

F32 = jnp.float32
BF16 = jnp.bfloat16

D_MODEL = 1024
BATCH = 32
SEQ = 256
DEPTH = 4
DEC_BATCH = 2
DEC_SEQ = 1024
PAST_LEN = 256
GRID_W = 64
N_EVEN = (DEPTH + 1) // 2
N_ODD = DEPTH // 2
EPS = 1e-6
CONV_K = 5
NH_A = 4
DK_A = 128
DV_A = 128
A_QK = NH_A * DK_A
A_V = NH_A * DV_A
NH_B = 8
HP_B = 64
DSTATE = 128
NG_B = 2
R_B = NH_B // NG_B
B_INNER = NH_B * HP_B
B_BC = NG_B * DSTATE
B_XBC = B_INNER + 2 * B_BC
NH_C = 8
NKV_C = 2
G_C = NH_C // NKV_C
HD_C = 64
WINDOW = 128
NH_D = 8
Q_RANK = 256
KV_RANK = 128
NOPE_D = 64
ROPE_D = 32
V_D = 64
MLA_SCALE = (NOPE_D + ROPE_D) ** -0.5
D_FF = 4 * D_MODEL
ROPE_BASE = 10000.0

LANES = 128
VMEM_LIMIT_BYTES = 56 * 1024 * 1024
ROW_TILE = 512
FF_TILE = 1024
SUB_ROWS = 256
STAGE_ROWS = 512
Q_TILE = 256
ADA_TILE = 1536
MOD_ROWS = 8
GATE_ROWS = 4 * NH_A + 2 * NH_B

EVEN_WIDTHS = (2 * A_QK, A_V, A_V, B_INNER, B_XBC)
ODD_WIDTHS = (NH_C * HD_C, NKV_C * HD_C, NKV_C * HD_C, Q_RANK, KV_RANK, LANES)
_GATES_LO = 2 * A_QK + 2 * A_V
_Z_LO = _GATES_LO + 4 * NH_A
_DT_LO = _Z_LO + B_INNER + B_XBC
EVEN_REGROUP = ((0, 0, _GATES_LO), (_GATES_LO, _Z_LO, B_INNER + B_XBC))
EVEN_GATE_COLS = ((_GATES_LO, _Z_LO), (_DT_LO, _DT_LO + 2 * NH_B))
ODD_IN = sum(ODD_WIDTHS) - LANES + ROPE_D
ODD_REGROUP = ((0, 0, ODD_IN), (ODD_IN, None, LANES - ROPE_D))

_NT = (((1,), (1,)), ((), ()))


def _params(*sem):
    return pltpu.CompilerParams(dimension_semantics=sem, vmem_limit_bytes=VMEM_LIMIT_BYTES)


def _rms(x, g):
    return x * lax.rsqrt(jnp.mean(x * x, axis=-1, keepdims=True) + EPS) * g


def _silu(x):
    return x * jax.nn.sigmoid(x)


def _softplus(x):
    return jnp.maximum(x, 0.0) + jnp.log1p(jnp.exp(-jnp.abs(x)))


def _dot(a, b):
    return jnp.dot(a, b, preferred_element_type=F32)


def _dot_nt(a, b):
    return lax.dot_general(a, b, _NT, preferred_element_type=F32)


def _layer_spec(arr, layer):
    tail = arr.shape[1:]
    zeros = (0,) * len(tail)
    return pl.BlockSpec((None,) + tail, lambda *_: (layer,) + zeros)


def _ada_kernel(c_ref, w_ref, b_ref, o_ref):
    s = _silu(c_ref[...]).astype(BF16)
    o_ref[...] = _dot(s, w_ref[...].astype(BF16)) + b_ref[...]


def _modulations(cond, w_ada, b_ada):
    out = pl.pallas_call(
        _ada_kernel,
        grid=(DEPTH, 6 * D_MODEL // ADA_TILE),
        in_specs=[pl.BlockSpec((MOD_ROWS, D_MODEL), lambda l, n: (0, 0)),
                  pl.BlockSpec((None, D_MODEL, ADA_TILE), lambda l, n: (l, 0, n)),
                  pl.BlockSpec((None, 1, ADA_TILE), lambda l, n: (l, 0, n))],
        out_specs=pl.BlockSpec((None, MOD_ROWS, ADA_TILE), lambda l, n: (l, 0, n)),
        out_shape=jax.ShapeDtypeStruct((DEPTH, MOD_ROWS, 6 * D_MODEL), F32),
        compiler_params=_params("arbitrary", "arbitrary"),
        name="ada",
    )(cond, w_ada, b_ada.reshape(DEPTH, 1, 6 * D_MODEL))
    return out.reshape(DEPTH, MOD_ROWS, 6, D_MODEL)


def _mod_spec(layer, latent):
    if latent:
        per_seq = DEC_SEQ // ROW_TILE
        return pl.BlockSpec((None, None, 6, D_MODEL), lambda i, *_: (layer, 1 + i // per_seq, 0, 0))
    return pl.BlockSpec((None, None, 6, D_MODEL), lambda i, *_: (layer, 0, 0, 0))


def _proj_kernel(x_ref, mod_ref, g_ref, w_ref, *rest, widths, regroup, gate_cols):
    n_out = len(widths) + (1 if gate_cols else 0)
    o_refs, w_s = rest[:n_out], rest[n_out]

    @pl.when(pl.program_id(0) == 0)
    def _():
        for dst, src, width in regroup:
            if src is None:
                w_s[:, dst:dst + width] = jnp.zeros((D_MODEL, width), BF16)
            else:
                w_s[:, dst:dst + width] = w_ref[:, src:src + width]
        if gate_cols:
            pieces = [w_ref[:, lo:hi].astype(F32) for lo, hi in gate_cols]
            n_gates = sum(hi - lo for lo, hi in gate_cols)
            gates = jnp.concatenate(pieces + [jnp.zeros((D_MODEL, LANES - n_gates), F32)], axis=1)
            rest[n_out + 1][...] = gates.T[0:n_gates, :].astype(BF16)

    h = _rms(x_ref[...], g_ref[0:1, :]) * (1.0 + mod_ref[1:2, :]) + mod_ref[0:1, :]
    hb = h.astype(BF16)
    off = 0
    for o_ref, wd in zip(o_refs, widths):
        o_ref[...] = _dot(hb, w_s[:, off:off + wd])
        off += wd
    if gate_cols:
        o_refs[-1][...] = _dot_nt(rest[n_out + 1][...], hb)


def _project(x, mods, layer, latent, gains, w_all, widths, regroup, gate_cols=()):
    n = x.shape[0]
    out_specs = [pl.BlockSpec((ROW_TILE, wd), lambda i: (i, 0)) for wd in widths]
    out_shape = [jax.ShapeDtypeStruct((n, wd), F32) for wd in widths]
    scratch = [pltpu.VMEM((D_MODEL, sum(widths)), BF16)]
    if gate_cols:
        n_gates = sum(hi - lo for lo, hi in gate_cols)
        out_specs.append(pl.BlockSpec((n_gates, ROW_TILE), lambda i: (0, i)))
        out_shape.append(jax.ShapeDtypeStruct((n_gates, n), F32))
        scratch.append(pltpu.VMEM((n_gates, D_MODEL), BF16))
    w_spec = pl.BlockSpec((None,) + w_all.shape[1:], lambda i: (layer // 2, 0, 0), pipeline_mode=pl.Buffered(1))
    return pl.pallas_call(
        functools.partial(_proj_kernel, widths=widths, regroup=regroup, gate_cols=gate_cols),
        grid=(n // ROW_TILE,),
        in_specs=[pl.BlockSpec((ROW_TILE, D_MODEL), lambda i: (i, 0)), _mod_spec(layer, latent),
                  _layer_spec(gains, layer), w_spec],
        out_specs=out_specs, out_shape=out_shape, scratch_shapes=scratch,
        compiler_params=_params("arbitrary"),
        name="proj",
    )(x, mods, gains, w_all)


def _weight_chunks(layer, sub_layer, wo_hbm, wu_hbm, wd_hbm, wo_s, wu_s, wd_s):
    chunks = []
    for r in range(0, D_MODEL, STAGE_ROWS):
        chunks.append((wo_hbm.at[sub_layer, pl.ds(r, STAGE_ROWS), :], wo_s.at[pl.ds(r, STAGE_ROWS), :]))
    for c in range(0, D_FF, FF_TILE):
        for r in range(0, D_MODEL, STAGE_ROWS):
            chunks.append((wu_hbm.at[layer, pl.ds(r, STAGE_ROWS), pl.ds(c, FF_TILE)],
                           wu_s.at[pl.ds(r, STAGE_ROWS), pl.ds(c, FF_TILE)]))
        for r in range(c, c + FF_TILE, STAGE_ROWS):
            chunks.append((wd_hbm.at[layer, pl.ds(r, STAGE_ROWS), :], wd_s.at[pl.ds(r, STAGE_ROWS), :]))
    return chunks


CHUNKS_OUT = D_MODEL // STAGE_ROWS
CHUNKS_PER_TILE = D_MODEL // STAGE_ROWS + FF_TILE // STAGE_ROWS


def _channel_kernel(a1p_ref, a2p_ref, xp_ref, a1s_ref, a2s_ref, xs_ref, mod_ref, g_ref, wo_hbm, wu_hbm, wd_hbm,
                    op_ref, os_ref, wo_s, wu_s, wd_s, stage, sem, *, layer, sub_layer, prompt_steps):
    step = pl.program_id(0)

    def staging_thunks():
        chunks = _weight_chunks(layer, sub_layer, wo_hbm, wu_hbm, wd_hbm, wo_s, wu_s, wd_s)
        copies = [pltpu.make_async_copy(src, stage.at[k % 2], sem.at[k % 2]) for k, (src, _) in enumerate(chunks)]
        copies[0].start()

        def land(k):
            def run():
                if k + 1 < len(chunks):
                    copies[k + 1].start()
                copies[k].wait()
                chunks[k][1][...] = stage[k % 2].astype(BF16)
            return run
        return [land(k) for k in range(len(chunks))]

    def rows_block(a1_ref, a2_ref, x_ref, o_ref, staging=()):
        staging = list(staging)

        def stage_next(count):
            for _ in range(min(count, len(staging))):
                staging.pop(0)()

        stage_next(CHUNKS_OUT)
        half = a1_ref.shape[1]
        tiles = range(0, D_FF, FF_TILE)
        blocks = [slice(r0, r0 + SUB_ROWS) for r0 in range(0, ROW_TILE, SUB_ROWS)]

        def prologue(rows):
            y = (_dot(a1_ref[rows, :].astype(BF16), wo_s[0:half, :])
                 + _dot(a2_ref[rows, :].astype(BF16), wo_s[half:, :]))
            x1 = x_ref[rows, :] + mod_ref[2:3, :] * _rms(y, g_ref[1:2, :])
            h = (_rms(x1, g_ref[2:3, :]) * (1.0 + mod_ref[4:5, :]) + mod_ref[3:4, :]).astype(BF16)
            return x1, h

        def mlp_tile(h, c):
            u = jnp.square(jnp.maximum(_dot(h, wu_s[:, c:c + FF_TILE]), 0.0)).astype(BF16)
            return _dot(u, wd_s[c:c + FF_TILE, :])

        def epilogue(rows, x1, acc):
            o_ref[rows, :] = x1 + mod_ref[5:6, :] * _rms(acc, g_ref[3:4, :])

        ready = {0: prologue(blocks[0])}
        stage_next(CHUNKS_PER_TILE)
        done = None
        for b, rows in enumerate(blocks):
            x1, h = ready.pop(b)
            acc = None
            for t, c in enumerate(tiles):
                part = mlp_tile(h, c)
                stage_next(CHUNKS_PER_TILE)
                acc = part if acc is None else acc + part
                if t == 0 and b + 1 < len(blocks):
                    ready[b + 1] = prologue(blocks[b + 1])
                if t == 0 and done is not None:
                    epilogue(*done)
                    done = None
            done = (rows, x1, acc)
        epilogue(*done)

    @pl.when(step == 0)
    def _():
        rows_block(a1p_ref, a2p_ref, xp_ref, op_ref, staging_thunks())

    @pl.when((step > 0) & (step < prompt_steps))
    def _():
        rows_block(a1p_ref, a2p_ref, xp_ref, op_ref)

    @pl.when(step >= prompt_steps)
    def _():
        rows_block(a1s_ref, a2s_ref, xs_ref, os_ref)


def _channel(prompt, latent, mods, layer, gains, w_out, w_up, w_down):
    n_p, n_s = prompt[2].shape[0], latent[2].shape[0]
    steps_p, steps_s = n_p // ROW_TILE, n_s // ROW_TILE
    per_seq = DEC_SEQ // ROW_TILE
    row_p = lambda i: (jnp.minimum(i, steps_p - 1), 0)
    row_s = lambda i: (jnp.maximum(i - steps_p, 0), 0)
    mod_spec = pl.BlockSpec((None, None, 6, D_MODEL),
                            lambda i: (layer, jnp.where(i < steps_p, 0, 1 + (i - steps_p) // per_seq), 0, 0))
    hbm = pl.BlockSpec(memory_space=pl.ANY)
    specs = lambda arrs, row: [pl.BlockSpec((ROW_TILE, a.shape[1]), row) for a in arrs]
    return pl.pallas_call(
        functools.partial(_channel_kernel, layer=layer, sub_layer=layer // 2, prompt_steps=steps_p),
        grid=(steps_p + steps_s,),
        in_specs=specs(prompt, row_p) + specs(latent, row_s) + [mod_spec, _layer_spec(gains, layer), hbm, hbm, hbm],
        out_specs=[pl.BlockSpec((ROW_TILE, D_MODEL), row_p), pl.BlockSpec((ROW_TILE, D_MODEL), row_s)],
        out_shape=[jax.ShapeDtypeStruct((n_p, D_MODEL), F32), jax.ShapeDtypeStruct((n_s, D_MODEL), F32)],
        scratch_shapes=[pltpu.VMEM((D_MODEL, D_MODEL), BF16), pltpu.VMEM((D_MODEL, D_FF), BF16),
                        pltpu.VMEM((D_FF, D_MODEL), BF16), pltpu.VMEM((2, STAGE_ROWS, D_MODEL), F32),
                        pltpu.SemaphoreType.DMA((2,))],
        compiler_params=_params("arbitrary"),
        name="channel",
    )(*prompt, *latent, mods, gains, w_out, w_up, w_down)


def _row_iota(shape):
    return lax.broadcasted_iota(jnp.int32, shape, 0)


def _lane_iota(shape):
    return lax.broadcasted_iota(jnp.int32, shape, 1)


def _pair_lanes(shape):
    return _lane_iota(shape) < LANES // 2


def _cumsum_lanes(x, n_fwd):
    t = x.shape[1]
    si, ti = _row_iota((t, t)), _lane_iota((t, t))
    upper = jnp.where(si <= ti, 1.0, 0.0).astype(BF16)
    lower = jnp.where(si >= ti, 1.0, 0.0).astype(BF16)
    hi = x.astype(BF16)
    rest = x - hi.astype(F32)
    mid = rest.astype(BF16)
    lo = (rest - mid.astype(F32)).astype(BF16)
    pre = _dot(hi, upper) + _dot(mid, upper) + _dot(lo, upper)
    suf = _dot(hi, lower) + _dot(mid, lower) + _dot(lo, lower)
    return jnp.where(_row_iota(x.shape) < n_fwd, pre, suf)


def _cummax_lanes(x, n_fwd):
    t = x.shape[1]
    lane = _lane_iota(x.shape)
    pre, suf = x, x
    k = 1
    while k < t:
        pre = jnp.maximum(pre, jnp.where(lane >= k, pltpu.roll(pre, k, 1), -jnp.inf))
        suf = jnp.maximum(suf, jnp.where(lane < t - k, pltpu.roll(suf, t - k, 1), -jnp.inf))
        k *= 2
    return jnp.where(_row_iota(x.shape) < n_fwd, pre, suf)


def _columns(row_arrays):
    t = row_arrays[0].shape[1]
    used = sum(a.shape[0] for a in row_arrays)
    return jnp.concatenate(list(row_arrays) + [jnp.zeros((LANES - used, t), F32)], axis=0).T


def _dwconv_silu(x, w, b):
    t = x.shape[0]
    row = _row_iota(x.shape)
    acc = x * w[CONV_K // 2:CONV_K // 2 + 1, :] + b
    for j in range(CONV_K):
        d = j - CONV_K // 2
        if d == 0:
            continue
        shifted = pltpu.roll(x, (-d) % t, 0)
        valid = (row >= -d) if d < 0 else (row < t - d)
        acc = acc + jnp.where(valid, shifted, 0.0) * w[j:j + 1, :]
    return _silu(acc)


def _causal_exponent(expo, r0, k0, reverse):
    ti = r0 + _row_iota(expo.shape)
    si = k0 + _lane_iota(expo.shape)
    keep = (si >= ti) if reverse else (si <= ti)
    return jnp.where(keep, expo, -jnp.inf)


def _pair_split(x):
    first = _pair_lanes(x.shape)
    zero = jnp.zeros_like(x)
    return jnp.concatenate([jnp.where(first, x, zero), jnp.where(first, zero, x)], axis=0)


def _key_range(d, r0, seq):
    return (0, r0 + Q_TILE) if d == 0 else (r0, seq)


N_FUSED_INPUTS = 5
PROJ_CHUNK = 512

MLSTM_PIECES = ((0, 2 * A_QK), (2 * A_QK, A_V), (2 * A_QK + A_V, A_V))
MLSTM_GATES = (_GATES_LO, _Z_LO)
MLSTM_W_COLS = _GATES_LO + LANES
SSD_PIECES = ((_Z_LO + B_INNER, B_XBC), (_Z_LO, B_INNER))
SSD_GATES = (_DT_LO, _DT_LO + 2 * NH_B)
CTX_PIECES = tuple((sum(ODD_WIDTHS[:i]), wd) for i, wd in enumerate(ODD_WIDTHS[:-1])) + (
    (sum(ODD_WIDTHS[:-1]), LANES, ROPE_D),)


def _fused_project(step, fused_refs, scratch, pieces, gate_cols, mix):
    x_cur_ref, x_next_ref, mod_ref, gain_ref, w_ref = fused_refs
    w_s = scratch[0]
    n_head = 2 if gate_cols else 1
    n = len(pieces) + (1 if gate_cols else 0)
    sets = (scratch[n_head:n_head + n], scratch[n_head + n:n_head + 2 * n])

    def normed(x_ref):
        h = _rms(x_ref[...], gain_ref[0:1, :]) * (1.0 + mod_ref[1:2, :]) + mod_ref[0:1, :]
        return h.astype(BF16)

    def chunk_thunks(hb, dst):
        def gates():
            dst[0][...] = _dot_nt(scratch[1][...], hb)

        def columns(ref, c, src, width):
            def run():
                ref[:, c:c + width] = _dot(hb, w_s[:, src:src + width])
            return run

        thunks = [gates] if gate_cols else []
        off = 0
        for ref, piece in zip(dst[-len(pieces):], pieces):
            width = piece[1]
            thunks += [columns(ref, c, off + c, min(PROJ_CHUNK, width - c)) for c in range(0, width, PROJ_CHUNK)]
            off += width
        return thunks

    @pl.when(step == 0)
    def _():
        off = 0
        for piece in pieces:
            src, width = piece[0], piece[1]
            valid = piece[2] if len(piece) > 2 else width
            w_s[:, off:off + valid] = w_ref[:, src:src + valid]
            if valid < width:
                w_s[:, off + valid:off + width] = jnp.zeros((D_MODEL, width - valid), BF16)
            off += width
        if gate_cols:
            lo, hi = gate_cols
            gates = jnp.concatenate([w_ref[:, lo:hi].astype(F32), jnp.zeros((D_MODEL, LANES - (hi - lo)), F32)], axis=1)
            scratch[1][...] = gates.T[0:hi - lo, :].astype(BF16)
        for thunk in chunk_thunks(normed(x_cur_ref), sets[0]):
            thunk()

    for parity in range(2):
        @pl.when(step % 2 == parity)
        def _(parity=parity):
            cur = sets[parity]
            refs = list(cur[1:]) + [cur[0]] if gate_cols else list(cur)
            mix(*refs, chunk_thunks(normed(x_next_ref), sets[1 - parity]))


def _fused_specs(seq, nseq, layer, w_all, w_cols):
    nxt = lambda s: (jnp.minimum(s + 1, nseq - 1), 0)
    return [pl.BlockSpec((seq, D_MODEL), lambda s: (s, 0)), pl.BlockSpec((seq, D_MODEL), nxt),
            pl.BlockSpec((None, None, 6, D_MODEL), lambda s: (layer, 0, 0, 0)),
            pl.BlockSpec((None, 4, D_MODEL), lambda s: (layer, 0, 0)),
            pl.BlockSpec((None, D_MODEL, w_cols), lambda s: (layer // 2, 0, 0), pipeline_mode=pl.Buffered(1))]


def _fused_scratch(seq, pieces, n_gates):
    head = [pltpu.VMEM((D_MODEL, sum(p[1] for p in pieces)), BF16)]
    one_set = [pltpu.VMEM((seq, p[1]), F32) for p in pieces]
    if n_gates:
        head.append(pltpu.VMEM((n_gates, D_MODEL), BF16))
        one_set = [pltpu.VMEM((n_gates, seq), F32)] + one_set
    return head + one_set + one_set


def _run_share(side_work, stages_left):
    for _ in range(-(-len(side_work) // stages_left)):
        side_work.pop(0)()


def _mlstm_kernel(*refs, seq, has_state, emit_state, n_carried, slot, fused):
    assert not (has_state and emit_state)
    it = iter(refs)
    if fused:
        fused_refs = [next(it) for _ in range(N_FUSED_INPUTS)]
    else:
        qk_ref, v_ref, o_ref, g_ref = (next(it) for _ in range(4))
    cw_ref, cb_ref, gb_ref, anw_ref = (next(it) for _ in range(4))
    if has_state:
        c0_ref, m0_ref = next(it), next(it)
    for _ in range(n_carried):
        next(it)
    ha_ref = next(it)
    if emit_state:
        cn_ref, nn_ref, mn_ref = next(it), next(it), next(it)
        if n_carried == 0:
            for other in range(N_EVEN):
                if other != slot:
                    cn_ref[other] = jnp.zeros(cn_ref.shape[1:], F32)
                    nn_ref[other] = jnp.zeros(nn_ref.shape[1:], F32)
                    mn_ref[other] = jnp.zeros(mn_ref.shape[1:], F32)
            cn_ref, nn_ref, mn_ref = cn_ref.at[slot], nn_ref.at[slot], mn_ref.at[slot]

    def mix(qk_ref, v_ref, o_ref, g_ref, side_work=()):
        side_work = list(side_work)
        n_ch = 2 * NH_A
        log_i = g_ref[0:n_ch, :] + gb_ref[0:n_ch, 0:1]
        f_pre = g_ref[n_ch:2 * n_ch, :] + gb_ref[n_ch:2 * n_ch, 0:1]
        log_f = jnp.minimum(f_pre, 0.0) - jnp.log1p(jnp.exp(-jnp.abs(f_pre)))
        b = _cumsum_lanes(log_f, NH_A)
        a = log_i - b
        m_run = _cummax_lanes(a, NH_A)
        if has_state:
            m0 = m0_ref[:, 0:1]
            m_run = jnp.maximum(m_run, m0)
        else:
            m_run = jnp.maximum(m_run, 0.0)
        by_time = [m_run, jnp.exp(-(b + m_run))]
        if has_state:
            by_time.append(jnp.exp(m0 - m_run))
        if emit_state:
            fwd = _row_iota((n_ch, 1)) < NH_A
            b_last = jnp.where(fwd, b[:, seq - 1:seq], b[:, 0:1])
            m_last = jnp.where(fwd, m_run[:, seq - 1:seq], m_run[:, 0:1])
            mn_ref[...] = jnp.broadcast_to(b_last + m_last, (n_ch, LANES))
            by_time.append(jnp.exp(a - m_last))
        cols = _columns(by_time)

        ones = jnp.ones((seq, LANES), F32)
        for h in range(NH_A):
            _run_share(side_work, NH_A - h)
            cq = slice(h * DK_A, (h + 1) * DK_A)
            ck = slice(A_QK + h * DK_A, A_QK + (h + 1) * DK_A)
            cv = slice(h * DV_A, (h + 1) * DV_A)
            q = _dwconv_silu(qk_ref[:, cq], cw_ref[:, cq], cb_ref[:, cq])
            k = _dwconv_silu(qk_ref[:, ck], cw_ref[:, ck], cb_ref[:, ck]) * (DK_A ** -0.5)
            qb = q.astype(BF16)
            kb = k.astype(BF16)
            vh = v_ref[:, cv]
            vaug = jnp.concatenate([vh, ones], axis=1).astype(BF16)
            for r0 in range(0, seq, Q_TILE):
                rows = slice(r0, r0 + Q_TILE)
                s = _dot_nt(qb[rows], kb)
                hsum = None
                for d in range(2):
                    c = d * NH_A + h
                    k0, k1 = _key_range(d, r0, seq)
                    expo = _causal_exponent(a[c:c + 1, k0:k1] - cols[rows, c:c + 1], r0, k0, d == 1)
                    p = (s[:, k0:k1] * jnp.exp(expo)).astype(BF16)
                    acc = _dot(p, vaug[k0:k1])
                    if has_state:
                        acc = acc + (cols[rows, 2 * n_ch + c:2 * n_ch + c + 1]
                                     * _dot(qb[rows], c0_ref[d, h].astype(BF16)))
                    hd = acc[:, 0:DV_A] / jnp.maximum(jnp.abs(acc[:, DV_A:]), cols[rows, n_ch + c:n_ch + c + 1])
                    hsum = hd if hsum is None else hsum + hd
                og = jax.nn.sigmoid(o_ref[rows, cv]) * hsum
                ha_ref[rows, cv] = _rms(og, anw_ref[:, cv])
            if emit_state:
                for d in range(2):
                    c = d * NH_A + h
                    kw = k * cols[:, 2 * n_ch + c:2 * n_ch + c + 1]
                    cn_ref[d, h] = _dot(kw.T.astype(BF16), vh.astype(BF16))
                    nn_ref[d, h:h + 1, :] = jnp.sum(kw, axis=0, keepdims=True)

    if fused:
        _fused_project(pl.program_id(0), fused_refs, list(it), MLSTM_PIECES, MLSTM_GATES, mix)
    else:
        mix(qk_ref, v_ref, o_ref, g_ref)


def _state_out_specs(shapes, nseq, sub_layer, carried):
    out_specs, out_shape = [], []
    for shp in shapes:
        zeros = (0,) * len(shp)
        if carried is None:
            out_specs.append(pl.BlockSpec((None, N_EVEN) + shp, lambda s, z=zeros: (s, 0) + z))
        else:
            out_specs.append(pl.BlockSpec((None, None) + shp, lambda s, z=zeros: (s, sub_layer) + z))
        out_shape.append(jax.ShapeDtypeStruct((nseq, N_EVEN) + shp, F32))
    return out_specs, out_shape


MLSTM_STATE_SHAPES = ((2, NH_A, DK_A, DV_A), (2, NH_A, DK_A), (2 * NH_A, LANES))
SSD_STATE_SHAPES = ((2, NH_B, HP_B, DSTATE),)


def _fused_mixer(kernel_fn, name, x, mods, layer, gains, w_all, w_cols, params, seq, pieces, gate_cols,
                 out_width, state_shapes, carried):
    n = x.shape[0]
    nseq = n // seq
    sub_layer = layer // 2
    in_specs = _fused_specs(seq, nseq, layer, w_all, w_cols) + [_layer_spec(p, sub_layer) for p in params]
    args = [x, x, mods, gains, w_all, *params]
    state_specs, state_shape = _state_out_specs(state_shapes, nseq, sub_layer, carried)
    aliases = {}
    if carried is not None:
        aliases = {len(args) + i: 1 + i for i in range(len(state_shapes))}
        in_specs += [pl.BlockSpec(memory_space=pl.ANY)] * len(state_shapes)
        args += list(carried)
    return pl.pallas_call(
        functools.partial(kernel_fn, seq=seq, has_state=False, emit_state=True, n_carried=len(aliases),
                          slot=sub_layer, fused=True),
        grid=(nseq,), in_specs=in_specs,
        out_specs=[pl.BlockSpec((seq, out_width), lambda s: (s, 0))] + state_specs,
        out_shape=[jax.ShapeDtypeStruct((n, out_width), F32)] + state_shape,
        scratch_shapes=_fused_scratch(seq, pieces, gate_cols[1] - gate_cols[0]),
        input_output_aliases=aliases, compiler_params=_params("arbitrary"), name=name,
    )(*args)


def _mlstm_latent(qk, v, o, gates, params, seq, sub_layer, state):
    n = qk.shape[0]
    row = lambda s: (s, 0)
    c0_aug, m0 = state
    in_specs = [pl.BlockSpec((seq, 2 * A_QK), row), pl.BlockSpec((seq, A_V), row), pl.BlockSpec((seq, A_V), row),
                pl.BlockSpec((GATE_ROWS, seq), lambda s: (0, s))]
    in_specs += [_layer_spec(p, sub_layer) for p in params]
    in_specs += [pl.BlockSpec((None, None) + c0_aug.shape[2:], lambda s: (s, sub_layer, 0, 0, 0, 0)),
                 pl.BlockSpec((None, None) + m0.shape[2:], lambda s: (s, sub_layer, 0, 0))]
    return pl.pallas_call(
        functools.partial(_mlstm_kernel, seq=seq, has_state=True, emit_state=False, n_carried=0, slot=sub_layer,
                          fused=False),
        grid=(n // seq,), in_specs=in_specs, out_specs=pl.BlockSpec((seq, A_V), row),
        out_shape=jax.ShapeDtypeStruct((n, A_V), F32),
        compiler_params=_params("arbitrary"), name="mlstm",
    )(qk, v, o, gates, *params, c0_aug, m0)


def _ssd_kernel(*refs, seq, has_state, emit_state, n_carried, slot, fused):
    assert not (has_state and emit_state)
    it = iter(refs)
    if fused:
        fused_refs = [next(it) for _ in range(N_FUSED_INPUTS)]
    else:
        xbc_ref, z_ref, g_ref = (next(it) for _ in range(3))
    cw_ref, cb_ref, dtb_ref, alog_ref, dsk_ref, bnw_ref = (next(it) for _ in range(6))
    if has_state:
        s0_ref = next(it)
    for _ in range(n_carried):
        next(it)
    yb_ref = next(it)
    if emit_state:
        sn_ref = next(it)
        if n_carried == 0:
            for other in range(N_EVEN):
                if other != slot:
                    sn_ref[other] = jnp.zeros(sn_ref.shape[1:], F32)
            sn_ref = sn_ref.at[slot]

    gate_row0 = 0 if fused else 4 * NH_A

    def mix(xbc_ref, z_ref, g_ref, side_work=()):
        side_work = list(side_work)
        n_ch = 2 * NH_B
        dt = _softplus(g_ref[gate_row0:gate_row0 + n_ch, :] + dtb_ref[:, 0:1])
        acum = _cumsum_lanes(dt * (-jnp.exp(alog_ref[:, 0:1])), NH_B)
        key_shift = acum - jnp.log(dt)
        by_time = [acum]
        if has_state:
            by_time.append(jnp.exp(acum))
        if emit_state:
            fwd = _row_iota((n_ch, 1)) < NH_B
            a_last = jnp.where(fwd, acum[:, seq - 1:seq], acum[:, 0:1])
            by_time.append(jnp.exp(a_last - acum) * dt)
        cols = _columns(by_time)

        gw = R_B * HP_B
        first = _pair_lanes((seq, LANES))
        for g in range(NG_B):
            _run_share(side_work, NG_B - g)
            cx = slice(g * gw, (g + 1) * gw)
            cb_ = slice(B_INNER + g * DSTATE, B_INNER + (g + 1) * DSTATE)
            cc = slice(B_INNER + B_BC + g * DSTATE, B_INNER + B_BC + (g + 1) * DSTATE)
            xg = _dwconv_silu(xbc_ref[:, cx], cw_ref[:, cx], cb_ref[:, cx])
            bg = _dwconv_silu(xbc_ref[:, cb_], cw_ref[:, cb_], cb_ref[:, cb_]).astype(BF16)
            cg = _dwconv_silu(xbc_ref[:, cc], cw_ref[:, cc], cb_ref[:, cc]).astype(BF16)
            xbd = [_pair_split(xg[:, p * LANES:(p + 1) * LANES]).astype(BF16) for p in range(R_B // 2)]
            for r0 in range(0, seq, Q_TILE):
                rows = slice(r0, r0 + Q_TILE)
                cb_scores = _dot_nt(cg[rows], bg)
                ys = []
                for p in range(R_B // 2):
                    weights, inputs = [], []
                    for d in range(2):
                        k0, k1 = _key_range(d, r0, seq)
                        for i in range(2):
                            c = d * NH_B + g * R_B + 2 * p + i
                            expo = _causal_exponent(cols[rows, c:c + 1] - key_shift[c:c + 1, k0:k1], r0, k0, d == 1)
                            weights.append((cb_scores[:, k0:k1] * jnp.exp(expo)).astype(BF16))
                            inputs.append(xbd[p][i * seq + k0:i * seq + k1])
                    yp = _dot(jnp.concatenate(weights, axis=1), jnp.concatenate(inputs, axis=0))
                    if has_state:
                        h0 = g * R_B + 2 * p
                        for d in range(2):
                            c = d * NH_B + h0
                            carry = jnp.where(_pair_lanes((Q_TILE, LANES)), cols[rows, n_ch + c:n_ch + c + 1],
                                              cols[rows, n_ch + c + 1:n_ch + c + 2])
                            s0_pair = s0_ref[d, h0:h0 + 2].reshape(2 * HP_B, DSTATE).astype(BF16)
                            yp = yp + carry * _dot_nt(cg[rows], s0_pair)
                    ys.append(yp)
                y = jnp.concatenate(ys, axis=1) + dsk_ref[:, cx] * xg[rows]
                y = y * _silu(z_ref[rows, cx])
                yb_ref[rows, cx] = _rms(y, bnw_ref[:, cx])
            if emit_state:
                for d in range(2):
                    c0 = n_ch + d * NH_B + g * R_B
                    spread = jnp.concatenate([jnp.where(first, cols[:, c0 + 2 * p:c0 + 2 * p + 1],
                                                        cols[:, c0 + 2 * p + 1:c0 + 2 * p + 2])
                                              for p in range(R_B // 2)], axis=1)
                    sn = _dot((xg * spread).T.astype(BF16), bg)
                    for r in range(R_B):
                        sn_ref[d, g * R_B + r] = sn[r * HP_B:(r + 1) * HP_B, :]

    if fused:
        _fused_project(pl.program_id(0), fused_refs, list(it), SSD_PIECES, SSD_GATES, mix)
    else:
        mix(xbc_ref, z_ref, g_ref)


def _ssd_latent(xbc, z, gates, params, seq, sub_layer, state):
    n = xbc.shape[0]
    row = lambda s: (s, 0)
    in_specs = [pl.BlockSpec((seq, B_XBC), row), pl.BlockSpec((seq, B_INNER), row),
                pl.BlockSpec((GATE_ROWS, seq), lambda s: (0, s))]
    in_specs += [_layer_spec(p, sub_layer) for p in params]
    in_specs += [pl.BlockSpec((None, None, 2, NH_B, HP_B, DSTATE), lambda s: (s, sub_layer, 0, 0, 0, 0))]
    return pl.pallas_call(
        functools.partial(_ssd_kernel, seq=seq, has_state=True, emit_state=False, n_carried=0, slot=sub_layer,
                          fused=False),
        grid=(n // seq,), in_specs=in_specs, out_specs=pl.BlockSpec((seq, B_INNER), row),
        out_shape=jax.ShapeDtypeStruct((n, B_INNER), F32),
        compiler_params=_params("arbitrary"), name="ssd",
    )(xbc, z, gates, *params, state)


def _shared_split(x, x_swapped, kh):
    first = _pair_lanes(x.shape)
    zero = jnp.zeros_like(x)
    top, bottom = (x, x_swapped) if kh == 0 else (x_swapped, x)
    return jnp.concatenate([jnp.where(first, top, zero), jnp.where(first, zero, bottom)], axis=0)


def _pair_probs(s, sinks=None, valid=None):
    n_keys = s.shape[1] // 2
    probs, maxes = [], []
    for i in range(2):
        si = s[:, i * n_keys:(i + 1) * n_keys]
        if valid is not None:
            si = jnp.where(valid, si, -jnp.inf)
        m = jnp.max(si, axis=1, keepdims=True)
        if sinks is not None:
            m = jnp.maximum(m, sinks[i])
        probs.append(jnp.exp(si - m))
        maxes.append(m)
    return jnp.concatenate(probs, axis=1).astype(BF16), maxes


def _pair_output(p, maxes, vbd, sinks=None):
    o = _dot(p, vbd)
    den = o[:, LANES:]
    if sinks is not None:
        den = den + jnp.where(_pair_lanes(den.shape), jnp.exp(sinks[0] - maxes[0]), jnp.exp(sinks[1] - maxes[1]))
    return o[:, :LANES] / den


def _run_pairs(items, valid=None, side_work=()):
    side_work = list(side_work)
    s_next = items[0][0]()
    for idx, (_, values, sinks, out_ref, cols) in enumerate(items):
        _run_share(side_work, len(items) - idx)
        s_cur = s_next
        if idx + 1 < len(items):
            s_next = items[idx + 1][0]()
        p, maxes = _pair_probs(s_cur, sinks, valid if sinks is not None else None)
        out_ref[:, cols] = _pair_output(p, maxes, values(), sinks)


def _pair_sinks(sink_ref, n):
    return sink_ref[0:1, n:n + 1], sink_ref[0:1, n + 1:n + 2]


def _mla_queries(qa_ref, qan_ref, wqb_ref):
    return _dot(_rms(qa_ref[...], qan_ref[...]).astype(BF16), wqb_ref[...]) * MLA_SCALE


def _attn_ctx_kernel(*refs, seq, n_carried, slot, fused):
    it = iter(refs)
    if fused:
        fused_refs = [next(it) for _ in range(N_FUSED_INPUTS)]
    else:
        proj_refs = [next(it) for _ in range(len(ODD_WIDTHS))]
    sink_ref, qan_ref, kvn_ref, wqb_ref, wkvb_ref = (next(it) for _ in range(5))
    for _ in range(n_carried):
        next(it)
    oc_ref, od_ref, ckv_ref = next(it), next(it), next(it)
    if fused:
        new_k_ref, new_v_ref, new_kpe_ref = next(it), next(it), next(it)
    if n_carried == 0:
        for other in range(N_ODD):
            if other != slot:
                ckv_ref[other] = jnp.zeros(ckv_ref.shape[1:], F32)
        ckv_ref = ckv_ref.at[slot]

    def mix(qc_ref, kc_ref, vc_ref, qa_ref, kva_ref, kpe_ref, side_work=()):
        if fused:
            new_k_ref[...] = kc_ref[...]
            new_v_ref[...] = vc_ref[...]
            new_kpe_ref[...] = kpe_ref[...]
        ones_bd = _pair_split(jnp.ones((seq, LANES), F32))
        kc, vc = kc_ref[...], vc_ref[...]
        kc_sw, vc_sw = pltpu.roll(kc, HD_C, 1), pltpu.roll(vc, HD_C, 1)
        qd = _mla_queries(qa_ref, qan_ref, wqb_ref)
        ckv = _rms(kva_ref[...], kvn_ref[...])
        ckv_ref[...] = ckv
        kv = _dot(ckv.astype(BF16), wkvb_ref[...])
        kpe = kpe_ref[...]
        kpe_bd = jnp.concatenate([kpe, pltpu.roll(kpe, ROPE_D, 1)], axis=0)
        nope_w = NH_D * NOPE_D
        items = []
        for kh in range(NKV_C):
            for n in range(kh * G_C, (kh + 1) * G_C, 2):
                cols = slice(n * HD_C, (n + 2) * HD_C)
                items.append((lambda cols=cols, kh=kh: _dot_nt((qc_ref[:, cols] * (HD_C ** -0.5)).astype(BF16),
                                                               _shared_split(kc, kc_sw, kh).astype(BF16)),
                              lambda kh=kh: jnp.concatenate([_shared_split(vc, vc_sw, kh), ones_bd], axis=1).astype(BF16),
                              _pair_sinks(sink_ref, n), oc_ref, cols))
        for i in range(NH_D // 2):
            cols = slice(i * LANES, (i + 1) * LANES)
            vcols = slice(nope_w + i * LANES, nope_w + (i + 1) * LANES)
            items.append((lambda cols=cols, vcols=vcols: _dot_nt(
                              jnp.concatenate([qd[:, cols], qd[:, vcols]], axis=1).astype(BF16),
                              jnp.concatenate([_pair_split(kv[:, cols]), kpe_bd], axis=1).astype(BF16)),
                          lambda vcols=vcols: jnp.concatenate([_pair_split(kv[:, vcols]), ones_bd], axis=1).astype(BF16),
                          None, od_ref, cols))
        _run_pairs(items, side_work=side_work)

    if fused:
        _fused_project(pl.program_id(0), fused_refs, list(it), CTX_PIECES, None, mix)
    else:
        mix(*proj_refs)


def _attn_ctx(x, mods, layer, gains, w_all, params, seq, carried=None):
    n = x.shape[0]
    nseq = n // seq
    sub_layer = layer // 2
    row = lambda s: (s, 0)
    in_specs = _fused_specs(seq, nseq, layer, w_all, w_all.shape[2]) + [_layer_spec(p, sub_layer) for p in params]
    args = [x, x, mods, gains, w_all, *params]
    half = NH_C * HD_C
    aliases = {}
    if carried is None:
        ckv_spec = pl.BlockSpec((None, N_ODD, seq, KV_RANK), lambda s: (s, 0, 0, 0))
    else:
        ckv_spec = pl.BlockSpec((None, None, seq, KV_RANK), lambda s: (s, sub_layer, 0, 0))
        aliases = {len(args): 2}
        in_specs.append(pl.BlockSpec(memory_space=pl.ANY))
        args.append(carried)
    kv_w = NKV_C * HD_C
    return pl.pallas_call(
        functools.partial(_attn_ctx_kernel, seq=seq, n_carried=len(aliases), slot=sub_layer, fused=True),
        grid=(nseq,), in_specs=in_specs,
        out_specs=[pl.BlockSpec((seq, half), row), pl.BlockSpec((seq, half), row), ckv_spec,
                   pl.BlockSpec((seq, kv_w), row), pl.BlockSpec((seq, kv_w), row), pl.BlockSpec((seq, LANES), row)],
        out_shape=[jax.ShapeDtypeStruct((n, half), F32), jax.ShapeDtypeStruct((n, half), F32),
                   jax.ShapeDtypeStruct((nseq, N_ODD, seq, KV_RANK), F32),
                   jax.ShapeDtypeStruct((n, kv_w), F32), jax.ShapeDtypeStruct((n, kv_w), F32),
                   jax.ShapeDtypeStruct((n, LANES), F32)],
        scratch_shapes=_fused_scratch(seq, CTX_PIECES, 0),
        input_output_aliases=aliases, compiler_params=_params("arbitrary"), name="attn_ctx",
    )(*args)


def _rope(x, cos, sin, half):
    parts = []
    lane = _lane_iota((x.shape[0], LANES))
    first = (lane & (2 * half - 1)) < half
    for i in range(x.shape[1] // LANES):
        xi = x[:, i * LANES:(i + 1) * LANES]
        partner = jnp.where(first, -pltpu.roll(xi, LANES - half, 1), pltpu.roll(xi, half, 1))
        parts.append(xi * cos + partner * sin)
    return parts[0] if len(parts) == 1 else jnp.concatenate(parts, axis=1)


def _attn_lat_kernel(qc_ref, qa_ref, ropeq_ref, kc_ref, vc_ref, kva_ref, kpe_ref, kctx_ref, vctx_ref, ckvctx_ref,
                     kpectx_ref, rope_ref, sink_ref, qan_ref, kvn_ref, wqb_ref, wkvb_ref, oc_ref, od_ref,
                     kwin_s, vwin_s, kext_s, vext_s, *, seq, past):
    qi = pl.program_id(1)
    nope_w = NH_D * NOPE_D
    n_all = past + seq
    ctx0 = 2 * WINDOW + seq

    @pl.when(qi == 0)
    def _():
        zeros = jnp.zeros((WINDOW, LANES), BF16)
        for ref, lat, ctx in ((kwin_s, _rope(kc_ref[...], rope_ref[0], rope_ref[1], HD_C // 2), kctx_ref[...]),
                              (vwin_s, vc_ref[...], vctx_ref[...])):
            lat_sw, ctx_sw = pltpu.roll(lat, HD_C, 1), pltpu.roll(ctx, HD_C, 1)
            for kh in range(NKV_C):
                lat_bd = _shared_split(lat, lat_sw, kh).astype(BF16)
                ctx_bd = _shared_split(ctx, ctx_sw, kh).astype(BF16)
                for i in range(2):
                    ref[kh, i, 0:WINDOW, :] = zeros
                    ref[kh, i, WINDOW:WINDOW + seq, :] = lat_bd[i * seq:(i + 1) * seq]
                    ref[kh, i, WINDOW + seq:ctx0, :] = zeros
                    ref[kh, i, ctx0:, :] = ctx_bd[i * past:(i + 1) * past]
        ckv = _rms(kva_ref[...], kvn_ref[...])
        kv = jnp.concatenate([_dot(ckvctx_ref[...].astype(BF16), wkvb_ref[...]),
                              _dot(ckv.astype(BF16), wkvb_ref[...])], axis=0)
        kpe = jnp.concatenate([kpectx_ref[...], _rope(kpe_ref[...], rope_ref[2], rope_ref[3], ROPE_D // 2)], axis=0)
        kpe_bd = jnp.concatenate([kpe, pltpu.roll(kpe, ROPE_D, 1)], axis=0).astype(BF16)
        ones_bd = _pair_split(jnp.ones((n_all, LANES), F32)).astype(BF16)
        for i in range(NH_D // 2):
            kext_s[i, :, 0:LANES] = _pair_split(kv[:, i * LANES:(i + 1) * LANES]).astype(BF16)
            kext_s[i, :, LANES:] = kpe_bd
            vext_s[i, :, 0:LANES] = _pair_split(kv[:, nope_w + i * LANES:nope_w + (i + 1) * LANES]).astype(BF16)
            vext_s[i, :, LANES:] = ones_bd

    r0 = pl.multiple_of(qi * Q_TILE, Q_TILE)
    nloc = Q_TILE + 2 * WINDOW
    n_keys = nloc + past
    qr = _rope(qc_ref[...], ropeq_ref[0], ropeq_ref[1], HD_C // 2) * (HD_C ** -0.5)
    ti = r0 + _row_iota((Q_TILE, n_keys))
    col = _lane_iota((Q_TILE, n_keys))
    pos = r0 - WINDOW + col
    valid = (col >= nloc) | ((jnp.abs(ti - pos) <= WINDOW) & (pos >= 0) & (pos < seq))
    ones_bd = _pair_split(jnp.ones((n_keys, LANES), F32)).astype(BF16)
    qd = _mla_queries(qa_ref, qan_ref, wqb_ref)
    q_pe = _rope(qd[:, nope_w:], ropeq_ref[2], ropeq_ref[3], ROPE_D // 2)

    def banded(ref, kh):
        return jnp.concatenate([ref[kh, 0, pl.ds(r0, nloc), :], ref[kh, 0, ctx0:, :],
                                ref[kh, 1, pl.ds(r0, nloc), :], ref[kh, 1, ctx0:, :]], axis=0)

    items = []
    for kh in range(NKV_C):
        for n in range(kh * G_C, (kh + 1) * G_C, 2):
            cols = slice(n * HD_C, (n + 2) * HD_C)
            items.append((lambda cols=cols, kh=kh: _dot_nt(qr[:, cols].astype(BF16), banded(kwin_s, kh)),
                          lambda kh=kh: jnp.concatenate([banded(vwin_s, kh), ones_bd], axis=1),
                          _pair_sinks(sink_ref, n), oc_ref, cols))
    for i in range(NH_D // 2):
        cols = slice(i * LANES, (i + 1) * LANES)
        items.append((lambda cols=cols, i=i: _dot_nt(jnp.concatenate([qd[:, cols], q_pe[:, cols]], axis=1).astype(BF16),
                                                     kext_s[i]),
                      lambda i=i: vext_s[i], None, od_ref, cols))
    _run_pairs(items, valid)


def _attn_lat(proj, caches, rope, params, seq, sub_layer):
    qc, kc, vc, qa, kva, kpe = proj
    n = qc.shape[0]
    past = caches[0].shape[2]
    nq = seq // Q_TILE
    qrow = lambda b, q: (b * nq + q, 0)
    krow = lambda b, q: (b, 0)
    kvw = NKV_C * HD_C
    in_specs = [pl.BlockSpec((Q_TILE, NH_C * HD_C), qrow), pl.BlockSpec((Q_TILE, Q_RANK), qrow),
                pl.BlockSpec((4, Q_TILE, LANES), lambda b, q: (0, q, 0)),
                pl.BlockSpec((seq, kvw), krow), pl.BlockSpec((seq, kvw), krow),
                pl.BlockSpec((seq, KV_RANK), krow), pl.BlockSpec((seq, LANES), krow)]
    in_specs += [pl.BlockSpec((None, None, past, LANES), lambda b, q: (b, sub_layer, 0, 0)) for _ in caches]
    in_specs += [pl.BlockSpec(rope.shape, lambda b, q: (0, 0, 0))]
    in_specs += [_layer_spec(p, sub_layer) for p in params]
    half = NH_C * HD_C
    win_rows = 2 * WINDOW + seq + past
    return pl.pallas_call(
        functools.partial(_attn_lat_kernel, seq=seq, past=past),
        grid=(n // seq, nq), in_specs=in_specs,
        out_specs=[pl.BlockSpec((Q_TILE, half), qrow), pl.BlockSpec((Q_TILE, half), qrow)],
        out_shape=[jax.ShapeDtypeStruct((n, half), F32), jax.ShapeDtypeStruct((n, half), F32)],
        scratch_shapes=[pltpu.VMEM((NKV_C, 2, win_rows, LANES), BF16), pltpu.VMEM((NKV_C, 2, win_rows, LANES), BF16),
                        pltpu.VMEM((NH_D // 2, 2 * (past + seq), 2 * LANES), BF16),
                        pltpu.VMEM((NH_D // 2, 2 * (past + seq), 2 * LANES), BF16)],
        compiler_params=_params("arbitrary", "arbitrary"), name="attn_lat",
    )(qc, qa, rope, kc, vc, kva, kpe, *caches, rope, *params)


def _pad_lanes(x, width=LANES):
    return jnp.pad(x, [(0, 0)] * (x.ndim - 1) + [(0, width - x.shape[-1])])


def _on_lanes(x):
    return jnp.broadcast_to(x[..., None], x.shape + (LANES,))


def _mla_query_weights(w):
    lead = w.shape[:-1]
    w4 = w.reshape(lead + (NH_D // 2, 2, NOPE_D + ROPE_D))
    nope = w4[..., :NOPE_D].reshape(lead + (NH_D * NOPE_D,))
    pe = _pad_lanes(w4[..., NOPE_D:].reshape(lead + (NH_D // 2, 2 * ROPE_D)))
    return jnp.concatenate([nope, pe.reshape(lead + (NH_D // 2 * LANES,))], axis=-1).astype(BF16)


def _mla_kv_weights(w):
    lead = w.shape[:-1]
    w3 = w.reshape(lead + (NH_D, NOPE_D + V_D))
    return jnp.concatenate([w3[..., :NOPE_D].reshape(lead + (NH_D * NOPE_D,)),
                            w3[..., NOPE_D:].reshape(lead + (NH_D * V_D,))], axis=-1).astype(BF16)


def _rope_tables(rows):
    def table(rot_dim):
        quarter = rot_dim // 4
        inv = ROPE_BASE ** (-jnp.arange(quarter, dtype=F32) / quarter)
        r = jnp.repeat(jnp.arange(rows, dtype=F32), GRID_W)
        col = jnp.tile(jnp.arange(GRID_W, dtype=F32), rows)
        ang = jnp.concatenate([r[:, None] * inv, col[:, None] * inv], axis=-1)
        reps = LANES // (rot_dim // 2)
        return jnp.tile(jnp.cos(ang), (1, reps)), jnp.tile(jnp.sin(ang), (1, reps))
    cos_c, sin_c = table(HD_C)
    cos_d, sin_d = table(ROPE_D)
    return jnp.stack([cos_c, sin_c, cos_d, sin_d])


def kernel(x_prompt, x_sample, c, state_mlstm_C, state_mlstm_n, state_mlstm_m, state_ssd, cache_gqa_k, cache_gqa_v,
           cache_mla_ckv, cache_mla_kpe, c_ctx, w_ada, b_ada, norm_g, w_up, w_down, w_in_even, conv_a_w, conv_a_b,
           conv_b_w, conv_b_b, gate_b, a_norm_w, dt_bias, a_log, d_skip, b_norm_w, w_out_even, w_in_odd, sink,
           q_a_norm, kv_a_norm, w_q_b, w_kv_b, w_out_odd):
    xp = x_prompt.reshape(BATCH * SEQ, D_MODEL)
    xs = x_sample.reshape(DEC_BATCH * DEC_SEQ, D_MODEL)
    cond = jnp.concatenate([c_ctx[None, :], c, jnp.zeros((MOD_ROWS - 1 - DEC_BATCH, D_MODEL), F32)], axis=0)
    mods = _modulations(cond, w_ada, b_ada)
    rope = _rope_tables(DEC_SEQ // GRID_W)

    w_even = w_in_even.astype(BF16)
    a_params = (conv_a_w, conv_a_b[:, None, :], _on_lanes(gate_b), a_norm_w[:, None, :])
    b_params = (conv_b_w, conv_b_b[:, None, :], _on_lanes(dt_bias.reshape(N_EVEN, 2 * NH_B)),
                _on_lanes(a_log.reshape(N_EVEN, 2 * NH_B)), jnp.repeat(d_skip, HP_B, axis=1)[:, None, :],
                b_norm_w[:, None, :])
    n0 = state_mlstm_n[..., None]
    mem_in = (jnp.concatenate([state_mlstm_C, jnp.broadcast_to(n0, n0.shape[:-1] + (LANES,))], axis=-1),
              _on_lanes(state_mlstm_m.reshape(DEC_BATCH, N_EVEN, 2 * NH_A)))
    w_odd = w_in_odd.astype(BF16)
    o_params = (_pad_lanes(sink)[:, None, :], q_a_norm[:, None, :], kv_a_norm[:, None, :],
                _mla_query_weights(w_q_b), _mla_kv_weights(w_kv_b))
    caches = (cache_gqa_k.reshape(DEC_BATCH, N_ODD, PAST_LEN, NKV_C * HD_C),
              cache_gqa_v.reshape(DEC_BATCH, N_ODD, PAST_LEN, NKV_C * HD_C),
              cache_mla_ckv, _pad_lanes(cache_mla_kpe))

    new_k, new_v, new_kpe = [], [], []
    mem_state, ssd_state, ckv_state = None, None, None
    for l in range(DEPTH):
        j = l // 2
        if l % 2 == 0:
            a1p, *mem_state = _fused_mixer(_mlstm_kernel, "mlstm", xp, mods, l, norm_g, w_even, MLSTM_W_COLS, a_params,
                                           SEQ, MLSTM_PIECES, MLSTM_GATES, A_V, MLSTM_STATE_SHAPES, mem_state)
            a2p, *ssd_state = _fused_mixer(_ssd_kernel, "ssd", xp, mods, l, norm_g, w_even, w_even.shape[2], b_params,
                                           SEQ, SSD_PIECES, SSD_GATES, B_INNER, SSD_STATE_SHAPES, ssd_state)
            qk, v, o, z, xbc, g = _project(xs, mods, l, True, norm_g, w_even, EVEN_WIDTHS, EVEN_REGROUP, EVEN_GATE_COLS)
            a1s = _mlstm_latent(qk, v, o, g, a_params, DEC_SEQ, j, mem_in)
            a2s = _ssd_latent(xbc, z, g, b_params, DEC_SEQ, j, state_ssd)
            w_out = w_out_even
        else:
            a1p, a2p, ckv_state, kc, vc, kpe = _attn_ctx(xp, mods, l, norm_g, w_odd, o_params, SEQ, carried=ckv_state)
            new_k.append(kc.reshape(BATCH, SEQ, NKV_C, HD_C))
            new_v.append(vc.reshape(BATCH, SEQ, NKV_C, HD_C))
            new_kpe.append(kpe[:, :ROPE_D].reshape(BATCH, SEQ, ROPE_D))
            proj = _project(xs, mods, l, True, norm_g, w_odd, ODD_WIDTHS, ODD_REGROUP)
            a1s, a2s = _attn_lat(proj, caches, rope, o_params, DEC_SEQ, j)
            w_out = w_out_odd
        xp, xs = _channel((a1p, a2p, xp), (a1s, a2s, xs), mods, l, norm_g, w_out, w_up, w_down)

    new_c, new_n, new_m = mem_state
    return (xp.reshape(BATCH, SEQ, D_MODEL), xs.reshape(DEC_BATCH, DEC_SEQ, D_MODEL),
            new_c, new_n, new_m[..., 0].reshape(BATCH, N_EVEN, 2, NH_A), ssd_state[0],
            jnp.stack(new_k, axis=1), jnp.stack(new_v, axis=1), ckv_state, jnp.stack(new_kpe, axis=1))
```

```python
import functools

import jax
import jax.numpy as jnp
from jax import lax
from jax.experimental import pallas as pl
from jax.experimental.pallas import tpu as pltpu

F32 = jnp.float32
BF16 = jnp.bfloat16

D_MODEL = 1024
BATCH = 32
SEQ = 256
DEPTH = 4
DEC_BATCH = 2
DEC_SEQ = 1024
PAST_LEN = 256
GRID_W = 64
N_EVEN = (DEPTH + 1) // 2
N_ODD = DEPTH // 2
EPS = 1e-6
CONV_K = 5
NH_A = 4
DK_A = 128
DV_A = 128
A_QK = NH_A * DK_A
A_V = NH_A * DV_A
NH_B = 8
HP_B = 64
DSTATE = 128
NG_B = 2
R_B = NH_B // NG_B
B_INNER = NH_B * HP_B
B_BC = NG_B * DSTATE
B_XBC = B_INNER + 2 * B_BC
NH_C = 8
NKV_C = 2
G_C = NH_C // NKV_C
HD_C = 64
WINDOW = 128
NH_D = 8
Q_RANK = 256
KV_RANK = 128
NOPE_D = 64
ROPE_D = 32
V_D = 64
MLA_SCALE = (NOPE_D + ROPE_D) ** -0.5
D_FF = 4 * D_MODEL
ROPE_BASE = 10000.0

LANES = 128
VMEM_LIMIT_BYTES = 56 * 1024 * 1024
ROW_TILE = 512
FF_TILE = 1024
SUB_ROWS = 256
STAGE_ROWS = 512
Q_TILE = 256
ADA_TILE = 3072
MOD_ROWS = 8
GATE_ROWS = 4 * NH_A + 2 * NH_B

EVEN_WIDTHS = (2 * A_QK, A_V, A_V, B_INNER, B_XBC)
ODD_WIDTHS = (NH_C * HD_C, NKV_C * HD_C, NKV_C * HD_C, Q_RANK, KV_RANK, LANES)
_GATES_LO = 2 * A_QK + 2 * A_V
_Z_LO = _GATES_LO + 4 * NH_A
_DT_LO = _Z_LO + B_INNER + B_XBC
EVEN_REGROUP = ((0, 0, _GATES_LO), (_GATES_LO, _Z_LO, B_INNER + B_XBC))
EVEN_GATE_COLS = ((_GATES_LO, _Z_LO), (_DT_LO, _DT_LO + 2 * NH_B))
ODD_IN = sum(ODD_WIDTHS) - LANES + ROPE_D
ODD_REGROUP = ((0, 0, ODD_IN), (ODD_IN, None, LANES - ROPE_D))

_NT = (((1,), (1,)), ((), ()))


def _params(*sem):
    return pltpu.CompilerParams(dimension_semantics=sem, vmem_limit_bytes=VMEM_LIMIT_BYTES)


def _rms(x, g):
    return x * lax.rsqrt(jnp.mean(x * x, axis=-1, keepdims=True) + EPS) * g


def _silu(x):
    return x * jax.nn.sigmoid(x)


def _softplus(x):
    return jnp.maximum(x, 0.0) + jnp.log1p(jnp.exp(-jnp.abs(x)))


def _dot(a, b):
    return jnp.dot(a, b, preferred_element_type=F32)


def _dot_nt(a, b):
    return lax.dot_general(a, b, _NT, preferred_element_type=F32)


def _layer_spec(arr, layer):
    tail = arr.shape[1:]
    zeros = (0,) * len(tail)
    return pl.BlockSpec((None,) + tail, lambda *_: (layer,) + zeros)


def _ada_kernel(c_ref, w_ref, b_ref, o_ref):
    s = _silu(c_ref[...]).astype(BF16)
    o_ref[...] = _dot(s, w_ref[...].astype(BF16)) + b_ref[...]


def _modulations(cond, w_ada, b_ada):
    out = pl.pallas_call(
        _ada_kernel,
        grid=(DEPTH, 6 * D_MODEL // ADA_TILE),
        in_specs=[pl.BlockSpec((MOD_ROWS, D_MODEL), lambda l, n: (0, 0)),
                  pl.BlockSpec((None, D_MODEL, ADA_TILE), lambda l, n: (l, 0, n)),
                  pl.BlockSpec((None, 1, ADA_TILE), lambda l, n: (l, 0, n))],
        out_specs=pl.BlockSpec((None, MOD_ROWS, ADA_TILE), lambda l, n: (l, 0, n)),
        out_shape=jax.ShapeDtypeStruct((DEPTH, MOD_ROWS, 6 * D_MODEL), F32),
        compiler_params=_params("arbitrary", "arbitrary"),
        name="ada",
    )(cond, w_ada, b_ada.reshape(DEPTH, 1, 6 * D_MODEL))
    return out.reshape(DEPTH, MOD_ROWS, 6, D_MODEL)


def _mod_spec(layer, latent):
    if latent:
        per_seq = DEC_SEQ // ROW_TILE
        return pl.BlockSpec((None, None, 6, D_MODEL), lambda i, *_: (layer, 1 + i // per_seq, 0, 0))
    return pl.BlockSpec((None, None, 6, D_MODEL), lambda i, *_: (layer, 0, 0, 0))


def _proj_kernel(x_ref, mod_ref, g_ref, w_ref, *rest, widths, regroup, gate_cols):
    n_out = len(widths) + (1 if gate_cols else 0)
    o_refs, w_s = rest[:n_out], rest[n_out]

    @pl.when(pl.program_id(0) == 0)
    def _():
        for dst, src, width in regroup:
            if src is None:
                w_s[:, dst:dst + width] = jnp.zeros((D_MODEL, width), BF16)
            else:
                w_s[:, dst:dst + width] = w_ref[:, src:src + width]
        if gate_cols:
            pieces = [w_ref[:, lo:hi].astype(F32) for lo, hi in gate_cols]
            n_gates = sum(hi - lo for lo, hi in gate_cols)
            gates = jnp.concatenate(pieces + [jnp.zeros((D_MODEL, LANES - n_gates), F32)], axis=1)
            rest[n_out + 1][...] = gates.T[0:n_gates, :].astype(BF16)

    h = _rms(x_ref[...], g_ref[0:1, :]) * (1.0 + mod_ref[1:2, :]) + mod_ref[0:1, :]
    hb = h.astype(BF16)
    off = 0
    for o_ref, wd in zip(o_refs, widths):
        o_ref[...] = _dot(hb, w_s[:, off:off + wd])
        off += wd
    if gate_cols:
        o_refs[-1][...] = _dot_nt(rest[n_out + 1][...], hb)


def _project(x, mods, layer, latent, gains, w_all, widths, regroup, gate_cols=()):
    n = x.shape[0]
    out_specs = [pl.BlockSpec((ROW_TILE, wd), lambda i: (i, 0)) for wd in widths]
    out_shape = [jax.ShapeDtypeStruct((n, wd), F32) for wd in widths]
    scratch = [pltpu.VMEM((D_MODEL, sum(widths)), BF16)]
    if gate_cols:
        n_gates = sum(hi - lo for lo, hi in gate_cols)
        out_specs.append(pl.BlockSpec((n_gates, ROW_TILE), lambda i: (0, i)))
        out_shape.append(jax.ShapeDtypeStruct((n_gates, n), F32))
        scratch.append(pltpu.VMEM((n_gates, D_MODEL), BF16))
    w_spec = pl.BlockSpec((None,) + w_all.shape[1:], lambda i: (layer // 2, 0, 0), pipeline_mode=pl.Buffered(1))
    return pl.pallas_call(
        functools.partial(_proj_kernel, widths=widths, regroup=regroup, gate_cols=gate_cols),
        grid=(n // ROW_TILE,),
        in_specs=[pl.BlockSpec((ROW_TILE, D_MODEL), lambda i: (i, 0)), _mod_spec(layer, latent),
                  _layer_spec(gains, layer), w_spec],
        out_specs=out_specs, out_shape=out_shape, scratch_shapes=scratch,
        compiler_params=_params("arbitrary"),
        name="proj",
    )(x, mods, gains, w_all)


def _weight_chunks(layer, sub_layer, wo_hbm, wu_hbm, wd_hbm, wo_s, wu_s, wd_s):
    chunks = []
    for r in range(0, D_MODEL, STAGE_ROWS):
        chunks.append((wo_hbm.at[sub_layer, pl.ds(r, STAGE_ROWS), :], wo_s.at[pl.ds(r, STAGE_ROWS), :]))
    for r in range(0, D_MODEL, STAGE_ROWS):
        for c in range(0, D_FF, D_MODEL):
            chunks.append((wu_hbm.at[layer, pl.ds(r, STAGE_ROWS), pl.ds(c, D_MODEL)],
                           wu_s.at[pl.ds(r, STAGE_ROWS), pl.ds(c, D_MODEL)]))
    for r in range(0, D_FF, STAGE_ROWS):
        chunks.append((wd_hbm.at[layer, pl.ds(r, STAGE_ROWS), :], wd_s.at[pl.ds(r, STAGE_ROWS), :]))
    return chunks


def _channel_kernel(a1p_ref, a2p_ref, xp_ref, a1s_ref, a2s_ref, xs_ref, mod_ref, g_ref, wo_hbm, wu_hbm, wd_hbm,
                    op_ref, os_ref, wo_s, wu_s, wd_s, stage, sem, *, layer, sub_layer, prompt_steps):
    step = pl.program_id(0)

    @pl.when(step == 0)
    def _():
        chunks = _weight_chunks(layer, sub_layer, wo_hbm, wu_hbm, wd_hbm, wo_s, wu_s, wd_s)
        copies = [pltpu.make_async_copy(src, stage.at[k % 2], sem.at[k % 2]) for k, (src, _) in enumerate(chunks)]
        copies[0].start()
        for k, (_, dst) in enumerate(chunks):
            if k + 1 < len(chunks):
                copies[k + 1].start()
            copies[k].wait()
            dst[...] = stage[k % 2].astype(BF16)

    def rows_block(a1_ref, a2_ref, x_ref, o_ref):
        half = a1_ref.shape[1]
        tiles = range(0, D_FF, FF_TILE)
        blocks = [slice(r0, r0 + SUB_ROWS) for r0 in range(0, ROW_TILE, SUB_ROWS)]

        def prologue(rows):
            y = (_dot(a1_ref[rows, :].astype(BF16), wo_s[0:half, :])
                 + _dot(a2_ref[rows, :].astype(BF16), wo_s[half:, :]))
            x1 = x_ref[rows, :] + mod_ref[2:3, :] * _rms(y, g_ref[1:2, :])
            h = (_rms(x1, g_ref[2:3, :]) * (1.0 + mod_ref[4:5, :]) + mod_ref[3:4, :]).astype(BF16)
            return x1, h

        def mlp_tile(h, c):
            u = jnp.square(jnp.maximum(_dot(h, wu_s[:, c:c + FF_TILE]), 0.0)).astype(BF16)
            return _dot(u, wd_s[c:c + FF_TILE, :])

        def epilogue(rows, x1, acc):
            o_ref[rows, :] = x1 + mod_ref[5:6, :] * _rms(acc, g_ref[3:4, :])

        ready = {0: prologue(blocks[0])}
        done = None
        for b, rows in enumerate(blocks):
            x1, h = ready.pop(b)
            acc = None
            for t, c in enumerate(tiles):
                part = mlp_tile(h, c)
                acc = part if acc is None else acc + part
                if t == 0 and b + 1 < len(blocks):
                    ready[b + 1] = prologue(blocks[b + 1])
                if t == 0 and done is not None:
                    epilogue(*done)
                    done = None
            done = (rows, x1, acc)
        epilogue(*done)

    @pl.when(step < prompt_steps)
    def _():
        rows_block(a1p_ref, a2p_ref, xp_ref, op_ref)

    @pl.when(step >= prompt_steps)
    def _():
        rows_block(a1s_ref, a2s_ref, xs_ref, os_ref)


def _channel(prompt, latent, mods, layer, gains, w_out, w_up, w_down):
    n_p, n_s = prompt[2].shape[0], latent[2].shape[0]
    steps_p, steps_s = n_p // ROW_TILE, n_s // ROW_TILE
    per_seq = DEC_SEQ // ROW_TILE
    row_p = lambda i: (jnp.minimum(i, steps_p - 1), 0)
    row_s = lambda i: (jnp.maximum(i - steps_p, 0), 0)
    mod_spec = pl.BlockSpec((None, None, 6, D_MODEL),
                            lambda i: (layer, jnp.where(i < steps_p, 0, 1 + (i - steps_p) // per_seq), 0, 0))
    hbm = pl.BlockSpec(memory_space=pl.ANY)
    specs = lambda arrs, row: [pl.BlockSpec((ROW_TILE, a.shape[1]), row) for a in arrs]
    return pl.pallas_call(
        functools.partial(_channel_kernel, layer=layer, sub_layer=layer // 2, prompt_steps=steps_p),
        grid=(steps_p + steps_s,),
        in_specs=specs(prompt, row_p) + specs(latent, row_s) + [mod_spec, _layer_spec(gains, layer), hbm, hbm, hbm],
        out_specs=[pl.BlockSpec((ROW_TILE, D_MODEL), row_p), pl.BlockSpec((ROW_TILE, D_MODEL), row_s)],
        out_shape=[jax.ShapeDtypeStruct((n_p, D_MODEL), F32), jax.ShapeDtypeStruct((n_s, D_MODEL), F32)],
        scratch_shapes=[pltpu.VMEM((D_MODEL, D_MODEL), BF16), pltpu.VMEM((D_MODEL, D_FF), BF16),
                        pltpu.VMEM((D_FF, D_MODEL), BF16), pltpu.VMEM((2, STAGE_ROWS, D_MODEL), F32),
                        pltpu.SemaphoreType.DMA((2,))],
        compiler_params=_params("arbitrary"),
        name="channel",
    )(*prompt, *latent, mods, gains, w_out, w_up, w_down)


def _row_iota(shape):
    return lax.broadcasted_iota(jnp.int32, shape, 0)


def _lane_iota(shape):
    return lax.broadcasted_iota(jnp.int32, shape, 1)


def _pair_lanes(shape):
    return _lane_iota(shape) < LANES // 2


def _cumsum_lanes(x, n_fwd):
    t = x.shape[1]
    si, ti = _row_iota((t, t)), _lane_iota((t, t))
    upper = jnp.where(si <= ti, 1.0, 0.0).astype(BF16)
    lower = jnp.where(si >= ti, 1.0, 0.0).astype(BF16)
    hi = x.astype(BF16)
    rest = x - hi.astype(F32)
    mid = rest.astype(BF16)
    lo = (rest - mid.astype(F32)).astype(BF16)
    pre = _dot(hi, upper) + _dot(mid, upper) + _dot(lo, upper)
    suf = _dot(hi, lower) + _dot(mid, lower) + _dot(lo, lower)
    return jnp.where(_row_iota(x.shape) < n_fwd, pre, suf)


def _cummax_lanes(x, n_fwd):
    t = x.shape[1]
    lane = _lane_iota(x.shape)
    pre, suf = x, x
    k = 1
    while k < t:
        pre = jnp.maximum(pre, jnp.where(lane >= k, pltpu.roll(pre, k, 1), -jnp.inf))
        suf = jnp.maximum(suf, jnp.where(lane < t - k, pltpu.roll(suf, t - k, 1), -jnp.inf))
        k *= 2
    return jnp.where(_row_iota(x.shape) < n_fwd, pre, suf)


def _columns(row_arrays):
    t = row_arrays[0].shape[1]
    used = sum(a.shape[0] for a in row_arrays)
    return jnp.concatenate(list(row_arrays) + [jnp.zeros((LANES - used, t), F32)], axis=0).T


def _dwconv_silu(x, w, b):
    t = x.shape[0]
    row = _row_iota(x.shape)
    acc = x * w[CONV_K // 2:CONV_K // 2 + 1, :] + b
    for j in range(CONV_K):
        d = j - CONV_K // 2
        if d == 0:
            continue
        shifted = pltpu.roll(x, (-d) % t, 0)
        valid = (row >= -d) if d < 0 else (row < t - d)
        acc = acc + jnp.where(valid, shifted, 0.0) * w[j:j + 1, :]
    return _silu(acc)


def _causal_exponent(expo, r0, k0, reverse):
    ti = r0 + _row_iota(expo.shape)
    si = k0 + _lane_iota(expo.shape)
    keep = (si >= ti) if reverse else (si <= ti)
    return jnp.where(keep, expo, -jnp.inf)


def _pair_split(x):
    first = _pair_lanes(x.shape)
    zero = jnp.zeros_like(x)
    return jnp.concatenate([jnp.where(first, x, zero), jnp.where(first, zero, x)], axis=0)


def _key_range(d, r0, seq):
    return (0, r0 + Q_TILE) if d == 0 else (r0, seq)


N_FUSED_INPUTS = 5
PROJ_CHUNK = 512
SSD_PROJ_CHUNK = 1024

MLSTM_PIECES = ((0, 2 * A_QK), (2 * A_QK, A_V), (2 * A_QK + A_V, A_V))
MLSTM_GATES = (_GATES_LO, _Z_LO)
MLSTM_W_COLS = _GATES_LO + LANES
SSD_PIECES = ((_Z_LO + B_INNER, B_XBC), (_Z_LO, B_INNER))
SSD_GATES = (_DT_LO, _DT_LO + 2 * NH_B)
CTX_PIECES = tuple((sum(ODD_WIDTHS[:i]), wd) for i, wd in enumerate(ODD_WIDTHS[:-1])) + (
    (sum(ODD_WIDTHS[:-1]), LANES, ROPE_D),)


def _fused_project(step, fused_refs, scratch, pieces, gate_cols, mix, chunk=PROJ_CHUNK):
    x_cur_ref, x_next_ref, mod_ref, gain_ref, w_ref = fused_refs
    w_s = scratch[0]
    n_head = 2 if gate_cols else 1
    n = len(pieces) + (1 if gate_cols else 0)
    sets = (scratch[n_head:n_head + n], scratch[n_head + n:n_head + 2 * n])

    def normed(x_ref):
        h = _rms(x_ref[...], gain_ref[0:1, :]) * (1.0 + mod_ref[1:2, :]) + mod_ref[0:1, :]
        return h.astype(BF16)

    def chunk_thunks(hb, dst):
        def gates():
            dst[0][...] = _dot_nt(scratch[1][...], hb)

        def columns(ref, c, src, width):
            def run():
                ref[:, c:c + width] = _dot(hb, w_s[:, src:src + width])
            return run

        thunks = [gates] if gate_cols else []
        off = 0
        for ref, piece in zip(dst[-len(pieces):], pieces):
            width = piece[1]
            thunks += [columns(ref, c, off + c, min(chunk, width - c)) for c in range(0, width, chunk)]
            off += width
        return thunks

    @pl.when(step == 0)
    def _():
        off = 0
        for piece in pieces:
            src, width = piece[0], piece[1]
            valid = piece[2] if len(piece) > 2 else width
            w_s[:, off:off + valid] = w_ref[:, src:src + valid]
            if valid < width:
                w_s[:, off + valid:off + width] = jnp.zeros((D_MODEL, width - valid), BF16)
            off += width
        if gate_cols:
            lo, hi = gate_cols
            gates = jnp.concatenate([w_ref[:, lo:hi].astype(F32), jnp.zeros((D_MODEL, LANES - (hi - lo)), F32)], axis=1)
            scratch[1][...] = gates.T[0:hi - lo, :].astype(BF16)
        for thunk in chunk_thunks(normed(x_cur_ref), sets[0]):
            thunk()

    for parity in range(2):
        @pl.when(step % 2 == parity)
        def _(parity=parity):
            cur = sets[parity]
            refs = list(cur[1:]) + [cur[0]] if gate_cols else list(cur)
            mix(*refs, chunk_thunks(normed(x_next_ref), sets[1 - parity]))


def _fused_specs(seq, nseq, layer, w_all, w_cols):
    nxt = lambda s: (jnp.minimum(s + 1, nseq - 1), 0)
    return [pl.BlockSpec((seq, D_MODEL), lambda s: (s, 0)), pl.BlockSpec((seq, D_MODEL), nxt),
            pl.BlockSpec((None, None, 6, D_MODEL), lambda s: (layer, 0, 0, 0)),
            pl.BlockSpec((None, 4, D_MODEL), lambda s: (layer, 0, 0)),
            pl.BlockSpec((None, D_MODEL, w_cols), lambda s: (layer // 2, 0, 0), pipeline_mode=pl.Buffered(1))]


def _fused_scratch(seq, pieces, n_gates):
    head = [pltpu.VMEM((D_MODEL, sum(p[1] for p in pieces)), BF16)]
    one_set = [pltpu.VMEM((seq, p[1]), F32) for p in pieces]
    if n_gates:
        head.append(pltpu.VMEM((n_gates, D_MODEL), BF16))
        one_set = [pltpu.VMEM((n_gates, seq), F32)] + one_set
    return head + one_set + one_set


def _run_share(side_work, stages_left):
    for _ in range(-(-len(side_work) // stages_left)):
        side_work.pop(0)()


def _mlstm_kernel(*refs, seq, has_state, emit_state, n_carried, slot, fused):
    assert not (has_state and emit_state)
    it = iter(refs)
    if fused:
        fused_refs = [next(it) for _ in range(N_FUSED_INPUTS)]
    else:
        qk_ref, v_ref, o_ref, g_ref = (next(it) for _ in range(4))
    cw_ref, cb_ref, gb_ref, anw_ref = (next(it) for _ in range(4))
    if has_state:
        c0_ref, m0_ref = next(it), next(it)
    for _ in range(n_carried):
        next(it)
    ha_ref = next(it)
    if emit_state:
        cn_ref, nn_ref, mn_ref = next(it), next(it), next(it)
        if n_carried == 0:
            for other in range(N_EVEN):
                if other != slot:
                    cn_ref[other] = jnp.zeros(cn_ref.shape[1:], F32)
                    nn_ref[other] = jnp.zeros(nn_ref.shape[1:], F32)
                    mn_ref[other] = jnp.zeros(mn_ref.shape[1:], F32)
            cn_ref, nn_ref, mn_ref = cn_ref.at[slot], nn_ref.at[slot], mn_ref.at[slot]

    def mix(qk_ref, v_ref, o_ref, g_ref, side_work=()):
        side_work = list(side_work)
        n_ch = 2 * NH_A
        log_i = g_ref[0:n_ch, :] + gb_ref[0:n_ch, 0:1]
        f_pre = g_ref[n_ch:2 * n_ch, :] + gb_ref[n_ch:2 * n_ch, 0:1]
        log_f = jnp.minimum(f_pre, 0.0) - jnp.log1p(jnp.exp(-jnp.abs(f_pre)))
        b = _cumsum_lanes(log_f, NH_A)
        a = log_i - b
        m_run = _cummax_lanes(a, NH_A)
        if has_state:
            m0 = m0_ref[:, 0:1]
            m_run = jnp.maximum(m_run, m0)
        else:
            m_run = jnp.maximum(m_run, 0.0)
        by_time = [m_run, jnp.exp(-(b + m_run))]
        if has_state:
            by_time.append(jnp.exp(m0 - m_run))
        if emit_state:
            fwd = _row_iota((n_ch, 1)) < NH_A
            b_last = jnp.where(fwd, b[:, seq - 1:seq], b[:, 0:1])
            m_last = jnp.where(fwd, m_run[:, seq - 1:seq], m_run[:, 0:1])
            mn_ref[...] = jnp.broadcast_to(b_last + m_last, (n_ch, LANES))
            by_time.append(jnp.exp(a - m_last))
        cols = _columns(by_time)

        ones = jnp.ones((seq, LANES), F32)
        for h in range(NH_A):
            _run_share(side_work, NH_A - h)
            cq = slice(h * DK_A, (h + 1) * DK_A)
            ck = slice(A_QK + h * DK_A, A_QK + (h + 1) * DK_A)
            cv = slice(h * DV_A, (h + 1) * DV_A)
            q = _dwconv_silu(qk_ref[:, cq], cw_ref[:, cq], cb_ref[:, cq])
            k = _dwconv_silu(qk_ref[:, ck], cw_ref[:, ck], cb_ref[:, ck]) * (DK_A ** -0.5)
            qb = q.astype(BF16)
            kb = k.astype(BF16)
            vh = v_ref[:, cv]
            vaug = jnp.concatenate([vh, ones], axis=1).astype(BF16)
            for r0 in range(0, seq, Q_TILE):
                rows = slice(r0, r0 + Q_TILE)
                s = _dot_nt(qb[rows], kb)
                hsum = None
                for d in range(2):
                    c = d * NH_A + h
                    k0, k1 = _key_range(d, r0, seq)
                    expo = _causal_exponent(a[c:c + 1, k0:k1] - cols[rows, c:c + 1], r0, k0, d == 1)
                    p = (s[:, k0:k1] * jnp.exp(expo)).astype(BF16)
                    acc = _dot(p, vaug[k0:k1])
                    if has_state:
                        acc = acc + (cols[rows, 2 * n_ch + c:2 * n_ch + c + 1]
                                     * _dot(qb[rows], c0_ref[d, h].astype(BF16)))
                    hd = acc[:, 0:DV_A] / jnp.maximum(jnp.abs(acc[:, DV_A:]), cols[rows, n_ch + c:n_ch + c + 1])
                    hsum = hd if hsum is None else hsum + hd
                og = jax.nn.sigmoid(o_ref[rows, cv]) * hsum
                ha_ref[rows, cv] = _rms(og, anw_ref[:, cv])
            if emit_state:
                for d in range(2):
                    c = d * NH_A + h
                    kw = k * cols[:, 2 * n_ch + c:2 * n_ch + c + 1]
                    cn_ref[d, h] = _dot(kw.T.astype(BF16), vh.astype(BF16))
                    nn_ref[d, h:h + 1, :] = jnp.sum(kw, axis=0, keepdims=True)

    if fused:
        _fused_project(pl.program_id(0), fused_refs, list(it), MLSTM_PIECES, MLSTM_GATES, mix)
    else:
        mix(qk_ref, v_ref, o_ref, g_ref)


def _state_out_specs(shapes, nseq, sub_layer, carried):
    out_specs, out_shape = [], []
    for shp in shapes:
        zeros = (0,) * len(shp)
        if carried is None:
            out_specs.append(pl.BlockSpec((None, N_EVEN) + shp, lambda s, z=zeros: (s, 0) + z))
        else:
            out_specs.append(pl.BlockSpec((None, None) + shp, lambda s, z=zeros: (s, sub_layer) + z))
        out_shape.append(jax.ShapeDtypeStruct((nseq, N_EVEN) + shp, F32))
    return out_specs, out_shape


MLSTM_STATE_SHAPES = ((2, NH_A, DK_A, DV_A), (2, NH_A, DK_A), (2 * NH_A, LANES))
SSD_STATE_SHAPES = ((2, NH_B, HP_B, DSTATE),)


def _fused_mixer(kernel_fn, name, x, mods, layer, gains, w_all, w_cols, params, seq, pieces, gate_cols,
                 out_width, state_shapes, carried):
    n = x.shape[0]
    nseq = n // seq
    sub_layer = layer // 2
    in_specs = _fused_specs(seq, nseq, layer, w_all, w_cols) + [_layer_spec(p, sub_layer) for p in params]
    args = [x, x, mods, gains, w_all, *params]
    state_specs, state_shape = _state_out_specs(state_shapes, nseq, sub_layer, carried)
    aliases = {}
    if carried is not None:
        aliases = {len(args) + i: 1 + i for i in range(len(state_shapes))}
        in_specs += [pl.BlockSpec(memory_space=pl.ANY)] * len(state_shapes)
        args += list(carried)
    return pl.pallas_call(
        functools.partial(kernel_fn, seq=seq, has_state=False, emit_state=True, n_carried=len(aliases),
                          slot=sub_layer, fused=True),
        grid=(nseq,), in_specs=in_specs,
        out_specs=[pl.BlockSpec((seq, out_width), lambda s: (s, 0))] + state_specs,
        out_shape=[jax.ShapeDtypeStruct((n, out_width), F32)] + state_shape,
        scratch_shapes=_fused_scratch(seq, pieces, gate_cols[1] - gate_cols[0]),
        input_output_aliases=aliases, compiler_params=_params("arbitrary"), name=name,
    )(*args)


def _mlstm_latent(qk, v, o, gates, params, seq, sub_layer, state):
    n = qk.shape[0]
    row = lambda s: (s, 0)
    c0_aug, m0 = state
    in_specs = [pl.BlockSpec((seq, 2 * A_QK), row), pl.BlockSpec((seq, A_V), row), pl.BlockSpec((seq, A_V), row),
                pl.BlockSpec((GATE_ROWS, seq), lambda s: (0, s))]
    in_specs += [_layer_spec(p, sub_layer) for p in params]
    in_specs += [pl.BlockSpec((None, None) + c0_aug.shape[2:], lambda s: (s, sub_layer, 0, 0, 0, 0)),
                 pl.BlockSpec((None, None) + m0.shape[2:], lambda s: (s, sub_layer, 0, 0))]
    return pl.pallas_call(
        functools.partial(_mlstm_kernel, seq=seq, has_state=True, emit_state=False, n_carried=0, slot=sub_layer,
                          fused=False),
        grid=(n // seq,), in_specs=in_specs, out_specs=pl.BlockSpec((seq, A_V), row),
        out_shape=jax.ShapeDtypeStruct((n, A_V), F32),
        compiler_params=_params("arbitrary"), name="mlstm",
    )(qk, v, o, gates, *params, c0_aug, m0)


def _ssd_kernel(*refs, seq, has_state, emit_state, n_carried, slot, fused):
    assert not (has_state and emit_state)
    it = iter(refs)
    if fused:
        fused_refs = [next(it) for _ in range(N_FUSED_INPUTS)]
    else:
        xbc_ref, z_ref, g_ref = (next(it) for _ in range(3))
    cw_ref, cb_ref, dtb_ref, alog_ref, dsk_ref, bnw_ref = (next(it) for _ in range(6))
    if has_state:
        s0_ref = next(it)
    for _ in range(n_carried):
        next(it)
    yb_ref = next(it)
    if emit_state:
        sn_ref = next(it)
        if n_carried == 0:
            for other in range(N_EVEN):
                if other != slot:
                    sn_ref[other] = jnp.zeros(sn_ref.shape[1:], F32)
            sn_ref = sn_ref.at[slot]

    gate_row0 = 0 if fused else 4 * NH_A

    def mix(xbc_ref, z_ref, g_ref, side_work=()):
        side_work = list(side_work)
        n_ch = 2 * NH_B
        dt = _softplus(g_ref[gate_row0:gate_row0 + n_ch, :] + dtb_ref[:, 0:1])
        acum = _cumsum_lanes(dt * (-jnp.exp(alog_ref[:, 0:1])), NH_B)
        key_shift = acum - jnp.log(dt)
        by_time = [acum]
        if has_state:
            by_time.append(jnp.exp(acum))
        if emit_state:
            fwd = _row_iota((n_ch, 1)) < NH_B
            a_last = jnp.where(fwd, acum[:, seq - 1:seq], acum[:, 0:1])
            by_time.append(jnp.exp(a_last - acum) * dt)
        cols = _columns(by_time)

        gw = R_B * HP_B
        first = _pair_lanes((seq, LANES))
        for g in range(NG_B):
            _run_share(side_work, NG_B - g)
            cx = slice(g * gw, (g + 1) * gw)
            cb_ = slice(B_INNER + g * DSTATE, B_INNER + (g + 1) * DSTATE)
            cc = slice(B_INNER + B_BC + g * DSTATE, B_INNER + B_BC + (g + 1) * DSTATE)
            xg = _dwconv_silu(xbc_ref[:, cx], cw_ref[:, cx], cb_ref[:, cx])
            bg = _dwconv_silu(xbc_ref[:, cb_], cw_ref[:, cb_], cb_ref[:, cb_]).astype(BF16)
            cg = _dwconv_silu(xbc_ref[:, cc], cw_ref[:, cc], cb_ref[:, cc]).astype(BF16)
            xbd = [_pair_split(xg[:, p * LANES:(p + 1) * LANES]).astype(BF16) for p in range(R_B // 2)]
            for r0 in range(0, seq, Q_TILE):
                rows = slice(r0, r0 + Q_TILE)
                cb_scores = _dot_nt(cg[rows], bg)
                ys = []
                for p in range(R_B // 2):
                    weights, inputs = [], []
                    for d in range(2):
                        k0, k1 = _key_range(d, r0, seq)
                        for i in range(2):
                            c = d * NH_B + g * R_B + 2 * p + i
                            expo = _causal_exponent(cols[rows, c:c + 1] - key_shift[c:c + 1, k0:k1], r0, k0, d == 1)
                            weights.append((cb_scores[:, k0:k1] * jnp.exp(expo)).astype(BF16))
                            inputs.append(xbd[p][i * seq + k0:i * seq + k1])
                    yp = _dot(jnp.concatenate(weights, axis=1), jnp.concatenate(inputs, axis=0))
                    if has_state:
                        h0 = g * R_B + 2 * p
                        for d in range(2):
                            c = d * NH_B + h0
                            carry = jnp.where(_pair_lanes((Q_TILE, LANES)), cols[rows, n_ch + c:n_ch + c + 1],
                                              cols[rows, n_ch + c + 1:n_ch + c + 2])
                            s0_pair = s0_ref[d, h0:h0 + 2].reshape(2 * HP_B, DSTATE).astype(BF16)
                            yp = yp + carry * _dot_nt(cg[rows], s0_pair)
                    ys.append(yp)
                y = jnp.concatenate(ys, axis=1) + dsk_ref[:, cx] * xg[rows]
                y = y * _silu(z_ref[rows, cx])
                yb_ref[rows, cx] = _rms(y, bnw_ref[:, cx])
            if emit_state:
                for d in range(2):
                    c0 = n_ch + d * NH_B + g * R_B
                    spread = jnp.concatenate([jnp.where(first, cols[:, c0 + 2 * p:c0 + 2 * p + 1],
                                                        cols[:, c0 + 2 * p + 1:c0 + 2 * p + 2])
                                              for p in range(R_B // 2)], axis=1)
                    sn = _dot((xg * spread).T.astype(BF16), bg)
                    for r in range(R_B):
                        sn_ref[d, g * R_B + r] = sn[r * HP_B:(r + 1) * HP_B, :]

    if fused:
        _fused_project(pl.program_id(0), fused_refs, list(it), SSD_PIECES, SSD_GATES, mix, SSD_PROJ_CHUNK)
    else:
        mix(xbc_ref, z_ref, g_ref)


def _ssd_latent(xbc, z, gates, params, seq, sub_layer, state):
    n = xbc.shape[0]
    row = lambda s: (s, 0)
    in_specs = [pl.BlockSpec((seq, B_XBC), row), pl.BlockSpec((seq, B_INNER), row),
                pl.BlockSpec((GATE_ROWS, seq), lambda s: (0, s))]
    in_specs += [_layer_spec(p, sub_layer) for p in params]
    in_specs += [pl.BlockSpec((None, None, 2, NH_B, HP_B, DSTATE), lambda s: (s, sub_layer, 0, 0, 0, 0))]
    return pl.pallas_call(
        functools.partial(_ssd_kernel, seq=seq, has_state=True, emit_state=False, n_carried=0, slot=sub_layer,
                          fused=False),
        grid=(n // seq,), in_specs=in_specs, out_specs=pl.BlockSpec((seq, B_INNER), row),
        out_shape=jax.ShapeDtypeStruct((n, B_INNER), F32),
        compiler_params=_params("arbitrary"), name="ssd",
    )(xbc, z, gates, *params, state)


def _shared_split(x, x_swapped, kh):
    first = _pair_lanes(x.shape)
    zero = jnp.zeros_like(x)
    top, bottom = (x, x_swapped) if kh == 0 else (x_swapped, x)
    return jnp.concatenate([jnp.where(first, top, zero), jnp.where(first, zero, bottom)], axis=0)


def _pair_probs(s, sinks=None, valid=None):
    n_keys = s.shape[1] // 2
    probs, maxes = [], []
    for i in range(2):
        si = s[:, i * n_keys:(i + 1) * n_keys]
        if valid is not None:
            si = jnp.where(valid, si, -jnp.inf)
        m = jnp.max(si, axis=1, keepdims=True)
        if sinks is not None:
            m = jnp.maximum(m, sinks[i])
        probs.append(jnp.exp(si - m))
        maxes.append(m)
    return jnp.concatenate(probs, axis=1).astype(BF16), maxes


def _pair_output(p, maxes, vbd, sinks=None):
    o = _dot(p, vbd)
    den = o[:, LANES:]
    if sinks is not None:
        den = den + jnp.where(_pair_lanes(den.shape), jnp.exp(sinks[0] - maxes[0]), jnp.exp(sinks[1] - maxes[1]))
    return o[:, :LANES] / den


def _run_pairs(items, valid=None, side_work=()):
    side_work = list(side_work)
    s_next = items[0][0]()
    for idx, (_, values, sinks, out_ref, cols) in enumerate(items):
        _run_share(side_work, len(items) - idx)
        s_cur = s_next
        if idx + 1 < len(items):
            s_next = items[idx + 1][0]()
        p, maxes = _pair_probs(s_cur, sinks, valid if sinks is not None else None)
        out_ref[:, cols] = _pair_output(p, maxes, values(), sinks)


def _pair_sinks(sink_ref, n):
    return sink_ref[0:1, n:n + 1], sink_ref[0:1, n + 1:n + 2]


def _mla_queries(qa_ref, qan_ref, wqb_ref):
    return _dot(_rms(qa_ref[...], qan_ref[...]).astype(BF16), wqb_ref[...]) * MLA_SCALE


def _attn_ctx_kernel(*refs, seq, n_carried, slot, fused):
    it = iter(refs)
    if fused:
        fused_refs = [next(it) for _ in range(N_FUSED_INPUTS)]
    else:
        proj_refs = [next(it) for _ in range(len(ODD_WIDTHS))]
    sink_ref, qan_ref, kvn_ref, wqb_ref, wkvb_ref = (next(it) for _ in range(5))
    for _ in range(n_carried):
        next(it)
    oc_ref, od_ref, ckv_ref = next(it), next(it), next(it)
    if fused:
        new_k_ref, new_v_ref, new_kpe_ref = next(it), next(it), next(it)
    if n_carried == 0:
        for other in range(N_ODD):
            if other != slot:
                ckv_ref[other] = jnp.zeros(ckv_ref.shape[1:], F32)
        ckv_ref = ckv_ref.at[slot]

    def mix(qc_ref, kc_ref, vc_ref, qa_ref, kva_ref, kpe_ref, side_work=()):
        if fused:
            new_k_ref[...] = kc_ref[...]
            new_v_ref[...] = vc_ref[...]
            new_kpe_ref[...] = kpe_ref[...]
        ones_bd = _pair_split(jnp.ones((seq, LANES), F32))
        kc, vc = kc_ref[...], vc_ref[...]
        kc_sw, vc_sw = pltpu.roll(kc, HD_C, 1), pltpu.roll(vc, HD_C, 1)
        qd = _mla_queries(qa_ref, qan_ref, wqb_ref)
        ckv = _rms(kva_ref[...], kvn_ref[...])
        ckv_ref[...] = ckv
        kv = _dot(ckv.astype(BF16), wkvb_ref[...])
        kpe = kpe_ref[...]
        kpe_bd = jnp.concatenate([kpe, pltpu.roll(kpe, ROPE_D, 1)], axis=0)
        nope_w = NH_D * NOPE_D
        items = []
        for kh in range(NKV_C):
            for n in range(kh * G_C, (kh + 1) * G_C, 2):
                cols = slice(n * HD_C, (n + 2) * HD_C)
                items.append((lambda cols=cols, kh=kh: _dot_nt((qc_ref[:, cols] * (HD_C ** -0.5)).astype(BF16),
                                                               _shared_split(kc, kc_sw, kh).astype(BF16)),
                              lambda kh=kh: jnp.concatenate([_shared_split(vc, vc_sw, kh), ones_bd], axis=1).astype(BF16),
                              _pair_sinks(sink_ref, n), oc_ref, cols))
        for i in range(NH_D // 2):
            cols = slice(i * LANES, (i + 1) * LANES)
            vcols = slice(nope_w + i * LANES, nope_w + (i + 1) * LANES)
            items.append((lambda cols=cols, vcols=vcols: _dot_nt(
                              jnp.concatenate([qd[:, cols], qd[:, vcols]], axis=1).astype(BF16),
                              jnp.concatenate([_pair_split(kv[:, cols]), kpe_bd], axis=1).astype(BF16)),
                          lambda vcols=vcols: jnp.concatenate([_pair_split(kv[:, vcols]), ones_bd], axis=1).astype(BF16),
                          None, od_ref, cols))
        _run_pairs(items, side_work=side_work)

    if fused:
        _fused_project(pl.program_id(0), fused_refs, list(it), CTX_PIECES, None, mix)
    else:
        mix(*proj_refs)


def _attn_ctx(x, mods, layer, gains, w_all, params, seq, carried=None):
    n = x.shape[0]
    nseq = n // seq
    sub_layer = layer // 2
    row = lambda s: (s, 0)
    in_specs = _fused_specs(seq, nseq, layer, w_all, w_all.shape[2]) + [_layer_spec(p, sub_layer) for p in params]
    args = [x, x, mods, gains, w_all, *params]
    half = NH_C * HD_C
    aliases = {}
    if carried is None:
        ckv_spec = pl.BlockSpec((None, N_ODD, seq, KV_RANK), lambda s: (s, 0, 0, 0))
    else:
        ckv_spec = pl.BlockSpec((None, None, seq, KV_RANK), lambda s: (s, sub_layer, 0, 0))
        aliases = {len(args): 2}
        in_specs.append(pl.BlockSpec(memory_space=pl.ANY))
        args.append(carried)
    kv_w = NKV_C * HD_C
    return pl.pallas_call(
        functools.partial(_attn_ctx_kernel, seq=seq, n_carried=len(aliases), slot=sub_layer, fused=True),
        grid=(nseq,), in_specs=in_specs,
        out_specs=[pl.BlockSpec((seq, half), row), pl.BlockSpec((seq, half), row), ckv_spec,
                   pl.BlockSpec((seq, kv_w), row), pl.BlockSpec((seq, kv_w), row), pl.BlockSpec((seq, LANES), row)],
        out_shape=[jax.ShapeDtypeStruct((n, half), F32), jax.ShapeDtypeStruct((n, half), F32),
                   jax.ShapeDtypeStruct((nseq, N_ODD, seq, KV_RANK), F32),
                   jax.ShapeDtypeStruct((n, kv_w), F32), jax.ShapeDtypeStruct((n, kv_w), F32),
                   jax.ShapeDtypeStruct((n, LANES), F32)],
        scratch_shapes=_fused_scratch(seq, CTX_PIECES, 0),
        input_output_aliases=aliases, compiler_params=_params("arbitrary"), name="attn_ctx",
    )(*args)


def _rope(x, cos, sin, half):
    parts = []
    lane = _lane_iota((x.shape[0], LANES))
    first = (lane & (2 * half - 1)) < half
    for i in range(x.shape[1] // LANES):
        xi = x[:, i * LANES:(i + 1) * LANES]
        partner = jnp.where(first, -pltpu.roll(xi, LANES - half, 1), pltpu.roll(xi, half, 1))
        parts.append(xi * cos + partner * sin)
    return parts[0] if len(parts) == 1 else jnp.concatenate(parts, axis=1)


def _attn_lat_kernel(qc_ref, qa_ref, ropeq_ref, kc_ref, vc_ref, kva_ref, kpe_ref, kctx_ref, vctx_ref, ckvctx_ref,
                     kpectx_ref, rope_ref, sink_ref, qan_ref, kvn_ref, wqb_ref, wkvb_ref, oc_ref, od_ref,
                     kwin_s, vwin_s, kext_s, vext_s, *, seq, past):
    qi = pl.program_id(1)
    nope_w = NH_D * NOPE_D
    n_all = past + seq
    ctx0 = 2 * WINDOW + seq

    @pl.when(qi == 0)
    def _():
        zeros = jnp.zeros((WINDOW, LANES), BF16)
        for ref, lat, ctx in ((kwin_s, _rope(kc_ref[...], rope_ref[0], rope_ref[1], HD_C // 2), kctx_ref[...]),
                              (vwin_s, vc_ref[...], vctx_ref[...])):
            lat_sw, ctx_sw = pltpu.roll(lat, HD_C, 1), pltpu.roll(ctx, HD_C, 1)
            for kh in range(NKV_C):
                lat_bd = _shared_split(lat, lat_sw, kh).astype(BF16)
                ctx_bd = _shared_split(ctx, ctx_sw, kh).astype(BF16)
                for i in range(2):
                    ref[kh, i, 0:WINDOW, :] = zeros
                    ref[kh, i, WINDOW:WINDOW + seq, :] = lat_bd[i * seq:(i + 1) * seq]
                    ref[kh, i, WINDOW + seq:ctx0, :] = zeros
                    ref[kh, i, ctx0:, :] = ctx_bd[i * past:(i + 1) * past]
        ckv = _rms(kva_ref[...], kvn_ref[...])
        kv = jnp.concatenate([_dot(ckvctx_ref[...].astype(BF16), wkvb_ref[...]),
                              _dot(ckv.astype(BF16), wkvb_ref[...])], axis=0)
        kpe = jnp.concatenate([kpectx_ref[...], _rope(kpe_ref[...], rope_ref[2], rope_ref[3], ROPE_D // 2)], axis=0)
        kpe_bd = jnp.concatenate([kpe, pltpu.roll(kpe, ROPE_D, 1)], axis=0).astype(BF16)
        ones_bd = _pair_split(jnp.ones((n_all, LANES), F32)).astype(BF16)
        for i in range(NH_D // 2):
            kext_s[i, :, 0:LANES] = _pair_split(kv[:, i * LANES:(i + 1) * LANES]).astype(BF16)
            kext_s[i, :, LANES:] = kpe_bd
            vext_s[i, :, 0:LANES] = _pair_split(kv[:, nope_w + i * LANES:nope_w + (i + 1) * LANES]).astype(BF16)
            vext_s[i, :, LANES:] = ones_bd

    r0 = pl.multiple_of(qi * Q_TILE, Q_TILE)
    nloc = Q_TILE + 2 * WINDOW
    n_keys = nloc + past
    qr = _rope(qc_ref[...], ropeq_ref[0], ropeq_ref[1], HD_C // 2) * (HD_C ** -0.5)
    ti = r0 + _row_iota((Q_TILE, n_keys))
    col = _lane_iota((Q_TILE, n_keys))
    pos = r0 - WINDOW + col
    valid = (col >= nloc) | ((jnp.abs(ti - pos) <= WINDOW) & (pos >= 0) & (pos < seq))
    ones_bd = _pair_split(jnp.ones((n_keys, LANES), F32)).astype(BF16)
    qd = _mla_queries(qa_ref, qan_ref, wqb_ref)
    q_pe = _rope(qd[:, nope_w:], ropeq_ref[2], ropeq_ref[3], ROPE_D // 2)

    def banded(ref, kh):
        return jnp.concatenate([ref[kh, 0, pl.ds(r0, nloc), :], ref[kh, 0, ctx0:, :],
                                ref[kh, 1, pl.ds(r0, nloc), :], ref[kh, 1, ctx0:, :]], axis=0)

    items = []
    for kh in range(NKV_C):
        for n in range(kh * G_C, (kh + 1) * G_C, 2):
            cols = slice(n * HD_C, (n + 2) * HD_C)
            items.append((lambda cols=cols, kh=kh: _dot_nt(qr[:, cols].astype(BF16), banded(kwin_s, kh)),
                          lambda kh=kh: jnp.concatenate([banded(vwin_s, kh), ones_bd], axis=1),
                          _pair_sinks(sink_ref, n), oc_ref, cols))
    for i in range(NH_D // 2):
        cols = slice(i * LANES, (i + 1) * LANES)
        items.append((lambda cols=cols, i=i: _dot_nt(jnp.concatenate([qd[:, cols], q_pe[:, cols]], axis=1).astype(BF16),
                                                     kext_s[i]),
                      lambda i=i: vext_s[i], None, od_ref, cols))
    _run_pairs(items, valid)


def _attn_lat(proj, caches, rope, params, seq, sub_layer):
    qc, kc, vc, qa, kva, kpe = proj
    n = qc.shape[0]
    past = caches[0].shape[2]
    nq = seq // Q_TILE
    qrow = lambda b, q: (b * nq + q, 0)
    krow = lambda b, q: (b, 0)
    kvw = NKV_C * HD_C
    in_specs = [pl.BlockSpec((Q_TILE, NH_C * HD_C), qrow), pl.BlockSpec((Q_TILE, Q_RANK), qrow),
                pl.BlockSpec((4, Q_TILE, LANES), lambda b, q: (0, q, 0)),
                pl.BlockSpec((seq, kvw), krow), pl.BlockSpec((seq, kvw), krow),
                pl.BlockSpec((seq, KV_RANK), krow), pl.BlockSpec((seq, LANES), krow)]
    in_specs += [pl.BlockSpec((None, None, past, LANES), lambda b, q: (b, sub_layer, 0, 0)) for _ in caches]
    in_specs += [pl.BlockSpec(rope.shape, lambda b, q: (0, 0, 0))]
    in_specs += [_layer_spec(p, sub_layer) for p in params]
    half = NH_C * HD_C
    win_rows = 2 * WINDOW + seq + past
    return pl.pallas_call(
        functools.partial(_attn_lat_kernel, seq=seq, past=past),
        grid=(n // seq, nq), in_specs=in_specs,
        out_specs=[pl.BlockSpec((Q_TILE, half), qrow), pl.BlockSpec((Q_TILE, half), qrow)],
        out_shape=[jax.ShapeDtypeStruct((n, half), F32), jax.ShapeDtypeStruct((n, half), F32)],
        scratch_shapes=[pltpu.VMEM((NKV_C, 2, win_rows, LANES), BF16), pltpu.VMEM((NKV_C, 2, win_rows, LANES), BF16),
                        pltpu.VMEM((NH_D // 2, 2 * (past + seq), 2 * LANES), BF16),
                        pltpu.VMEM((NH_D // 2, 2 * (past + seq), 2 * LANES), BF16)],
        compiler_params=_params("arbitrary", "arbitrary"), name="attn_lat",
    )(qc, qa, rope, kc, vc, kva, kpe, *caches, rope, *params)


def _pad_lanes(x, width=LANES):
    return jnp.pad(x, [(0, 0)] * (x.ndim - 1) + [(0, width - x.shape[-1])])


def _on_lanes(x):
    return jnp.broadcast_to(x[..., None], x.shape + (LANES,))


def _mla_query_weights(w):
    lead = w.shape[:-1]
    w4 = w.reshape(lead + (NH_D // 2, 2, NOPE_D + ROPE_D))
    nope = w4[..., :NOPE_D].reshape(lead + (NH_D * NOPE_D,))
    pe = _pad_lanes(w4[..., NOPE_D:].reshape(lead + (NH_D // 2, 2 * ROPE_D)))
    return jnp.concatenate([nope, pe.reshape(lead + (NH_D // 2 * LANES,))], axis=-1).astype(BF16)


def _mla_kv_weights(w):
    lead = w.shape[:-1]
    w3 = w.reshape(lead + (NH_D, NOPE_D + V_D))
    return jnp.concatenate([w3[..., :NOPE_D].reshape(lead + (NH_D * NOPE_D,)),
                            w3[..., NOPE_D:].reshape(lead + (NH_D * V_D,))], axis=-1).astype(BF16)


def _rope_tables(rows):
    def table(rot_dim):
        quarter = rot_dim // 4
        inv = ROPE_BASE ** (-jnp.arange(quarter, dtype=F32) / quarter)
        r = jnp.repeat(jnp.arange(rows, dtype=F32), GRID_W)
        col = jnp.tile(jnp.arange(GRID_W, dtype=F32), rows)
        ang = jnp.concatenate([r[:, None] * inv, col[:, None] * inv], axis=-1)
        reps = LANES // (rot_dim // 2)
        return jnp.tile(jnp.cos(ang), (1, reps)), jnp.tile(jnp.sin(ang), (1, reps))
    cos_c, sin_c = table(HD_C)
    cos_d, sin_d = table(ROPE_D)
    return jnp.stack([cos_c, sin_c, cos_d, sin_d])


def kernel(x_prompt, x_sample, c, state_mlstm_C, state_mlstm_n, state_mlstm_m, state_ssd, cache_gqa_k, cache_gqa_v,
           cache_mla_ckv, cache_mla_kpe, c_ctx, w_ada, b_ada, norm_g, w_up, w_down, w_in_even, conv_a_w, conv_a_b,
           conv_b_w, conv_b_b, gate_b, a_norm_w, dt_bias, a_log, d_skip, b_norm_w, w_out_even, w_in_odd, sink,
           q_a_norm, kv_a_norm, w_q_b, w_kv_b, w_out_odd):
    xp = x_prompt.reshape(BATCH * SEQ, D_MODEL)
    xs = x_sample.reshape(DEC_BATCH * DEC_SEQ, D_MODEL)
    cond = jnp.concatenate([c_ctx[None, :], c, jnp.zeros((MOD_ROWS - 1 - DEC_BATCH, D_MODEL), F32)], axis=0)
    mods = _modulations(cond, w_ada, b_ada)
    rope = _rope_tables(DEC_SEQ // GRID_W)

    w_even = w_in_even.astype(BF16)
    a_params = (conv_a_w, conv_a_b[:, None, :], _on_lanes(gate_b), a_norm_w[:, None, :])
    b_params = (conv_b_w, conv_b_b[:, None, :], _on_lanes(dt_bias.reshape(N_EVEN, 2 * NH_B)),
                _on_lanes(a_log.reshape(N_EVEN, 2 * NH_B)), jnp.repeat(d_skip, HP_B, axis=1)[:, None, :],
                b_norm_w[:, None, :])
    n0 = state_mlstm_n[..., None]
    mem_in = (jnp.concatenate([state_mlstm_C, jnp.broadcast_to(n0, n0.shape[:-1] + (LANES,))], axis=-1),
              _on_lanes(state_mlstm_m.reshape(DEC_BATCH, N_EVEN, 2 * NH_A)))
    w_odd = w_in_odd.astype(BF16)
    o_params = (_pad_lanes(sink)[:, None, :], q_a_norm[:, None, :], kv_a_norm[:, None, :],
                _mla_query_weights(w_q_b), _mla_kv_weights(w_kv_b))
    caches = (cache_gqa_k.reshape(DEC_BATCH, N_ODD, PAST_LEN, NKV_C * HD_C),
              cache_gqa_v.reshape(DEC_BATCH, N_ODD, PAST_LEN, NKV_C * HD_C),
              cache_mla_ckv, _pad_lanes(cache_mla_kpe))

    new_k, new_v, new_kpe = [], [], []
    mem_state, ssd_state, ckv_state = None, None, None
    for l in range(DEPTH):
        j = l // 2
        if l % 2 == 0:
            a1p, *mem_state = _fused_mixer(_mlstm_kernel, "mlstm", xp, mods, l, norm_g, w_even, MLSTM_W_COLS, a_params,
                                           SEQ, MLSTM_PIECES, MLSTM_GATES, A_V, MLSTM_STATE_SHAPES, mem_state)
            a2p, *ssd_state = _fused_mixer(_ssd_kernel, "ssd", xp, mods, l, norm_g, w_even, w_even.shape[2], b_params,
                                           SEQ, SSD_PIECES, SSD_GATES, B_INNER, SSD_STATE_SHAPES, ssd_state)
            qk, v, o, z, xbc, g = _project(xs, mods, l, True, norm_g, w_even, EVEN_WIDTHS, EVEN_REGROUP, EVEN_GATE_COLS)
            a1s = _mlstm_latent(qk, v, o, g, a_params, DEC_SEQ, j, mem_in)
            a2s = _ssd_latent(xbc, z, g, b_params, DEC_SEQ, j, state_ssd)
            w_out = w_out_even
        else:
            a1p, a2p, ckv_state, kc, vc, kpe = _attn_ctx(xp, mods, l, norm_g, w_odd, o_params, SEQ, carried=ckv_state)
            new_k.append(kc.reshape(BATCH, SEQ, NKV_C, HD_C))
            new_v.append(vc.reshape(BATCH, SEQ, NKV_C, HD_C))
            new_kpe.append(kpe[:, :ROPE_D].reshape(BATCH, SEQ, ROPE_D))
            proj = _project(xs, mods, l, True, norm_g, w_odd, ODD_WIDTHS, ODD_REGROUP)
            a1s, a2s = _attn_lat(proj, caches, rope, o_params, DEC_SEQ, j)
            w_out = w_out_odd
        xp, xs = _channel((a1p, a2p, xp), (a1s, a2s, xs), mods, l, norm_g, w_out, w_up, w_down)

    new_c, new_n, new_m = mem_state
    return (xp.reshape(BATCH, SEQ, D_MODEL), xs.reshape(DEC_BATCH, DEC_SEQ, D_MODEL),
            new_c, new_n, new_m[..., 0].reshape(BATCH, N_EVEN, 2, NH_A), ssd_state[0],
            jnp.stack(new_k, axis=1), jnp.stack(new_v, axis=1), ckv_state, jnp.stack(new_kpe, axis=1))
```

```python
import functools

import jax
import jax.numpy as jnp
from jax import lax
from jax.experimental import pallas as pl
from jax.experimental.pallas import tpu as pltpu

F32 = jnp.float32
BF16 = jnp.bfloat16

D_MODEL = 1024
BATCH = 32
SEQ = 256
DEPTH = 4
DEC_BATCH = 2
DEC_SEQ = 1024
PAST_LEN = 256
GRID_W = 64
N_EVEN = (DEPTH + 1) // 2
N_ODD = DEPTH // 2
EPS = 1e-6
CONV_K = 5
NH_A = 4
DK_A = 128
DV_A = 128
A_QK = NH_A * DK_A
A_V = NH_A * DV_A
NH_B = 8
HP_B = 64
DSTATE = 128
NG_B = 2
R_B = NH_B // NG_B
B_INNER = NH_B * HP_B
B_BC = NG_B * DSTATE
B_XBC = B_INNER + 2 * B_BC
NH_C = 8
NKV_C = 2
G_C = NH_C // NKV_C
HD_C = 64
WINDOW = 128
NH_D = 8
Q_RANK = 256
KV_RANK = 128
NOPE_D = 64
ROPE_D = 32
V_D = 64
MLA_SCALE = (NOPE_D + ROPE_D) ** -0.5
D_FF = 4 * D_MODEL
ROPE_BASE = 10000.0

LANES = 128
VMEM_LIMIT_BYTES = 56 * 1024 * 1024
ROW_TILE = 512
FF_TILE = 1024
SUB_ROWS = 256
STAGE_ROWS = 512
Q_TILE = 256
ADA_TILE = 1536
MOD_ROWS = 8
GATE_ROWS = 4 * NH_A + 2 * NH_B

EVEN_WIDTHS = (2 * A_QK, A_V, A_V, B_INNER, B_XBC)
ODD_WIDTHS = (NH_C * HD_C, NKV_C * HD_C, NKV_C * HD_C, Q_RANK, KV_RANK, LANES)
_GATES_LO = 2 * A_QK + 2 * A_V
_Z_LO = _GATES_LO + 4 * NH_A
_DT_LO = _Z_LO + B_INNER + B_XBC
EVEN_REGROUP = ((0, 0, _GATES_LO), (_GATES_LO, _Z_LO, B_INNER + B_XBC))
EVEN_GATE_COLS = ((_GATES_LO, _Z_LO), (_DT_LO, _DT_LO + 2 * NH_B))
ODD_IN = sum(ODD_WIDTHS) - LANES + ROPE_D
ODD_REGROUP = ((0, 0, ODD_IN), (ODD_IN, None, LANES - ROPE_D))

_NT = (((1,), (1,)), ((), ()))


def _params(*sem):
    return pltpu.CompilerParams(dimension_semantics=sem, vmem_limit_bytes=VMEM_LIMIT_BYTES)


def _rms(x, g):
    return x * lax.rsqrt(jnp.mean(x * x, axis=-1, keepdims=True) + EPS) * g


def _silu(x):
    return x * jax.nn.sigmoid(x)


def _softplus(x):
    return jnp.maximum(x, 0.0) + jnp.log1p(jnp.exp(-jnp.abs(x)))


def _dot(a, b):
    return jnp.dot(a, b, preferred_element_type=F32)


def _dot_nt(a, b):
    return lax.dot_general(a, b, _NT, preferred_element_type=F32)


def _layer_spec(arr, layer):
    tail = arr.shape[1:]
    zeros = (0,) * len(tail)
    return pl.BlockSpec((None,) + tail, lambda *_: (layer,) + zeros)


def _ada_kernel(c_ref, w_ref, b_ref, o_ref):
    s = _silu(c_ref[...]).astype(BF16)
    o_ref[...] = _dot(s, w_ref[...].astype(BF16)) + b_ref[...]


def _modulations(cond, w_ada, b_ada):
    out = pl.pallas_call(
        _ada_kernel,
        grid=(DEPTH, 6 * D_MODEL // ADA_TILE),
        in_specs=[pl.BlockSpec((MOD_ROWS, D_MODEL), lambda l, n: (0, 0)),
                  pl.BlockSpec((None, D_MODEL, ADA_TILE), lambda l, n: (l, 0, n)),
                  pl.BlockSpec((None, 1, ADA_TILE), lambda l, n: (l, 0, n))],
        out_specs=pl.BlockSpec((None, MOD_ROWS, ADA_TILE), lambda l, n: (l, 0, n)),
        out_shape=jax.ShapeDtypeStruct((DEPTH, MOD_ROWS, 6 * D_MODEL), F32),
        compiler_params=_params("arbitrary", "arbitrary"),
        name="ada",
    )(cond, w_ada, b_ada.reshape(DEPTH, 1, 6 * D_MODEL))
    return out.reshape(DEPTH, MOD_ROWS, 6, D_MODEL)


def _mod_spec(layer, latent):
    if latent:
        per_seq = DEC_SEQ // ROW_TILE
        return pl.BlockSpec((None, None, 6, D_MODEL), lambda i, *_: (layer, 1 + i // per_seq, 0, 0))
    return pl.BlockSpec((None, None, 6, D_MODEL), lambda i, *_: (layer, 0, 0, 0))


def _proj_kernel(x_ref, mod_ref, g_ref, w_ref, *rest, widths, regroup, gate_cols):
    n_out = len(widths) + (1 if gate_cols else 0)
    o_refs, w_s = rest[:n_out], rest[n_out]

    @pl.when(pl.program_id(0) == 0)
    def _():
        for dst, src, width in regroup:
            if src is None:
                w_s[:, dst:dst + width] = jnp.zeros((D_MODEL, width), BF16)
            else:
                w_s[:, dst:dst + width] = w_ref[:, src:src + width]
        if gate_cols:
            pieces = [w_ref[:, lo:hi].astype(F32) for lo, hi in gate_cols]
            n_gates = sum(hi - lo for lo, hi in gate_cols)
            gates = jnp.concatenate(pieces + [jnp.zeros((D_MODEL, LANES - n_gates), F32)], axis=1)
            rest[n_out + 1][...] = gates.T[0:n_gates, :].astype(BF16)

    h = _rms(x_ref[...], g_ref[0:1, :]) * (1.0 + mod_ref[1:2, :]) + mod_ref[0:1, :]
    hb = h.astype(BF16)
    off = 0
    for o_ref, wd in zip(o_refs, widths):
        o_ref[...] = _dot(hb, w_s[:, off:off + wd])
        off += wd
    if gate_cols:
        o_refs[-1][...] = _dot_nt(rest[n_out + 1][...], hb)


def _project(x, mods, layer, latent, gains, w_all, widths, regroup, gate_cols=()):
    n = x.shape[0]
    out_specs = [pl.BlockSpec((ROW_TILE, wd), lambda i: (i, 0)) for wd in widths]
    out_shape = [jax.ShapeDtypeStruct((n, wd), F32) for wd in widths]
    scratch = [pltpu.VMEM((D_MODEL, sum(widths)), BF16)]
    if gate_cols:
        n_gates = sum(hi - lo for lo, hi in gate_cols)
        out_specs.append(pl.BlockSpec((n_gates, ROW_TILE), lambda i: (0, i)))
        out_shape.append(jax.ShapeDtypeStruct((n_gates, n), F32))
        scratch.append(pltpu.VMEM((n_gates, D_MODEL), BF16))
    w_spec = pl.BlockSpec((None,) + w_all.shape[1:], lambda i: (layer // 2, 0, 0), pipeline_mode=pl.Buffered(1))
    return pl.pallas_call(
        functools.partial(_proj_kernel, widths=widths, regroup=regroup, gate_cols=gate_cols),
        grid=(n // ROW_TILE,),
        in_specs=[pl.BlockSpec((ROW_TILE, D_MODEL), lambda i: (i, 0)), _mod_spec(layer, latent),
                  _layer_spec(gains, layer), w_spec],
        out_specs=out_specs, out_shape=out_shape, scratch_shapes=scratch,
        compiler_params=_params("arbitrary"),
        name="proj",
    )(x, mods, gains, w_all)


def _weight_chunks(layer, sub_layer, wo_hbm, wu_hbm, wd_hbm, wo_s, wu_s, wd_s):
    chunks = []
    for r in range(0, D_MODEL, STAGE_ROWS):
        chunks.append((wo_hbm.at[sub_layer, pl.ds(r, STAGE_ROWS), :], wo_s.at[pl.ds(r, STAGE_ROWS), :]))
    for r in range(0, D_MODEL, STAGE_ROWS):
        for c in range(0, D_FF, D_MODEL):
            chunks.append((wu_hbm.at[layer, pl.ds(r, STAGE_ROWS), pl.ds(c, D_MODEL)],
                           wu_s.at[pl.ds(r, STAGE_ROWS), pl.ds(c, D_MODEL)]))
    for r in range(0, D_FF, STAGE_ROWS):
        chunks.append((wd_hbm.at[layer, pl.ds(r, STAGE_ROWS), :], wd_s.at[pl.ds(r, STAGE_ROWS), :]))
    return chunks


def _channel_kernel(a1p_ref, a2p_ref, xp_ref, a1s_ref, a2s_ref, xs_ref, mod_ref, g_ref, wo_hbm, wu_hbm, wd_hbm,
                    op_ref, os_ref, wo_s, wu_s, wd_s, stage, sem, *, layer, sub_layer, prompt_steps):
    step = pl.program_id(0)

    @pl.when(step == 0)
    def _():
        chunks = _weight_chunks(layer, sub_layer, wo_hbm, wu_hbm, wd_hbm, wo_s, wu_s, wd_s)
        copies = [pltpu.make_async_copy(src, stage.at[k % 2], sem.at[k % 2]) for k, (src, _) in enumerate(chunks)]
        copies[0].start()
        for k, (_, dst) in enumerate(chunks):
            if k + 1 < len(chunks):
                copies[k + 1].start()
            copies[k].wait()
            dst[...] = stage[k % 2].astype(BF16)

    def rows_block(a1_ref, a2_ref, x_ref, o_ref):
        half = a1_ref.shape[1]
        tiles = range(0, D_FF, FF_TILE)
        blocks = [slice(r0, r0 + SUB_ROWS) for r0 in range(0, ROW_TILE, SUB_ROWS)]

        def prologue(rows):
            y = (_dot(a1_ref[rows, :].astype(BF16), wo_s[0:half, :])
                 + _dot(a2_ref[rows, :].astype(BF16), wo_s[half:, :]))
            x1 = x_ref[rows, :] + mod_ref[2:3, :] * _rms(y, g_ref[1:2, :])
            h = (_rms(x1, g_ref[2:3, :]) * (1.0 + mod_ref[4:5, :]) + mod_ref[3:4, :]).astype(BF16)
            return x1, h

        def mlp_tile(h, c):
            u = jnp.square(jnp.maximum(_dot(h, wu_s[:, c:c + FF_TILE]), 0.0)).astype(BF16)
            return _dot(u, wd_s[c:c + FF_TILE, :])

        def epilogue(rows, x1, acc):
            o_ref[rows, :] = x1 + mod_ref[5:6, :] * _rms(acc, g_ref[3:4, :])

        ready = {0: prologue(blocks[0])}
        done = None
        for b, rows in enumerate(blocks):
            x1, h = ready.pop(b)
            acc = None
            for t, c in enumerate(tiles):
                part = mlp_tile(h, c)
                acc = part if acc is None else acc + part
                if t == 0 and b + 1 < len(blocks):
                    ready[b + 1] = prologue(blocks[b + 1])
                if t == 0 and done is not None:
                    epilogue(*done)
                    done = None
            done = (rows, x1, acc)
        epilogue(*done)

    @pl.when(step < prompt_steps)
    def _():
        rows_block(a1p_ref, a2p_ref, xp_ref, op_ref)

    @pl.when(step >= prompt_steps)
    def _():
        rows_block(a1s_ref, a2s_ref, xs_ref, os_ref)


def _channel(prompt, latent, mods, layer, gains, w_out, w_up, w_down):
    n_p, n_s = prompt[2].shape[0], latent[2].shape[0]
    steps_p, steps_s = n_p // ROW_TILE, n_s // ROW_TILE
    per_seq = DEC_SEQ // ROW_TILE
    row_p = lambda i: (jnp.minimum(i, steps_p - 1), 0)
    row_s = lambda i: (jnp.maximum(i - steps_p, 0), 0)
    mod_spec = pl.BlockSpec((None, None, 6, D_MODEL),
                            lambda i: (layer, jnp.where(i < steps_p, 0, 1 + (i - steps_p) // per_seq), 0, 0))
    hbm = pl.BlockSpec(memory_space=pl.ANY)
    specs = lambda arrs, row: [pl.BlockSpec((ROW_TILE, a.shape[1]), row) for a in arrs]
    return pl.pallas_call(
        functools.partial(_channel_kernel, layer=layer, sub_layer=layer // 2, prompt_steps=steps_p),
        grid=(steps_p + steps_s,),
        in_specs=specs(prompt, row_p) + specs(latent, row_s) + [mod_spec, _layer_spec(gains, layer), hbm, hbm, hbm],
        out_specs=[pl.BlockSpec((ROW_TILE, D_MODEL), row_p), pl.BlockSpec((ROW_TILE, D_MODEL), row_s)],
        out_shape=[jax.ShapeDtypeStruct((n_p, D_MODEL), F32), jax.ShapeDtypeStruct((n_s, D_MODEL), F32)],
        scratch_shapes=[pltpu.VMEM((D_MODEL, D_MODEL), BF16), pltpu.VMEM((D_MODEL, D_FF), BF16),
                        pltpu.VMEM((D_FF, D_MODEL), BF16), pltpu.VMEM((2, STAGE_ROWS, D_MODEL), F32),
                        pltpu.SemaphoreType.DMA((2,))],
        compiler_params=_params("arbitrary"),
        name="channel",
    )(*prompt, *latent, mods, gains, w_out, w_up, w_down)


def _row_iota(shape):
    return lax.broadcasted_iota(jnp.int32, shape, 0)


def _lane_iota(shape):
    return lax.broadcasted_iota(jnp.int32, shape, 1)


def _pair_lanes(shape):
    return _lane_iota(shape) < LANES // 2


def _cumsum_lanes(x, n_fwd):
    t = x.shape[1]
    si, ti = _row_iota((t, t)), _lane_iota((t, t))
    upper = jnp.where(si <= ti, 1.0, 0.0).astype(BF16)
    lower = jnp.where(si >= ti, 1.0, 0.0).astype(BF16)
    hi = x.astype(BF16)
    rest = x - hi.astype(F32)
    mid = rest.astype(BF16)
    lo = (rest - mid.astype(F32)).astype(BF16)
    pre = _dot(hi, upper) + _dot(mid, upper) + _dot(lo, upper)
    suf = _dot(hi, lower) + _dot(mid, lower) + _dot(lo, lower)
    return jnp.where(_row_iota(x.shape) < n_fwd, pre, suf)


def _cummax_lanes(x, n_fwd):
    t = x.shape[1]
    lane = _lane_iota(x.shape)
    pre, suf = x, x
    k = 1
    while k < t:
        pre = jnp.maximum(pre, jnp.where(lane >= k, pltpu.roll(pre, k, 1), -jnp.inf))
        suf = jnp.maximum(suf, jnp.where(lane < t - k, pltpu.roll(suf, t - k, 1), -jnp.inf))
        k *= 2
    return jnp.where(_row_iota(x.shape) < n_fwd, pre, suf)


def _columns(row_arrays):
    t = row_arrays[0].shape[1]
    used = sum(a.shape[0] for a in row_arrays)
    return jnp.concatenate(list(row_arrays) + [jnp.zeros((LANES - used, t), F32)], axis=0).T


def _dwconv_silu(x, w, b):
    t = x.shape[0]
    row = _row_iota(x.shape)
    acc = x * w[CONV_K // 2:CONV_K // 2 + 1, :] + b
    for j in range(CONV_K):
        d = j - CONV_K // 2
        if d == 0:
            continue
        shifted = pltpu.roll(x, (-d) % t, 0)
        valid = (row >= -d) if d < 0 else (row < t - d)
        acc = acc + jnp.where(valid, shifted, 0.0) * w[j:j + 1, :]
    return _silu(acc)


def _causal_exponent(expo, r0, k0, reverse):
    ti = r0 + _row_iota(expo.shape)
    si = k0 + _lane_iota(expo.shape)
    keep = (si >= ti) if reverse else (si <= ti)
    return jnp.where(keep, expo, -jnp.inf)


def _pair_split(x):
    first = _pair_lanes(x.shape)
    zero = jnp.zeros_like(x)
    return jnp.concatenate([jnp.where(first, x, zero), jnp.where(first, zero, x)], axis=0)


def _key_range(d, r0, seq):
    return (0, r0 + Q_TILE) if d == 0 else (r0, seq)


N_FUSED_INPUTS = 5
PROJ_CHUNK = 512

MLSTM_PIECES = ((0, 2 * A_QK), (2 * A_QK, A_V), (2 * A_QK + A_V, A_V))
MLSTM_GATES = (_GATES_LO, _Z_LO)
MLSTM_W_COLS = _GATES_LO + LANES
SSD_PIECES = ((_Z_LO + B_INNER, B_XBC), (_Z_LO, B_INNER))
SSD_GATES = (_DT_LO, _DT_LO + 2 * NH_B)
CTX_PIECES = tuple((sum(ODD_WIDTHS[:i]), wd) for i, wd in enumerate(ODD_WIDTHS[:-1])) + (
    (sum(ODD_WIDTHS[:-1]), LANES, ROPE_D),)


def _fused_project(step, fused_refs, scratch, pieces, gate_cols, mix):
    x_cur_ref, x_next_ref, mod_ref, gain_ref, w_ref = fused_refs
    w_s = scratch[0]
    n_head = 2 if gate_cols else 1
    n = len(pieces) + (1 if gate_cols else 0)
    sets = (scratch[n_head:n_head + n], scratch[n_head + n:n_head + 2 * n])

    def normed(x_ref):
        h = _rms(x_ref[...], gain_ref[0:1, :]) * (1.0 + mod_ref[1:2, :]) + mod_ref[0:1, :]
        return h.astype(BF16)

    def chunk_thunks(hb, dst):
        def gates():
            dst[0][...] = _dot_nt(scratch[1][...], hb)

        def columns(ref, c, src, width):
            def run():
                ref[:, c:c + width] = _dot(hb, w_s[:, src:src + width])
            return run

        thunks = [gates] if gate_cols else []
        off = 0
        for ref, piece in zip(dst[-len(pieces):], pieces):
            width = piece[1]
            thunks += [columns(ref, c, off + c, min(PROJ_CHUNK, width - c)) for c in range(0, width, PROJ_CHUNK)]
            off += width
        return thunks

    @pl.when(step == 0)
    def _():
        off = 0
        for piece in pieces:
            src, width = piece[0], piece[1]
            valid = piece[2] if len(piece) > 2 else width
            w_s[:, off:off + valid] = w_ref[:, src:src + valid]
            if valid < width:
                w_s[:, off + valid:off + width] = jnp.zeros((D_MODEL, width - valid), BF16)
            off += width
        if gate_cols:
            lo, hi = gate_cols
            gates = jnp.concatenate([w_ref[:, lo:hi].astype(F32), jnp.zeros((D_MODEL, LANES - (hi - lo)), F32)], axis=1)
            scratch[1][...] = gates.T[0:hi - lo, :].astype(BF16)
        for thunk in chunk_thunks(normed(x_cur_ref), sets[0]):
            thunk()

    for parity in range(2):
        @pl.when(step % 2 == parity)
        def _(parity=parity):
            cur = sets[parity]
            refs = list(cur[1:]) + [cur[0]] if gate_cols else list(cur)
            mix(*refs, chunk_thunks(normed(x_next_ref), sets[1 - parity]))


def _fused_specs(seq, nseq, layer, w_all, w_cols):
    nxt = lambda s: (jnp.minimum(s + 1, nseq - 1), 0)
    return [pl.BlockSpec((seq, D_MODEL), lambda s: (0, 0)), pl.BlockSpec((seq, D_MODEL), nxt),
            pl.BlockSpec((None, None, 6, D_MODEL), lambda s: (layer, 0, 0, 0)),
            pl.BlockSpec((None, 4, D_MODEL), lambda s: (layer, 0, 0)),
            pl.BlockSpec((None, D_MODEL, w_cols), lambda s: (layer // 2, 0, 0), pipeline_mode=pl.Buffered(1))]


def _fused_scratch(seq, pieces, n_gates):
    head = [pltpu.VMEM((D_MODEL, sum(p[1] for p in pieces)), BF16)]
    one_set = [pltpu.VMEM((seq, p[1]), F32) for p in pieces]
    if n_gates:
        head.append(pltpu.VMEM((n_gates, D_MODEL), BF16))
        one_set = [pltpu.VMEM((n_gates, seq), F32)] + one_set
    return head + one_set + one_set


def _run_share(side_work, stages_left):
    for _ in range(-(-len(side_work) // stages_left)):
        side_work.pop(0)()


def _mlstm_kernel(*refs, seq, has_state, emit_state, n_carried, slot, fused):
    assert not (has_state and emit_state)
    it = iter(refs)
    if fused:
        fused_refs = [next(it) for _ in range(N_FUSED_INPUTS)]
    else:
        qk_ref, v_ref, o_ref, g_ref = (next(it) for _ in range(4))
    cw_ref, cb_ref, gb_ref, anw_ref = (next(it) for _ in range(4))
    if has_state:
        c0_ref, m0_ref = next(it), next(it)
    for _ in range(n_carried):
        next(it)
    ha_ref = next(it)
    if emit_state:
        cn_ref, nn_ref, mn_ref = next(it), next(it), next(it)
        if n_carried == 0:
            for other in range(N_EVEN):
                if other != slot:
                    cn_ref[other] = jnp.zeros(cn_ref.shape[1:], F32)
                    nn_ref[other] = jnp.zeros(nn_ref.shape[1:], F32)
                    mn_ref[other] = jnp.zeros(mn_ref.shape[1:], F32)
            cn_ref, nn_ref, mn_ref = cn_ref.at[slot], nn_ref.at[slot], mn_ref.at[slot]

    def mix(qk_ref, v_ref, o_ref, g_ref, side_work=()):
        side_work = list(side_work)
        n_ch = 2 * NH_A
        log_i = g_ref[0:n_ch, :] + gb_ref[0:n_ch, 0:1]
        f_pre = g_ref[n_ch:2 * n_ch, :] + gb_ref[n_ch:2 * n_ch, 0:1]
        log_f = jnp.minimum(f_pre, 0.0) - jnp.log1p(jnp.exp(-jnp.abs(f_pre)))
        b = _cumsum_lanes(log_f, NH_A)
        a = log_i - b
        m_run = _cummax_lanes(a, NH_A)
        if has_state:
            m0 = m0_ref[:, 0:1]
            m_run = jnp.maximum(m_run, m0)
        else:
            m_run = jnp.maximum(m_run, 0.0)
        by_time = [m_run, jnp.exp(-(b + m_run))]
        if has_state:
            by_time.append(jnp.exp(m0 - m_run))
        if emit_state:
            fwd = _row_iota((n_ch, 1)) < NH_A
            b_last = jnp.where(fwd, b[:, seq - 1:seq], b[:, 0:1])
            m_last = jnp.where(fwd, m_run[:, seq - 1:seq], m_run[:, 0:1])
            mn_ref[...] = jnp.broadcast_to(b_last + m_last, (n_ch, LANES))
            by_time.append(jnp.exp(a - m_last))
        cols = _columns(by_time)

        ones = jnp.ones((seq, LANES), F32)
        for h in range(NH_A):
            _run_share(side_work, NH_A - h)
            cq = slice(h * DK_A, (h + 1) * DK_A)
            ck = slice(A_QK + h * DK_A, A_QK + (h + 1) * DK_A)
            cv = slice(h * DV_A, (h + 1) * DV_A)
            q = _dwconv_silu(qk_ref[:, cq], cw_ref[:, cq], cb_ref[:, cq])
            k = _dwconv_silu(qk_ref[:, ck], cw_ref[:, ck], cb_ref[:, ck]) * (DK_A ** -0.5)
            qb = q.astype(BF16)
            kb = k.astype(BF16)
            vh = v_ref[:, cv]
            vaug = jnp.concatenate([vh, ones], axis=1).astype(BF16)
            for r0 in range(0, seq, Q_TILE):
                rows = slice(r0, r0 + Q_TILE)
                s = _dot_nt(qb[rows], kb)
                hsum = None
                for d in range(2):
                    c = d * NH_A + h
                    k0, k1 = _key_range(d, r0, seq)
                    expo = _causal_exponent(a[c:c + 1, k0:k1] - cols[rows, c:c + 1], r0, k0, d == 1)
                    p = (s[:, k0:k1] * jnp.exp(expo)).astype(BF16)
                    acc = _dot(p, vaug[k0:k1])
                    if has_state:
                        acc = acc + (cols[rows, 2 * n_ch + c:2 * n_ch + c + 1]
                                     * _dot(qb[rows], c0_ref[d, h].astype(BF16)))
                    hd = acc[:, 0:DV_A] / jnp.maximum(jnp.abs(acc[:, DV_A:]), cols[rows, n_ch + c:n_ch + c + 1])
                    hsum = hd if hsum is None else hsum + hd
                og = jax.nn.sigmoid(o_ref[rows, cv]) * hsum
                ha_ref[rows, cv] = _rms(og, anw_ref[:, cv])
            if emit_state:
                for d in range(2):
                    c = d * NH_A + h
                    kw = k * cols[:, 2 * n_ch + c:2 * n_ch + c + 1]
                    cn_ref[d, h] = _dot(kw.T.astype(BF16), vh.astype(BF16))
                    nn_ref[d, h:h + 1, :] = jnp.sum(kw, axis=0, keepdims=True)

    if fused:
        _fused_project(pl.program_id(0), fused_refs, list(it), MLSTM_PIECES, MLSTM_GATES, mix)
    else:
        mix(qk_ref, v_ref, o_ref, g_ref)


def _state_out_specs(shapes, nseq, sub_layer, carried):
    out_specs, out_shape = [], []
    for shp in shapes:
        zeros = (0,) * len(shp)
        if carried is None:
            out_specs.append(pl.BlockSpec((None, N_EVEN) + shp, lambda s, z=zeros: (s, 0) + z))
        else:
            out_specs.append(pl.BlockSpec((None, None) + shp, lambda s, z=zeros: (s, sub_layer) + z))
        out_shape.append(jax.ShapeDtypeStruct((nseq, N_EVEN) + shp, F32))
    return out_specs, out_shape


MLSTM_STATE_SHAPES = ((2, NH_A, DK_A, DV_A), (2, NH_A, DK_A), (2 * NH_A, LANES))
SSD_STATE_SHAPES = ((2, NH_B, HP_B, DSTATE),)


def _fused_mixer(kernel_fn, name, x, mods, layer, gains, w_all, w_cols, params, seq, pieces, gate_cols,
                 out_width, state_shapes, carried):
    n = x.shape[0]
    nseq = n // seq
    sub_layer = layer // 2
    in_specs = _fused_specs(seq, nseq, layer, w_all, w_cols) + [_layer_spec(p, sub_layer) for p in params]
    args = [x, x, mods, gains, w_all, *params]
    state_specs, state_shape = _state_out_specs(state_shapes, nseq, sub_layer, carried)
    aliases = {}
    if carried is not None:
        aliases = {len(args) + i: 1 + i for i in range(len(state_shapes))}
        in_specs += [pl.BlockSpec(memory_space=pl.ANY)] * len(state_shapes)
        args += list(carried)
    return pl.pallas_call(
        functools.partial(kernel_fn, seq=seq, has_state=False, emit_state=True, n_carried=len(aliases),
                          slot=sub_layer, fused=True),
        grid=(nseq,), in_specs=in_specs,
        out_specs=[pl.BlockSpec((seq, out_width), lambda s: (s, 0))] + state_specs,
        out_shape=[jax.ShapeDtypeStruct((n, out_width), F32)] + state_shape,
        scratch_shapes=_fused_scratch(seq, pieces, gate_cols[1] - gate_cols[0]),
        input_output_aliases=aliases, compiler_params=_params("arbitrary"), name=name,
    )(*args)


def _mlstm_latent(qk, v, o, gates, params, seq, sub_layer, state):
    n = qk.shape[0]
    row = lambda s: (s, 0)
    c0_aug, m0 = state
    in_specs = [pl.BlockSpec((seq, 2 * A_QK), row), pl.BlockSpec((seq, A_V), row), pl.BlockSpec((seq, A_V), row),
                pl.BlockSpec((GATE_ROWS, seq), lambda s: (0, s))]
    in_specs += [_layer_spec(p, sub_layer) for p in params]
    in_specs += [pl.BlockSpec((None, None) + c0_aug.shape[2:], lambda s: (s, sub_layer, 0, 0, 0, 0)),
                 pl.BlockSpec((None, None) + m0.shape[2:], lambda s: (s, sub_layer, 0, 0))]
    return pl.pallas_call(
        functools.partial(_mlstm_kernel, seq=seq, has_state=True, emit_state=False, n_carried=0, slot=sub_layer,
                          fused=False),
        grid=(n // seq,), in_specs=in_specs, out_specs=pl.BlockSpec((seq, A_V), row),
        out_shape=jax.ShapeDtypeStruct((n, A_V), F32),
        compiler_params=_params("arbitrary"), name="mlstm",
    )(qk, v, o, gates, *params, c0_aug, m0)


def _ssd_kernel(*refs, seq, has_state, emit_state, n_carried, slot, fused):
    assert not (has_state and emit_state)
    it = iter(refs)
    if fused:
        fused_refs = [next(it) for _ in range(N_FUSED_INPUTS)]
    else:
        xbc_ref, z_ref, g_ref = (next(it) for _ in range(3))
    cw_ref, cb_ref, dtb_ref, alog_ref, dsk_ref, bnw_ref = (next(it) for _ in range(6))
    if has_state:
        s0_ref = next(it)
    for _ in range(n_carried):
        next(it)
    yb_ref = next(it)
    if emit_state:
        sn_ref = next(it)
        if n_carried == 0:
            for other in range(N_EVEN):
                if other != slot:
                    sn_ref[other] = jnp.zeros(sn_ref.shape[1:], F32)
            sn_ref = sn_ref.at[slot]

    gate_row0 = 0 if fused else 4 * NH_A

    def mix(xbc_ref, z_ref, g_ref, side_work=()):
        side_work = list(side_work)
        n_ch = 2 * NH_B
        dt = _softplus(g_ref[gate_row0:gate_row0 + n_ch, :] + dtb_ref[:, 0:1])
        acum = _cumsum_lanes(dt * (-jnp.exp(alog_ref[:, 0:1])), NH_B)
        key_shift = acum - jnp.log(dt)
        by_time = [acum]
        if has_state:
            by_time.append(jnp.exp(acum))
        if emit_state:
            fwd = _row_iota((n_ch, 1)) < NH_B
            a_last = jnp.where(fwd, acum[:, seq - 1:seq], acum[:, 0:1])
            by_time.append(jnp.exp(a_last - acum) * dt)
        cols = _columns(by_time)

        gw = R_B * HP_B
        first = _pair_lanes((seq, LANES))
        for g in range(NG_B):
            _run_share(side_work, NG_B - g)
            cx = slice(g * gw, (g + 1) * gw)
            cb_ = slice(B_INNER + g * DSTATE, B_INNER + (g + 1) * DSTATE)
            cc = slice(B_INNER + B_BC + g * DSTATE, B_INNER + B_BC + (g + 1) * DSTATE)
            xg = _dwconv_silu(xbc_ref[:, cx], cw_ref[:, cx], cb_ref[:, cx])
            bg = _dwconv_silu(xbc_ref[:, cb_], cw_ref[:, cb_], cb_ref[:, cb_]).astype(BF16)
            cg = _dwconv_silu(xbc_ref[:, cc], cw_ref[:, cc], cb_ref[:, cc]).astype(BF16)
            xbd = [_pair_split(xg[:, p * LANES:(p + 1) * LANES]).astype(BF16) for p in range(R_B // 2)]
            for r0 in range(0, seq, Q_TILE):
                rows = slice(r0, r0 + Q_TILE)
                cb_scores = _dot_nt(cg[rows], bg)
                ys = []
                for p in range(R_B // 2):
                    weights, inputs = [], []
                    for d in range(2):
                        k0, k1 = _key_range(d, r0, seq)
                        for i in range(2):
                            c = d * NH_B + g * R_B + 2 * p + i
                            expo = _causal_exponent(cols[rows, c:c + 1] - key_shift[c:c + 1, k0:k1], r0, k0, d == 1)
                            weights.append((cb_scores[:, k0:k1] * jnp.exp(expo)).astype(BF16))
                            inputs.append(xbd[p][i * seq + k0:i * seq + k1])
                    yp = _dot(jnp.concatenate(weights, axis=1), jnp.concatenate(inputs, axis=0))
                    if has_state:
                        h0 = g * R_B + 2 * p
                        for d in range(2):
                            c = d * NH_B + h0
                            carry = jnp.where(_pair_lanes((Q_TILE, LANES)), cols[rows, n_ch + c:n_ch + c + 1],
                                              cols[rows, n_ch + c + 1:n_ch + c + 2])
                            s0_pair = s0_ref[d, h0:h0 + 2].reshape(2 * HP_B, DSTATE).astype(BF16)
                            yp = yp + carry * _dot_nt(cg[rows], s0_pair)
                    ys.append(yp)
                y = jnp.concatenate(ys, axis=1) + dsk_ref[:, cx] * xg[rows]
                y = y * _silu(z_ref[rows, cx])
                yb_ref[rows, cx] = _rms(y, bnw_ref[:, cx])
            if emit_state:
                for d in range(2):
                    c0 = n_ch + d * NH_B + g * R_B
                    spread = jnp.concatenate([jnp.where(first, cols[:, c0 + 2 * p:c0 + 2 * p + 1],
                                                        cols[:, c0 + 2 * p + 1:c0 + 2 * p + 2])
                                              for p in range(R_B // 2)], axis=1)
                    sn = _dot((xg * spread).T.astype(BF16), bg)
                    for r in range(R_B):
                        sn_ref[d, g * R_B + r] = sn[r * HP_B:(r + 1) * HP_B, :]

    if fused:
        _fused_project(pl.program_id(0), fused_refs, list(it), SSD_PIECES, SSD_GATES, mix)
    else:
        mix(xbc_ref, z_ref, g_ref)


def _ssd_latent(xbc, z, gates, params, seq, sub_layer, state):
    n = xbc.shape[0]
    row = lambda s: (s, 0)
    in_specs = [pl.BlockSpec((seq, B_XBC), row), pl.BlockSpec((seq, B_INNER), row),
                pl.BlockSpec((GATE_ROWS, seq), lambda s: (0, s))]
    in_specs += [_layer_spec(p, sub_layer) for p in params]
    in_specs += [pl.BlockSpec((None, None, 2, NH_B, HP_B, DSTATE), lambda s: (s, sub_layer, 0, 0, 0, 0))]
    return pl.pallas_call(
        functools.partial(_ssd_kernel, seq=seq, has_state=True, emit_state=False, n_carried=0, slot=sub_layer,
                          fused=False),
        grid=(n // seq,), in_specs=in_specs, out_specs=pl.BlockSpec((seq, B_INNER), row),
        out_shape=jax.ShapeDtypeStruct((n, B_INNER), F32),
        compiler_params=_params("arbitrary"), name="ssd",
    )(xbc, z, gates, *params, state)


def _shared_split(x, x_swapped, kh):
    first = _pair_lanes(x.shape)
    zero = jnp.zeros_like(x)
    top, bottom = (x, x_swapped) if kh == 0 else (x_swapped, x)
    return jnp.concatenate([jnp.where(first, top, zero), jnp.where(first, zero, bottom)], axis=0)


def _pair_probs(s, sinks=None, valid=None):
    n_keys = s.shape[1] // 2
    probs, maxes = [], []
    for i in range(2):
        si = s[:, i * n_keys:(i + 1) * n_keys]
        if valid is not None:
            si = jnp.where(valid, si, -jnp.inf)
        m = jnp.max(si, axis=1, keepdims=True)
        if sinks is not None:
            m = jnp.maximum(m, sinks[i])
        probs.append(jnp.exp(si - m))
        maxes.append(m)
    return jnp.concatenate(probs, axis=1).astype(BF16), maxes


def _pair_output(p, maxes, vbd, sinks=None):
    o = _dot(p, vbd)
    den = o[:, LANES:]
    if sinks is not None:
        den = den + jnp.where(_pair_lanes(den.shape), jnp.exp(sinks[0] - maxes[0]), jnp.exp(sinks[1] - maxes[1]))
    return o[:, :LANES] / den


def _run_pairs(items, valid=None, side_work=()):
    side_work = list(side_work)
    s_next = items[0][0]()
    for idx, (_, values, sinks, out_ref, cols) in enumerate(items):
        _run_share(side_work, len(items) - idx)
        s_cur = s_next
        if idx + 1 < len(items):
            s_next = items[idx + 1][0]()
        p, maxes = _pair_probs(s_cur, sinks, valid if sinks is not None else None)
        out_ref[:, cols] = _pair_output(p, maxes, values(), sinks)


def _pair_sinks(sink_ref, n):
    return sink_ref[0:1, n:n + 1], sink_ref[0:1, n + 1:n + 2]


def _mla_queries(qa_ref, qan_ref, wqb_ref):
    return _dot(_rms(qa_ref[...], qan_ref[...]).astype(BF16), wqb_ref[...]) * MLA_SCALE


def _attn_ctx_kernel(*refs, seq, n_carried, slot, fused):
    it = iter(refs)
    if fused:
        fused_refs = [next(it) for _ in range(N_FUSED_INPUTS)]
    else:
        proj_refs = [next(it) for _ in range(len(ODD_WIDTHS))]
    sink_ref, qan_ref, kvn_ref, wqb_ref, wkvb_ref = (next(it) for _ in range(5))
    for _ in range(n_carried):
        next(it)
    oc_ref, od_ref, ckv_ref = next(it), next(it), next(it)
    if fused:
        new_k_ref, new_v_ref, new_kpe_ref = next(it), next(it), next(it)
    if n_carried == 0:
        for other in range(N_ODD):
            if other != slot:
                ckv_ref[other] = jnp.zeros(ckv_ref.shape[1:], F32)
        ckv_ref = ckv_ref.at[slot]

    def mix(qc_ref, kc_ref, vc_ref, qa_ref, kva_ref, kpe_ref, side_work=()):
        if fused:
            new_k_ref[...] = kc_ref[...]
            new_v_ref[...] = vc_ref[...]
            new_kpe_ref[...] = kpe_ref[...]
        ones_bd = _pair_split(jnp.ones((seq, LANES), F32))
        kc, vc = kc_ref[...], vc_ref[...]
        kc_sw, vc_sw = pltpu.roll(kc, HD_C, 1), pltpu.roll(vc, HD_C, 1)
        qd = _mla_queries(qa_ref, qan_ref, wqb_ref)
        ckv = _rms(kva_ref[...], kvn_ref[...])
        ckv_ref[...] = ckv
        kv = _dot(ckv.astype(BF16), wkvb_ref[...])
        kpe = kpe_ref[...]
        kpe_bd = jnp.concatenate([kpe, pltpu.roll(kpe, ROPE_D, 1)], axis=0)
        nope_w = NH_D * NOPE_D
        items = []
        for kh in range(NKV_C):
            for n in range(kh * G_C, (kh + 1) * G_C, 2):
                cols = slice(n * HD_C, (n + 2) * HD_C)
                items.append((lambda cols=cols, kh=kh: _dot_nt((qc_ref[:, cols] * (HD_C ** -0.5)).astype(BF16),
                                                               _shared_split(kc, kc_sw, kh).astype(BF16)),
                              lambda kh=kh: jnp.concatenate([_shared_split(vc, vc_sw, kh), ones_bd], axis=1).astype(BF16),
                              _pair_sinks(sink_ref, n), oc_ref, cols))
        for i in range(NH_D // 2):
            cols = slice(i * LANES, (i + 1) * LANES)
            vcols = slice(nope_w + i * LANES, nope_w + (i + 1) * LANES)
            items.append((lambda cols=cols, vcols=vcols: _dot_nt(
                              jnp.concatenate([qd[:, cols], qd[:, vcols]], axis=1).astype(BF16),
                              jnp.concatenate([_pair_split(kv[:, cols]), kpe_bd], axis=1).astype(BF16)),
                          lambda vcols=vcols: jnp.concatenate([_pair_split(kv[:, vcols]), ones_bd], axis=1).astype(BF16),
                          None, od_ref, cols))
        _run_pairs(items, side_work=side_work)

    if fused:
        _fused_project(pl.program_id(0), fused_refs, list(it), CTX_PIECES, None, mix)
    else:
        mix(*proj_refs)


def _attn_ctx(x, mods, layer, gains, w_all, params, seq, carried=None):
    n = x.shape[0]
    nseq = n // seq
    sub_layer = layer // 2
    row = lambda s: (s, 0)
    in_specs = _fused_specs(seq, nseq, layer, w_all, w_all.shape[2]) + [_layer_spec(p, sub_layer) for p in params]
    args = [x, x, mods, gains, w_all, *params]
    half = NH_C * HD_C
    aliases = {}
    if carried is None:
        ckv_spec = pl.BlockSpec((None, N_ODD, seq, KV_RANK), lambda s: (s, 0, 0, 0))
    else:
        ckv_spec = pl.BlockSpec((None, None, seq, KV_RANK), lambda s: (s, sub_layer, 0, 0))
        aliases = {len(args): 2}
        in_specs.append(pl.BlockSpec(memory_space=pl.ANY))
        args.append(carried)
    kv_w = NKV_C * HD_C
    return pl.pallas_call(
        functools.partial(_attn_ctx_kernel, seq=seq, n_carried=len(aliases), slot=sub_layer, fused=True),
        grid=(nseq,), in_specs=in_specs,
        out_specs=[pl.BlockSpec((seq, half), row), pl.BlockSpec((seq, half), row), ckv_spec,
                   pl.BlockSpec((seq, kv_w), row), pl.BlockSpec((seq, kv_w), row), pl.BlockSpec((seq, LANES), row)],
        out_shape=[jax.ShapeDtypeStruct((n, half), F32), jax.ShapeDtypeStruct((n, half), F32),
                   jax.ShapeDtypeStruct((nseq, N_ODD, seq, KV_RANK), F32),
                   jax.ShapeDtypeStruct((n, kv_w), F32), jax.ShapeDtypeStruct((n, kv_w), F32),
                   jax.ShapeDtypeStruct((n, LANES), F32)],
        scratch_shapes=_fused_scratch(seq, CTX_PIECES, 0),
        input_output_aliases=aliases, compiler_params=_params("arbitrary"), name="attn_ctx",
    )(*args)


def _rope(x, cos, sin, half):
    parts = []
    lane = _lane_iota((x.shape[0], LANES))
    first = (lane & (2 * half - 1)) < half
    for i in range(x.shape[1] // LANES):
        xi = x[:, i * LANES:(i + 1) * LANES]
        partner = jnp.where(first, -pltpu.roll(xi, LANES - half, 1), pltpu.roll(xi, half, 1))
        parts.append(xi * cos + partner * sin)
    return parts[0] if len(parts) == 1 else jnp.concatenate(parts, axis=1)


def _attn_lat_kernel(qc_ref, qa_ref, ropeq_ref, kc_ref, vc_ref, kva_ref, kpe_ref, kctx_ref, vctx_ref, ckvctx_ref,
                     kpectx_ref, rope_ref, sink_ref, qan_ref, kvn_ref, wqb_ref, wkvb_ref, oc_ref, od_ref,
                     kwin_s, vwin_s, kext_s, vext_s, *, seq, past):
    qi = pl.program_id(1)
    nope_w = NH_D * NOPE_D
    n_all = past + seq
    ctx0 = 2 * WINDOW + seq

    @pl.when(qi == 0)
    def _():
        zeros = jnp.zeros((WINDOW, LANES), BF16)
        for ref, lat, ctx in ((kwin_s, _rope(kc_ref[...], rope_ref[0], rope_ref[1], HD_C // 2), kctx_ref[...]),
                              (vwin_s, vc_ref[...], vctx_ref[...])):
            lat_sw, ctx_sw = pltpu.roll(lat, HD_C, 1), pltpu.roll(ctx, HD_C, 1)
            for kh in range(NKV_C):
                lat_bd = _shared_split(lat, lat_sw, kh).astype(BF16)
                ctx_bd = _shared_split(ctx, ctx_sw, kh).astype(BF16)
                for i in range(2):
                    ref[kh, i, 0:WINDOW, :] = zeros
                    ref[kh, i, WINDOW:WINDOW + seq, :] = lat_bd[i * seq:(i + 1) * seq]
                    ref[kh, i, WINDOW + seq:ctx0, :] = zeros
                    ref[kh, i, ctx0:, :] = ctx_bd[i * past:(i + 1) * past]
        ckv = _rms(kva_ref[...], kvn_ref[...])
        kv = jnp.concatenate([_dot(ckvctx_ref[...].astype(BF16), wkvb_ref[...]),
                              _dot(ckv.astype(BF16), wkvb_ref[...])], axis=0)
        kpe = jnp.concatenate([kpectx_ref[...], _rope(kpe_ref[...], rope_ref[2], rope_ref[3], ROPE_D // 2)], axis=0)
        kpe_bd = jnp.concatenate([kpe, pltpu.roll(kpe, ROPE_D, 1)], axis=0).astype(BF16)
        ones_bd = _pair_split(jnp.ones((n_all, LANES), F32)).astype(BF16)
        for i in range(NH_D // 2):
            kext_s[i, :, 0:LANES] = _pair_split(kv[:, i * LANES:(i + 1) * LANES]).astype(BF16)
            kext_s[i, :, LANES:] = kpe_bd
            vext_s[i, :, 0:LANES] = _pair_split(kv[:, nope_w + i * LANES:nope_w + (i + 1) * LANES]).astype(BF16)
            vext_s[i, :, LANES:] = ones_bd

    r0 = pl.multiple_of(qi * Q_TILE, Q_TILE)
    nloc = Q_TILE + 2 * WINDOW
    n_keys = nloc + past
    qr = _rope(qc_ref[...], ropeq_ref[0], ropeq_ref[1], HD_C // 2) * (HD_C ** -0.5)
    ti = r0 + _row_iota((Q_TILE, n_keys))
    col = _lane_iota((Q_TILE, n_keys))
    pos = r0 - WINDOW + col
    valid = (col >= nloc) | ((jnp.abs(ti - pos) <= WINDOW) & (pos >= 0) & (pos < seq))
    ones_bd = _pair_split(jnp.ones((n_keys, LANES), F32)).astype(BF16)
    qd = _mla_queries(qa_ref, qan_ref, wqb_ref)
    q_pe = _rope(qd[:, nope_w:], ropeq_ref[2], ropeq_ref[3], ROPE_D // 2)

    def banded(ref, kh):
        return jnp.concatenate([ref[kh, 0, pl.ds(r0, nloc), :], ref[kh, 0, ctx0:, :],
                                ref[kh, 1, pl.ds(r0, nloc), :], ref[kh, 1, ctx0:, :]], axis=0)

    items = []
    for kh in range(NKV_C):
        for n in range(kh * G_C, (kh + 1) * G_C, 2):
            cols = slice(n * HD_C, (n + 2) * HD_C)
            items.append((lambda cols=cols, kh=kh: _dot_nt(qr[:, cols].astype(BF16), banded(kwin_s, kh)),
                          lambda kh=kh: jnp.concatenate([banded(vwin_s, kh), ones_bd], axis=1),
                          _pair_sinks(sink_ref, n), oc_ref, cols))
    for i in range(NH_D // 2):
        cols = slice(i * LANES, (i + 1) * LANES)
        items.append((lambda cols=cols, i=i: _dot_nt(jnp.concatenate([qd[:, cols], q_pe[:, cols]], axis=1).astype(BF16),
                                                     kext_s[i]),
                      lambda i=i: vext_s[i], None, od_ref, cols))
    _run_pairs(items, valid)


def _attn_lat(proj, caches, rope, params, seq, sub_layer):
    qc, kc, vc, qa, kva, kpe = proj
    n = qc.shape[0]
    past = caches[0].shape[2]
    nq = seq // Q_TILE
    qrow = lambda b, q: (b * nq + q, 0)
    krow = lambda b, q: (b, 0)
    kvw = NKV_C * HD_C
    in_specs = [pl.BlockSpec((Q_TILE, NH_C * HD_C), qrow), pl.BlockSpec((Q_TILE, Q_RANK), qrow),
                pl.BlockSpec((4, Q_TILE, LANES), lambda b, q: (0, q, 0)),
                pl.BlockSpec((seq, kvw), krow), pl.BlockSpec((seq, kvw), krow),
                pl.BlockSpec((seq, KV_RANK), krow), pl.BlockSpec((seq, LANES), krow)]
    in_specs += [pl.BlockSpec((None, None, past, LANES), lambda b, q: (b, sub_layer, 0, 0)) for _ in caches]
    in_specs += [pl.BlockSpec(rope.shape, lambda b, q: (0, 0, 0))]
    in_specs += [_layer_spec(p, sub_layer) for p in params]
    half = NH_C * HD_C
    win_rows = 2 * WINDOW + seq + past
    return pl.pallas_call(
        functools.partial(_attn_lat_kernel, seq=seq, past=past),
        grid=(n // seq, nq), in_specs=in_specs,
        out_specs=[pl.BlockSpec((Q_TILE, half), qrow), pl.BlockSpec((Q_TILE, half), qrow)],
        out_shape=[jax.ShapeDtypeStruct((n, half), F32), jax.ShapeDtypeStruct((n, half), F32)],
        scratch_shapes=[pltpu.VMEM((NKV_C, 2, win_rows, LANES), BF16), pltpu.VMEM((NKV_C, 2, win_rows, LANES), BF16),
                        pltpu.VMEM((NH_D // 2, 2 * (past + seq), 2 * LANES), BF16),
                        pltpu.VMEM((NH_D // 2, 2 * (past + seq), 2 * LANES), BF16)],
        compiler_params=_params("arbitrary", "arbitrary"), name="attn_lat",
    )(qc, qa, rope, kc, vc, kva, kpe, *caches, rope, *params)


def _pad_lanes(x, width=LANES):
    return jnp.pad(x, [(0, 0)] * (x.ndim - 1) + [(0, width - x.shape[-1])])


def _on_lanes(x):
    return jnp.broadcast_to(x[..., None], x.shape + (LANES,))


def _mla_query_weights(w):
    lead = w.shape[:-1]
    w4 = w.reshape(lead + (NH_D // 2, 2, NOPE_D + ROPE_D))
    nope = w4[..., :NOPE_D].reshape(lead + (NH_D * NOPE_D,))
    pe = _pad_lanes(w4[..., NOPE_D:].reshape(lead + (NH_D // 2, 2 * ROPE_D)))
    return jnp.concatenate([nope, pe.reshape(lead + (NH_D // 2 * LANES,))], axis=-1).astype(BF16)


def _mla_kv_weights(w):
    lead = w.shape[:-1]
    w3 = w.reshape(lead + (NH_D, NOPE_D + V_D))
    return jnp.concatenate([w3[..., :NOPE_D].reshape(lead + (NH_D * NOPE_D,)),
                            w3[..., NOPE_D:].reshape(lead + (NH_D * V_D,))], axis=-1).astype(BF16)


def _rope_tables(rows):
    def table(rot_dim):
        quarter = rot_dim // 4
        inv = ROPE_BASE ** (-jnp.arange(quarter, dtype=F32) / quarter)
        r = jnp.repeat(jnp.arange(rows, dtype=F32), GRID_W)
        col = jnp.tile(jnp.arange(GRID_W, dtype=F32), rows)
        ang = jnp.concatenate([r[:, None] * inv, col[:, None] * inv], axis=-1)
        reps = LANES // (rot_dim // 2)
        return jnp.tile(jnp.cos(ang), (1, reps)), jnp.tile(jnp.sin(ang), (1, reps))
    cos_c, sin_c = table(HD_C)
    cos_d, sin_d = table(ROPE_D)
    return jnp.stack([cos_c, sin_c, cos_d, sin_d])


def kernel(x_prompt, x_sample, c, state_mlstm_C, state_mlstm_n, state_mlstm_m, state_ssd, cache_gqa_k, cache_gqa_v,
           cache_mla_ckv, cache_mla_kpe, c_ctx, w_ada, b_ada, norm_g, w_up, w_down, w_in_even, conv_a_w, conv_a_b,
           conv_b_w, conv_b_b, gate_b, a_norm_w, dt_bias, a_log, d_skip, b_norm_w, w_out_even, w_in_odd, sink,
           q_a_norm, kv_a_norm, w_q_b, w_kv_b, w_out_odd):
    xp = x_prompt.reshape(BATCH * SEQ, D_MODEL)
    xs = x_sample.reshape(DEC_BATCH * DEC_SEQ, D_MODEL)
    cond = jnp.concatenate([c_ctx[None, :], c, jnp.zeros((MOD_ROWS - 1 - DEC_BATCH, D_MODEL), F32)], axis=0)
    mods = _modulations(cond, w_ada, b_ada)
    rope = _rope_tables(DEC_SEQ // GRID_W)

    w_even = w_in_even.astype(BF16)
    a_params = (conv_a_w, conv_a_b[:, None, :], _on_lanes(gate_b), a_norm_w[:, None, :])
    b_params = (conv_b_w, conv_b_b[:, None, :], _on_lanes(dt_bias.reshape(N_EVEN, 2 * NH_B)),
                _on_lanes(a_log.reshape(N_EVEN, 2 * NH_B)), jnp.repeat(d_skip, HP_B, axis=1)[:, None, :],
                b_norm_w[:, None, :])
    n0 = state_mlstm_n[..., None]
    mem_in = (jnp.concatenate([state_mlstm_C, jnp.broadcast_to(n0, n0.shape[:-1] + (LANES,))], axis=-1),
              _on_lanes(state_mlstm_m.reshape(DEC_BATCH, N_EVEN, 2 * NH_A)))
    w_odd = w_in_odd.astype(BF16)
    o_params = (_pad_lanes(sink)[:, None, :], q_a_norm[:, None, :], kv_a_norm[:, None, :],
                _mla_query_weights(w_q_b), _mla_kv_weights(w_kv_b))
    caches = (cache_gqa_k.reshape(DEC_BATCH, N_ODD, PAST_LEN, NKV_C * HD_C),
              cache_gqa_v.reshape(DEC_BATCH, N_ODD, PAST_LEN, NKV_C * HD_C),
              cache_mla_ckv, _pad_lanes(cache_mla_kpe))

    new_k, new_v, new_kpe = [], [], []
    mem_state, ssd_state, ckv_state = None, None, None
    for l in range(DEPTH):
        j = l // 2
        if l % 2 == 0:
            a1p, *mem_state = _fused_mixer(_mlstm_kernel, "mlstm", xp, mods, l, norm_g, w_even, MLSTM_W_COLS, a_params,
                                           SEQ, MLSTM_PIECES, MLSTM_GATES, A_V, MLSTM_STATE_SHAPES, mem_state)
            a2p, *ssd_state = _fused_mixer(_ssd_kernel, "ssd", xp, mods, l, norm_g, w_even, w_even.shape[2], b_params,
                                           SEQ, SSD_PIECES, SSD_GATES, B_INNER, SSD_STATE_SHAPES, ssd_state)
            qk, v, o, z, xbc, g = _project(xs, mods, l, True, norm_g, w_even, EVEN_WIDTHS, EVEN_REGROUP, EVEN_GATE_COLS)
            a1s = _mlstm_latent(qk, v, o, g, a_params, DEC_SEQ, j, mem_in)
            a2s = _ssd_latent(xbc, z, g, b_params, DEC_SEQ, j, state_ssd)
            w_out = w_out_even
        else:
            a1p, a2p, ckv_state, kc, vc, kpe = _attn_ctx(xp, mods, l, norm_g, w_odd, o_params, SEQ, carried=ckv_state)
            new_k.append(kc.reshape(BATCH, SEQ, NKV_C, HD_C))
            new_v.append(vc.reshape(BATCH, SEQ, NKV_C, HD_C))
            new_kpe.append(kpe[:, :ROPE_D].reshape(BATCH, SEQ, ROPE_D))
            proj = _project(xs, mods, l, True, norm_g, w_odd, ODD_WIDTHS, ODD_REGROUP)
            a1s, a2s = _attn_lat(proj, caches, rope, o_params, DEC_SEQ, j)
            w_out = w_out_odd
        xp, xs = _channel((a1p, a2p, xp), (a1s, a2s, xs), mods, l, norm_g, w_out, w_up, w_down)

    new_c, new_n, new_m = mem_state
    return (xp.reshape(BATCH, SEQ, D_MODEL), xs.reshape(DEC_BATCH, DEC_SEQ, D_MODEL),
            new_c, new_n, new_m[..., 0].reshape(BATCH, N_EVEN, 2, NH_A), ssd_state[0],
            jnp.stack(new_k, axis=1), jnp.stack(new_v, axis=1), ckv_state, jnp.stack(new_kpe, axis=1))
```

```python
import functools

import jax
import jax.numpy as jnp
from jax import lax
from jax.experimental import pallas as pl
from jax.experimental.pallas import tpu as pltpu

F32 = jnp.float32
BF16 = jnp.bfloat16

D_MODEL = 1024
BATCH = 32
SEQ = 256
DEPTH = 4
DEC_BATCH = 2
DEC_SEQ = 1024
PAST_LEN = 256
GRID_W = 64
N_EVEN = (DEPTH + 1) // 2
N_ODD = DEPTH // 2
EPS = 1e-6
CONV_K = 5
NH_A = 4
DK_A = 128
DV_A = 128
A_QK = NH_A * DK_A
A_V = NH_A * DV_A
NH_B = 8
HP_B = 64
DSTATE = 128
NG_B = 2
R_B = NH_B // NG_B
B_INNER = NH_B * HP_B
B_BC = NG_B * DSTATE
B_XBC = B_INNER + 2 * B_BC
NH_C = 8
NKV_C = 2
G_C = NH_C // NKV_C
HD_C = 64
WINDOW = 128
NH_D = 8
Q_RANK = 256
KV_RANK = 128
NOPE_D = 64
ROPE_D = 32
V_D = 64
MLA_SCALE = (NOPE_D + ROPE_D) ** -0.5
D_FF = 4 * D_MODEL
ROPE_BASE = 10000.0

LANES = 128
VMEM_LIMIT_BYTES = 56 * 1024 * 1024
ROW_TILE = 512
FF_TILE = 1024
SUB_ROWS = 256
STAGE_ROWS = 512
Q_TILE = 256
ADA_TILE = 1536
MOD_ROWS = 8
GATE_ROWS = 4 * NH_A + 2 * NH_B

EVEN_WIDTHS = (2 * A_QK, A_V, A_V, B_INNER, B_XBC)
ODD_WIDTHS = (NH_C * HD_C, NKV_C * HD_C, NKV_C * HD_C, Q_RANK, KV_RANK, LANES)
_GATES_LO = 2 * A_QK + 2 * A_V
_Z_LO = _GATES_LO + 4 * NH_A
_DT_LO = _Z_LO + B_INNER + B_XBC
EVEN_REGROUP = ((0, 0, _GATES_LO), (_GATES_LO, _Z_LO, B_INNER + B_XBC))
EVEN_GATE_COLS = ((_GATES_LO, _Z_LO), (_DT_LO, _DT_LO + 2 * NH_B))
ODD_IN = sum(ODD_WIDTHS) - LANES + ROPE_D
ODD_REGROUP = ((0, 0, ODD_IN), (ODD_IN, None, LANES - ROPE_D))

_NT = (((1,), (1,)), ((), ()))


def _params(*sem):
    return pltpu.CompilerParams(dimension_semantics=sem, vmem_limit_bytes=VMEM_LIMIT_BYTES)


def _rms(x, g):
    return x * lax.rsqrt(jnp.mean(x * x, axis=-1, keepdims=True) + EPS) * g


def _silu(x):
    return x * jax.nn.sigmoid(x)


def _softplus(x):
    return jnp.maximum(x, 0.0) + jnp.log1p(jnp.exp(-jnp.abs(x)))


def _dot(a, b):
    return jnp.dot(a, b, preferred_element_type=F32)


def _dot_nt(a, b):
    return lax.dot_general(a, b, _NT, preferred_element_type=F32)


def _layer_spec(arr, layer):
    tail = arr.shape[1:]
    zeros = (0,) * len(tail)
    return pl.BlockSpec((None,) + tail, lambda *_: (layer,) + zeros)


def _ada_kernel(c_ref, w_ref, b_ref, o_ref):
    s = _silu(c_ref[...]).astype(BF16)
    o_ref[...] = _dot(s, w_ref[...].astype(BF16)) + b_ref[...]


def _modulations(cond, w_ada, b_ada):
    out = pl.pallas_call(
        _ada_kernel,
        grid=(DEPTH, 6 * D_MODEL // ADA_TILE),
        in_specs=[pl.BlockSpec((MOD_ROWS, D_MODEL), lambda l, n: (0, 0)),
                  pl.BlockSpec((None, D_MODEL, ADA_TILE), lambda l, n: (l, 0, n)),
                  pl.BlockSpec((None, 1, ADA_TILE), lambda l, n: (l, 0, n))],
        out_specs=pl.BlockSpec((None, MOD_ROWS, ADA_TILE), lambda l, n: (l, 0, n)),
        out_shape=jax.ShapeDtypeStruct((DEPTH, MOD_ROWS, 6 * D_MODEL), F32),
        compiler_params=_params("arbitrary", "arbitrary"),
        name="ada",
    )(cond, w_ada, b_ada.reshape(DEPTH, 1, 6 * D_MODEL))
    return out.reshape(DEPTH, MOD_ROWS, 6, D_MODEL)


def _mod_spec(layer, latent):
    if latent:
        per_seq = DEC_SEQ // ROW_TILE
        return pl.BlockSpec((None, None, 6, D_MODEL), lambda i, *_: (layer, 1 + i // per_seq, 0, 0))
    return pl.BlockSpec((None, None, 6, D_MODEL), lambda i, *_: (layer, 0, 0, 0))


def _proj_kernel(x_ref, mod_ref, g_ref, w_ref, *rest, widths, regroup, gate_cols):
    n_out = len(widths) + (1 if gate_cols else 0)
    o_refs, w_s = rest[:n_out], rest[n_out]

    @pl.when(pl.program_id(0) == 0)
    def _():
        for dst, src, width in regroup:
            if src is None:
                w_s[:, dst:dst + width] = jnp.zeros((D_MODEL, width), BF16)
            else:
                w_s[:, dst:dst + width] = w_ref[:, src:src + width]
        if gate_cols:
            pieces = [w_ref[:, lo:hi].astype(F32) for lo, hi in gate_cols]
            n_gates = sum(hi - lo for lo, hi in gate_cols)
            gates = jnp.concatenate(pieces + [jnp.zeros((D_MODEL, LANES - n_gates), F32)], axis=1)
            rest[n_out + 1][...] = gates.T[0:n_gates, :].astype(BF16)

    h = _rms(x_ref[...], g_ref[0:1, :]) * (1.0 + mod_ref[1:2, :]) + mod_ref[0:1, :]
    hb = h.astype(BF16)
    off = 0
    for o_ref, wd in zip(o_refs, widths):
        o_ref[...] = _dot(hb, w_s[:, off:off + wd])
        off += wd
    if gate_cols:
        o_refs[-1][...] = _dot_nt(rest[n_out + 1][...], hb)


def _project(x, mods, layer, latent, gains, w_all, widths, regroup, gate_cols=()):
    n = x.shape[0]
    out_specs = [pl.BlockSpec((ROW_TILE, wd), lambda i: (i, 0)) for wd in widths]
    out_shape = [jax.ShapeDtypeStruct((n, wd), F32) for wd in widths]
    scratch = [pltpu.VMEM((D_MODEL, sum(widths)), BF16)]
    if gate_cols:
        n_gates = sum(hi - lo for lo, hi in gate_cols)
        out_specs.append(pl.BlockSpec((n_gates, ROW_TILE), lambda i: (0, i)))
        out_shape.append(jax.ShapeDtypeStruct((n_gates, n), F32))
        scratch.append(pltpu.VMEM((n_gates, D_MODEL), BF16))
    w_spec = pl.BlockSpec((None,) + w_all.shape[1:], lambda i: (layer // 2, 0, 0), pipeline_mode=pl.Buffered(1))
    return pl.pallas_call(
        functools.partial(_proj_kernel, widths=widths, regroup=regroup, gate_cols=gate_cols),
        grid=(n // ROW_TILE,),
        in_specs=[pl.BlockSpec((ROW_TILE, D_MODEL), lambda i: (i, 0)), _mod_spec(layer, latent),
                  _layer_spec(gains, layer), w_spec],
        out_specs=out_specs, out_shape=out_shape, scratch_shapes=scratch,
        compiler_params=_params("arbitrary"),
        name="proj",
    )(x, mods, gains, w_all)


def _weight_chunks(layer, sub_layer, wo_hbm, wu_hbm, wd_hbm, wo_s, wu_s, wd_s):
    chunks = []
    for r in range(0, D_MODEL, STAGE_ROWS):
        chunks.append((wo_hbm.at[sub_layer, pl.ds(r, STAGE_ROWS), :], wo_s.at[pl.ds(r, STAGE_ROWS), :]))
    for r in range(0, D_MODEL, STAGE_ROWS):
        for c in range(0, D_FF, D_MODEL):
            chunks.append((wu_hbm.at[layer, pl.ds(r, STAGE_ROWS), pl.ds(c, D_MODEL)],
                           wu_s.at[pl.ds(r, STAGE_ROWS), pl.ds(c, D_MODEL)]))
    for r in range(0, D_FF, STAGE_ROWS):
        chunks.append((wd_hbm.at[layer, pl.ds(r, STAGE_ROWS), :], wd_s.at[pl.ds(r, STAGE_ROWS), :]))
    return chunks


def _channel_kernel(a1p_ref, a2p_ref, xp_ref, a1s_ref, a2s_ref, xs_ref, mod_ref, g_ref, wo_hbm, wu_hbm, wd_hbm,
                    op_ref, os_ref, wo_s, wu_s, wd_s, stage, sem, *, layer, sub_layer, prompt_steps):
    step = pl.program_id(0)

    @pl.when(step == 0)
    def _():
        chunks = _weight_chunks(layer, sub_layer, wo_hbm, wu_hbm, wd_hbm, wo_s, wu_s, wd_s)
        copies = [pltpu.make_async_copy(src, stage.at[k % 2], sem.at[k % 2]) for k, (src, _) in enumerate(chunks)]
        copies[0].start()
        for k, (_, dst) in enumerate(chunks):
            if k + 1 < len(chunks):
                copies[k + 1].start()
            copies[k].wait()
            dst[...] = stage[k % 2].astype(BF16)

    def rows_block(a1_ref, a2_ref, x_ref, o_ref):
        half = a1_ref.shape[1]
        tiles = range(0, D_FF, FF_TILE)
        blocks = [slice(r0, r0 + SUB_ROWS) for r0 in range(0, ROW_TILE, SUB_ROWS)]

        def prologue(rows):
            y = (_dot(a1_ref[rows, :].astype(BF16), wo_s[0:half, :])
                 + _dot(a2_ref[rows, :].astype(BF16), wo_s[half:, :]))
            x1 = x_ref[rows, :] + mod_ref[2:3, :] * _rms(y, g_ref[1:2, :])
            h = (_rms(x1, g_ref[2:3, :]) * (1.0 + mod_ref[4:5, :]) + mod_ref[3:4, :]).astype(BF16)
            return x1, h

        def mlp_tile(h, c):
            u = jnp.square(jnp.maximum(_dot(h, wu_s[:, c:c + FF_TILE]), 0.0)).astype(BF16)
            return _dot(u, wd_s[c:c + FF_TILE, :])

        def epilogue(rows, x1, acc):
            o_ref[rows, :] = x1 + mod_ref[5:6, :] * _rms(acc, g_ref[3:4, :])

        ready = {0: prologue(blocks[0])}
        done = None
        for b, rows in enumerate(blocks):
            x1, h = ready.pop(b)
            acc = None
            for t, c in enumerate(tiles):
                part = mlp_tile(h, c)
                acc = part if acc is None else acc + part
                if t == 0 and b + 1 < len(blocks):
                    ready[b + 1] = prologue(blocks[b + 1])
                if t == 0 and done is not None:
                    epilogue(*done)
                    done = None
            done = (rows, x1, acc)
        epilogue(*done)

    @pl.when(step < prompt_steps)
    def _():
        rows_block(a1p_ref, a2p_ref, xp_ref, op_ref)

    @pl.when(step >= prompt_steps)
    def _():
        rows_block(a1s_ref, a2s_ref, xs_ref, os_ref)


def _channel(prompt, latent, mods, layer, gains, w_out, w_up, w_down):
    n_p, n_s = prompt[2].shape[0], latent[2].shape[0]
    steps_p, steps_s = n_p // ROW_TILE, n_s // ROW_TILE
    per_seq = DEC_SEQ // ROW_TILE
    row_p = lambda i: (jnp.minimum(i, steps_p - 1), 0)
    row_s = lambda i: (jnp.maximum(i - steps_p, 0), 0)
    mod_spec = pl.BlockSpec((None, None, 6, D_MODEL),
                            lambda i: (layer, jnp.where(i < steps_p, 0, 1 + (i - steps_p) // per_seq), 0, 0))
    hbm = pl.BlockSpec(memory_space=pl.ANY)
    specs = lambda arrs, row: [pl.BlockSpec((ROW_TILE, a.shape[1]), row) for a in arrs]
    return pl.pallas_call(
        functools.partial(_channel_kernel, layer=layer, sub_layer=layer // 2, prompt_steps=steps_p),
        grid=(steps_p + steps_s,),
        in_specs=specs(prompt, row_p) + specs(latent, row_s) + [mod_spec, _layer_spec(gains, layer), hbm, hbm, hbm],
        out_specs=[pl.BlockSpec((ROW_TILE, D_MODEL), row_p), pl.BlockSpec((ROW_TILE, D_MODEL), row_s)],
        out_shape=[jax.ShapeDtypeStruct((n_p, D_MODEL), F32), jax.ShapeDtypeStruct((n_s, D_MODEL), F32)],
        scratch_shapes=[pltpu.VMEM((D_MODEL, D_MODEL), BF16), pltpu.VMEM((D_MODEL, D_FF), BF16),
                        pltpu.VMEM((D_FF, D_MODEL), BF16), pltpu.VMEM((2, STAGE_ROWS, D_MODEL), F32),
                        pltpu.SemaphoreType.DMA((2,))],
        compiler_params=_params("arbitrary"),
        name="channel",
    )(*prompt, *latent, mods, gains, w_out, w_up, w_down)


def _row_iota(shape):
    return lax.broadcasted_iota(jnp.int32, shape, 0)


def _lane_iota(shape):
    return lax.broadcasted_iota(jnp.int32, shape, 1)


def _pair_lanes(shape):
    return _lane_iota(shape) < LANES // 2


def _cumsum_lanes(x, n_fwd):
    t = x.shape[1]
    si, ti = _row_iota((t, t)), _lane_iota((t, t))
    upper = jnp.where(si <= ti, 1.0, 0.0).astype(BF16)
    lower = jnp.where(si >= ti, 1.0, 0.0).astype(BF16)
    hi = x.astype(BF16)
    rest = x - hi.astype(F32)
    mid = rest.astype(BF16)
    lo = (rest - mid.astype(F32)).astype(BF16)
    pre = _dot(hi, upper) + _dot(mid, upper) + _dot(lo, upper)
    suf = _dot(hi, lower) + _dot(mid, lower) + _dot(lo, lower)
    return jnp.where(_row_iota(x.shape) < n_fwd, pre, suf)


def _cummax_lanes(x, n_fwd):
    t = x.shape[1]
    lane = _lane_iota(x.shape)
    pre, suf = x, x
    k = 1
    while k < t:
        pre = jnp.maximum(pre, jnp.where(lane >= k, pltpu.roll(pre, k, 1), -jnp.inf))
        suf = jnp.maximum(suf, jnp.where(lane < t - k, pltpu.roll(suf, t - k, 1), -jnp.inf))
        k *= 2
    return jnp.where(_row_iota(x.shape) < n_fwd, pre, suf)


def _columns(row_arrays):
    t = row_arrays[0].shape[1]
    used = sum(a.shape[0] for a in row_arrays)
    return jnp.concatenate(list(row_arrays) + [jnp.zeros((LANES - used, t), F32)], axis=0).T


def _dwconv_silu(x, w, b):
    t = x.shape[0]
    row = _row_iota(x.shape)
    acc = x * w[CONV_K // 2:CONV_K // 2 + 1, :] + b
    for j in range(CONV_K):
        d = j - CONV_K // 2
        if d == 0:
            continue
        shifted = pltpu.roll(x, (-d) % t, 0)
        valid = (row >= -d) if d < 0 else (row < t - d)
        acc = acc + jnp.where(valid, shifted, 0.0) * w[j:j + 1, :]
    return _silu(acc)


def _causal_exponent(expo, r0, k0, reverse):
    ti = r0 + _row_iota(expo.shape)
    si = k0 + _lane_iota(expo.shape)
    keep = (si >= ti) if reverse else (si <= ti)
    return jnp.where(keep, expo, -jnp.inf)


def _pair_split(x):
    first = _pair_lanes(x.shape)
    zero = jnp.zeros_like(x)
    return jnp.concatenate([jnp.where(first, x, zero), jnp.where(first, zero, x)], axis=0)


def _key_range(d, r0, seq):
    return (0, r0 + Q_TILE) if d == 0 else (r0, seq)


N_FUSED_INPUTS = 5
PROJ_CHUNK = 512

MLSTM_PIECES = ((0, 2 * A_QK), (2 * A_QK, A_V), (2 * A_QK + A_V, A_V))
MLSTM_GATES = (_GATES_LO, _Z_LO)
MLSTM_W_COLS = _GATES_LO + LANES
SSD_PIECES = ((_Z_LO + B_INNER, B_XBC), (_Z_LO, B_INNER))
SSD_GATES = (_DT_LO, _DT_LO + 2 * NH_B)
CTX_PIECES = tuple((sum(ODD_WIDTHS[:i]), wd) for i, wd in enumerate(ODD_WIDTHS[:-1])) + (
    (sum(ODD_WIDTHS[:-1]), LANES, ROPE_D),)


def _fused_project(step, fused_refs, scratch, pieces, gate_cols, mix):
    x_cur_ref, x_next_ref, mod_ref, gain_ref, w_ref = fused_refs
    w_s = scratch[0]
    n_head = 2 if gate_cols else 1
    n = len(pieces) + (1 if gate_cols else 0)
    sets = (scratch[n_head:n_head + n], scratch[n_head + n:n_head + 2 * n])

    def normed(x_ref):
        h = _rms(x_ref[...], gain_ref[0:1, :]) * (1.0 + mod_ref[1:2, :]) + mod_ref[0:1, :]
        return h.astype(BF16)

    def chunk_thunks(hb, dst):
        def gates():
            dst[0][...] = _dot_nt(scratch[1][...], hb)

        def columns(ref, c, src, width):
            def run():
                ref[:, c:c + width] = _dot(hb, w_s[:, src:src + width])
            return run

        thunks = [gates] if gate_cols else []
        off = 0
        for ref, piece in zip(dst[-len(pieces):], pieces):
            width = piece[1]
            thunks += [columns(ref, c, off + c, min(PROJ_CHUNK, width - c)) for c in range(0, width, PROJ_CHUNK)]
            off += width
        return thunks

    @pl.when(step == 0)
    def _():
        off = 0
        for piece in pieces:
            src, width = piece[0], piece[1]
            valid = piece[2] if len(piece) > 2 else width
            w_s[:, off:off + valid] = w_ref[:, src:src + valid]
            if valid < width:
                w_s[:, off + valid:off + width] = jnp.zeros((D_MODEL, width - valid), BF16)
            off += width
        if gate_cols:
            lo, hi = gate_cols
            gates = jnp.concatenate([w_ref[:, lo:hi].astype(F32), jnp.zeros((D_MODEL, LANES - (hi - lo)), F32)], axis=1)
            scratch[1][...] = gates.T[0:hi - lo, :].astype(BF16)
        for thunk in chunk_thunks(normed(x_cur_ref), sets[0]):
            thunk()

    for parity in range(2):
        @pl.when(step % 2 == parity)
        def _(parity=parity):
            cur = sets[parity]
            refs = list(cur[1:]) + [cur[0]] if gate_cols else list(cur)
            mix(*refs, chunk_thunks(normed(x_next_ref), sets[1 - parity]))


def _fused_specs(seq, nseq, layer, w_all, w_cols):
    nxt = lambda s: (jnp.minimum(s + 1, nseq - 1), 0)
    return [pl.BlockSpec((seq, D_MODEL), lambda s: (0, 0)), pl.BlockSpec((seq, D_MODEL), nxt),
            pl.BlockSpec((None, None, 6, D_MODEL), lambda s: (layer, 0, 0, 0)),
            pl.BlockSpec((None, 4, D_MODEL), lambda s: (layer, 0, 0)),
            pl.BlockSpec((None, D_MODEL, w_cols), lambda s: (layer // 2, 0, 0), pipeline_mode=pl.Buffered(1))]


def _fused_scratch(seq, pieces, n_gates):
    head = [pltpu.VMEM((D_MODEL, sum(p[1] for p in pieces)), BF16)]
    one_set = [pltpu.VMEM((seq, p[1]), F32) for p in pieces]
    if n_gates:
        head.append(pltpu.VMEM((n_gates, D_MODEL), BF16))
        one_set = [pltpu.VMEM((n_gates, seq), F32)] + one_set
    return head + one_set + one_set


def _run_share(side_work, stages_left):
    for _ in range(-(-len(side_work) // stages_left)):
        side_work.pop(0)()


def _mlstm_kernel(*refs, seq, has_state, emit_state, n_carried, slot, fused):
    assert not (has_state and emit_state)
    it = iter(refs)
    if fused:
        fused_refs = [next(it) for _ in range(N_FUSED_INPUTS)]
    else:
        qk_ref, v_ref, o_ref, g_ref = (next(it) for _ in range(4))
    cw_ref, cb_ref, gb_ref, anw_ref = (next(it) for _ in range(4))
    if has_state:
        c0_ref, m0_ref = next(it), next(it)
    for _ in range(n_carried):
        next(it)
    ha_ref = next(it)
    if emit_state:
        cn_ref, nn_ref, mn_ref = next(it), next(it), next(it)
        if n_carried == 0:
            for other in range(N_EVEN):
                if other != slot:
                    cn_ref[other] = jnp.zeros(cn_ref.shape[1:], F32)
                    nn_ref[other] = jnp.zeros(nn_ref.shape[1:], F32)
                    mn_ref[other] = jnp.zeros(mn_ref.shape[1:], F32)
            cn_ref, nn_ref, mn_ref = cn_ref.at[slot], nn_ref.at[slot], mn_ref.at[slot]

    def mix(qk_ref, v_ref, o_ref, g_ref, side_work=()):
        side_work = list(side_work)
        n_ch = 2 * NH_A
        log_i = g_ref[0:n_ch, :] + gb_ref[0:n_ch, 0:1]
        f_pre = g_ref[n_ch:2 * n_ch, :] + gb_ref[n_ch:2 * n_ch, 0:1]
        log_f = jnp.minimum(f_pre, 0.0) - jnp.log1p(jnp.exp(-jnp.abs(f_pre)))
        b = _cumsum_lanes(log_f, NH_A)
        a = log_i - b
        m_run = _cummax_lanes(a, NH_A)
        if has_state:
            m0 = m0_ref[:, 0:1]
            m_run = jnp.maximum(m_run, m0)
        else:
            m_run = jnp.maximum(m_run, 0.0)
        by_time = [m_run, jnp.exp(-(b + m_run))]
        if has_state:
            by_time.append(jnp.exp(m0 - m_run))
        if emit_state:
            fwd = _row_iota((n_ch, 1)) < NH_A
            b_last = jnp.where(fwd, b[:, seq - 1:seq], b[:, 0:1])
            m_last = jnp.where(fwd, m_run[:, seq - 1:seq], m_run[:, 0:1])
            mn_ref[...] = jnp.broadcast_to(b_last + m_last, (n_ch, LANES))
            by_time.append(jnp.exp(a - m_last))
        cols = _columns(by_time)

        ones = jnp.ones((seq, LANES), F32)
        for h in range(NH_A):
            _run_share(side_work, NH_A - h)
            cq = slice(h * DK_A, (h + 1) * DK_A)
            ck = slice(A_QK + h * DK_A, A_QK + (h + 1) * DK_A)
            cv = slice(h * DV_A, (h + 1) * DV_A)
            q = _dwconv_silu(qk_ref[:, cq], cw_ref[:, cq], cb_ref[:, cq])
            k = _dwconv_silu(qk_ref[:, ck], cw_ref[:, ck], cb_ref[:, ck]) * (DK_A ** -0.5)
            qb = q.astype(BF16)
            kb = k.astype(BF16)
            vh = v_ref[:, cv]
            vaug = jnp.concatenate([vh, ones], axis=1).astype(BF16)
            for r0 in range(0, seq, Q_TILE):
                rows = slice(r0, r0 + Q_TILE)
                s = _dot_nt(qb[rows], kb)
                hsum = None
                for d in range(2):
                    c = d * NH_A + h
                    k0, k1 = _key_range(d, r0, seq)
                    expo = _causal_exponent(a[c:c + 1, k0:k1] - cols[rows, c:c + 1], r0, k0, d == 1)
                    p = (s[:, k0:k1] * jnp.exp(expo)).astype(BF16)
                    acc = _dot(p, vaug[k0:k1])
                    if has_state:
                        acc = acc + (cols[rows, 2 * n_ch + c:2 * n_ch + c + 1]
                                     * _dot(qb[rows], c0_ref[d, h].astype(BF16)))
                    hd = acc[:, 0:DV_A] / jnp.maximum(jnp.abs(acc[:, DV_A:]), cols[rows, n_ch + c:n_ch + c + 1])
                    hsum = hd if hsum is None else hsum + hd
                og = jax.nn.sigmoid(o_ref[rows, cv]) * hsum
                ha_ref[rows, cv] = _rms(og, anw_ref[:, cv])
            if emit_state:
                for d in range(2):
                    c = d * NH_A + h
                    kw = k * cols[:, 2 * n_ch + c:2 * n_ch + c + 1]
                    cn_ref[d, h] = _dot(kw.T.astype(BF16), vh.astype(BF16))
                    nn_ref[d, h:h + 1, :] = jnp.sum(kw, axis=0, keepdims=True)

    if fused:
        _fused_project(pl.program_id(0), fused_refs, list(it), MLSTM_PIECES, MLSTM_GATES, mix)
    else:
        mix(qk_ref, v_ref, o_ref, g_ref)


def _state_out_specs(shapes, nseq, sub_layer, carried):
    out_specs, out_shape = [], []
    for shp in shapes:
        zeros = (0,) * len(shp)
        if carried is None:
            out_specs.append(pl.BlockSpec((None, N_EVEN) + shp, lambda s, z=zeros: (s, 0) + z))
        else:
            out_specs.append(pl.BlockSpec((None, None) + shp, lambda s, z=zeros: (s, sub_layer) + z))
        out_shape.append(jax.ShapeDtypeStruct((nseq, N_EVEN) + shp, F32))
    return out_specs, out_shape


MLSTM_STATE_SHAPES = ((2, NH_A, DK_A, DV_A), (2, NH_A, DK_A), (2 * NH_A, LANES))
SSD_STATE_SHAPES = ((2, NH_B, HP_B, DSTATE),)


def _fused_mixer(kernel_fn, name, x, mods, layer, gains, w_all, w_cols, params, seq, pieces, gate_cols,
                 out_width, state_shapes, carried):
    n = x.shape[0]
    nseq = n // seq
    sub_layer = layer // 2
    in_specs = _fused_specs(seq, nseq, layer, w_all, w_cols) + [_layer_spec(p, sub_layer) for p in params]
    args = [x, x, mods, gains, w_all, *params]
    state_specs, state_shape = _state_out_specs(state_shapes, nseq, sub_layer, carried)
    aliases = {}
    if carried is not None:
        aliases = {len(args) + i: 1 + i for i in range(len(state_shapes))}
        in_specs += [pl.BlockSpec(memory_space=pl.ANY)] * len(state_shapes)
        args += list(carried)
    return pl.pallas_call(
        functools.partial(kernel_fn, seq=seq, has_state=False, emit_state=True, n_carried=len(aliases),
                          slot=sub_layer, fused=True),
        grid=(nseq,), in_specs=in_specs,
        out_specs=[pl.BlockSpec((seq, out_width), lambda s: (s, 0))] + state_specs,
        out_shape=[jax.ShapeDtypeStruct((n, out_width), F32)] + state_shape,
        scratch_shapes=_fused_scratch(seq, pieces, gate_cols[1] - gate_cols[0]),
        input_output_aliases=aliases, compiler_params=_params("arbitrary"), name=name,
    )(*args)


def _mlstm_latent(qk, v, o, gates, params, seq, sub_layer, state):
    n = qk.shape[0]
    row = lambda s: (s, 0)
    c0_aug, m0 = state
    in_specs = [pl.BlockSpec((seq, 2 * A_QK), row), pl.BlockSpec((seq, A_V), row), pl.BlockSpec((seq, A_V), row),
                pl.BlockSpec((GATE_ROWS, seq), lambda s: (0, s))]
    in_specs += [_layer_spec(p, sub_layer) for p in params]
    in_specs += [pl.BlockSpec((None, None) + c0_aug.shape[2:], lambda s: (s, sub_layer, 0, 0, 0, 0)),
                 pl.BlockSpec((None, None) + m0.shape[2:], lambda s: (s, sub_layer, 0, 0))]
    return pl.pallas_call(
        functools.partial(_mlstm_kernel, seq=seq, has_state=True, emit_state=False, n_carried=0, slot=sub_layer,
                          fused=False),
        grid=(n // seq,), in_specs=in_specs, out_specs=pl.BlockSpec((seq, A_V), row),
        out_shape=jax.ShapeDtypeStruct((n, A_V), F32),
        compiler_params=_params("arbitrary"), name="mlstm",
    )(qk, v, o, gates, *params, c0_aug, m0)


def _ssd_kernel(*refs, seq, has_state, emit_state, n_carried, slot, fused):
    assert not (has_state and emit_state)
    it = iter(refs)
    if fused:
        fused_refs = [next(it) for _ in range(N_FUSED_INPUTS)]
    else:
        xbc_ref, z_ref, g_ref = (next(it) for _ in range(3))
    cw_ref, cb_ref, dtb_ref, alog_ref, dsk_ref, bnw_ref = (next(it) for _ in range(6))
    if has_state:
        s0_ref = next(it)
    for _ in range(n_carried):
        next(it)
    yb_ref = next(it)
    if emit_state:
        sn_ref = next(it)
        if n_carried == 0:
            for other in range(N_EVEN):
                if other != slot:
                    sn_ref[other] = jnp.zeros(sn_ref.shape[1:], F32)
            sn_ref = sn_ref.at[slot]

    gate_row0 = 0 if fused else 4 * NH_A

    def mix(xbc_ref, z_ref, g_ref, side_work=()):
        side_work = list(side_work)
        n_ch = 2 * NH_B
        dt = _softplus(g_ref[gate_row0:gate_row0 + n_ch, :] + dtb_ref[:, 0:1])
        acum = _cumsum_lanes(dt * (-jnp.exp(alog_ref[:, 0:1])), NH_B)
        key_shift = acum - jnp.log(dt)
        by_time = [acum]
        if has_state:
            by_time.append(jnp.exp(acum))
        if emit_state:
            fwd = _row_iota((n_ch, 1)) < NH_B
            a_last = jnp.where(fwd, acum[:, seq - 1:seq], acum[:, 0:1])
            by_time.append(jnp.exp(a_last - acum) * dt)
        cols = _columns(by_time)

        gw = R_B * HP_B
        first = _pair_lanes((seq, LANES))
        for g in range(NG_B):
            _run_share(side_work, NG_B - g)
            cx = slice(g * gw, (g + 1) * gw)
            cb_ = slice(B_INNER + g * DSTATE, B_INNER + (g + 1) * DSTATE)
            cc = slice(B_INNER + B_BC + g * DSTATE, B_INNER + B_BC + (g + 1) * DSTATE)
            xg = _dwconv_silu(xbc_ref[:, cx], cw_ref[:, cx], cb_ref[:, cx])
            bg = _dwconv_silu(xbc_ref[:, cb_], cw_ref[:, cb_], cb_ref[:, cb_]).astype(BF16)
            cg = _dwconv_silu(xbc_ref[:, cc], cw_ref[:, cc], cb_ref[:, cc]).astype(BF16)
            xbd = [_pair_split(xg[:, p * LANES:(p + 1) * LANES]).astype(BF16) for p in range(R_B // 2)]
            for r0 in range(0, seq, Q_TILE):
                rows = slice(r0, r0 + Q_TILE)
                cb_scores = _dot_nt(cg[rows], bg)
                ys = []
                for p in range(R_B // 2):
                    weights, inputs = [], []
                    for d in range(2):
                        k0, k1 = _key_range(d, r0, seq)
                        for i in range(2):
                            c = d * NH_B + g * R_B + 2 * p + i
                            expo = _causal_exponent(cols[rows, c:c + 1] - key_shift[c:c + 1, k0:k1], r0, k0, d == 1)
                            weights.append((cb_scores[:, k0:k1] * jnp.exp(expo)).astype(BF16))
                            inputs.append(xbd[p][i * seq + k0:i * seq + k1])
                    yp = _dot(jnp.concatenate(weights, axis=1), jnp.concatenate(inputs, axis=0))
                    if has_state:
                        h0 = g * R_B + 2 * p
                        for d in range(2):
                            c = d * NH_B + h0
                            carry = jnp.where(_pair_lanes((Q_TILE, LANES)), cols[rows, n_ch + c:n_ch + c + 1],
                                              cols[rows, n_ch + c + 1:n_ch + c + 2])
                            s0_pair = s0_ref[d, h0:h0 + 2].reshape(2 * HP_B, DSTATE).astype(BF16)
                            yp = yp + carry * _dot_nt(cg[rows], s0_pair)
                    ys.append(yp)
                y = jnp.concatenate(ys, axis=1) + dsk_ref[:, cx] * xg[rows]
                y = y * _silu(z_ref[rows, cx])
                yb_ref[rows, cx] = _rms(y, bnw_ref[:, cx])
            if emit_state:
                for d in range(2):
                    c0 = n_ch + d * NH_B + g * R_B
                    spread = jnp.concatenate([jnp.where(first, cols[:, c0 + 2 * p:c0 + 2 * p + 1],
                                                        cols[:, c0 + 2 * p + 1:c0 + 2 * p + 2])
                                              for p in range(R_B // 2)], axis=1)
                    sn = _dot((xg * spread).T.astype(BF16), bg)
                    for r in range(R_B):
                        sn_ref[d, g * R_B + r] = sn[r * HP_B:(r + 1) * HP_B, :]

    if fused:
        _fused_project(pl.program_id(0), fused_refs, list(it), SSD_PIECES, SSD_GATES, mix)
    else:
        mix(xbc_ref, z_ref, g_ref)


def _ssd_latent(xbc, z, gates, params, seq, sub_layer, state):
    n = xbc.shape[0]
    row = lambda s: (s, 0)
    in_specs = [pl.BlockSpec((seq, B_XBC), row), pl.BlockSpec((seq, B_INNER), row),
                pl.BlockSpec((GATE_ROWS, seq), lambda s: (0, s))]
    in_specs += [_layer_spec(p, sub_layer) for p in params]
    in_specs += [pl.BlockSpec((None, None, 2, NH_B, HP_B, DSTATE), lambda s: (s, sub_layer, 0, 0, 0, 0))]
    return pl.pallas_call(
        functools.partial(_ssd_kernel, seq=seq, has_state=True, emit_state=False, n_carried=0, slot=sub_layer,
                          fused=False),
        grid=(n // seq,), in_specs=in_specs, out_specs=pl.BlockSpec((seq, B_INNER), row),
        out_shape=jax.ShapeDtypeStruct((n, B_INNER), F32),
        compiler_params=_params("arbitrary"), name="ssd",
    )(xbc, z, gates, *params, state)


def _shared_split(x, x_swapped, kh):
    first = _pair_lanes(x.shape)
    zero = jnp.zeros_like(x)
    top, bottom = (x, x_swapped) if kh == 0 else (x_swapped, x)
    return jnp.concatenate([jnp.where(first, top, zero), jnp.where(first, zero, bottom)], axis=0)


def _pair_probs(s, sinks=None, valid=None):
    n_keys = s.shape[1] // 2
    probs, maxes = [], []
    for i in range(2):
        si = s[:, i * n_keys:(i + 1) * n_keys]
        if valid is not None:
            si = jnp.where(valid, si, -jnp.inf)
        m = jnp.max(si, axis=1, keepdims=True)
        if sinks is not None:
            m = jnp.maximum(m, sinks[i])
        probs.append(jnp.exp(si - m))
        maxes.append(m)
    return jnp.concatenate(probs, axis=1).astype(BF16), maxes


def _pair_output(p, maxes, vbd, sinks=None):
    o = _dot(p, vbd)
    den = o[:, LANES:]
    if sinks is not None:
        den = den + jnp.where(_pair_lanes(den.shape), jnp.exp(sinks[0] - maxes[0]), jnp.exp(sinks[1] - maxes[1]))
    return o[:, :LANES] / den


def _run_pairs(items, valid=None, side_work=()):
    side_work = list(side_work)
    s_next = items[0][0]()
    for idx, (_, values, sinks, out_ref, cols) in enumerate(items):
        _run_share(side_work, len(items) - idx)
        s_cur = s_next
        if idx + 1 < len(items):
            s_next = items[idx + 1][0]()
        p, maxes = _pair_probs(s_cur, sinks, valid if sinks is not None else None)
        out_ref[:, cols] = _pair_output(p, maxes, values(), sinks)


def _pair_sinks(sink_ref, n):
    return sink_ref[0:1, n:n + 1], sink_ref[0:1, n + 1:n + 2]


def _mla_queries(qa_ref, qan_ref, wqb_ref):
    return _dot(_rms(qa_ref[...], qan_ref[...]).astype(BF16), wqb_ref[...]) * MLA_SCALE


def _attn_ctx_kernel(*refs, seq, n_carried, slot, fused):
    it = iter(refs)
    if fused:
        fused_refs = [next(it) for _ in range(N_FUSED_INPUTS)]
    else:
        proj_refs = [next(it) for _ in range(len(ODD_WIDTHS))]
    sink_ref, qan_ref, kvn_ref, wqb_ref, wkvb_ref = (next(it) for _ in range(5))
    for _ in range(n_carried):
        next(it)
    oc_ref, od_ref = next(it), next(it)
    caches = [next(it) for _ in range(4 if fused else 1)]
    if n_carried == 0:
        for other in range(N_ODD):
            if other != slot:
                for ref in caches:
                    ref[other] = jnp.zeros(ref.shape[1:], F32)
        caches = [ref.at[slot] for ref in caches]
    ckv_ref = caches[0]
    if fused:
        new_k_ref, new_v_ref, new_kpe_ref = caches[1:]

    def mix(qc_ref, kc_ref, vc_ref, qa_ref, kva_ref, kpe_ref, side_work=()):
        if fused:
            new_k_ref[...] = kc_ref[...]
            new_v_ref[...] = vc_ref[...]
            new_kpe_ref[...] = kpe_ref[...]
        ones_bd = _pair_split(jnp.ones((seq, LANES), F32))
        kc, vc = kc_ref[...], vc_ref[...]
        kc_sw, vc_sw = pltpu.roll(kc, HD_C, 1), pltpu.roll(vc, HD_C, 1)
        qd = _mla_queries(qa_ref, qan_ref, wqb_ref)
        ckv = _rms(kva_ref[...], kvn_ref[...])
        ckv_ref[...] = ckv
        kv = _dot(ckv.astype(BF16), wkvb_ref[...])
        kpe = kpe_ref[...]
        kpe_bd = jnp.concatenate([kpe, pltpu.roll(kpe, ROPE_D, 1)], axis=0)
        nope_w = NH_D * NOPE_D
        items = []
        for kh in range(NKV_C):
            for n in range(kh * G_C, (kh + 1) * G_C, 2):
                cols = slice(n * HD_C, (n + 2) * HD_C)
                items.append((lambda cols=cols, kh=kh: _dot_nt((qc_ref[:, cols] * (HD_C ** -0.5)).astype(BF16),
                                                               _shared_split(kc, kc_sw, kh).astype(BF16)),
                              lambda kh=kh: jnp.concatenate([_shared_split(vc, vc_sw, kh), ones_bd], axis=1).astype(BF16),
                              _pair_sinks(sink_ref, n), oc_ref, cols))
        for i in range(NH_D // 2):
            cols = slice(i * LANES, (i + 1) * LANES)
            vcols = slice(nope_w + i * LANES, nope_w + (i + 1) * LANES)
            items.append((lambda cols=cols, vcols=vcols: _dot_nt(
                              jnp.concatenate([qd[:, cols], qd[:, vcols]], axis=1).astype(BF16),
                              jnp.concatenate([_pair_split(kv[:, cols]), kpe_bd], axis=1).astype(BF16)),
                          lambda vcols=vcols: jnp.concatenate([_pair_split(kv[:, vcols]), ones_bd], axis=1).astype(BF16),
                          None, od_ref, cols))
        _run_pairs(items, side_work=side_work)

    if fused:
        _fused_project(pl.program_id(0), fused_refs, list(it), CTX_PIECES, None, mix)
    else:
        mix(*proj_refs)


def _attn_ctx(x, mods, layer, gains, w_all, params, seq, carried=None):
    n = x.shape[0]
    nseq = n // seq
    sub_layer = layer // 2
    row = lambda s: (s, 0)
    in_specs = _fused_specs(seq, nseq, layer, w_all, w_all.shape[2]) + [_layer_spec(p, sub_layer) for p in params]
    args = [x, x, mods, gains, w_all, *params]
    half = NH_C * HD_C
    cache_widths = (KV_RANK, NKV_C * HD_C, NKV_C * HD_C, LANES)
    aliases = {}
    if carried is None:
        cache_specs = [pl.BlockSpec((None, N_ODD, seq, w), lambda s: (s, 0, 0, 0)) for w in cache_widths]
    else:
        cache_specs = [pl.BlockSpec((None, None, seq, w), lambda s: (s, sub_layer, 0, 0)) for w in cache_widths]
        aliases = {len(args) + i: 2 + i for i in range(len(cache_widths))}
        in_specs += [pl.BlockSpec(memory_space=pl.ANY)] * len(cache_widths)
        args += list(carried)
    return pl.pallas_call(
        functools.partial(_attn_ctx_kernel, seq=seq, n_carried=len(aliases), slot=sub_layer, fused=True),
        grid=(nseq,), in_specs=in_specs,
        out_specs=[pl.BlockSpec((seq, half), row), pl.BlockSpec((seq, half), row)] + cache_specs,
        out_shape=[jax.ShapeDtypeStruct((n, half), F32), jax.ShapeDtypeStruct((n, half), F32)]
                  + [jax.ShapeDtypeStruct((nseq, N_ODD, seq, w), F32) for w in cache_widths],
        scratch_shapes=_fused_scratch(seq, CTX_PIECES, 0),
        input_output_aliases=aliases, compiler_params=_params("arbitrary"), name="attn_ctx",
    )(*args)


def _rope(x, cos, sin, half):
    parts = []
    lane = _lane_iota((x.shape[0], LANES))
    first = (lane & (2 * half - 1)) < half
    for i in range(x.shape[1] // LANES):
        xi = x[:, i * LANES:(i + 1) * LANES]
        partner = jnp.where(first, -pltpu.roll(xi, LANES - half, 1), pltpu.roll(xi, half, 1))
        parts.append(xi * cos + partner * sin)
    return parts[0] if len(parts) == 1 else jnp.concatenate(parts, axis=1)


def _attn_lat_kernel(qc_ref, qa_ref, ropeq_ref, kc_ref, vc_ref, kva_ref, kpe_ref, kctx_ref, vctx_ref, ckvctx_ref,
                     kpectx_ref, rope_ref, sink_ref, qan_ref, kvn_ref, wqb_ref, wkvb_ref, oc_ref, od_ref,
                     kwin_s, vwin_s, kext_s, vext_s, *, seq, past):
    qi = pl.program_id(1)
    nope_w = NH_D * NOPE_D
    n_all = past + seq
    ctx0 = 2 * WINDOW + seq

    @pl.when(qi == 0)
    def _():
        zeros = jnp.zeros((WINDOW, LANES), BF16)
        for ref, lat, ctx in ((kwin_s, _rope(kc_ref[...], rope_ref[0], rope_ref[1], HD_C // 2), kctx_ref[...]),
                              (vwin_s, vc_ref[...], vctx_ref[...])):
            lat_sw, ctx_sw = pltpu.roll(lat, HD_C, 1), pltpu.roll(ctx, HD_C, 1)
            for kh in range(NKV_C):
                lat_bd = _shared_split(lat, lat_sw, kh).astype(BF16)
                ctx_bd = _shared_split(ctx, ctx_sw, kh).astype(BF16)
                for i in range(2):
                    ref[kh, i, 0:WINDOW, :] = zeros
                    ref[kh, i, WINDOW:WINDOW + seq, :] = lat_bd[i * seq:(i + 1) * seq]
                    ref[kh, i, WINDOW + seq:ctx0, :] = zeros
                    ref[kh, i, ctx0:, :] = ctx_bd[i * past:(i + 1) * past]
        ckv = _rms(kva_ref[...], kvn_ref[...])
        kv = jnp.concatenate([_dot(ckvctx_ref[...].astype(BF16), wkvb_ref[...]),
                              _dot(ckv.astype(BF16), wkvb_ref[...])], axis=0)
        kpe = jnp.concatenate([kpectx_ref[...], _rope(kpe_ref[...], rope_ref[2], rope_ref[3], ROPE_D // 2)], axis=0)
        kpe_bd = jnp.concatenate([kpe, pltpu.roll(kpe, ROPE_D, 1)], axis=0).astype(BF16)
        ones_bd = _pair_split(jnp.ones((n_all, LANES), F32)).astype(BF16)
        for i in range(NH_D // 2):
            kext_s[i, :, 0:LANES] = _pair_split(kv[:, i * LANES:(i + 1) * LANES]).astype(BF16)
            kext_s[i, :, LANES:] = kpe_bd
            vext_s[i, :, 0:LANES] = _pair_split(kv[:, nope_w + i * LANES:nope_w + (i + 1) * LANES]).astype(BF16)
            vext_s[i, :, LANES:] = ones_bd

    r0 = pl.multiple_of(qi * Q_TILE, Q_TILE)
    nloc = Q_TILE + 2 * WINDOW
    n_keys = nloc + past
    qr = _rope(qc_ref[...], ropeq_ref[0], ropeq_ref[1], HD_C // 2) * (HD_C ** -0.5)
    ti = r0 + _row_iota((Q_TILE, n_keys))
    col = _lane_iota((Q_TILE, n_keys))
    pos = r0 - WINDOW + col
    valid = (col >= nloc) | ((jnp.abs(ti - pos) <= WINDOW) & (pos >= 0) & (pos < seq))
    ones_bd = _pair_split(jnp.ones((n_keys, LANES), F32)).astype(BF16)
    qd = _mla_queries(qa_ref, qan_ref, wqb_ref)
    q_pe = _rope(qd[:, nope_w:], ropeq_ref[2], ropeq_ref[3], ROPE_D // 2)

    def banded(ref, kh):
        return jnp.concatenate([ref[kh, 0, pl.ds(r0, nloc), :], ref[kh, 0, ctx0:, :],
                                ref[kh, 1, pl.ds(r0, nloc), :], ref[kh, 1, ctx0:, :]], axis=0)

    items = []
    for kh in range(NKV_C):
        for n in range(kh * G_C, (kh + 1) * G_C, 2):
            cols = slice(n * HD_C, (n + 2) * HD_C)
            items.append((lambda cols=cols, kh=kh: _dot_nt(qr[:, cols].astype(BF16), banded(kwin_s, kh)),
                          lambda kh=kh: jnp.concatenate([banded(vwin_s, kh), ones_bd], axis=1),
                          _pair_sinks(sink_ref, n), oc_ref, cols))
    for i in range(NH_D // 2):
        cols = slice(i * LANES, (i + 1) * LANES)
        items.append((lambda cols=cols, i=i: _dot_nt(jnp.concatenate([qd[:, cols], q_pe[:, cols]], axis=1).astype(BF16),
                                                     kext_s[i]),
                      lambda i=i: vext_s[i], None, od_ref, cols))
    _run_pairs(items, valid)


def _attn_lat(proj, caches, rope, params, seq, sub_layer):
    qc, kc, vc, qa, kva, kpe = proj
    n = qc.shape[0]
    past = caches[0].shape[2]
    nq = seq // Q_TILE
    qrow = lambda b, q: (b * nq + q, 0)
    krow = lambda b, q: (b, 0)
    kvw = NKV_C * HD_C
    in_specs = [pl.BlockSpec((Q_TILE, NH_C * HD_C), qrow), pl.BlockSpec((Q_TILE, Q_RANK), qrow),
                pl.BlockSpec((4, Q_TILE, LANES), lambda b, q: (0, q, 0)),
                pl.BlockSpec((seq, kvw), krow), pl.BlockSpec((seq, kvw), krow),
                pl.BlockSpec((seq, KV_RANK), krow), pl.BlockSpec((seq, LANES), krow)]
    in_specs += [pl.BlockSpec((None, None, past, LANES), lambda b, q: (b, sub_layer, 0, 0)) for _ in caches]
    in_specs += [pl.BlockSpec(rope.shape, lambda b, q: (0, 0, 0))]
    in_specs += [_layer_spec(p, sub_layer) for p in params]
    half = NH_C * HD_C
    win_rows = 2 * WINDOW + seq + past
    return pl.pallas_call(
        functools.partial(_attn_lat_kernel, seq=seq, past=past),
        grid=(n // seq, nq), in_specs=in_specs,
        out_specs=[pl.BlockSpec((Q_TILE, half), qrow), pl.BlockSpec((Q_TILE, half), qrow)],
        out_shape=[jax.ShapeDtypeStruct((n, half), F32), jax.ShapeDtypeStruct((n, half), F32)],
        scratch_shapes=[pltpu.VMEM((NKV_C, 2, win_rows, LANES), BF16), pltpu.VMEM((NKV_C, 2, win_rows, LANES), BF16),
                        pltpu.VMEM((NH_D // 2, 2 * (past + seq), 2 * LANES), BF16),
                        pltpu.VMEM((NH_D // 2, 2 * (past + seq), 2 * LANES), BF16)],
        compiler_params=_params("arbitrary", "arbitrary"), name="attn_lat",
    )(qc, qa, rope, kc, vc, kva, kpe, *caches, rope, *params)


def _pad_lanes(x, width=LANES):
    return jnp.pad(x, [(0, 0)] * (x.ndim - 1) + [(0, width - x.shape[-1])])


def _on_lanes(x):
    return jnp.broadcast_to(x[..., None], x.shape + (LANES,))


def _mla_query_weights(w):
    lead = w.shape[:-1]
    w4 = w.reshape(lead + (NH_D // 2, 2, NOPE_D + ROPE_D))
    nope = w4[..., :NOPE_D].reshape(lead + (NH_D * NOPE_D,))
    pe = _pad_lanes(w4[..., NOPE_D:].reshape(lead + (NH_D // 2, 2 * ROPE_D)))
    return jnp.concatenate([nope, pe.reshape(lead + (NH_D // 2 * LANES,))], axis=-1).astype(BF16)


def _mla_kv_weights(w):
    lead = w.shape[:-1]
    w3 = w.reshape(lead + (NH_D, NOPE_D + V_D))
    return jnp.concatenate([w3[..., :NOPE_D].reshape(lead + (NH_D * NOPE_D,)),
                            w3[..., NOPE_D:].reshape(lead + (NH_D * V_D,))], axis=-1).astype(BF16)


def _rope_tables(rows):
    def table(rot_dim):
        quarter = rot_dim // 4
        inv = ROPE_BASE ** (-jnp.arange(quarter, dtype=F32) / quarter)
        r = jnp.repeat(jnp.arange(rows, dtype=F32), GRID_W)
        col = jnp.tile(jnp.arange(GRID_W, dtype=F32), rows)
        ang = jnp.concatenate([r[:, None] * inv, col[:, None] * inv], axis=-1)
        reps = LANES // (rot_dim // 2)
        return jnp.tile(jnp.cos(ang), (1, reps)), jnp.tile(jnp.sin(ang), (1, reps))
    cos_c, sin_c = table(HD_C)
    cos_d, sin_d = table(ROPE_D)
    return jnp.stack([cos_c, sin_c, cos_d, sin_d])


def kernel(x_prompt, x_sample, c, state_mlstm_C, state_mlstm_n, state_mlstm_m, state_ssd, cache_gqa_k, cache_gqa_v,
           cache_mla_ckv, cache_mla_kpe, c_ctx, w_ada, b_ada, norm_g, w_up, w_down, w_in_even, conv_a_w, conv_a_b,
           conv_b_w, conv_b_b, gate_b, a_norm_w, dt_bias, a_log, d_skip, b_norm_w, w_out_even, w_in_odd, sink,
           q_a_norm, kv_a_norm, w_q_b, w_kv_b, w_out_odd):
    xp = x_prompt.reshape(BATCH * SEQ, D_MODEL)
    xs = x_sample.reshape(DEC_BATCH * DEC_SEQ, D_MODEL)
    cond = jnp.concatenate([c_ctx[None, :], c, jnp.zeros((MOD_ROWS - 1 - DEC_BATCH, D_MODEL), F32)], axis=0)
    mods = _modulations(cond, w_ada, b_ada)
    rope = _rope_tables(DEC_SEQ // GRID_W)

    w_even = w_in_even.astype(BF16)
    a_params = (conv_a_w, conv_a_b[:, None, :], _on_lanes(gate_b), a_norm_w[:, None, :])
    b_params = (conv_b_w, conv_b_b[:, None, :], _on_lanes(dt_bias.reshape(N_EVEN, 2 * NH_B)),
                _on_lanes(a_log.reshape(N_EVEN, 2 * NH_B)), jnp.repeat(d_skip, HP_B, axis=1)[:, None, :],
                b_norm_w[:, None, :])
    n0 = state_mlstm_n[..., None]
    mem_in = (jnp.concatenate([state_mlstm_C, jnp.broadcast_to(n0, n0.shape[:-1] + (LANES,))], axis=-1),
              _on_lanes(state_mlstm_m.reshape(DEC_BATCH, N_EVEN, 2 * NH_A)))
    w_odd = w_in_odd.astype(BF16)
    o_params = (_pad_lanes(sink)[:, None, :], q_a_norm[:, None, :], kv_a_norm[:, None, :],
                _mla_query_weights(w_q_b), _mla_kv_weights(w_kv_b))
    caches = (cache_gqa_k.reshape(DEC_BATCH, N_ODD, PAST_LEN, NKV_C * HD_C),
              cache_gqa_v.reshape(DEC_BATCH, N_ODD, PAST_LEN, NKV_C * HD_C),
              cache_mla_ckv, _pad_lanes(cache_mla_kpe))

    mem_state, ssd_state, odd_caches = None, None, None
    for l in range(DEPTH):
        j = l // 2
        if l % 2 == 0:
            a1p, *mem_state = _fused_mixer(_mlstm_kernel, "mlstm", xp, mods, l, norm_g, w_even, MLSTM_W_COLS, a_params,
                                           SEQ, MLSTM_PIECES, MLSTM_GATES, A_V, MLSTM_STATE_SHAPES, mem_state)
            a2p, *ssd_state = _fused_mixer(_ssd_kernel, "ssd", xp, mods, l, norm_g, w_even, w_even.shape[2], b_params,
                                           SEQ, SSD_PIECES, SSD_GATES, B_INNER, SSD_STATE_SHAPES, ssd_state)
            qk, v, o, z, xbc, g = _project(xs, mods, l, True, norm_g, w_even, EVEN_WIDTHS, EVEN_REGROUP, EVEN_GATE_COLS)
            a1s = _mlstm_latent(qk, v, o, g, a_params, DEC_SEQ, j, mem_in)
            a2s = _ssd_latent(xbc, z, g, b_params, DEC_SEQ, j, state_ssd)
            w_out = w_out_even
        else:
            a1p, a2p, *odd_caches = _attn_ctx(xp, mods, l, norm_g, w_odd, o_params, SEQ, carried=odd_caches)
            proj = _project(xs, mods, l, True, norm_g, w_odd, ODD_WIDTHS, ODD_REGROUP)
            a1s, a2s = _attn_lat(proj, caches, rope, o_params, DEC_SEQ, j)
            w_out = w_out_odd
        xp, xs = _channel((a1p, a2p, xp), (a1s, a2s, xs), mods, l, norm_g, w_out, w_up, w_down)

    new_c, new_n, new_m = mem_state
    new_ckv, new_k, new_v, new_kpe = odd_caches
    return (xp.reshape(BATCH, SEQ, D_MODEL), xs.reshape(DEC_BATCH, DEC_SEQ, D_MODEL),
            new_c, new_n, new_m[..., 0].reshape(BATCH, N_EVEN, 2, NH_A), ssd_state[0],
            new_k.reshape(BATCH, N_ODD, SEQ, NKV_C, HD_C), new_v.reshape(BATCH, N_ODD, SEQ, NKV_C, HD_C),
            new_ckv, new_kpe[..., :ROPE_D])
```

```python
import functools

import jax
import jax.numpy as jnp
from jax import lax
from jax.experimental import pallas as pl
from jax.experimental.pallas import tpu as pltpu

F32 = jnp.float32
BF16 = jnp.bfloat16

D_MODEL = 1024
BATCH = 32
SEQ = 256
DEPTH = 4
DEC_BATCH = 2
DEC_SEQ = 1024
PAST_LEN = 256
GRID_W = 64
N_EVEN = (DEPTH + 1) // 2
N_ODD = DEPTH // 2
EPS = 1e-6
CONV_K = 5
NH_A = 4
DK_A = 128
DV_A = 128
A_QK = NH_A * DK_A
A_V = NH_A * DV_A
NH_B = 8
HP_B = 64
DSTATE = 128
NG_B = 2
R_B = NH_B // NG_B
B_INNER = NH_B * HP_B
B_BC = NG_B * DSTATE
B_XBC = B_INNER + 2 * B_BC
NH_C = 8
NKV_C = 2
G_C = NH_C // NKV_C
HD_C = 64
WINDOW = 128
NH_D = 8
Q_RANK = 256
KV_RANK = 128
NOPE_D = 64
ROPE_D = 32
V_D = 64
MLA_SCALE = (NOPE_D + ROPE_D) ** -0.5
D_FF = 4 * D_MODEL
ROPE_BASE = 10000.0

LANES = 128
VMEM_LIMIT_BYTES = 56 * 1024 * 1024
ROW_TILE = 512
FF_TILE = 1024
SUB_ROWS = 256
STAGE_ROWS = 512
Q_TILE = 256
ADA_TILE = 1536
MOD_ROWS = 8
GATE_ROWS = 4 * NH_A + 2 * NH_B

EVEN_WIDTHS = (2 * A_QK, A_V, A_V, B_INNER, B_XBC)
ODD_WIDTHS = (NH_C * HD_C, NKV_C * HD_C, NKV_C * HD_C, Q_RANK, KV_RANK, LANES)
_GATES_LO = 2 * A_QK + 2 * A_V
_Z_LO = _GATES_LO + 4 * NH_A
_DT_LO = _Z_LO + B_INNER + B_XBC
EVEN_REGROUP = ((0, 0, _GATES_LO), (_GATES_LO, _Z_LO, B_INNER + B_XBC))
EVEN_GATE_COLS = ((_GATES_LO, _Z_LO), (_DT_LO, _DT_LO + 2 * NH_B))
ODD_IN = sum(ODD_WIDTHS) - LANES + ROPE_D
ODD_REGROUP = ((0, 0, ODD_IN), (ODD_IN, None, LANES - ROPE_D))

_NT = (((1,), (1,)), ((), ()))


def _params(*sem):
    return pltpu.CompilerParams(dimension_semantics=sem, vmem_limit_bytes=VMEM_LIMIT_BYTES)


def _rms(x, g):
    return x * lax.rsqrt(jnp.mean(x * x, axis=-1, keepdims=True) + EPS) * g


def _silu(x):
    return x * jax.nn.sigmoid(x)


def _softplus(x):
    return jnp.maximum(x, 0.0) + jnp.log1p(jnp.exp(-jnp.abs(x)))


def _dot(a, b):
    return jnp.dot(a, b, preferred_element_type=F32)


def _dot_nt(a, b):
    return lax.dot_general(a, b, _NT, preferred_element_type=F32)


def _layer_spec(arr, layer):
    tail = arr.shape[1:]
    zeros = (0,) * len(tail)
    return pl.BlockSpec((None,) + tail, lambda *_: (layer,) + zeros)


def _ada_kernel(c_ref, w_ref, b_ref, o_ref):
    s = _silu(c_ref[...]).astype(BF16)
    o_ref[...] = _dot(s, w_ref[...].astype(BF16)) + b_ref[...]


def _modulations(cond, w_ada, b_ada):
    out = pl.pallas_call(
        _ada_kernel,
        grid=(DEPTH, 6 * D_MODEL // ADA_TILE),
        in_specs=[pl.BlockSpec((MOD_ROWS, D_MODEL), lambda l, n: (0, 0)),
                  pl.BlockSpec((None, D_MODEL, ADA_TILE), lambda l, n: (l, 0, n)),
                  pl.BlockSpec((None, 1, ADA_TILE), lambda l, n: (l, 0, n))],
        out_specs=pl.BlockSpec((None, MOD_ROWS, ADA_TILE), lambda l, n: (l, 0, n)),
        out_shape=jax.ShapeDtypeStruct((DEPTH, MOD_ROWS, 6 * D_MODEL), F32),
        compiler_params=_params("arbitrary", "arbitrary"),
        name="ada",
    )(cond, w_ada, b_ada.reshape(DEPTH, 1, 6 * D_MODEL))
    return out.reshape(DEPTH, MOD_ROWS, 6, D_MODEL)


def _mod_spec(layer, latent):
    if latent:
        per_seq = DEC_SEQ // ROW_TILE
        return pl.BlockSpec((None, None, 6, D_MODEL), lambda i, *_: (layer, 1 + i // per_seq, 0, 0))
    return pl.BlockSpec((None, None, 6, D_MODEL), lambda i, *_: (layer, 0, 0, 0))


def _proj_kernel(x_ref, mod_ref, g_ref, w_ref, *rest, widths, regroup, gate_cols):
    n_out = len(widths) + (1 if gate_cols else 0)
    o_refs, w_s = rest[:n_out], rest[n_out]

    @pl.when(pl.program_id(0) == 0)
    def _():
        for dst, src, width in regroup:
            if src is None:
                w_s[:, dst:dst + width] = jnp.zeros((D_MODEL, width), BF16)
            else:
                w_s[:, dst:dst + width] = w_ref[:, src:src + width]
        if gate_cols:
            pieces = [w_ref[:, lo:hi].astype(F32) for lo, hi in gate_cols]
            n_gates = sum(hi - lo for lo, hi in gate_cols)
            gates = jnp.concatenate(pieces + [jnp.zeros((D_MODEL, LANES - n_gates), F32)], axis=1)
            rest[n_out + 1][...] = gates.T[0:n_gates, :].astype(BF16)

    h = _rms(x_ref[...], g_ref[0:1, :]) * (1.0 + mod_ref[1:2, :]) + mod_ref[0:1, :]
    hb = h.astype(BF16)
    off = 0
    for o_ref, wd in zip(o_refs, widths):
        o_ref[...] = _dot(hb, w_s[:, off:off + wd])
        off += wd
    if gate_cols:
        o_refs[-1][...] = _dot_nt(rest[n_out + 1][...], hb)


def _project(x, mods, layer, latent, gains, w_all, widths, regroup, gate_cols=()):
    n = x.shape[0]
    out_specs = [pl.BlockSpec((ROW_TILE, wd), lambda i: (i, 0)) for wd in widths]
    out_shape = [jax.ShapeDtypeStruct((n, wd), F32) for wd in widths]
    scratch = [pltpu.VMEM((D_MODEL, sum(widths)), BF16)]
    if gate_cols:
        n_gates = sum(hi - lo for lo, hi in gate_cols)
        out_specs.append(pl.BlockSpec((n_gates, ROW_TILE), lambda i: (0, i)))
        out_shape.append(jax.ShapeDtypeStruct((n_gates, n), F32))
        scratch.append(pltpu.VMEM((n_gates, D_MODEL), BF16))
    w_spec = pl.BlockSpec((None,) + w_all.shape[1:], lambda i: (layer // 2, 0, 0), pipeline_mode=pl.Buffered(1))
    return pl.pallas_call(
        functools.partial(_proj_kernel, widths=widths, regroup=regroup, gate_cols=gate_cols),
        grid=(n // ROW_TILE,),
        in_specs=[pl.BlockSpec((ROW_TILE, D_MODEL), lambda i: (i, 0)), _mod_spec(layer, latent),
                  _layer_spec(gains, layer), w_spec],
        out_specs=out_specs, out_shape=out_shape, scratch_shapes=scratch,
        compiler_params=_params("arbitrary"),
        name="proj",
    )(x, mods, gains, w_all)


def _weight_chunks(layer, sub_layer, wo_hbm, wu_hbm, wd_hbm, wo_s, wu_s, wd_s):
    chunks = []
    for r in range(0, D_MODEL, STAGE_ROWS):
        chunks.append((wo_hbm.at[sub_layer, pl.ds(r, STAGE_ROWS), :], wo_s.at[pl.ds(r, STAGE_ROWS), :]))
    for r in range(0, D_MODEL, STAGE_ROWS):
        for c in range(0, D_FF, D_MODEL):
            chunks.append((wu_hbm.at[layer, pl.ds(r, STAGE_ROWS), pl.ds(c, D_MODEL)],
                           wu_s.at[pl.ds(r, STAGE_ROWS), pl.ds(c, D_MODEL)]))
    for r in range(0, D_FF, STAGE_ROWS):
        chunks.append((wd_hbm.at[layer, pl.ds(r, STAGE_ROWS), :], wd_s.at[pl.ds(r, STAGE_ROWS), :]))
    return chunks


def _channel_kernel(a1p_ref, a2p_ref, xp_ref, a1s_ref, a2s_ref, xs_ref, mod_ref, g_ref, wo_hbm, wu_hbm, wd_hbm,
                    op_ref, os_ref, wo_s, wu_s, wd_s, stage, sem, *, layer, sub_layer, prompt_steps):
    step = pl.program_id(0)

    @pl.when(step == 0)
    def _():
        chunks = _weight_chunks(layer, sub_layer, wo_hbm, wu_hbm, wd_hbm, wo_s, wu_s, wd_s)
        copies = [pltpu.make_async_copy(src, stage.at[k % 2], sem.at[k % 2]) for k, (src, _) in enumerate(chunks)]
        copies[0].start()
        for k, (_, dst) in enumerate(chunks):
            if k + 1 < len(chunks):
                copies[k + 1].start()
            copies[k].wait()
            dst[...] = stage[k % 2].astype(BF16)

    def rows_block(a1_ref, a2_ref, x_ref, o_ref):
        half = a1_ref.shape[1]
        tiles = range(0, D_FF, FF_TILE)
        blocks = [slice(r0, r0 + SUB_ROWS) for r0 in range(0, ROW_TILE, SUB_ROWS)]

        def prologue(rows):
            y = (_dot(a1_ref[rows, :].astype(BF16), wo_s[0:half, :])
                 + _dot(a2_ref[rows, :].astype(BF16), wo_s[half:, :]))
            x1 = x_ref[rows, :] + mod_ref[2:3, :] * _rms(y, g_ref[1:2, :])
            h = (_rms(x1, g_ref[2:3, :]) * (1.0 + mod_ref[4:5, :]) + mod_ref[3:4, :]).astype(BF16)
            return x1, h

        def mlp_tile(h, c):
            u = jnp.square(jnp.maximum(_dot(h, wu_s[:, c:c + FF_TILE]), 0.0)).astype(BF16)
            return _dot(u, wd_s[c:c + FF_TILE, :])

        def epilogue(rows, x1, acc):
            o_ref[rows, :] = x1 + mod_ref[5:6, :] * _rms(acc, g_ref[3:4, :])

        ready = {0: prologue(blocks[0])}
        done = None
        for b, rows in enumerate(blocks):
            x1, h = ready.pop(b)
            acc = None
            for t, c in enumerate(tiles):
                part = mlp_tile(h, c)
                acc = part if acc is None else acc + part
                if t == 0 and b + 1 < len(blocks):
                    ready[b + 1] = prologue(blocks[b + 1])
                if t == 0 and done is not None:
                    epilogue(*done)
                    done = None
            done = (rows, x1, acc)
        epilogue(*done)

    @pl.when(step < prompt_steps)
    def _():
        rows_block(a1p_ref, a2p_ref, xp_ref, op_ref)

    @pl.when(step >= prompt_steps)
    def _():
        rows_block(a1s_ref, a2s_ref, xs_ref, os_ref)


def _channel(prompt, latent, mods, layer, gains, w_out, w_up, w_down):
    n_p, n_s = prompt[2].shape[0], latent[2].shape[0]
    steps_p, steps_s = n_p // ROW_TILE, n_s // ROW_TILE
    per_seq = DEC_SEQ // ROW_TILE
    row_p = lambda i: (jnp.minimum(i, steps_p - 1), 0)
    row_s = lambda i: (jnp.maximum(i - steps_p, 0), 0)
    mod_spec = pl.BlockSpec((None, None, 6, D_MODEL),
                            lambda i: (layer, jnp.where(i < steps_p, 0, 1 + (i - steps_p) // per_seq), 0, 0))
    hbm = pl.BlockSpec(memory_space=pl.ANY)
    specs = lambda arrs, row: [pl.BlockSpec((ROW_TILE, a.shape[1]), row) for a in arrs]
    return pl.pallas_call(
        functools.partial(_channel_kernel, layer=layer, sub_layer=layer // 2, prompt_steps=steps_p),
        grid=(steps_p + steps_s,),
        in_specs=specs(prompt, row_p) + specs(latent, row_s) + [mod_spec, _layer_spec(gains, layer), hbm, hbm, hbm],
        out_specs=[pl.BlockSpec((ROW_TILE, D_MODEL), row_p), pl.BlockSpec((ROW_TILE, D_MODEL), row_s)],
        out_shape=[jax.ShapeDtypeStruct((n_p, D_MODEL), F32), jax.ShapeDtypeStruct((n_s, D_MODEL), F32)],
        scratch_shapes=[pltpu.VMEM((D_MODEL, D_MODEL), BF16), pltpu.VMEM((D_MODEL, D_FF), BF16),
                        pltpu.VMEM((D_FF, D_MODEL), BF16), pltpu.VMEM((2, STAGE_ROWS, D_MODEL), F32),
                        pltpu.SemaphoreType.DMA((2,))],
        compiler_params=_params("arbitrary"),
        name="channel",
    )(*prompt, *latent, mods, gains, w_out, w_up, w_down)


def _row_iota(shape):
    return lax.broadcasted_iota(jnp.int32, shape, 0)


def _lane_iota(shape):
    return lax.broadcasted_iota(jnp.int32, shape, 1)


def _pair_lanes(shape):
    return _lane_iota(shape) < LANES // 2


def _cumsum_lanes(x, n_fwd):
    t = x.shape[1]
    si, ti = _row_iota((t, t)), _lane_iota((t, t))
    upper = jnp.where(si <= ti, 1.0, 0.0).astype(BF16)
    lower = jnp.where(si >= ti, 1.0, 0.0).astype(BF16)
    hi = x.astype(BF16)
    rest = x - hi.astype(F32)
    mid = rest.astype(BF16)
    lo = (rest - mid.astype(F32)).astype(BF16)
    pre = _dot(hi, upper) + _dot(mid, upper) + _dot(lo, upper)
    suf = _dot(hi, lower) + _dot(mid, lower) + _dot(lo, lower)
    return jnp.where(_row_iota(x.shape) < n_fwd, pre, suf)


def _cummax_lanes(x, n_fwd):
    t = x.shape[1]
    lane = _lane_iota(x.shape)
    pre, suf = x, x
    k = 1
    while k < t:
        pre = jnp.maximum(pre, jnp.where(lane >= k, pltpu.roll(pre, k, 1), -jnp.inf))
        suf = jnp.maximum(suf, jnp.where(lane < t - k, pltpu.roll(suf, t - k, 1), -jnp.inf))
        k *= 2
    return jnp.where(_row_iota(x.shape) < n_fwd, pre, suf)


def _columns(row_arrays):
    t = row_arrays[0].shape[1]
    used = sum(a.shape[0] for a in row_arrays)
    return jnp.concatenate(list(row_arrays) + [jnp.zeros((LANES - used, t), F32)], axis=0).T


def _dwconv_silu(x, w, b):
    t = x.shape[0]
    row = _row_iota(x.shape)
    acc = x * w[CONV_K // 2:CONV_K // 2 + 1, :] + b
    for j in range(CONV_K):
        d = j - CONV_K // 2
        if d == 0:
            continue
        shifted = pltpu.roll(x, (-d) % t, 0)
        valid = (row >= -d) if d < 0 else (row < t - d)
        acc = acc + jnp.where(valid, shifted, 0.0) * w[j:j + 1, :]
    return _silu(acc)


def _causal_exponent(expo, r0, k0, reverse):
    ti = r0 + _row_iota(expo.shape)
    si = k0 + _lane_iota(expo.shape)
    keep = (si >= ti) if reverse else (si <= ti)
    return jnp.where(keep, expo, -jnp.inf)


def _pair_split(x):
    first = _pair_lanes(x.shape)
    zero = jnp.zeros_like(x)
    return jnp.concatenate([jnp.where(first, x, zero), jnp.where(first, zero, x)], axis=0)


def _key_range(d, r0, seq):
    return (0, r0 + Q_TILE) if d == 0 else (r0, seq)


N_FUSED_INPUTS = 5
PROJ_CHUNK = 512

MLSTM_PIECES = ((0, 2 * A_QK), (2 * A_QK, A_V), (2 * A_QK + A_V, A_V))
MLSTM_GATES = (_GATES_LO, _Z_LO)
MLSTM_W_COLS = _GATES_LO + LANES
SSD_PIECES = ((_Z_LO + B_INNER, B_XBC), (_Z_LO, B_INNER))
SSD_GATES = (_DT_LO, _DT_LO + 2 * NH_B)
CTX_PIECES = tuple((sum(ODD_WIDTHS[:i]), wd) for i, wd in enumerate(ODD_WIDTHS[:-1])) + (
    (sum(ODD_WIDTHS[:-1]), LANES, ROPE_D),)


def _fused_project(step, fused_refs, scratch, pieces, gate_cols, mix):
    x_cur_ref, x_next_ref, mod_ref, gain_ref, w_ref = fused_refs
    w_s = scratch[0]
    n_head = 2 if gate_cols else 1
    n = len(pieces) + (1 if gate_cols else 0)
    sets = (scratch[n_head:n_head + n], scratch[n_head + n:n_head + 2 * n])

    def normed(x_ref):
        h = _rms(x_ref[...], gain_ref[0:1, :]) * (1.0 + mod_ref[1:2, :]) + mod_ref[0:1, :]
        return h.astype(BF16)

    def chunk_thunks(hb, dst):
        def gates():
            dst[0][...] = _dot_nt(scratch[1][...], hb)

        def columns(ref, c, src, width):
            def run():
                ref[:, c:c + width] = _dot(hb, w_s[:, src:src + width])
            return run

        thunks = [gates] if gate_cols else []
        off = 0
        for ref, piece in zip(dst[-len(pieces):], pieces):
            width = piece[1]
            thunks += [columns(ref, c, off + c, min(PROJ_CHUNK, width - c)) for c in range(0, width, PROJ_CHUNK)]
            off += width
        return thunks

    @pl.when(step == 0)
    def _():
        off = 0
        for piece in pieces:
            src, width = piece[0], piece[1]
            valid = piece[2] if len(piece) > 2 else width
            w_s[:, off:off + valid] = w_ref[:, src:src + valid]
            if valid < width:
                w_s[:, off + valid:off + width] = jnp.zeros((D_MODEL, width - valid), BF16)
            off += width
        if gate_cols:
            lo, hi = gate_cols
            gates = jnp.concatenate([w_ref[:, lo:hi].astype(F32), jnp.zeros((D_MODEL, LANES - (hi - lo)), F32)], axis=1)
            scratch[1][...] = gates.T[0:hi - lo, :].astype(BF16)
        for thunk in chunk_thunks(normed(x_cur_ref), sets[0]):
            thunk()

    for parity in range(2):
        @pl.when(step % 2 == parity)
        def _(parity=parity):
            cur = sets[parity]
            refs = list(cur[1:]) + [cur[0]] if gate_cols else list(cur)
            mix(*refs, chunk_thunks(normed(x_next_ref), sets[1 - parity]))


def _fused_specs(seq, nseq, layer, w_all, w_cols):
    nxt = lambda s: (jnp.minimum(s + 1, nseq - 1), 0)
    return [pl.BlockSpec((seq, D_MODEL), lambda s: (0, 0)), pl.BlockSpec((seq, D_MODEL), nxt),
            pl.BlockSpec((None, None, 6, D_MODEL), lambda s: (layer, 0, 0, 0)),
            pl.BlockSpec((None, 4, D_MODEL), lambda s: (layer, 0, 0)),
            pl.BlockSpec((None, D_MODEL, w_cols), lambda s: (layer // 2, 0, 0), pipeline_mode=pl.Buffered(1))]


def _fused_scratch(seq, pieces, n_gates):
    head = [pltpu.VMEM((D_MODEL, sum(p[1] for p in pieces)), BF16)]
    one_set = [pltpu.VMEM((seq, p[1]), F32) for p in pieces]
    if n_gates:
        head.append(pltpu.VMEM((n_gates, D_MODEL), BF16))
        one_set = [pltpu.VMEM((n_gates, seq), F32)] + one_set
    return head + one_set + one_set


def _run_share(side_work, stages_left):
    for _ in range(-(-len(side_work) // stages_left)):
        side_work.pop(0)()


def _mlstm_kernel(*refs, seq, has_state, emit_state, n_carried, slot, fused):
    assert not (has_state and emit_state)
    it = iter(refs)
    if fused:
        fused_refs = [next(it) for _ in range(N_FUSED_INPUTS)]
    else:
        qk_ref, v_ref, o_ref, g_ref = (next(it) for _ in range(4))
    cw_ref, cb_ref, gb_ref, anw_ref = (next(it) for _ in range(4))
    if has_state:
        c0_ref, m0_ref = next(it), next(it)
    for _ in range(n_carried):
        next(it)
    ha_ref = next(it)
    if emit_state:
        cn_ref, nn_ref, mn_ref = next(it), next(it), next(it)
        if n_carried == 0:
            for other in range(N_EVEN):
                if other != slot:
                    cn_ref[other] = jnp.zeros(cn_ref.shape[1:], F32)
                    nn_ref[other] = jnp.zeros(nn_ref.shape[1:], F32)
                    mn_ref[other] = jnp.zeros(mn_ref.shape[1:], F32)
            cn_ref, nn_ref, mn_ref = cn_ref.at[slot], nn_ref.at[slot], mn_ref.at[slot]

    def mix(qk_ref, v_ref, o_ref, g_ref, side_work=()):
        side_work = list(side_work)
        n_ch = 2 * NH_A
        log_i = g_ref[0:n_ch, :] + gb_ref[0:n_ch, 0:1]
        f_pre = g_ref[n_ch:2 * n_ch, :] + gb_ref[n_ch:2 * n_ch, 0:1]
        log_f = jnp.minimum(f_pre, 0.0) - jnp.log1p(jnp.exp(-jnp.abs(f_pre)))
        b = _cumsum_lanes(log_f, NH_A)
        a = log_i - b
        m_run = _cummax_lanes(a, NH_A)
        if has_state:
            m0 = m0_ref[:, 0:1]
            m_run = jnp.maximum(m_run, m0)
        else:
            m_run = jnp.maximum(m_run, 0.0)
        by_time = [m_run, jnp.exp(-(b + m_run))]
        if has_state:
            by_time.append(jnp.exp(m0 - m_run))
        if emit_state:
            fwd = _row_iota((n_ch, 1)) < NH_A
            b_last = jnp.where(fwd, b[:, seq - 1:seq], b[:, 0:1])
            m_last = jnp.where(fwd, m_run[:, seq - 1:seq], m_run[:, 0:1])
            mn_ref[...] = jnp.broadcast_to(b_last + m_last, (n_ch, LANES))
            by_time.append(jnp.exp(a - m_last))
        cols = _columns(by_time)

        ones = jnp.ones((seq, LANES), F32)
        for h in range(NH_A):
            _run_share(side_work, NH_A - h)
            cq = slice(h * DK_A, (h + 1) * DK_A)
            ck = slice(A_QK + h * DK_A, A_QK + (h + 1) * DK_A)
            cv = slice(h * DV_A, (h + 1) * DV_A)
            q = _dwconv_silu(qk_ref[:, cq], cw_ref[:, cq], cb_ref[:, cq])
            k = _dwconv_silu(qk_ref[:, ck], cw_ref[:, ck], cb_ref[:, ck]) * (DK_A ** -0.5)
            qb = q.astype(BF16)
            kb = k.astype(BF16)
            vh = v_ref[:, cv]
            vaug = jnp.concatenate([vh, ones], axis=1).astype(BF16)
            for r0 in range(0, seq, Q_TILE):
                rows = slice(r0, r0 + Q_TILE)
                s = _dot_nt(qb[rows], kb)
                hsum = None
                for d in range(2):
                    c = d * NH_A + h
                    k0, k1 = _key_range(d, r0, seq)
                    expo = _causal_exponent(a[c:c + 1, k0:k1] - cols[rows, c:c + 1], r0, k0, d == 1)
                    p = (s[:, k0:k1] * jnp.exp(expo)).astype(BF16)
                    acc = _dot(p, vaug[k0:k1])
                    if has_state:
                        acc = acc + (cols[rows, 2 * n_ch + c:2 * n_ch + c + 1]
                                     * _dot(qb[rows], c0_ref[d, h].astype(BF16)))
                    hd = acc[:, 0:DV_A] / jnp.maximum(jnp.abs(acc[:, DV_A:]), cols[rows, n_ch + c:n_ch + c + 1])
                    hsum = hd if hsum is None else hsum + hd
                og = jax.nn.sigmoid(o_ref[rows, cv]) * hsum
                ha_ref[rows, cv] = _rms(og, anw_ref[:, cv])
            if emit_state:
                for d in range(2):
                    c = d * NH_A + h
                    kw = k * cols[:, 2 * n_ch + c:2 * n_ch + c + 1]
                    cn_ref[d, h] = _dot(kw.T.astype(BF16), vh.astype(BF16))
                    nn_ref[d, h:h + 1, :] = jnp.sum(kw, axis=0, keepdims=True)

    if fused:
        _fused_project(pl.program_id(0), fused_refs, list(it), MLSTM_PIECES, MLSTM_GATES, mix)
    else:
        mix(qk_ref, v_ref, o_ref, g_ref)


def _state_out_specs(shapes, nseq, sub_layer, carried):
    out_specs, out_shape = [], []
    for shp in shapes:
        zeros = (0,) * len(shp)
        if carried is None:
            out_specs.append(pl.BlockSpec((None, N_EVEN) + shp, lambda s, z=zeros: (s, 0) + z))
        else:
            out_specs.append(pl.BlockSpec((None, None) + shp, lambda s, z=zeros: (s, sub_layer) + z))
        out_shape.append(jax.ShapeDtypeStruct((nseq, N_EVEN) + shp, F32))
    return out_specs, out_shape


MLSTM_STATE_SHAPES = ((2, NH_A, DK_A, DV_A), (2, NH_A, DK_A), (2 * NH_A, LANES))
SSD_STATE_SHAPES = ((2, NH_B, HP_B, DSTATE),)


def _fused_mixer(kernel_fn, name, x, mods, layer, gains, w_all, w_cols, params, seq, pieces, gate_cols,
                 out_width, state_shapes, carried):
    n = x.shape[0]
    nseq = n // seq
    sub_layer = layer // 2
    in_specs = _fused_specs(seq, nseq, layer, w_all, w_cols) + [_layer_spec(p, sub_layer) for p in params]
    args = [x, x, mods, gains, w_all, *params]
    state_specs, state_shape = _state_out_specs(state_shapes, nseq, sub_layer, carried)
    aliases = {}
    if carried is not None:
        aliases = {len(args) + i: 1 + i for i in range(len(state_shapes))}
        in_specs += [pl.BlockSpec(memory_space=pl.ANY)] * len(state_shapes)
        args += list(carried)
    return pl.pallas_call(
        functools.partial(kernel_fn, seq=seq, has_state=False, emit_state=True, n_carried=len(aliases),
                          slot=sub_layer, fused=True),
        grid=(nseq,), in_specs=in_specs,
        out_specs=[pl.BlockSpec((seq, out_width), lambda s: (s, 0))] + state_specs,
        out_shape=[jax.ShapeDtypeStruct((n, out_width), F32)] + state_shape,
        scratch_shapes=_fused_scratch(seq, pieces, gate_cols[1] - gate_cols[0]),
        input_output_aliases=aliases, compiler_params=_params("arbitrary"), name=name,
    )(*args)


def _mlstm_latent(qk, v, o, gates, params, seq, sub_layer, state):
    n = qk.shape[0]
    row = lambda s: (s, 0)
    c0_aug, m0 = state
    in_specs = [pl.BlockSpec((seq, 2 * A_QK), row), pl.BlockSpec((seq, A_V), row), pl.BlockSpec((seq, A_V), row),
                pl.BlockSpec((GATE_ROWS, seq), lambda s: (0, s))]
    in_specs += [_layer_spec(p, sub_layer) for p in params]
    in_specs += [pl.BlockSpec((None, None) + c0_aug.shape[2:], lambda s: (s, sub_layer, 0, 0, 0, 0)),
                 pl.BlockSpec((None, None) + m0.shape[2:], lambda s: (s, sub_layer, 0, 0))]
    return pl.pallas_call(
        functools.partial(_mlstm_kernel, seq=seq, has_state=True, emit_state=False, n_carried=0, slot=sub_layer,
                          fused=False),
        grid=(n // seq,), in_specs=in_specs, out_specs=pl.BlockSpec((seq, A_V), row),
        out_shape=jax.ShapeDtypeStruct((n, A_V), F32),
        compiler_params=_params("arbitrary"), name="mlstm",
    )(qk, v, o, gates, *params, c0_aug, m0)


def _ssd_kernel(*refs, seq, has_state, emit_state, n_carried, slot, fused):
    assert not (has_state and emit_state)
    it = iter(refs)
    if fused:
        fused_refs = [next(it) for _ in range(N_FUSED_INPUTS)]
    else:
        xbc_ref, z_ref, g_ref = (next(it) for _ in range(3))
    cw_ref, cb_ref, dtb_ref, alog_ref, dsk_ref, bnw_ref = (next(it) for _ in range(6))
    if has_state:
        s0_ref = next(it)
    for _ in range(n_carried):
        next(it)
    yb_ref = next(it)
    if emit_state:
        sn_ref = next(it)
        if n_carried == 0:
            for other in range(N_EVEN):
                if other != slot:
                    sn_ref[other] = jnp.zeros(sn_ref.shape[1:], F32)
            sn_ref = sn_ref.at[slot]

    gate_row0 = 0 if fused else 4 * NH_A

    def mix(xbc_ref, z_ref, g_ref, side_work=()):
        side_work = list(side_work)
        n_ch = 2 * NH_B
        dt = _softplus(g_ref[gate_row0:gate_row0 + n_ch, :] + dtb_ref[:, 0:1])
        acum = _cumsum_lanes(dt * (-jnp.exp(alog_ref[:, 0:1])), NH_B)
        key_shift = acum - jnp.log(dt)
        by_time = [acum]
        if has_state:
            by_time.append(jnp.exp(acum))
        if emit_state:
            fwd = _row_iota((n_ch, 1)) < NH_B
            a_last = jnp.where(fwd, acum[:, seq - 1:seq], acum[:, 0:1])
            by_time.append(jnp.exp(a_last - acum) * dt)
        cols = _columns(by_time)

        gw = R_B * HP_B
        first = _pair_lanes((seq, LANES))
        for g in range(NG_B):
            _run_share(side_work, NG_B - g)
            cx = slice(g * gw, (g + 1) * gw)
            cb_ = slice(B_INNER + g * DSTATE, B_INNER + (g + 1) * DSTATE)
            cc = slice(B_INNER + B_BC + g * DSTATE, B_INNER + B_BC + (g + 1) * DSTATE)
            xg = _dwconv_silu(xbc_ref[:, cx], cw_ref[:, cx], cb_ref[:, cx])
            bg = _dwconv_silu(xbc_ref[:, cb_], cw_ref[:, cb_], cb_ref[:, cb_]).astype(BF16)
            cg = _dwconv_silu(xbc_ref[:, cc], cw_ref[:, cc], cb_ref[:, cc]).astype(BF16)
            xbd = [_pair_split(xg[:, p * LANES:(p + 1) * LANES]).astype(BF16) for p in range(R_B // 2)]
            for r0 in range(0, seq, Q_TILE):
                rows = slice(r0, r0 + Q_TILE)
                cb_scores = _dot_nt(cg[rows], bg)
                ys = []
                for p in range(R_B // 2):
                    weights, inputs = [], []
                    for d in range(2):
                        k0, k1 = _key_range(d, r0, seq)
                        for i in range(2):
                            c = d * NH_B + g * R_B + 2 * p + i
                            expo = _causal_exponent(cols[rows, c:c + 1] - key_shift[c:c + 1, k0:k1], r0, k0, d == 1)
                            weights.append((cb_scores[:, k0:k1] * jnp.exp(expo)).astype(BF16))
                            inputs.append(xbd[p][i * seq + k0:i * seq + k1])
                    yp = _dot(jnp.concatenate(weights, axis=1), jnp.concatenate(inputs, axis=0))
                    if has_state:
                        h0 = g * R_B + 2 * p
                        for d in range(2):
                            c = d * NH_B + h0
                            carry = jnp.where(_pair_lanes((Q_TILE, LANES)), cols[rows, n_ch + c:n_ch + c + 1],
                                              cols[rows, n_ch + c + 1:n_ch + c + 2])
                            s0_pair = s0_ref[d, h0:h0 + 2].reshape(2 * HP_B, DSTATE).astype(BF16)
                            yp = yp + carry * _dot_nt(cg[rows], s0_pair)
                    ys.append(yp)
                y = jnp.concatenate(ys, axis=1) + dsk_ref[:, cx] * xg[rows]
                y = y * _silu(z_ref[rows, cx])
                yb_ref[rows, cx] = _rms(y, bnw_ref[:, cx])
            if emit_state:
                for d in range(2):
                    c0 = n_ch + d * NH_B + g * R_B
                    spread = jnp.concatenate([jnp.where(first, cols[:, c0 + 2 * p:c0 + 2 * p + 1],
                                                        cols[:, c0 + 2 * p + 1:c0 + 2 * p + 2])
                                              for p in range(R_B // 2)], axis=1)
                    sn = _dot((xg * spread).T.astype(BF16), bg)
                    for r in range(R_B):
                        sn_ref[d, g * R_B + r] = sn[r * HP_B:(r + 1) * HP_B, :]

    if fused:
        _fused_project(pl.program_id(0), fused_refs, list(it), SSD_PIECES, SSD_GATES, mix)
    else:
        mix(xbc_ref, z_ref, g_ref)


def _ssd_latent(xbc, z, gates, params, seq, sub_layer, state):
    n = xbc.shape[0]
    row = lambda s: (s, 0)
    in_specs = [pl.BlockSpec((seq, B_XBC), row), pl.BlockSpec((seq, B_INNER), row),
                pl.BlockSpec((GATE_ROWS, seq), lambda s: (0, s))]
    in_specs += [_layer_spec(p, sub_layer) for p in params]
    in_specs += [pl.BlockSpec((None, None, 2, NH_B, HP_B, DSTATE), lambda s: (s, sub_layer, 0, 0, 0, 0))]
    return pl.pallas_call(
        functools.partial(_ssd_kernel, seq=seq, has_state=True, emit_state=False, n_carried=0, slot=sub_layer,
                          fused=False),
        grid=(n // seq,), in_specs=in_specs, out_specs=pl.BlockSpec((seq, B_INNER), row),
        out_shape=jax.ShapeDtypeStruct((n, B_INNER), F32),
        compiler_params=_params("arbitrary"), name="ssd",
    )(xbc, z, gates, *params, state)


def _shared_split(x, x_swapped, kh):
    first = _pair_lanes(x.shape)
    zero = jnp.zeros_like(x)
    top, bottom = (x, x_swapped) if kh == 0 else (x_swapped, x)
    return jnp.concatenate([jnp.where(first, top, zero), jnp.where(first, zero, bottom)], axis=0)


def _pair_probs(s, sinks=None, valid=None):
    n_keys = s.shape[1] // 2
    probs, maxes = [], []
    for i in range(2):
        si = s[:, i * n_keys:(i + 1) * n_keys]
        if valid is not None:
            si = jnp.where(valid, si, -jnp.inf)
        m = jnp.max(si, axis=1, keepdims=True)
        if sinks is not None:
            m = jnp.maximum(m, sinks[i])
        probs.append(jnp.exp(si - m))
        maxes.append(m)
    return jnp.concatenate(probs, axis=1).astype(BF16), maxes


def _pair_output(p, maxes, vbd, sinks=None):
    o = _dot(p, vbd)
    den = o[:, LANES:]
    if sinks is not None:
        den = den + jnp.where(_pair_lanes(den.shape), jnp.exp(sinks[0] - maxes[0]), jnp.exp(sinks[1] - maxes[1]))
    return o[:, :LANES] / den


def _run_pairs(items, valid=None, side_work=()):
    side_work = list(side_work)
    s_next = items[0][0]()
    for idx, (_, values, sinks, out_ref, cols) in enumerate(items):
        _run_share(side_work, len(items) - idx)
        s_cur = s_next
        if idx + 1 < len(items):
            s_next = items[idx + 1][0]()
        p, maxes = _pair_probs(s_cur, sinks, valid if sinks is not None else None)
        out_ref[:, cols] = _pair_output(p, maxes, values(), sinks)


def _pair_sinks(sink_ref, n):
    return sink_ref[0:1, n:n + 1], sink_ref[0:1, n + 1:n + 2]


def _mla_queries(qa_ref, qan_ref, wqb_ref):
    return _dot(_rms(qa_ref[...], qan_ref[...]).astype(BF16), wqb_ref[...]) * MLA_SCALE


def _attn_ctx_kernel(*refs, seq, n_carried, slot, fused):
    it = iter(refs)
    if fused:
        fused_refs = [next(it) for _ in range(N_FUSED_INPUTS)]
    else:
        proj_refs = [next(it) for _ in range(len(ODD_WIDTHS))]
    sink_ref, qan_ref, kvn_ref, wqb_ref, wkvb_ref = (next(it) for _ in range(5))
    for _ in range(n_carried):
        next(it)
    oc_ref, od_ref = next(it), next(it)
    caches = [next(it) for _ in range(4 if fused else 1)]
    if n_carried == 0:
        for other in range(N_ODD):
            if other != slot:
                for ref in caches:
                    ref[other] = jnp.zeros(ref.shape[1:], F32)
        caches = [ref.at[slot] for ref in caches]
    ckv_ref = caches[0]
    if fused:
        new_k_ref, new_v_ref, new_kpe_ref = caches[1:]

    def mix(qc_ref, kc_ref, vc_ref, qa_ref, kva_ref, kpe_ref, side_work=()):
        if fused:
            new_k_ref[...] = kc_ref[...]
            new_v_ref[...] = vc_ref[...]
            new_kpe_ref[...] = kpe_ref[...]
        ones_bd = _pair_split(jnp.ones((seq, LANES), F32))
        kc, vc = kc_ref[...], vc_ref[...]
        kc_sw, vc_sw = pltpu.roll(kc, HD_C, 1), pltpu.roll(vc, HD_C, 1)
        qd = _mla_queries(qa_ref, qan_ref, wqb_ref)
        ckv = _rms(kva_ref[...], kvn_ref[...])
        ckv_ref[...] = ckv
        kv = _dot(ckv.astype(BF16), wkvb_ref[...])
        kpe = kpe_ref[...]
        kpe_bd = jnp.concatenate([kpe, pltpu.roll(kpe, ROPE_D, 1)], axis=0)
        nope_w = NH_D * NOPE_D
        items = []
        for kh in range(NKV_C):
            for n in range(kh * G_C, (kh + 1) * G_C, 2):
                cols = slice(n * HD_C, (n + 2) * HD_C)
                items.append((lambda cols=cols, kh=kh: _dot_nt((qc_ref[:, cols] * (HD_C ** -0.5)).astype(BF16),
                                                               _shared_split(kc, kc_sw, kh).astype(BF16)),
                              lambda kh=kh: jnp.concatenate([_shared_split(vc, vc_sw, kh), ones_bd], axis=1).astype(BF16),
                              _pair_sinks(sink_ref, n), oc_ref, cols))
        for i in range(NH_D // 2):
            cols = slice(i * LANES, (i + 1) * LANES)
            vcols = slice(nope_w + i * LANES, nope_w + (i + 1) * LANES)
            items.append((lambda cols=cols, vcols=vcols: _dot_nt(
                              jnp.concatenate([qd[:, cols], qd[:, vcols]], axis=1).astype(BF16),
                              jnp.concatenate([_pair_split(kv[:, cols]), kpe_bd], axis=1).astype(BF16)),
                          lambda vcols=vcols: jnp.concatenate([_pair_split(kv[:, vcols]), ones_bd], axis=1).astype(BF16),
                          None, od_ref, cols))
        _run_pairs(items, side_work=side_work)

    if fused:
        _fused_project(pl.program_id(0), fused_refs, list(it), CTX_PIECES, None, mix)
    else:
        mix(*proj_refs)


def _attn_ctx(x, mods, layer, gains, w_all, params, seq, carried=None):
    n = x.shape[0]
    nseq = n // seq
    sub_layer = layer // 2
    row = lambda s: (s, 0)
    in_specs = _fused_specs(seq, nseq, layer, w_all, w_all.shape[2]) + [_layer_spec(p, sub_layer) for p in params]
    args = [x, x, mods, gains, w_all, *params]
    half = NH_C * HD_C
    cache_widths = (KV_RANK, NKV_C * HD_C, NKV_C * HD_C, LANES)
    aliases = {}
    if carried is None:
        cache_specs = [pl.BlockSpec((None, N_ODD, seq, w), lambda s: (s, 0, 0, 0)) for w in cache_widths]
    else:
        cache_specs = [pl.BlockSpec((None, None, seq, w), lambda s: (s, sub_layer, 0, 0)) for w in cache_widths]
        aliases = {len(args) + i: 2 + i for i in range(len(cache_widths))}
        in_specs += [pl.BlockSpec(memory_space=pl.ANY)] * len(cache_widths)
        args += list(carried)
    return pl.pallas_call(
        functools.partial(_attn_ctx_kernel, seq=seq, n_carried=len(aliases), slot=sub_layer, fused=True),
        grid=(nseq,), in_specs=in_specs,
        out_specs=[pl.BlockSpec((seq, half), row), pl.BlockSpec((seq, half), row)] + cache_specs,
        out_shape=[jax.ShapeDtypeStruct((n, half), F32), jax.ShapeDtypeStruct((n, half), F32)]
                  + [jax.ShapeDtypeStruct((nseq, N_ODD, seq, w), F32) for w in cache_widths],
        scratch_shapes=_fused_scratch(seq, CTX_PIECES, 0),
        input_output_aliases=aliases, compiler_params=_params("arbitrary"), name="attn_ctx",
    )(*args)


def _rope(x, cos, sin, half):
    parts = []
    lane = _lane_iota((x.shape[0], LANES))
    first = (lane & (2 * half - 1)) < half
    for i in range(x.shape[1] // LANES):
        xi = x[:, i * LANES:(i + 1) * LANES]
        partner = jnp.where(first, -pltpu.roll(xi, LANES - half, 1), pltpu.roll(xi, half, 1))
        parts.append(xi * cos + partner * sin)
    return parts[0] if len(parts) == 1 else jnp.concatenate(parts, axis=1)


def _attn_lat_kernel(qc_ref, qa_ref, ropeq_ref, kc_ref, vc_ref, kva_ref, kpe_ref, kctx_ref, vctx_ref, ckvctx_ref,
                     kpectx_ref, rope_ref, sink_ref, qan_ref, kvn_ref, wqb_ref, wkvb_ref, oc_ref, od_ref,
                     kwin_s, vwin_s, kext_s, vext_s, *, seq, past):
    qi = pl.program_id(1)
    nope_w = NH_D * NOPE_D
    n_all = past + seq
    ctx0 = 2 * WINDOW + seq

    @pl.when(qi == 0)
    def _():
        zeros = jnp.zeros((WINDOW, LANES), BF16)
        for ref, lat, ctx in ((kwin_s, _rope(kc_ref[...], rope_ref[0], rope_ref[1], HD_C // 2), kctx_ref[...]),
                              (vwin_s, vc_ref[...], vctx_ref[...])):
            lat_sw, ctx_sw = pltpu.roll(lat, HD_C, 1), pltpu.roll(ctx, HD_C, 1)
            for kh in range(NKV_C):
                lat_bd = _shared_split(lat, lat_sw, kh).astype(BF16)
                ctx_bd = _shared_split(ctx, ctx_sw, kh).astype(BF16)
                for i in range(2):
                    ref[kh, i, 0:WINDOW, :] = zeros
                    ref[kh, i, WINDOW:WINDOW + seq, :] = lat_bd[i * seq:(i + 1) * seq]
                    ref[kh, i, WINDOW + seq:ctx0, :] = zeros
                    ref[kh, i, ctx0:, :] = ctx_bd[i * past:(i + 1) * past]
        ckv = _rms(kva_ref[...], kvn_ref[...])
        kv = jnp.concatenate([_dot(ckvctx_ref[...].astype(BF16), wkvb_ref[...]),
                              _dot(ckv.astype(BF16), wkvb_ref[...])], axis=0)
        kpe = jnp.concatenate([kpectx_ref[...], _rope(kpe_ref[...], rope_ref[2], rope_ref[3], ROPE_D // 2)], axis=0)
        kpe_bd = jnp.concatenate([kpe, pltpu.roll(kpe, ROPE_D, 1)], axis=0).astype(BF16)
        ones_bd = _pair_split(jnp.ones((n_all, LANES), F32)).astype(BF16)
        for i in range(NH_D // 2):
            kext_s[i, :, 0:LANES] = _pair_split(kv[:, i * LANES:(i + 1) * LANES]).astype(BF16)
            kext_s[i, :, LANES:] = kpe_bd
            vext_s[i, :, 0:LANES] = _pair_split(kv[:, nope_w + i * LANES:nope_w + (i + 1) * LANES]).astype(BF16)
            vext_s[i, :, LANES:] = ones_bd

    r0 = pl.multiple_of(qi * Q_TILE, Q_TILE)
    nloc = Q_TILE + 2 * WINDOW
    n_keys = nloc + past
    qr = _rope(qc_ref[...], ropeq_ref[0], ropeq_ref[1], HD_C // 2) * (HD_C ** -0.5)
    ti = r0 + _row_iota((Q_TILE, n_keys))
    col = _lane_iota((Q_TILE, n_keys))
    pos = r0 - WINDOW + col
    valid = (col >= nloc) | ((jnp.abs(ti - pos) <= WINDOW) & (pos >= 0) & (pos < seq))
    ones_bd = _pair_split(jnp.ones((n_keys, LANES), F32)).astype(BF16)
    qd = _mla_queries(qa_ref, qan_ref, wqb_ref)
    q_pe = _rope(qd[:, nope_w:], ropeq_ref[2], ropeq_ref[3], ROPE_D // 2)

    def banded(ref, kh):
        return jnp.concatenate([ref[kh, 0, pl.ds(r0, nloc), :], ref[kh, 0, ctx0:, :],
                                ref[kh, 1, pl.ds(r0, nloc), :], ref[kh, 1, ctx0:, :]], axis=0)

    items = []
    for kh in range(NKV_C):
        for n in range(kh * G_C, (kh + 1) * G_C, 2):
            cols = slice(n * HD_C, (n + 2) * HD_C)
            items.append((lambda cols=cols, kh=kh: _dot_nt(qr[:, cols].astype(BF16), banded(kwin_s, kh)),
                          lambda kh=kh: jnp.concatenate([banded(vwin_s, kh), ones_bd], axis=1),
                          _pair_sinks(sink_ref, n), oc_ref, cols))
    for i in range(NH_D // 2):
        cols = slice(i * LANES, (i + 1) * LANES)
        items.append((lambda cols=cols, i=i: _dot_nt(jnp.concatenate([qd[:, cols], q_pe[:, cols]], axis=1).astype(BF16),
                                                     kext_s[i]),
                      lambda i=i: vext_s[i], None, od_ref, cols))
    _run_pairs(items, valid)


def _attn_lat(proj, caches, rope, params, seq, sub_layer):
    qc, kc, vc, qa, kva, kpe = proj
    n = qc.shape[0]
    past = caches[0].shape[2]
    nq = seq // Q_TILE
    qrow = lambda b, q: (b * nq + q, 0)
    krow = lambda b, q: (b, 0)
    kvw = NKV_C * HD_C
    in_specs = [pl.BlockSpec((Q_TILE, NH_C * HD_C), qrow), pl.BlockSpec((Q_TILE, Q_RANK), qrow),
                pl.BlockSpec((4, Q_TILE, LANES), lambda b, q: (0, q, 0)),
                pl.BlockSpec((seq, kvw), krow), pl.BlockSpec((seq, kvw), krow),
                pl.BlockSpec((seq, KV_RANK), krow), pl.BlockSpec((seq, LANES), krow)]
    in_specs += [pl.BlockSpec((None, None, past, LANES), lambda b, q: (b, sub_layer, 0, 0)) for _ in caches]
    in_specs += [pl.BlockSpec(rope.shape, lambda b, q: (0, 0, 0))]
    in_specs += [_layer_spec(p, sub_layer) for p in params]
    half = NH_C * HD_C
    win_rows = 2 * WINDOW + seq + past
    return pl.pallas_call(
        functools.partial(_attn_lat_kernel, seq=seq, past=past),
        grid=(n // seq, nq), in_specs=in_specs,
        out_specs=[pl.BlockSpec((Q_TILE, half), qrow), pl.BlockSpec((Q_TILE, half), qrow)],
        out_shape=[jax.ShapeDtypeStruct((n, half), F32), jax.ShapeDtypeStruct((n, half), F32)],
        scratch_shapes=[pltpu.VMEM((NKV_C, 2, win_rows, LANES), BF16), pltpu.VMEM((NKV_C, 2, win_rows, LANES), BF16),
                        pltpu.VMEM((NH_D // 2, 2 * (past + seq), 2 * LANES), BF16),
                        pltpu.VMEM((NH_D // 2, 2 * (past + seq), 2 * LANES), BF16)],
        compiler_params=_params("arbitrary", "arbitrary"), name="attn_lat",
    )(qc, qa, rope, kc, vc, kva, kpe, *caches, rope, *params)


def _pad_lanes(x, width=LANES):
    return jnp.pad(x, [(0, 0)] * (x.ndim - 1) + [(0, width - x.shape[-1])])


def _on_lanes(x):
    return x[..., None]


def _mla_query_weights(w):
    lead = w.shape[:-1]
    w4 = w.reshape(lead + (NH_D // 2, 2, NOPE_D + ROPE_D))
    nope = w4[..., :NOPE_D].reshape(lead + (NH_D * NOPE_D,))
    pe = _pad_lanes(w4[..., NOPE_D:].reshape(lead + (NH_D // 2, 2 * ROPE_D)))
    return jnp.concatenate([nope, pe.reshape(lead + (NH_D // 2 * LANES,))], axis=-1).astype(BF16)


def _mla_kv_weights(w):
    lead = w.shape[:-1]
    w3 = w.reshape(lead + (NH_D, NOPE_D + V_D))
    return jnp.concatenate([w3[..., :NOPE_D].reshape(lead + (NH_D * NOPE_D,)),
                            w3[..., NOPE_D:].reshape(lead + (NH_D * V_D,))], axis=-1).astype(BF16)


def _rope_tables(rows):
    def table(rot_dim):
        quarter = rot_dim // 4
        inv = ROPE_BASE ** (-jnp.arange(quarter, dtype=F32) / quarter)
        r = jnp.repeat(jnp.arange(rows, dtype=F32), GRID_W)
        col = jnp.tile(jnp.arange(GRID_W, dtype=F32), rows)
        ang = jnp.concatenate([r[:, None] * inv, col[:, None] * inv], axis=-1)
        reps = LANES // (rot_dim // 2)
        return jnp.tile(jnp.cos(ang), (1, reps)), jnp.tile(jnp.sin(ang), (1, reps))
    cos_c, sin_c = table(HD_C)
    cos_d, sin_d = table(ROPE_D)
    return jnp.stack([cos_c, sin_c, cos_d, sin_d])


def kernel(x_prompt, x_sample, c, state_mlstm_C, state_mlstm_n, state_mlstm_m, state_ssd, cache_gqa_k, cache_gqa_v,
           cache_mla_ckv, cache_mla_kpe, c_ctx, w_ada, b_ada, norm_g, w_up, w_down, w_in_even, conv_a_w, conv_a_b,
           conv_b_w, conv_b_b, gate_b, a_norm_w, dt_bias, a_log, d_skip, b_norm_w, w_out_even, w_in_odd, sink,
           q_a_norm, kv_a_norm, w_q_b, w_kv_b, w_out_odd):
    xp = x_prompt.reshape(BATCH * SEQ, D_MODEL)
    xs = x_sample.reshape(DEC_BATCH * DEC_SEQ, D_MODEL)
    cond = jnp.concatenate([c_ctx[None, :], c, jnp.zeros((MOD_ROWS - 1 - DEC_BATCH, D_MODEL), F32)], axis=0)
    mods = _modulations(cond, w_ada, b_ada)
    rope = _rope_tables(DEC_SEQ // GRID_W)

    w_even = w_in_even.astype(BF16)
    a_params = (conv_a_w, conv_a_b[:, None, :], _on_lanes(gate_b), a_norm_w[:, None, :])
    b_params = (conv_b_w, conv_b_b[:, None, :], _on_lanes(dt_bias.reshape(N_EVEN, 2 * NH_B)),
                _on_lanes(a_log.reshape(N_EVEN, 2 * NH_B)), jnp.repeat(d_skip, HP_B, axis=1)[:, None, :],
                b_norm_w[:, None, :])
    n0 = state_mlstm_n[..., None]
    mem_in = (jnp.concatenate([state_mlstm_C, jnp.broadcast_to(n0, n0.shape[:-1] + (LANES,))], axis=-1),
              _on_lanes(state_mlstm_m.reshape(DEC_BATCH, N_EVEN, 2 * NH_A)))
    w_odd = w_in_odd.astype(BF16)
    o_params = (_pad_lanes(sink)[:, None, :], q_a_norm[:, None, :], kv_a_norm[:, None, :],
                _mla_query_weights(w_q_b), _mla_kv_weights(w_kv_b))
    caches = (cache_gqa_k.reshape(DEC_BATCH, N_ODD, PAST_LEN, NKV_C * HD_C),
              cache_gqa_v.reshape(DEC_BATCH, N_ODD, PAST_LEN, NKV_C * HD_C),
              cache_mla_ckv, _pad_lanes(cache_mla_kpe))

    mem_state, ssd_state, odd_caches = None, None, None
    for l in range(DEPTH):
        j = l // 2
        if l % 2 == 0:
            a1p, *mem_state = _fused_mixer(_mlstm_kernel, "mlstm", xp, mods, l, norm_g, w_even, MLSTM_W_COLS, a_params,
                                           SEQ, MLSTM_PIECES, MLSTM_GATES, A_V, MLSTM_STATE_SHAPES, mem_state)
            a2p, *ssd_state = _fused_mixer(_ssd_kernel, "ssd", xp, mods, l, norm_g, w_even, w_even.shape[2], b_params,
                                           SEQ, SSD_PIECES, SSD_GATES, B_INNER, SSD_STATE_SHAPES, ssd_state)
            qk, v, o, z, xbc, g = _project(xs, mods, l, True, norm_g, w_even, EVEN_WIDTHS, EVEN_REGROUP, EVEN_GATE_COLS)
            a1s = _mlstm_latent(qk, v, o, g, a_params, DEC_SEQ, j, mem_in)
            a2s = _ssd_latent(xbc, z, g, b_params, DEC_SEQ, j, state_ssd)
            w_out = w_out_even
        else:
            a1p, a2p, *odd_caches = _attn_ctx(xp, mods, l, norm_g, w_odd, o_params, SEQ, carried=odd_caches)
            proj = _project(xs, mods, l, True, norm_g, w_odd, ODD_WIDTHS, ODD_REGROUP)
            a1s, a2s = _attn_lat(proj, caches, rope, o_params, DEC_SEQ, j)
            w_out = w_out_odd
        xp, xs = _channel((a1p, a2p, xp), (a1s, a2s, xs), mods, l, norm_g, w_out, w_up, w_down)

    new_c, new_n, new_m = mem_state
    new_ckv, new_k, new_v, new_kpe = odd_caches
    return (xp.reshape(BATCH, SEQ, D_MODEL), xs.reshape(DEC_BATCH, DEC_SEQ, D_MODEL),
            new_c, new_n, new_m[..., 0].reshape(BATCH, N_EVEN, 2, NH_A), ssd_state[0],
            new_k.reshape(BATCH, N_ODD, SEQ, NKV_C, HD_C), new_v.reshape(BATCH, N_ODD, SEQ, NKV_C, HD_C),
            new_ckv, new_kpe[..., :ROPE_D])
```
